```python
import jax, jax.numpy as jnp
from jax import lax
import numpy as np

D_MODEL = 1024
BATCH = 8
SEQ = 2048
DEPTH = 2
DEC_BATCH = 128
DEC_SEQ = 1
PAST_LEN = 16384
PAGE_SIZE = 128

GLA_HEADS = 4
GLA_KEY_WIDTH = D_MODEL // 2
GLA_VAL_WIDTH = D_MODEL
GLA_DK = GLA_KEY_WIDTH // GLA_HEADS
GLA_DV = GLA_VAL_WIDTH // GLA_HEADS
GLA_GATE_RANK = 16
GLA_GATE_TAU = 16.0
GLA_CHUNK = 64
RG_WIDTH = D_MODEL
RG_BLOCKS = 8
RG_BLOCK_W = RG_WIDTH // RG_BLOCKS
CONV_W = 4
RG_C = 8.0
N_BRANCH = 2
N_IN = 2 * GLA_KEY_WIDTH + 2 * GLA_VAL_WIDTH + GLA_GATE_RANK + RG_WIDTH + N_BRANCH * D_MODEL
N_EXPERTS = 32
TOP_K = 4
D_EXPERT = D_MODEL
SWIGLU_LIMIT = 7.0
SWIGLU_ALPHA = 1.702
MOE_BLOCK = 128
DEEPNORM_ALPHA = (2.0 * DEPTH) ** 0.25
DEEPNORM_BETA = (8.0 * DEPTH) ** -0.25
LN_EPS = 1e-5

kernel_name = 'gla_rglru_moe_deepnorm_step'


def _layernorm(x, g, b):
    xf = x.astype(jnp.float32)
    mu = jnp.mean(xf, axis=-1, keepdims=True)
    var = jnp.mean(jnp.square(xf - mu), axis=-1, keepdims=True)
    return ((xf - mu) * lax.rsqrt(var + LN_EPS) * g.astype(jnp.float32) + b.astype(jnp.float32)).astype(x.dtype)


def _gla(q, k, v, log_a, s0):
    n_b, s_len, n_h, _ = q.shape
    dv = v.shape[-1]
    c = min(GLA_CHUNK, s_len)
    n_chunks = -(-s_len // c)
    pad = n_chunks * c - s_len

    def to_chunks(t):
        t = jnp.pad(t, ((0, 0), (0, pad), (0, 0), (0, 0)))
        return t.reshape(n_b, n_chunks, c, n_h, t.shape[-1]).transpose(1, 0, 3, 2, 4)

    mask = jnp.tril(jnp.ones((c, c), dtype=bool))

    def step(state, chunk):
        qc, kc, vc, gc = chunk
        b = jnp.cumsum(gc, axis=2)
        b_last = b[:, :, -1:, :]
        q_e = qc * jnp.exp(b)
        k_e = kc * jnp.exp(-b)
        scores = jnp.where(mask, jnp.einsum('bhtd,bhsd->bhts', q_e, k_e), 0.0)
        o = jnp.einsum('bhts,bhsv->bhtv', scores, vc) + jnp.einsum('bhtd,bhdv->bhtv', q_e, state)
        state = jnp.exp(b_last)[:, :, 0, :, None] * state + jnp.einsum('bhsd,bhsv->bhdv', kc * jnp.exp(b_last - b), vc)
        return state, o

    s_final, o = lax.scan(step, s0, (to_chunks(q), to_chunks(k), to_chunks(v), to_chunks(log_a)))
    o = o.transpose(1, 0, 3, 2, 4).reshape(n_b, n_chunks * c, n_h, dv)[:, :s_len]
    return o, s_final


def _rglru_scan(a, bx, h0):
    def combine(l, r):
        a_l, b_l = l
        a_r, b_r = r
        return a_l * a_r, a_r * b_l + b_r
    a_cum, h = lax.associative_scan(combine, (a, bx), axis=1)
    h = h + a_cum * h0[:, None, :]
    return h, h[:, -1]


def _token_mixers(x, s_gla, h0, conv_buf, w_in, gla_w_a2, gla_b_a, gla_norm_g, rg_conv_w, rg_conv_b,
                  rg_w_a, rg_b_a, rg_w_x, rg_b_x, rg_lambda, b_merge, w_branch, w_o):
    f32 = jnp.float32
    n_b, s_len, _ = x.shape
    proj = jnp.einsum('bsd,dn->bsn', x, w_in)
    cuts = (GLA_KEY_WIDTH, 2 * GLA_KEY_WIDTH, 2 * GLA_KEY_WIDTH + GLA_VAL_WIDTH,
            2 * GLA_KEY_WIDTH + 2 * GLA_VAL_WIDTH, 2 * GLA_KEY_WIDTH + 2 * GLA_VAL_WIDTH + GLA_GATE_RANK,
            2 * GLA_KEY_WIDTH + 2 * GLA_VAL_WIDTH + GLA_GATE_RANK + RG_WIDTH)
    q, k, v, g_out, a_low, xr, merge = jnp.split(proj, cuts, axis=-1)

    q = q.astype(f32).reshape(n_b, s_len, GLA_HEADS, GLA_DK) * (GLA_DK ** -0.5)
    k = k.astype(f32).reshape(n_b, s_len, GLA_HEADS, GLA_DK)
    v = v.astype(f32).reshape(n_b, s_len, GLA_HEADS, GLA_DV)
    log_a = jax.nn.log_sigmoid((a_low @ gla_w_a2 + gla_b_a).astype(f32)) / GLA_GATE_TAU
    o, s_gla_new = _gla(q, k, v, log_a.reshape(n_b, s_len, GLA_HEADS, GLA_DK), s_gla.astype(f32))
    mu = jnp.mean(o, axis=-1, keepdims=True)
    var = jnp.mean(jnp.square(o - mu), axis=-1, keepdims=True)
    o = (o - mu) * lax.rsqrt(var + LN_EPS) * gla_norm_g.astype(f32).reshape(GLA_HEADS, GLA_DV)
    o = o.reshape(n_b, s_len, GLA_VAL_WIDTH) * jax.nn.silu(g_out.astype(f32))

    conv_in = jnp.concatenate([conv_buf.astype(xr.dtype), xr], axis=1)
    xc = rg_conv_b.astype(f32) + sum(conv_in[:, j:j + s_len].astype(f32) * rg_conv_w[j].astype(f32)
                                     for j in range(CONV_W))
    conv_new = conv_in[:, s_len:]
    xb = xc.reshape(n_b, s_len, RG_BLOCKS, RG_BLOCK_W)
    r = jax.nn.sigmoid(jnp.einsum('bsnc,ncd->bsnd', xb, rg_w_a.astype(f32)).reshape(n_b, s_len, RG_WIDTH) + rg_b_a)
    i = jax.nn.sigmoid(jnp.einsum('bsnc,ncd->bsnd', xb, rg_w_x.astype(f32)).reshape(n_b, s_len, RG_WIDTH) + rg_b_x)
    log_a_rg = -RG_C * r * jax.nn.softplus(-rg_lambda.astype(f32))
    a = jnp.exp(log_a_rg)
    bx = jnp.sqrt(-jnp.expm1(2.0 * log_a_rg)) * (i * xc)
    h, h_new = _rglru_scan(a, bx, h0.astype(f32))

    gate_gla, gate_rg = jnp.split(jax.nn.sigmoid((merge + b_merge).astype(f32)), N_BRANCH, axis=-1)
    merged = (gate_gla * jnp.einsum('bsv,vd->bsd', o.astype(x.dtype), w_branch[0])
              + gate_rg * jnp.einsum('bsv,vd->bsd', h.astype(x.dtype), w_branch[1]))
    out = jnp.einsum('bsd,de->bse', merged.astype(x.dtype), w_o)
    return out, s_gla_new, h_new, conv_new


def _moe(x2d, router_w, router_b, moe_w_gu, moe_b_gu, moe_w_down, moe_b_down):
    f32 = jnp.float32
    n_tok, d = x2d.shape
    m = n_tok * TOP_K
    logits = jnp.dot(x2d.astype(f32), router_w.astype(f32)) + router_b.astype(f32)
    top_val, top_idx = lax.top_k(logits, TOP_K)
    top_w = jax.nn.softmax(top_val, axis=-1)
    flat_e = top_idx.reshape(-1).astype(jnp.int32)
    flat_tok = jnp.repeat(jnp.arange(n_tok, dtype=jnp.int32), TOP_K)
    flat_w = top_w.reshape(-1)
    order = jnp.argsort(flat_e)
    e_sorted = flat_e[order]
    counts = jnp.bincount(flat_e, length=N_EXPERTS).astype(jnp.int32)
    padded = (counts + MOE_BLOCK - 1) // MOE_BLOCK * MOE_BLOCK
    start_sorted = jnp.cumsum(counts) - counts
    ends_pad = jnp.cumsum(padded)
    start_pad = ends_pad - padded
    dest = start_pad[e_sorted] + jnp.arange(m, dtype=jnp.int32) - start_sorted[e_sorted]
    n_blocks = -(-m // MOE_BLOCK) + N_EXPERTS
    n_rows = n_blocks * MOE_BLOCK
    row_tok = jnp.full((n_rows,), n_tok, jnp.int32).at[dest].set(flat_tok[order])
    row_w = jnp.zeros((n_rows,), f32).at[dest].set(flat_w[order])
    block_e = jnp.minimum(jnp.searchsorted(ends_pad, jnp.arange(n_blocks, dtype=jnp.int32) * MOE_BLOCK, side='right'),
                          N_EXPERTS - 1)
    x_pad = jnp.concatenate([x2d, jnp.zeros((1, d), x2d.dtype)], axis=0)

    def expert_block(args):
        e, tok = args
        xb = x_pad[tok]
        hgu = xb @ moe_w_gu[e] + moe_b_gu[e]
        gate, up = hgu[:, :D_EXPERT], hgu[:, D_EXPERT:]
        gate = jnp.minimum(gate, SWIGLU_LIMIT)
        up = jnp.clip(up, -SWIGLU_LIMIT, SWIGLU_LIMIT)
        act = (up + 1.0) * gate * jax.nn.sigmoid(SWIGLU_ALPHA * gate)
        return act @ moe_w_down[e] + moe_b_down[e]

    out = lax.map(expert_block, (block_e, row_tok.reshape(n_blocks, MOE_BLOCK)))
    y = jnp.zeros((n_tok + 1, d), f32).at[row_tok].add(row_w[:, None] * out.reshape(n_rows, d).astype(f32))
    return y[:n_tok].astype(x2d.dtype)


def _decoder_layer(x, s_gla, h0, conv_buf, ln1_g, ln1_b, w_in, gla_w_a2, gla_b_a, gla_norm_g, rg_conv_w,
                   rg_conv_b, rg_w_a, rg_b_a, rg_w_x, rg_b_x, rg_lambda, b_merge, w_branch, w_o, ln2_g, ln2_b,
                   router_w, router_b, moe_w_gu, moe_b_gu, moe_w_down, moe_b_down):
    mix, s_gla_new, h_new, conv_new = _token_mixers(
        x, s_gla, h0, conv_buf, w_in, gla_w_a2, gla_b_a, gla_norm_g, rg_conv_w, rg_conv_b,
        rg_w_a, rg_b_a, rg_w_x, rg_b_x, rg_lambda, b_merge, w_branch, w_o)
    x = _layernorm(DEEPNORM_ALPHA * x + mix, ln1_g, ln1_b)
    n_b, s_len, d = x.shape
    ffn = _moe(x.reshape(n_b * s_len, d), router_w, router_b, moe_w_gu, moe_b_gu, moe_w_down, moe_b_down)
    x = _layernorm(DEEPNORM_ALPHA * x + ffn.reshape(n_b, s_len, d), ln2_g, ln2_b)
    return x, s_gla_new, h_new, conv_new


def setup_inputs(seed: int = 0) -> dict:
    key = jax.random.key(seed)
    ks = jax.random.split(key, 32)
    f32 = jnp.float32
    L, D, E, F = DEPTH, D_MODEL, N_EXPERTS, D_EXPERT

    def nrm(k, shape, scale):
        return jax.random.normal(k, shape, f32) * scale

    u = jax.random.uniform(ks[17], (L, RG_WIDTH), f32, 0.9, 0.999)
    s = u ** (1.0 / RG_C)
    rg_lambda = jnp.log(s) - jnp.log1p(-s)
    return {
        'x_prompt': nrm(ks[0], (BATCH, SEQ, D), 1.0),
        'x_sample': nrm(ks[1], (DEC_BATCH, DEC_SEQ, D), 1.0),
        'state_gla': nrm(ks[2], (L, DEC_BATCH, GLA_HEADS, GLA_DK, GLA_DV), 0.1),
        'state_rglru': nrm(ks[3], (L, DEC_BATCH, RG_WIDTH), 0.5),
        'state_conv': nrm(ks[4], (L, DEC_BATCH, CONV_W - 1, RG_WIDTH), 1.0),
        'ln1_g': 1.0 + nrm(ks[5], (L, D), 0.02),
        'ln1_b': nrm(ks[6], (L, D), 0.02),
        'w_in': nrm(ks[7], (L, D, N_IN), D ** -0.5),
        'gla_w_a2': nrm(ks[8], (L, GLA_GATE_RANK, GLA_KEY_WIDTH), GLA_GATE_RANK ** -0.5),
        'gla_b_a': nrm(ks[9], (L, GLA_KEY_WIDTH), 0.1),
        'gla_norm_g': 1.0 + nrm(ks[10], (L, GLA_VAL_WIDTH), 0.02),
        'rg_conv_w': nrm(ks[11], (L, CONV_W, RG_WIDTH), CONV_W ** -0.5),
        'rg_conv_b': nrm(ks[12], (L, RG_WIDTH), 0.02),
        'rg_w_a': nrm(ks[13], (L, RG_BLOCKS, RG_BLOCK_W, RG_BLOCK_W), RG_BLOCK_W ** -0.5),
        'rg_b_a': nrm(ks[14], (L, RG_WIDTH), 0.02),
        'rg_w_x': nrm(ks[15], (L, RG_BLOCKS, RG_BLOCK_W, RG_BLOCK_W), RG_BLOCK_W ** -0.5),
        'rg_b_x': nrm(ks[16], (L, RG_WIDTH), 0.02),
        'rg_lambda': rg_lambda,
        'b_merge': nrm(ks[18], (L, N_BRANCH * D), 0.02),
        'w_branch': nrm(ks[19], (L, N_BRANCH, GLA_VAL_WIDTH, D), GLA_VAL_WIDTH ** -0.5),
        'w_o': nrm(ks[20], (L, D, D), D ** -0.5 * DEEPNORM_BETA),
        'ln2_g': 1.0 + nrm(ks[21], (L, D), 0.02),
        'ln2_b': nrm(ks[22], (L, D), 0.02),
        'router_w': nrm(ks[23], (L, D, E), D ** -0.5),
        'router_b': nrm(ks[24], (L, E), 0.01),
        'moe_w_gu': nrm(ks[25], (L, E, D, 2 * F), D ** -0.5),
        'moe_b_gu': nrm(ks[26], (L, E, 2 * F), 0.02),
        'moe_w_down': nrm(ks[27], (L, E, F, D), F ** -0.5 * DEEPNORM_BETA),
        'moe_b_down': nrm(ks[28], (L, E, D), 0.02),
    }


def reference(x_prompt, x_sample, state_gla, state_rglru, state_conv, ln1_g, ln1_b, w_in, gla_w_a2, gla_b_a,
              gla_norm_g, rg_conv_w, rg_conv_b, rg_w_a, rg_b_a, rg_w_x, rg_b_x, rg_lambda, b_merge, w_branch, w_o,
              ln2_g, ln2_b, router_w, router_b, moe_w_gu, moe_b_gu, moe_w_down, moe_b_down):
    n_prompt = x_prompt.shape[0]
    xp, xs = x_prompt, x_sample
    gla_p, rg_p, conv_p, gla_s, rg_s, conv_s = [], [], [], [], [], []
    for layer in range(DEPTH):
        params = (ln1_g[layer], ln1_b[layer], w_in[layer], gla_w_a2[layer], gla_b_a[layer], gla_norm_g[layer],
                  rg_conv_w[layer], rg_conv_b[layer], rg_w_a[layer], rg_b_a[layer], rg_w_x[layer], rg_b_x[layer],
                  rg_lambda[layer], b_merge[layer], w_branch[layer], w_o[layer], ln2_g[layer], ln2_b[layer],
                  router_w[layer], router_b[layer], moe_w_gu[layer], moe_b_gu[layer], moe_w_down[layer],
                  moe_b_down[layer])
        xp, g_new, h_new, c_new = _decoder_layer(
            xp, jnp.zeros((n_prompt, GLA_HEADS, GLA_DK, GLA_DV), jnp.float32),
            jnp.zeros((n_prompt, RG_WIDTH), jnp.float32),
            jnp.zeros((n_prompt, CONV_W - 1, RG_WIDTH), x_prompt.dtype), *params)
        gla_p.append(g_new)
        rg_p.append(h_new)
        conv_p.append(c_new)
        xs, g_new, h_new, c_new = _decoder_layer(
            xs, state_gla[layer], state_rglru[layer], state_conv[layer], *params)
        gla_s.append(g_new)
        rg_s.append(h_new)
        conv_s.append(c_new)
    return (xp, xs, jnp.stack(gla_p), jnp.stack(rg_p), jnp.stack(conv_p),
            jnp.stack(gla_s), jnp.stack(rg_s), jnp.stack(conv_s))
```

```python
import functools

import jax
import jax.numpy as jnp
from jax import lax
from jax.experimental import pallas as pl
from jax.experimental.pallas import tpu as pltpu

F32 = jnp.float32
BF16 = jnp.bfloat16

TOP_K = 4
GLA_GATE_TAU = 16.0
GLA_CHUNK = 64
RG_C = 8.0
SWIGLU_LIMIT = 7.0
SWIGLU_ALPHA = 1.702
LN_EPS = 1e-5

LANES = 128
BF16_SUBLANES = 16
VMEM_LIMIT = 56 * 1024 * 1024

PROJ_ROWS = 256
SEQ_ROWS = 256
MOE_ROWS = 256
FFN_COLS = 512


def _params(*sem):
    return pltpu.CompilerParams(dimension_semantics=sem, vmem_limit_bytes=VMEM_LIMIT)


def _const_spec(shape):
    nd = len(shape)
    return pl.BlockSpec(shape, lambda *_: (0,) * nd)


def _bdot(a, b):
    return jnp.dot(a, b, preferred_element_type=F32)


def _split3(x):
    hi = x.astype(BF16)
    r1 = x - hi.astype(F32)
    mid = r1.astype(BF16)
    lo = (r1 - mid.astype(F32)).astype(BF16)
    return hi, mid, lo


def _sigmoid(x):
    return 1.0 / (1.0 + jnp.exp(-x))


def _log_sigmoid(x):
    return jnp.minimum(x, 0.0) - jnp.log1p(jnp.exp(-jnp.abs(x)))


def _softplus(x):
    return jnp.maximum(x, 0.0) + jnp.log1p(jnp.exp(-jnp.abs(x)))


def _layernorm(x, g, b):
    mu = jnp.mean(x, axis=-1, keepdims=True)
    xc = x - mu
    var = jnp.mean(xc * xc, axis=-1, keepdims=True)
    return xc * lax.rsqrt(var + LN_EPS) * g + b


def _proj_kernel(x_ref, wq_ref, wal_ref, wa2_ref, ba_ref, wxr_ref, wmg_ref, bmg_ref,
                 qkvg_ref, loga_ref, xr_ref, gate_ref):
    xb = x_ref[...].astype(BF16)
    d = x_ref.shape[1]
    for c in range(0, qkvg_ref.shape[1], d):
        qkvg_ref[:, c:c + d] = _bdot(xb, wq_ref[:, c:c + d]).astype(BF16)
    a_low = _bdot(xb, wal_ref[...])
    z = _bdot(a_low.astype(BF16), wa2_ref[...]) + ba_ref[...]
    loga_ref[...] = _log_sigmoid(z) * (1.0 / GLA_GATE_TAU)
    xr_ref[...] = _bdot(xb, wxr_ref[...])
    for c in range(0, gate_ref.shape[1], d):
        gate_ref[:, c:c + d] = _sigmoid(_bdot(xb, wmg_ref[:, c:c + d]) + bmg_ref[:, c:c + d]).astype(BF16)


def _proj(x2d, w, tm):
    t, d = x2d.shape
    kw = w["wq"].shape[1]
    nk = w["wa2"].shape[1]
    consts = [w["wq"], w["wal"], w["wa2"], w["ba"], w["wxr"], w["wmg"], w["bmg"]]
    return pl.pallas_call(
        _proj_kernel,
        grid=(t // tm,),
        in_specs=[pl.BlockSpec((tm, d), lambda i: (i, 0))] + [_const_spec(c.shape) for c in consts],
        out_specs=[pl.BlockSpec((tm, kw), lambda i: (i, 0)),
                   pl.BlockSpec((tm, nk), lambda i: (i, 0)),
                   pl.BlockSpec((tm, d), lambda i: (i, 0)),
                   pl.BlockSpec((tm, 2 * d), lambda i: (i, 0))],
        out_shape=[jax.ShapeDtypeStruct((t, kw), BF16),
                   jax.ShapeDtypeStruct((t, nk), F32),
                   jax.ShapeDtypeStruct((t, d), F32),
                   jax.ShapeDtypeStruct((t, 2 * d), BF16)],
        compiler_params=_params("arbitrary"),
        name="proj",
    )(x2d, *consts)


def _gla_kernel(*refs, chunk, n_chunks, heads, has_state):
    if has_state:
        q_ref, k_ref, v_ref, la_ref, g_ref, ng_ref, s0_ref, o_ref, so_ref, s_scr = refs
    else:
        q_ref, k_ref, v_ref, la_ref, g_ref, ng_ref, o_ref, so_ref, s_scr = refs
    t = pl.program_id(1)
    dk = q_ref.shape[1] // heads
    dv = v_ref.shape[1] // heads
    scale = dk ** -0.5

    @pl.when(t == 0)
    def _():
        if has_state:
            s_scr[...] = s0_ref[0]
        else:
            s_scr[...] = jnp.zeros_like(s_scr)

    row = lax.broadcasted_iota(jnp.int32, (chunk, chunk), 0)
    col = lax.broadcasted_iota(jnp.int32, (chunk, chunk), 1)
    causal = row >= col
    tril = jnp.where(causal, 1.0, 0.0).astype(BF16)

    for ci in range(n_chunks):
        rows = slice(ci * chunk, (ci + 1) * chunk)
        la_hi, la_mid, la_lo = _split3(la_ref[rows, :])
        b = _bdot(tril, la_hi) + _bdot(tril, la_mid) + _bdot(tril, la_lo)
        b_last = b[chunk - 1:chunk, :]
        q_e = (q_ref[rows, :].astype(F32) * scale * jnp.exp(b)).astype(BF16)
        k = k_ref[rows, :].astype(F32)
        k_e = (k * jnp.exp(-b)).astype(BF16)
        k_d = (k * jnp.exp(b_last - b)).astype(BF16)
        decay = jnp.exp(b_last)
        for h in range(heads):
            ks = slice(h * dk, (h + 1) * dk)
            vs = slice(h * dv, (h + 1) * dv)
            v_h = v_ref[rows, vs]
            s_old = s_scr[h]
            scores = lax.dot_general(q_e[:, ks], k_e[:, ks], (((1,), (1,)), ((), ())),
                                     preferred_element_type=F32)
            scores = jnp.where(causal, scores, 0.0)
            o = _bdot(scores.astype(BF16), v_h) + _bdot(q_e[:, ks], s_old.astype(BF16))
            kv = lax.dot_general(k_d[:, ks], v_h, (((0,), (0,)), ((), ())), preferred_element_type=F32)
            s_scr[h] = jnp.transpose(decay[:, ks]) * s_old + kv
            mu = jnp.mean(o, axis=-1, keepdims=True)
            oc = o - mu
            var = jnp.mean(oc * oc, axis=-1, keepdims=True)
            on = oc * lax.rsqrt(var + LN_EPS) * ng_ref[:, vs]
            g = g_ref[rows, vs].astype(F32)
            o_ref[rows, vs] = (on * (g * _sigmoid(g))).astype(BF16)

    @pl.when(t == pl.num_programs(1) - 1)
    def _():
        so_ref[0] = s_scr[...]


def _gla(qkvg, loga, norm_g, s0, nb, rows, chunk, heads, dk, dv):
    t = qkvg.shape[0]
    tps = t // nb // rows
    kw, vw = heads * dk, heads * dv
    has_state = s0 is not None
    in_specs = [pl.BlockSpec((rows, kw), lambda b, i: (b * tps + i, 0)),
                pl.BlockSpec((rows, kw), lambda b, i: (b * tps + i, 1)),
                pl.BlockSpec((rows, vw), lambda b, i: (b * tps + i, (2 * kw) // vw)),
                pl.BlockSpec((rows, kw), lambda b, i: (b * tps + i, 0)),
                pl.BlockSpec((rows, vw), lambda b, i: (b * tps + i, (2 * kw) // vw + 1)),
                _const_spec(norm_g.shape)]
    args = [qkvg, qkvg, qkvg, loga, qkvg, norm_g]
    if has_state:
        in_specs.append(pl.BlockSpec((1, heads, dk, dv), lambda b, i: (b, 0, 0, 0)))
        args.append(s0)
    return pl.pallas_call(
        functools.partial(_gla_kernel, chunk=chunk, n_chunks=rows // chunk, heads=heads, has_state=has_state),
        grid=(nb, tps),
        in_specs=in_specs,
        out_specs=[pl.BlockSpec((rows, vw), lambda b, i: (b * tps + i, 0)),
                   pl.BlockSpec((1, heads, dk, dv), lambda b, i: (b, 0, 0, 0))],
        out_shape=[jax.ShapeDtypeStruct((t, vw), BF16),
                   jax.ShapeDtypeStruct((nb, heads, dk, dv), F32)],
        scratch_shapes=[pltpu.VMEM((heads, dk, dv), F32)],
        compiler_params=_params("arbitrary", "arbitrary"),
        name="gla",
    )(*args)


def _rg_gates(xc, wax_ref, ba_ref, bx_ref, lam_ref):
    n_blocks, bw, _ = wax_ref.shape
    r_parts, i_parts = [], []
    for n in range(n_blocks):
        cs = slice(n * bw, (n + 1) * bw)
        pre = _bdot(xc[:, cs].astype(BF16), wax_ref[n])
        r_parts.append(_sigmoid(pre[:, :bw] + ba_ref[:, cs]))
        i_parts.append(_sigmoid(pre[:, bw:] + bx_ref[:, cs]))
    r = jnp.concatenate(r_parts, axis=1)
    i = jnp.concatenate(i_parts, axis=1)
    log_a = -RG_C * r * _softplus(-lam_ref[...])
    a = jnp.exp(log_a)
    bx = jnp.sqrt(jnp.tanh(-log_a) * (1.0 + a * a)) * (i * xc)
    return a, bx


def _rglru_seq_kernel(xr_ref, cw_ref, cb_ref, wax_ref, ba_ref, bx_ref, lam_ref,
                      h_ref, hl_ref, cv_ref, cbuf, a_scr, b_scr, h_scr, hc):
    t = pl.program_id(1)
    rows = xr_ref.shape[0]
    taps = cw_ref.shape[0]
    head = 8

    @pl.when(t == 0)
    def _():
        cbuf[0:head, :] = jnp.zeros((head, cbuf.shape[1]), F32)
        hc[...] = jnp.zeros_like(hc)

    cbuf[head:head + rows, :] = xr_ref[...]
    first = head - (taps - 1)
    xc = cb_ref[...] + sum(cbuf[first + j:first + j + rows, :] * cw_ref[j:j + 1, :] for j in range(taps))
    tail = cbuf[first + rows:head + rows, :]
    cbuf[first:head, :] = tail

    a, bx = _rg_gates(xc, wax_ref, ba_ref, bx_ref, lam_ref)
    a_scr[...] = a
    b_scr[...] = bx

    def step(i, h):
        h = a_scr[pl.ds(i, 1), :] * h + b_scr[pl.ds(i, 1), :]
        h_scr[pl.ds(i, 1), :] = h
        return h

    h_last = lax.fori_loop(0, rows, step, hc[...], unroll=8)
    hc[...] = h_last
    h_ref[...] = h_scr[...].astype(BF16)

    @pl.when(t == pl.num_programs(1) - 1)
    def _():
        hl_ref[0] = h_last
        cv_ref[0] = tail


def _rglru_seq(xr, w, nb, rows):
    t, width = xr.shape
    tps = t // nb // rows
    taps = w["cw"].shape[0]
    consts = [w["cw"], w["cb"], w["wax"], w["rba"], w["rbx"], w["lam"]]
    return pl.pallas_call(
        _rglru_seq_kernel,
        grid=(nb, tps),
        in_specs=[pl.BlockSpec((rows, width), lambda b, i: (b * tps + i, 0))] + [_const_spec(c.shape) for c in consts],
        out_specs=[pl.BlockSpec((rows, width), lambda b, i: (b * tps + i, 0)),
                   pl.BlockSpec((1, 1, width), lambda b, i: (b, 0, 0)),
                   pl.BlockSpec((1, taps - 1, width), lambda b, i: (b, 0, 0))],
        out_shape=[jax.ShapeDtypeStruct((t, width), BF16),
                   jax.ShapeDtypeStruct((nb, 1, width), F32),
                   jax.ShapeDtypeStruct((nb, taps - 1, width), F32)],
        scratch_shapes=[pltpu.VMEM((rows + 8, width), F32), pltpu.VMEM((rows, width), F32),
                        pltpu.VMEM((rows, width), F32), pltpu.VMEM((rows, width), F32),
                        pltpu.VMEM((1, width), F32)],
        compiler_params=_params("arbitrary", "arbitrary"),
        name="rglru_seq",
    )(xr, *consts)


def _rglru_step_kernel(xr_ref, sc_ref, h0_ref, cw_ref, cb_ref, wax_ref, ba_ref, bx_ref, lam_ref,
                       h_ref, hn_ref, cv_ref):
    taps = cw_ref.shape[0]
    xr = xr_ref[...]
    xc = cb_ref[...] + sum(sc_ref[j] * cw_ref[j:j + 1, :] for j in range(taps - 1)) + xr * cw_ref[taps - 1:taps, :]
    a, bx = _rg_gates(xc, wax_ref, ba_ref, bx_ref, lam_ref)
    h = a * h0_ref[...] + bx
    h_ref[...] = h.astype(BF16)
    hn_ref[...] = h
    for j in range(taps - 2):
        cv_ref[j] = sc_ref[j + 1]
    cv_ref[taps - 2] = xr


def _rglru_step(xr, conv_t, h0, w):
    n, width = xr.shape
    consts = [w["cw"], w["cb"], w["wax"], w["rba"], w["rbx"], w["lam"]]
    args = [xr, conv_t, h0] + consts
    return pl.pallas_call(
        _rglru_step_kernel,
        grid=(1,),
        in_specs=[_const_spec(a.shape) for a in args],
        out_specs=[_const_spec((n, width)), _const_spec((n, width)), _const_spec(conv_t.shape)],
        out_shape=[jax.ShapeDtypeStruct((n, width), BF16),
                   jax.ShapeDtypeStruct((n, width), F32),
                   jax.ShapeDtypeStruct(conv_t.shape, F32)],
        compiler_params=_params("arbitrary"),
        name="rglru_step",
    )(*args)


def _post_kernel(x_ref, o_ref, h_ref, gate_ref, wb0_ref, wb1_ref, wo_ref, g1_ref, b1_ref, rw_ref, rb_ref,
                 x1_ref, idx_ref, wt_ref, *, alpha, n_experts):
    d = x_ref.shape[1]
    gate = gate_ref[...].astype(F32)
    merged = gate[:, :d] * _bdot(o_ref[...], wb0_ref[...]) + gate[:, d:] * _bdot(h_ref[...], wb1_ref[...])
    mix = _bdot(merged.astype(BF16), wo_ref[...])
    x1 = _layernorm(alpha * x_ref[...] + mix, g1_ref[...], b1_ref[...])
    x1_ref[...] = x1

    xh, xm, xl = _split3(x1)
    wh, wm, wl = _split3(rw_ref[...])
    logits = (_bdot(xl, wh) + _bdot(xh, wl) + _bdot(xm, wm)) + (_bdot(xm, wh) + _bdot(xh, wm)) + _bdot(xh, wh)
    logits = logits + rb_ref[...]
    lane = lax.broadcasted_iota(jnp.int32, logits.shape, 1)
    lane_f = lane.astype(F32)
    neg_inf = jnp.float32(-jnp.inf)
    cur = jnp.where(lane < n_experts, logits, neg_inf)
    vals, idxs = [], []
    for _ in range(TOP_K):
        m = jnp.max(cur, axis=-1, keepdims=True)
        sel = jnp.min(jnp.where(cur == m, lane_f, float(LANES)), axis=-1, keepdims=True)
        vals.append(m)
        idxs.append(sel)
        cur = jnp.where(lane_f == sel, neg_inf, cur)
    exps = [jnp.exp(v - vals[0]) for v in vals]
    total = sum(exps)
    idx_out = jnp.zeros(logits.shape, F32)
    wt_out = jnp.zeros(logits.shape, F32)
    for j in range(TOP_K):
        idx_out = jnp.where(lane == j, idxs[j], idx_out)
        wt_out = jnp.where(lane == j, exps[j] / total, wt_out)
    idx_ref[...] = idx_out.astype(jnp.int32)
    wt_ref[...] = wt_out


def _post(x2d, o, h, gate, w, tm, alpha, n_experts):
    t, d = x2d.shape
    consts = [w["wb0"], w["wb1"], w["wo"], w["ln1g"], w["ln1b"], w["rw"], w["rb"]]
    row = lambda width: pl.BlockSpec((tm, width), lambda i: (i, 0))
    return pl.pallas_call(
        functools.partial(_post_kernel, alpha=alpha, n_experts=n_experts),
        grid=(t // tm,),
        in_specs=[row(d), row(d), row(d), row(2 * d)] + [_const_spec(c.shape) for c in consts],
        out_specs=[row(d), row(LANES), row(LANES)],
        out_shape=[jax.ShapeDtypeStruct((t, d), F32),
                   jax.ShapeDtypeStruct((t, LANES), jnp.int32),
                   jax.ShapeDtypeStruct((t, LANES), F32)],
        compiler_params=_params("arbitrary"),
        name="post",
    )(x2d, o, h, gate, *consts)


def _ffn_kernel(be_ref, nb_ref, x_ref, wgu_ref, bgu_ref, wd_ref, bd_ref, o_ref, wgu_bf, wd_bf):
    i = pl.program_id(0)
    f = wd_ref.shape[1]
    new_expert = jnp.logical_or(i == 0, be_ref[i] != be_ref[jnp.maximum(i - 1, 0)])

    @pl.when(new_expert)
    def _():
        wgu_bf[...] = wgu_ref[0].astype(BF16)
        wd_bf[...] = wd_ref[0].astype(BF16)

    @pl.when(i < nb_ref[0])
    def _():
        x = x_ref[...]
        acc = jnp.zeros(o_ref.shape, F32)
        for c in range(0, f, FFN_COLS):
            gate = _bdot(x, wgu_bf[:, c:c + FFN_COLS]) + bgu_ref[0, :, c:c + FFN_COLS]
            up = _bdot(x, wgu_bf[:, f + c:f + c + FFN_COLS]) + bgu_ref[0, :, f + c:f + c + FFN_COLS]
            gate = jnp.minimum(gate, SWIGLU_LIMIT)
            up = jnp.clip(up, -SWIGLU_LIMIT, SWIGLU_LIMIT)
            act = (up + 1.0) * gate * _sigmoid(SWIGLU_ALPHA * gate)
            acc = acc + _bdot(act.astype(BF16), wd_bf[c:c + FFN_COLS, :])
        o_ref[...] = acc + bd_ref[0]

    @pl.when(i >= nb_ref[0])
    def _():
        o_ref[...] = jnp.zeros_like(o_ref)


def _ffn(block_e, n_used, xs, w_gu, b_gu, w_down, b_down, bm):
    n_rows, d = xs.shape
    n_exp, _, f2 = w_gu.shape
    f = f2 // 2
    grid_spec = pltpu.PrefetchScalarGridSpec(
        num_scalar_prefetch=2,
        grid=(n_rows // bm,),
        in_specs=[pl.BlockSpec((bm, d), lambda i, be, nb: (i, 0)),
                  pl.BlockSpec((1, d, f2), lambda i, be, nb: (be[i], 0, 0)),
                  pl.BlockSpec((1, 1, f2), lambda i, be, nb: (be[i], 0, 0)),
                  pl.BlockSpec((1, f, d), lambda i, be, nb: (be[i], 0, 0)),
                  pl.BlockSpec((1, 1, d), lambda i, be, nb: (be[i], 0, 0))],
        out_specs=pl.BlockSpec((bm, d), lambda i, be, nb: (i, 0)),
        scratch_shapes=[pltpu.VMEM((d, f2), BF16), pltpu.VMEM((f, d), BF16)],
    )
    return pl.pallas_call(
        _ffn_kernel,
        grid_spec=grid_spec,
        out_shape=jax.ShapeDtypeStruct((n_rows, d), F32),
        compiler_params=_params("arbitrary"),
        name="ffn",
    )(block_e, n_used, xs, w_gu, b_gu.reshape(n_exp, 1, f2), w_down, b_down.reshape(n_exp, 1, d))


def _combine_kernel(x_ref, y_ref, wt_ref, g_ref, b_ref, o_ref, *, alpha):
    wt = wt_ref[...]
    ffn = sum(wt[:, j:j + 1] * y_ref[j] for j in range(TOP_K))
    o_ref[...] = _layernorm(alpha * x_ref[...] + ffn, g_ref[...], b_ref[...])


def _combine(x1, ys, wt, g, b, tm, alpha):
    t, d = x1.shape
    return pl.pallas_call(
        functools.partial(_combine_kernel, alpha=alpha),
        grid=(t // tm,),
        in_specs=[pl.BlockSpec((tm, d), lambda i: (i, 0)),
                  pl.BlockSpec((TOP_K, tm, d), lambda i: (0, i, 0)),
                  pl.BlockSpec((tm, LANES), lambda i: (i, 0)),
                  _const_spec(g.shape), _const_spec(b.shape)],
        out_specs=pl.BlockSpec((tm, d), lambda i: (i, 0)),
        out_shape=jax.ShapeDtypeStruct((t, d), F32),
        compiler_params=_params("arbitrary"),
        name="combine",
    )(x1, ys, wt, g, b)


def _route(top_idx, n_experts, bm):
    n_tok = top_idx.shape[0]
    m = n_tok * TOP_K
    flat_e = top_idx.reshape(-1)
    order = jnp.argsort(flat_e)
    e_sorted = flat_e[order]
    counts = jnp.bincount(flat_e, length=n_experts).astype(jnp.int32)
    padded = (counts + bm - 1) // bm * bm
    start_sorted = jnp.cumsum(counts) - counts
    ends_pad = jnp.cumsum(padded)
    start_pad = ends_pad - padded
    dest = (start_pad[e_sorted] + jnp.arange(m, dtype=jnp.int32) - start_sorted[e_sorted]).astype(jnp.int32)
    n_blocks = -(-m // bm) + n_experts
    row_tok = jnp.full((n_blocks * bm,), n_tok, jnp.int32).at[dest].set((order // TOP_K).astype(jnp.int32))
    pos = jnp.zeros((m,), jnp.int32).at[order].set(dest)
    block_e = jnp.minimum(jnp.searchsorted(ends_pad, jnp.arange(n_blocks, dtype=jnp.int32) * bm, side="right"),
                          n_experts - 1).astype(jnp.int32)
    n_used = (ends_pad[-1:] // bm).astype(jnp.int32)
    return row_tok, pos, block_e, n_used


def _moe(x1, top_idx, top_w_pad, w, alpha, tm):
    n_tok, d = x1.shape
    n_experts = w["w_gu"].shape[0]
    row_tok, pos, block_e, n_used = _route(top_idx, n_experts, MOE_ROWS)
    x_pad = jnp.concatenate([x1.astype(BF16), jnp.zeros((1, d), BF16)], axis=0)
    xs = x_pad[row_tok]
    out_rows = _ffn(block_e, n_used, xs, w["w_gu"], w["b_gu"], w["w_down"], w["b_down"], MOE_ROWS)
    ys = out_rows[pos.reshape(n_tok, TOP_K).T]
    return _combine(x1, ys, top_w_pad, w["ln2g"], w["ln2b"], tm, alpha)


def _layer_weights(layer, w_in, gla_w_a2, gla_b_a, gla_norm_g, rg_conv_w, rg_conv_b, rg_w_a, rg_b_a, rg_w_x,
                   rg_b_x, rg_lambda, b_merge, w_branch, w_o, ln1_g, ln1_b, ln2_g, ln2_b, router_w, router_b,
                   moe_w_gu, moe_b_gu, moe_w_down, moe_b_down, kw, vw):
    d = w_in.shape[1]
    rank = gla_w_a2.shape[1]
    width = rg_conv_w.shape[2]
    n_exp = router_w.shape[2]
    c0 = 2 * kw + 2 * vw
    wi = w_in[layer]
    row = lambda v: v.reshape(1, -1)
    return {
        "wq": wi[:, :c0].astype(BF16),
        "wal": jnp.pad(wi[:, c0:c0 + rank], ((0, 0), (0, LANES - rank))).astype(BF16),
        "wa2": jnp.pad(gla_w_a2[layer], ((0, LANES - rank), (0, 0))).astype(BF16),
        "ba": row(gla_b_a[layer]),
        "wxr": wi[:, c0 + rank:c0 + rank + width].astype(BF16),
        "wmg": wi[:, c0 + rank + width:].astype(BF16),
        "bmg": row(b_merge[layer]),
        "ng": row(gla_norm_g[layer]),
        "cw": rg_conv_w[layer], "cb": row(rg_conv_b[layer]),
        "wax": jnp.concatenate([rg_w_a[layer], rg_w_x[layer]], axis=-1).astype(BF16),
        "rba": row(rg_b_a[layer]), "rbx": row(rg_b_x[layer]), "lam": row(rg_lambda[layer]),
        "wb0": w_branch[layer, 0].astype(BF16), "wb1": w_branch[layer, 1].astype(BF16),
        "wo": w_o[layer].astype(BF16),
        "ln1g": row(ln1_g[layer]), "ln1b": row(ln1_b[layer]),
        "ln2g": row(ln2_g[layer]), "ln2b": row(ln2_b[layer]),
        "rw": jnp.pad(router_w[layer], ((0, 0), (0, LANES - n_exp))),
        "rb": jnp.pad(row(router_b[layer]), ((0, 0), (0, LANES - n_exp))),
        "w_gu": moe_w_gu[layer], "b_gu": moe_b_gu[layer],
        "w_down": moe_w_down[layer], "b_down": moe_b_down[layer],
    }


def kernel(x_prompt, x_sample, state_gla, state_rglru, state_conv, ln1_g, ln1_b, w_in, gla_w_a2, gla_b_a, gla_norm_g, rg_conv_w, rg_conv_b, rg_w_a, rg_b_a, rg_w_x, rg_b_x, rg_lambda, b_merge, w_branch, w_o, ln2_g, ln2_b, router_w, router_b, moe_w_gu, moe_b_gu, moe_w_down, moe_b_down):
    n_p, seq, d = x_prompt.shape
    n_s, dec_seq, _ = x_sample.shape
    assert dec_seq == 1, "the sample group carries one new token per sequence"
    depth, _, heads, dk, dv = state_gla.shape
    kw, vw = heads * dk, heads * dv
    n_exp = router_w.shape[2]
    alpha = (2.0 * depth) ** 0.25
    t_p = n_p * seq
    pad_rows = BF16_SUBLANES

    xp = x_prompt.reshape(t_p, d)
    xs = x_sample.reshape(n_s, d)
    outs = {k: [] for k in ("gla_p", "rg_p", "cv_p", "gla_s", "rg_s", "cv_s")}
    for layer in range(depth):
        w = _layer_weights(layer, w_in, gla_w_a2, gla_b_a, gla_norm_g, rg_conv_w, rg_conv_b, rg_w_a, rg_b_a,
                           rg_w_x, rg_b_x, rg_lambda, b_merge, w_branch, w_o, ln1_g, ln1_b, ln2_g, ln2_b,
                           router_w, router_b, moe_w_gu, moe_b_gu, moe_w_down, moe_b_down, kw, vw)
        qkvg, loga, xr, gate = _proj(xp, w, PROJ_ROWS)
        o_p, s_p = _gla(qkvg, loga, w["ng"], None, n_p, SEQ_ROWS, GLA_CHUNK, heads, dk, dv)
        h_p, hl_p, cv_p = _rglru_seq(xr, w, n_p, SEQ_ROWS)
        x1_p, idx_p, wt_p = _post(xp, o_p, h_p, gate, w, PROJ_ROWS, alpha, n_exp)
        qkvg_s, loga_s, xr_s, gate_s = _proj(xs, w, n_s)
        pad = lambda a: jnp.pad(a[:, None, :], ((0, 0), (0, pad_rows - 1), (0, 0))).reshape(n_s * pad_rows, -1)
        o_s, s_s = _gla(pad(qkvg_s), pad(loga_s), w["ng"], state_gla[layer], n_s, pad_rows, pad_rows,
                        heads, dk, dv)
        o_s = o_s.reshape(n_s, pad_rows, vw)[:, 0]
        h_s, hn_s, cv_s = _rglru_step(xr_s, jnp.swapaxes(state_conv[layer], 0, 1), state_rglru[layer], w)
        x1_s, idx_s, wt_s = _post(xs, o_s, h_s, gate_s, w, n_s, alpha, n_exp)
        x1 = jnp.concatenate([x1_p, x1_s], axis=0)
        idx = jnp.concatenate([idx_p, idx_s], axis=0)[:, :TOP_K]
        wt = jnp.concatenate([wt_p, wt_s], axis=0)
        x2 = _moe(x1, idx, wt, w, alpha, n_s)
        xp, xs = x2[:t_p], x2[t_p:]
        outs["gla_p"].append(s_p)
        outs["rg_p"].append(hl_p.reshape(n_p, -1))
        outs["cv_p"].append(cv_p)
        outs["gla_s"].append(s_s)
        outs["rg_s"].append(hn_s)
        outs["cv_s"].append(jnp.swapaxes(cv_s, 0, 1))
    return (xp.reshape(n_p, seq, d), xs.reshape(n_s, dec_seq, d),
            jnp.stack(outs["gla_p"]), jnp.stack(outs["rg_p"]), jnp.stack(outs["cv_p"]),
            jnp.stack(outs["gla_s"]), jnp.stack(outs["rg_s"]), jnp.stack(outs["cv_s"]))
```

```python
import functools

import jax
import jax.numpy as jnp
from jax import lax
from jax.experimental import pallas as pl
from jax.experimental.pallas import tpu as pltpu

F32 = jnp.float32
BF16 = jnp.bfloat16

TOP_K = 4
GLA_GATE_TAU = 16.0
GLA_CHUNK = 64
RG_C = 8.0
SWIGLU_LIMIT = 7.0
SWIGLU_ALPHA = 1.702
LN_EPS = 1e-5

LANES = 128
BF16_SUBLANES = 16
VMEM_LIMIT = 56 * 1024 * 1024

PROJ_ROWS = 256
SEQ_ROWS = 256
MOE_ROWS = 256
FFN_COLS = 512


def _params(*sem):
    return pltpu.CompilerParams(dimension_semantics=sem, vmem_limit_bytes=VMEM_LIMIT)


def _const_spec(shape):
    nd = len(shape)
    return pl.BlockSpec(shape, lambda *_: (0,) * nd)


def _bdot(a, b):
    return jnp.dot(a, b, preferred_element_type=F32)


def _split3(x):
    hi = x.astype(BF16)
    r1 = x - hi.astype(F32)
    mid = r1.astype(BF16)
    lo = (r1 - mid.astype(F32)).astype(BF16)
    return hi, mid, lo


def _sigmoid(x):
    return 1.0 / (1.0 + jnp.exp(-x))


def _log_sigmoid(x):
    return jnp.minimum(x, 0.0) - jnp.log1p(jnp.exp(-jnp.abs(x)))


def _softplus(x):
    return jnp.maximum(x, 0.0) + jnp.log1p(jnp.exp(-jnp.abs(x)))


def _layernorm(x, g, b):
    mu = jnp.mean(x, axis=-1, keepdims=True)
    xc = x - mu
    var = jnp.mean(xc * xc, axis=-1, keepdims=True)
    return xc * lax.rsqrt(var + LN_EPS) * g + b


def _proj_kernel(x_ref, wq_ref, wal_ref, wa2_ref, ba_ref, wxr_ref, wmg_ref, bmg_ref,
                 qkvg_ref, loga_ref, xr_ref, gate_ref):
    xb = x_ref[...].astype(BF16)
    d = x_ref.shape[1]
    for c in range(0, qkvg_ref.shape[1], d):
        qkvg_ref[:, c:c + d] = _bdot(xb, wq_ref[:, c:c + d]).astype(BF16)
    a_low = _bdot(xb, wal_ref[...])
    z = _bdot(a_low.astype(BF16), wa2_ref[...]) + ba_ref[...]
    loga_ref[...] = _log_sigmoid(z) * (1.0 / GLA_GATE_TAU)
    xr_ref[...] = _bdot(xb, wxr_ref[...])
    for c in range(0, gate_ref.shape[1], d):
        gate_ref[:, c:c + d] = _sigmoid(_bdot(xb, wmg_ref[:, c:c + d]) + bmg_ref[:, c:c + d]).astype(BF16)


def _proj(x2d, w, tm):
    t, d = x2d.shape
    kw = w["wq"].shape[1]
    nk = w["wa2"].shape[1]
    consts = [w["wq"], w["wal"], w["wa2"], w["ba"], w["wxr"], w["wmg"], w["bmg"]]
    return pl.pallas_call(
        _proj_kernel,
        grid=(t // tm,),
        in_specs=[pl.BlockSpec((tm, d), lambda i: (i, 0))] + [_const_spec(c.shape) for c in consts],
        out_specs=[pl.BlockSpec((tm, kw), lambda i: (i, 0)),
                   pl.BlockSpec((tm, nk), lambda i: (i, 0)),
                   pl.BlockSpec((tm, d), lambda i: (i, 0)),
                   pl.BlockSpec((tm, 2 * d), lambda i: (i, 0))],
        out_shape=[jax.ShapeDtypeStruct((t, kw), BF16),
                   jax.ShapeDtypeStruct((t, nk), F32),
                   jax.ShapeDtypeStruct((t, d), F32),
                   jax.ShapeDtypeStruct((t, 2 * d), BF16)],
        compiler_params=_params("arbitrary"),
        name="proj",
    )(x2d, *consts)


def _gla_kernel(*refs, chunk, n_chunks, heads, has_state):
    if has_state:
        q_ref, k_ref, v_ref, la_ref, g_ref, ng_ref, s0_ref, o_ref, so_ref, s_scr = refs
    else:
        q_ref, k_ref, v_ref, la_ref, g_ref, ng_ref, o_ref, so_ref, s_scr = refs
    t = pl.program_id(1)
    dk = q_ref.shape[1] // heads
    dv = v_ref.shape[1] // heads
    scale = dk ** -0.5

    @pl.when(t == 0)
    def _():
        if has_state:
            s_scr[...] = s0_ref[0]
        else:
            s_scr[...] = jnp.zeros_like(s_scr)

    row = lax.broadcasted_iota(jnp.int32, (chunk, chunk), 0)
    col = lax.broadcasted_iota(jnp.int32, (chunk, chunk), 1)
    causal = row >= col
    tril = jnp.where(causal, 1.0, 0.0).astype(BF16)

    for ci in range(n_chunks):
        rows = slice(ci * chunk, (ci + 1) * chunk)
        la_hi, la_mid, la_lo = _split3(la_ref[rows, :])
        b = _bdot(tril, la_hi) + _bdot(tril, la_mid) + _bdot(tril, la_lo)
        b_last = b[chunk - 1:chunk, :]
        q_e = (q_ref[rows, :].astype(F32) * scale * jnp.exp(b)).astype(BF16)
        k = k_ref[rows, :].astype(F32)
        k_e = (k * jnp.exp(-b)).astype(BF16)
        k_d = (k * jnp.exp(b_last - b)).astype(BF16)
        decay = jnp.exp(b_last)
        for h in range(heads):
            ks = slice(h * dk, (h + 1) * dk)
            vs = slice(h * dv, (h + 1) * dv)
            v_h = v_ref[rows, vs]
            s_old = s_scr[h]
            scores = lax.dot_general(q_e[:, ks], k_e[:, ks], (((1,), (1,)), ((), ())),
                                     preferred_element_type=F32)
            scores = jnp.where(causal, scores, 0.0)
            o = _bdot(scores.astype(BF16), v_h) + _bdot(q_e[:, ks], s_old.astype(BF16))
            kv = lax.dot_general(k_d[:, ks], v_h, (((0,), (0,)), ((), ())), preferred_element_type=F32)
            s_scr[h] = jnp.transpose(decay[:, ks]) * s_old + kv
            mu = jnp.mean(o, axis=-1, keepdims=True)
            oc = o - mu
            var = jnp.mean(oc * oc, axis=-1, keepdims=True)
            on = oc * lax.rsqrt(var + LN_EPS) * ng_ref[:, vs]
            g = g_ref[rows, vs].astype(F32)
            o_ref[rows, vs] = (on * (g * _sigmoid(g))).astype(BF16)

    @pl.when(t == pl.num_programs(1) - 1)
    def _():
        so_ref[0] = s_scr[...]


def _gla(qkvg, loga, norm_g, s0, nb, rows, chunk, heads, dk, dv):
    t = qkvg.shape[0]
    tps = t // nb // rows
    kw, vw = heads * dk, heads * dv
    has_state = s0 is not None
    in_specs = [pl.BlockSpec((rows, kw), lambda b, i: (b * tps + i, 0)),
                pl.BlockSpec((rows, kw), lambda b, i: (b * tps + i, 1)),
                pl.BlockSpec((rows, vw), lambda b, i: (b * tps + i, (2 * kw) // vw)),
                pl.BlockSpec((rows, kw), lambda b, i: (b * tps + i, 0)),
                pl.BlockSpec((rows, vw), lambda b, i: (b * tps + i, (2 * kw) // vw + 1)),
                _const_spec(norm_g.shape)]
    args = [qkvg, qkvg, qkvg, loga, qkvg, norm_g]
    if has_state:
        in_specs.append(pl.BlockSpec((1, heads, dk, dv), lambda b, i: (b, 0, 0, 0)))
        args.append(s0)
    return pl.pallas_call(
        functools.partial(_gla_kernel, chunk=chunk, n_chunks=rows // chunk, heads=heads, has_state=has_state),
        grid=(nb, tps),
        in_specs=in_specs,
        out_specs=[pl.BlockSpec((rows, vw), lambda b, i: (b * tps + i, 0)),
                   pl.BlockSpec((1, heads, dk, dv), lambda b, i: (b, 0, 0, 0))],
        out_shape=[jax.ShapeDtypeStruct((t, vw), BF16),
                   jax.ShapeDtypeStruct((nb, heads, dk, dv), F32)],
        scratch_shapes=[pltpu.VMEM((heads, dk, dv), F32)],
        compiler_params=_params("arbitrary", "arbitrary"),
        name="gla",
    )(*args)


def _rg_gates(xc, wax_ref, ba_ref, bx_ref, lam_ref):
    n_blocks, bw, _ = wax_ref.shape
    r_parts, i_parts = [], []
    for n in range(n_blocks):
        cs = slice(n * bw, (n + 1) * bw)
        pre = _bdot(xc[:, cs].astype(BF16), wax_ref[n])
        r_parts.append(_sigmoid(pre[:, :bw] + ba_ref[:, cs]))
        i_parts.append(_sigmoid(pre[:, bw:] + bx_ref[:, cs]))
    r = jnp.concatenate(r_parts, axis=1)
    i = jnp.concatenate(i_parts, axis=1)
    log_a = -RG_C * r * _softplus(-lam_ref[...])
    a = jnp.exp(log_a)
    bx = jnp.sqrt(jnp.tanh(-log_a) * (1.0 + a * a)) * (i * xc)
    return a, bx


def _rglru_seq_kernel(xr_ref, cw_ref, cb_ref, wax_ref, ba_ref, bx_ref, lam_ref,
                      h_ref, hl_ref, cv_ref, cbuf, a_scr, b_scr, h_scr, hc):
    t = pl.program_id(1)
    rows = xr_ref.shape[0]
    taps = cw_ref.shape[0]
    head = 8

    @pl.when(t == 0)
    def _():
        cbuf[0:head, :] = jnp.zeros((head, cbuf.shape[1]), F32)
        hc[...] = jnp.zeros_like(hc)

    cbuf[head:head + rows, :] = xr_ref[...]
    first = head - (taps - 1)
    xc = cb_ref[...] + sum(cbuf[first + j:first + j + rows, :] * cw_ref[j:j + 1, :] for j in range(taps))
    tail = cbuf[first + rows:head + rows, :]
    cbuf[first:head, :] = tail

    a, bx = _rg_gates(xc, wax_ref, ba_ref, bx_ref, lam_ref)
    a_scr[...] = a
    b_scr[...] = bx

    def step(i, h):
        h = a_scr[pl.ds(i, 1), :] * h + b_scr[pl.ds(i, 1), :]
        h_scr[pl.ds(i, 1), :] = h
        return h

    h_last = lax.fori_loop(0, rows, step, hc[...], unroll=8)
    hc[...] = h_last
    h_ref[...] = h_scr[...].astype(BF16)

    @pl.when(t == pl.num_programs(1) - 1)
    def _():
        hl_ref[0] = h_last
        cv_ref[0] = tail


def _rglru_seq(xr, w, nb, rows):
    t, width = xr.shape
    tps = t // nb // rows
    taps = w["cw"].shape[0]
    consts = [w["cw"], w["cb"], w["wax"], w["rba"], w["rbx"], w["lam"]]
    return pl.pallas_call(
        _rglru_seq_kernel,
        grid=(nb, tps),
        in_specs=[pl.BlockSpec((rows, width), lambda b, i: (b * tps + i, 0))] + [_const_spec(c.shape) for c in consts],
        out_specs=[pl.BlockSpec((rows, width), lambda b, i: (b * tps + i, 0)),
                   pl.BlockSpec((1, 1, width), lambda b, i: (b, 0, 0)),
                   pl.BlockSpec((1, taps - 1, width), lambda b, i: (b, 0, 0))],
        out_shape=[jax.ShapeDtypeStruct((t, width), BF16),
                   jax.ShapeDtypeStruct((nb, 1, width), F32),
                   jax.ShapeDtypeStruct((nb, taps - 1, width), F32)],
        scratch_shapes=[pltpu.VMEM((rows + 8, width), F32), pltpu.VMEM((rows, width), F32),
                        pltpu.VMEM((rows, width), F32), pltpu.VMEM((rows, width), F32),
                        pltpu.VMEM((1, width), F32)],
        compiler_params=_params("arbitrary", "arbitrary"),
        name="rglru_seq",
    )(xr, *consts)


def _rglru_step_kernel(xr_ref, sc_ref, h0_ref, cw_ref, cb_ref, wax_ref, ba_ref, bx_ref, lam_ref,
                       h_ref, hn_ref, cv_ref):
    taps = cw_ref.shape[0]
    xr = xr_ref[...]
    xc = cb_ref[...] + sum(sc_ref[j] * cw_ref[j:j + 1, :] for j in range(taps - 1)) + xr * cw_ref[taps - 1:taps, :]
    a, bx = _rg_gates(xc, wax_ref, ba_ref, bx_ref, lam_ref)
    h = a * h0_ref[...] + bx
    h_ref[...] = h.astype(BF16)
    hn_ref[...] = h
    for j in range(taps - 2):
        cv_ref[j] = sc_ref[j + 1]
    cv_ref[taps - 2] = xr


def _rglru_step(xr, conv_t, h0, w):
    n, width = xr.shape
    consts = [w["cw"], w["cb"], w["wax"], w["rba"], w["rbx"], w["lam"]]
    args = [xr, conv_t, h0] + consts
    return pl.pallas_call(
        _rglru_step_kernel,
        grid=(1,),
        in_specs=[_const_spec(a.shape) for a in args],
        out_specs=[_const_spec((n, width)), _const_spec((n, width)), _const_spec(conv_t.shape)],
        out_shape=[jax.ShapeDtypeStruct((n, width), BF16),
                   jax.ShapeDtypeStruct((n, width), F32),
                   jax.ShapeDtypeStruct(conv_t.shape, F32)],
        compiler_params=_params("arbitrary"),
        name="rglru_step",
    )(*args)


def _post_kernel(x_ref, o_ref, h_ref, gate_ref, wb0_ref, wb1_ref, wo_ref, g1_ref, b1_ref, rw_ref, rb_ref,
                 x1_ref, idx_ref, wt_ref, *, alpha, n_experts):
    d = x_ref.shape[1]
    gate = gate_ref[...].astype(F32)
    merged = gate[:, :d] * _bdot(o_ref[...], wb0_ref[...]) + gate[:, d:] * _bdot(h_ref[...], wb1_ref[...])
    mix = _bdot(merged.astype(BF16), wo_ref[...])
    x1 = _layernorm(alpha * x_ref[...] + mix, g1_ref[...], b1_ref[...])
    x1_ref[...] = x1

    xh, xm, xl = _split3(x1)
    wh, wm, wl = _split3(rw_ref[...])
    logits = (_bdot(xl, wh) + _bdot(xh, wl) + _bdot(xm, wm)) + (_bdot(xm, wh) + _bdot(xh, wm)) + _bdot(xh, wh)
    logits = logits + rb_ref[...]
    lane = lax.broadcasted_iota(jnp.int32, logits.shape, 1)
    lane_f = lane.astype(F32)
    neg_inf = jnp.float32(-jnp.inf)
    cur = jnp.where(lane < n_experts, logits, neg_inf)
    vals, idxs = [], []
    for _ in range(TOP_K):
        m = jnp.max(cur, axis=-1, keepdims=True)
        sel = jnp.min(jnp.where(cur == m, lane_f, float(LANES)), axis=-1, keepdims=True)
        vals.append(m)
        idxs.append(sel)
        cur = jnp.where(lane_f == sel, neg_inf, cur)
    exps = [jnp.exp(v - vals[0]) for v in vals]
    total = sum(exps)
    idx_out = jnp.zeros(logits.shape, F32)
    wt_out = jnp.zeros(logits.shape, F32)
    for j in range(TOP_K):
        idx_out = jnp.where(lane == j, idxs[j], idx_out)
        wt_out = jnp.where(lane == j, exps[j] / total, wt_out)
    idx_ref[...] = idx_out.astype(jnp.int32)
    wt_ref[...] = wt_out


def _post(x2d, o, h, gate, w, tm, alpha, n_experts):
    t, d = x2d.shape
    consts = [w["wb0"], w["wb1"], w["wo"], w["ln1g"], w["ln1b"], w["rw"], w["rb"]]
    row = lambda width: pl.BlockSpec((tm, width), lambda i: (i, 0))
    return pl.pallas_call(
        functools.partial(_post_kernel, alpha=alpha, n_experts=n_experts),
        grid=(t // tm,),
        in_specs=[row(d), row(d), row(d), row(2 * d)] + [_const_spec(c.shape) for c in consts],
        out_specs=[row(d), row(LANES), row(LANES)],
        out_shape=[jax.ShapeDtypeStruct((t, d), F32),
                   jax.ShapeDtypeStruct((t, LANES), jnp.int32),
                   jax.ShapeDtypeStruct((t, LANES), F32)],
        compiler_params=_params("arbitrary"),
        name="post",
    )(x2d, o, h, gate, *consts)


def _ffn_kernel(be_ref, nu_ref, rows_ref, x_hbm, wgu_ref, bgu_ref, wd_ref, bd_ref, y_hbm,
                wgu_bf, wd_bf, xbuf, obuf, gsem, ssem, *, dst_bits):
    i = pl.program_id(0)
    bm = xbuf.shape[1]
    f = wd_ref.shape[1]
    n_used = nu_ref[0]
    slot = i % 2
    other = 1 - slot

    def gather_row(block, k, s):
        tok = lax.shift_right_logical(rows_ref[block * bm + k], dst_bits)
        return pltpu.make_async_copy(x_hbm.at[tok], xbuf.at[s, k], gsem.at[s])

    def scatter_row(block, k, s):
        dst = rows_ref[block * bm + k] & ((1 << dst_bits) - 1)
        return pltpu.make_async_copy(obuf.at[s, k], y_hbm.at[dst], ssem.at[s])

    def wait_gather(s):
        pltpu.make_async_copy(x_hbm.at[pl.ds(0, bm)], xbuf.at[s], gsem.at[s]).wait()

    def wait_scatter(s):
        pltpu.make_async_copy(obuf.at[s], y_hbm.at[pl.ds(0, bm)], ssem.at[s]).wait()

    @pl.when(i == 0)
    def _():
        obuf[1] = jnp.zeros(obuf.shape[1:], F32)
        for k in range(bm):
            gather_row(1, k, 0).start()

    new_expert = jnp.logical_or(i == 0, be_ref[i] != be_ref[jnp.maximum(i - 1, 0)])

    @pl.when(new_expert)
    def _():
        wgu_bf[...] = wgu_ref[0].astype(BF16)
        wd_bf[...] = wd_ref[0].astype(BF16)

    @pl.when(i < n_used)
    def _():
        wait_gather(slot)

        @pl.when(i >= 1)
        def _():
            wait_scatter(slot)

        x = xbuf[slot].astype(BF16)
        acc = jnp.zeros((bm, wd_ref.shape[2]), F32)
        n_chunks = f // FFN_COLS
        rows_per_chunk = bm // n_chunks
        for ci in range(n_chunks):
            c = ci * FFN_COLS
            for k in range(ci * rows_per_chunk, (ci + 1) * rows_per_chunk):
                gather_row(i + 2, k, other).start()
                scatter_row(i, k, other).start()
            gate = _bdot(x, wgu_bf[:, c:c + FFN_COLS]) + bgu_ref[0, :, c:c + FFN_COLS]
            up = _bdot(x, wgu_bf[:, f + c:f + c + FFN_COLS]) + bgu_ref[0, :, f + c:f + c + FFN_COLS]
            gate = jnp.minimum(gate, SWIGLU_LIMIT)
            up = jnp.clip(up, -SWIGLU_LIMIT, SWIGLU_LIMIT)
            act = (up + 1.0) * gate * _sigmoid(SWIGLU_ALPHA * gate)
            acc = acc + _bdot(act.astype(BF16), wd_bf[c:c + FFN_COLS, :])
        obuf[slot] = acc + bd_ref[0]

        @pl.when(i == n_used - 1)
        def _():
            for k in range(bm):
                scatter_row(i + 1, k, slot).start()
            wait_scatter(other)
            wait_scatter(slot)
            wait_gather(other)


def _ffn(block_e, n_used, rows, dst_bits, x1, w_gu, b_gu, w_down, b_down, bm):
    n_tok, d = x1.shape
    n_exp, _, f2 = w_gu.shape
    f = f2 // 2
    n_blocks = block_e.shape[0]
    grid_spec = pltpu.PrefetchScalarGridSpec(
        num_scalar_prefetch=3,
        grid=(n_blocks,),
        in_specs=[pl.BlockSpec(memory_space=pl.ANY),
                  pl.BlockSpec((1, d, f2), lambda i, be, nu, asg: (be[i], 0, 0)),
                  pl.BlockSpec((1, 1, f2), lambda i, be, nu, asg: (be[i], 0, 0)),
                  pl.BlockSpec((1, f, d), lambda i, be, nu, asg: (be[i], 0, 0)),
                  pl.BlockSpec((1, 1, d), lambda i, be, nu, asg: (be[i], 0, 0))],
        out_specs=pl.BlockSpec(memory_space=pl.ANY),
        scratch_shapes=[pltpu.VMEM((d, f2), BF16), pltpu.VMEM((f, d), BF16),
                        pltpu.VMEM((2, bm, d), F32), pltpu.VMEM((2, bm, d), F32),
                        pltpu.SemaphoreType.DMA((2,)), pltpu.SemaphoreType.DMA((2,))],
    )
    return pl.pallas_call(
        functools.partial(_ffn_kernel, dst_bits=dst_bits),
        grid_spec=grid_spec,
        out_shape=jax.ShapeDtypeStruct((n_tok * TOP_K + bm, d), F32),
        compiler_params=_params("arbitrary"),
        name="ffn",
    )(block_e, n_used, rows, x1, w_gu, b_gu.reshape(n_exp, 1, f2), w_down, b_down.reshape(n_exp, 1, d))


def _combine_kernel(x_ref, *refs, alpha):
    y_refs, (wt_ref, g_ref, b_ref, o_ref) = refs[:TOP_K], refs[TOP_K:]
    wt = wt_ref[...]
    ffn = sum(wt[:, j:j + 1] * y_refs[j][...] for j in range(TOP_K))
    o_ref[...] = _layernorm(alpha * x_ref[...] + ffn, g_ref[...], b_ref[...])


def _combine(x1, y, wt, g, b, tm, alpha):
    t, d = x1.shape
    planes = [pl.BlockSpec((tm, d), lambda i, j=j: (j * (t // tm) + i, 0)) for j in range(TOP_K)]
    return pl.pallas_call(
        functools.partial(_combine_kernel, alpha=alpha),
        grid=(t // tm,),
        in_specs=[pl.BlockSpec((tm, d), lambda i: (i, 0))] + planes +
                 [pl.BlockSpec((tm, LANES), lambda i: (i, 0)), _const_spec(g.shape), _const_spec(b.shape)],
        out_specs=pl.BlockSpec((tm, d), lambda i: (i, 0)),
        out_shape=jax.ShapeDtypeStruct((t, d), F32),
        compiler_params=_params("arbitrary"),
        name="combine",
    )(x1, *([y] * TOP_K), wt, g, b)


def _route(top_idx, n_experts, bm):
    n_tok = top_idx.shape[0]
    m = n_tok * TOP_K
    dst_bits = (m + bm - 1).bit_length()
    assert (n_tok - 1).bit_length() + dst_bits <= 32, "packed (token, row) word must fit in 32 bits"
    i32 = jnp.int32
    flat_e = top_idx.reshape(-1)
    order = jnp.argsort(flat_e).astype(i32)
    e_sorted = flat_e[order]
    bounds = jnp.searchsorted(e_sorted, jnp.arange(n_experts + 1, dtype=i32), side="left").astype(i32)
    start_sorted, counts = bounds[:-1], bounds[1:] - bounds[:-1]
    padded = (counts + bm - 1) // bm * bm
    ends_pad = jnp.cumsum(padded)
    start_pad = ends_pad - padded
    n_blocks = -(-m // bm) + n_experts
    block_e = jnp.minimum(jnp.searchsorted(ends_pad, jnp.arange(n_blocks, dtype=i32) * bm, side="right"),
                          n_experts - 1).astype(i32)
    k = jnp.arange(bm, dtype=i32)[None, :]
    off = jnp.arange(n_blocks, dtype=i32)[:, None] * bm + k - start_pad[block_e][:, None]
    src = jnp.clip(start_sorted[block_e][:, None] + off, 0, m - 1)
    dump = jnp.broadcast_to(m + k, (n_blocks, bm))
    asg = order[src]
    tok = asg // TOP_K
    word = (tok << dst_bits) | ((asg % TOP_K) * n_tok + tok)
    rows = jnp.where(off < counts[block_e][:, None], word, dump)
    rows = jnp.concatenate([dump[:1], rows, dump[:1]], axis=0).reshape(-1)
    n_used = (ends_pad[-1:] // bm).astype(i32)
    return block_e, n_used, rows, dst_bits


def _moe(x1, top_idx, top_w_pad, w, alpha, tm):
    n_experts = w["w_gu"].shape[0]
    block_e, n_used, rows, dst_bits = _route(top_idx, n_experts, MOE_ROWS)
    y = _ffn(block_e, n_used, rows, dst_bits, x1, w["w_gu"], w["b_gu"], w["w_down"], w["b_down"], MOE_ROWS)
    return _combine(x1, y, top_w_pad, w["ln2g"], w["ln2b"], tm, alpha)


def _layer_weights(layer, w_in, gla_w_a2, gla_b_a, gla_norm_g, rg_conv_w, rg_conv_b, rg_w_a, rg_b_a, rg_w_x,
                   rg_b_x, rg_lambda, b_merge, w_branch, w_o, ln1_g, ln1_b, ln2_g, ln2_b, router_w, router_b,
                   moe_w_gu, moe_b_gu, moe_w_down, moe_b_down, kw, vw):
    d = w_in.shape[1]
    rank = gla_w_a2.shape[1]
    width = rg_conv_w.shape[2]
    n_exp = router_w.shape[2]
    c0 = 2 * kw + 2 * vw
    wi = w_in[layer]
    row = lambda v: v.reshape(1, -1)
    return {
        "wq": wi[:, :c0].astype(BF16),
        "wal": jnp.pad(wi[:, c0:c0 + rank], ((0, 0), (0, LANES - rank))).astype(BF16),
        "wa2": jnp.pad(gla_w_a2[layer], ((0, LANES - rank), (0, 0))).astype(BF16),
        "ba": row(gla_b_a[layer]),
        "wxr": wi[:, c0 + rank:c0 + rank + width].astype(BF16),
        "wmg": wi[:, c0 + rank + width:].astype(BF16),
        "bmg": row(b_merge[layer]),
        "ng": row(gla_norm_g[layer]),
        "cw": rg_conv_w[layer], "cb": row(rg_conv_b[layer]),
        "wax": jnp.concatenate([rg_w_a[layer], rg_w_x[layer]], axis=-1).astype(BF16),
        "rba": row(rg_b_a[layer]), "rbx": row(rg_b_x[layer]), "lam": row(rg_lambda[layer]),
        "wb0": w_branch[layer, 0].astype(BF16), "wb1": w_branch[layer, 1].astype(BF16),
        "wo": w_o[layer].astype(BF16),
        "ln1g": row(ln1_g[layer]), "ln1b": row(ln1_b[layer]),
        "ln2g": row(ln2_g[layer]), "ln2b": row(ln2_b[layer]),
        "rw": jnp.pad(router_w[layer], ((0, 0), (0, LANES - n_exp))),
        "rb": jnp.pad(row(router_b[layer]), ((0, 0), (0, LANES - n_exp))),
        "w_gu": moe_w_gu[layer], "b_gu": moe_b_gu[layer],
        "w_down": moe_w_down[layer], "b_down": moe_b_down[layer],
    }


def kernel(x_prompt, x_sample, state_gla, state_rglru, state_conv, ln1_g, ln1_b, w_in, gla_w_a2, gla_b_a, gla_norm_g, rg_conv_w, rg_conv_b, rg_w_a, rg_b_a, rg_w_x, rg_b_x, rg_lambda, b_merge, w_branch, w_o, ln2_g, ln2_b, router_w, router_b, moe_w_gu, moe_b_gu, moe_w_down, moe_b_down):
    n_p, seq, d = x_prompt.shape
    n_s, dec_seq, _ = x_sample.shape
    assert dec_seq == 1, "the sample group carries one new token per sequence"
    depth, _, heads, dk, dv = state_gla.shape
    kw, vw = heads * dk, heads * dv
    n_exp = router_w.shape[2]
    alpha = (2.0 * depth) ** 0.25
    t_p = n_p * seq
    pad_rows = BF16_SUBLANES

    xp = x_prompt.reshape(t_p, d)
    xs = x_sample.reshape(n_s, d)
    outs = {k: [] for k in ("gla_p", "rg_p", "cv_p", "gla_s", "rg_s", "cv_s")}
    for layer in range(depth):
        w = _layer_weights(layer, w_in, gla_w_a2, gla_b_a, gla_norm_g, rg_conv_w, rg_conv_b, rg_w_a, rg_b_a,
                           rg_w_x, rg_b_x, rg_lambda, b_merge, w_branch, w_o, ln1_g, ln1_b, ln2_g, ln2_b,
                           router_w, router_b, moe_w_gu, moe_b_gu, moe_w_down, moe_b_down, kw, vw)
        qkvg, loga, xr, gate = _proj(xp, w, PROJ_ROWS)
        o_p, s_p = _gla(qkvg, loga, w["ng"], None, n_p, SEQ_ROWS, GLA_CHUNK, heads, dk, dv)
        h_p, hl_p, cv_p = _rglru_seq(xr, w, n_p, SEQ_ROWS)
        x1_p, idx_p, wt_p = _post(xp, o_p, h_p, gate, w, PROJ_ROWS, alpha, n_exp)
        qkvg_s, loga_s, xr_s, gate_s = _proj(xs, w, n_s)
        pad = lambda a: jnp.pad(a[:, None, :], ((0, 0), (0, pad_rows - 1), (0, 0))).reshape(n_s * pad_rows, -1)
        o_s, s_s = _gla(pad(qkvg_s), pad(loga_s), w["ng"], state_gla[layer], n_s, pad_rows, pad_rows,
                        heads, dk, dv)
        o_s = o_s.reshape(n_s, pad_rows, vw)[:, 0]
        h_s, hn_s, cv_s = _rglru_step(xr_s, jnp.swapaxes(state_conv[layer], 0, 1), state_rglru[layer], w)
        x1_s, idx_s, wt_s = _post(xs, o_s, h_s, gate_s, w, n_s, alpha, n_exp)
        x1 = jnp.concatenate([x1_p, x1_s], axis=0)
        idx = jnp.concatenate([idx_p, idx_s], axis=0)[:, :TOP_K]
        wt = jnp.concatenate([wt_p, wt_s], axis=0)
        x2 = _moe(x1, idx, wt, w, alpha, n_s)
        xp, xs = x2[:t_p], x2[t_p:]
        outs["gla_p"].append(s_p)
        outs["rg_p"].append(hl_p.reshape(n_p, -1))
        outs["cv_p"].append(cv_p)
        outs["gla_s"].append(s_s)
        outs["rg_s"].append(hn_s)
        outs["cv_s"].append(jnp.swapaxes(cv_s, 0, 1))
    return (xp.reshape(n_p, seq, d), xs.reshape(n_s, dec_seq, d),
            jnp.stack(outs["gla_p"]), jnp.stack(outs["rg_p"]), jnp.stack(outs["cv_p"]),
            jnp.stack(outs["gla_s"]), jnp.stack(outs["rg_s"]), jnp.stack(outs["cv_s"]))
```

```python
import functools

import jax
import jax.numpy as jnp
from jax import lax
from jax.experimental import pallas as pl
from jax.experimental.pallas import tpu as pltpu

F32 = jnp.float32
BF16 = jnp.bfloat16

TOP_K = 4
GLA_GATE_TAU = 16.0
GLA_CHUNK = 64
RG_C = 8.0
SWIGLU_LIMIT = 7.0
SWIGLU_ALPHA = 1.702
LN_EPS = 1e-5

LANES = 128
BF16_SUBLANES = 16
VMEM_LIMIT = 56 * 1024 * 1024

PROJ_ROWS = 256
SEQ_ROWS = 256
MOE_ROWS = 256
FFN_COLS = 512


def _params(*sem):
    return pltpu.CompilerParams(dimension_semantics=sem, vmem_limit_bytes=VMEM_LIMIT)


def _const_spec(shape):
    nd = len(shape)
    return pl.BlockSpec(shape, lambda *_: (0,) * nd)


def _bdot(a, b):
    return jnp.dot(a, b, preferred_element_type=F32)


def _split3(x):
    hi = x.astype(BF16)
    r1 = x - hi.astype(F32)
    mid = r1.astype(BF16)
    lo = (r1 - mid.astype(F32)).astype(BF16)
    return hi, mid, lo


def _sigmoid(x):
    return 1.0 / (1.0 + jnp.exp(-x))


def _log_sigmoid(x):
    return jnp.minimum(x, 0.0) - jnp.log1p(jnp.exp(-jnp.abs(x)))


def _softplus(x):
    return jnp.maximum(x, 0.0) + jnp.log1p(jnp.exp(-jnp.abs(x)))


def _load_row_tiled(ref, n, lead=()):
    chunks = ref.shape[-2] // n
    return jnp.concatenate([ref[(*lead, pl.ds(c, n, stride=chunks), slice(None))] for c in range(chunks)], axis=1)


def _store_row_tiled(ref, x, lead=()):
    n, d = x.shape
    chunks = d // LANES
    for c in range(chunks):
        ref[(*lead, pl.ds(c, n, stride=chunks), slice(None))] = x[:, c * LANES:(c + 1) * LANES]


def _layernorm(x, g, b):
    mu = jnp.mean(x, axis=-1, keepdims=True)
    xc = x - mu
    var = jnp.mean(xc * xc, axis=-1, keepdims=True)
    return xc * lax.rsqrt(var + LN_EPS) * g + b


def _proj_kernel(x_ref, wq_ref, wal_ref, wa2_ref, ba_ref, wxr_ref, wmg_ref, bmg_ref,
                 qkvg_ref, loga_ref, xr_ref, gate_ref):
    xb = x_ref[...].astype(BF16)
    d = x_ref.shape[1]
    for c in range(0, qkvg_ref.shape[1], d):
        qkvg_ref[:, c:c + d] = _bdot(xb, wq_ref[:, c:c + d]).astype(BF16)
    a_low = _bdot(xb, wal_ref[...])
    z = _bdot(a_low.astype(BF16), wa2_ref[...]) + ba_ref[...]
    loga_ref[...] = _log_sigmoid(z) * (1.0 / GLA_GATE_TAU)
    xr_ref[...] = _bdot(xb, wxr_ref[...])
    for c in range(0, gate_ref.shape[1], d):
        gate_ref[:, c:c + d] = _sigmoid(_bdot(xb, wmg_ref[:, c:c + d]) + bmg_ref[:, c:c + d]).astype(BF16)


def _proj(x2d, row0, t, w, tm):
    d = x2d.shape[1]
    kw = w["wq"].shape[1]
    nk = w["wa2"].shape[1]
    consts = [w["wq"], w["wal"], w["wa2"], w["ba"], w["wxr"], w["wmg"], w["bmg"]]
    return pl.pallas_call(
        _proj_kernel,
        grid=(t // tm,),
        in_specs=[pl.BlockSpec((tm, d), lambda i: (row0 // tm + i, 0))] + [_const_spec(c.shape) for c in consts],
        out_specs=[pl.BlockSpec((tm, kw), lambda i: (i, 0)),
                   pl.BlockSpec((tm, nk), lambda i: (i, 0)),
                   pl.BlockSpec((tm, d), lambda i: (i, 0)),
                   pl.BlockSpec((tm, 2 * d), lambda i: (i, 0))],
        out_shape=[jax.ShapeDtypeStruct((t, kw), BF16),
                   jax.ShapeDtypeStruct((t, nk), F32),
                   jax.ShapeDtypeStruct((t, d), F32),
                   jax.ShapeDtypeStruct((t, 2 * d), BF16)],
        compiler_params=_params("arbitrary"),
        name="proj",
    )(x2d, *consts)


def _gla_kernel(*refs, chunk, n_chunks, heads, has_state):
    q_ref, k_ref, v_ref, la_ref, g_ref, ng_ref = refs[:6]
    s0_ref = refs[6] if has_state else None
    o_ref, so_ref, s_scr = refs[-3:]
    t = pl.program_id(1)
    dk = q_ref.shape[1] // heads
    dv = v_ref.shape[1] // heads
    scale = dk ** -0.5

    @pl.when(t == 0)
    def _():
        if has_state:
            s_scr[...] = s0_ref[0, 0]
        else:
            s_scr[...] = jnp.zeros_like(s_scr)

    row = lax.broadcasted_iota(jnp.int32, (chunk, chunk), 0)
    col = lax.broadcasted_iota(jnp.int32, (chunk, chunk), 1)
    causal = row >= col
    tril = jnp.where(causal, 1.0, 0.0).astype(BF16)

    for ci in range(n_chunks):
        rows = slice(ci * chunk, (ci + 1) * chunk)
        la_hi, la_mid, la_lo = _split3(la_ref[rows, :])
        b = _bdot(tril, la_hi) + _bdot(tril, la_mid) + _bdot(tril, la_lo)
        b_last = b[chunk - 1:chunk, :]
        q_e = (q_ref[rows, :].astype(F32) * scale * jnp.exp(b)).astype(BF16)
        k = k_ref[rows, :].astype(F32)
        k_e = (k * jnp.exp(-b)).astype(BF16)
        k_d = (k * jnp.exp(b_last - b)).astype(BF16)
        decay = jnp.exp(b_last)
        for h in range(heads):
            ks = slice(h * dk, (h + 1) * dk)
            vs = slice(h * dv, (h + 1) * dv)
            v_h = v_ref[rows, vs]
            s_old = s_scr[h]
            scores = lax.dot_general(q_e[:, ks], k_e[:, ks], (((1,), (1,)), ((), ())),
                                     preferred_element_type=F32)
            scores = jnp.where(causal, scores, 0.0)
            o = _bdot(scores.astype(BF16), v_h) + _bdot(q_e[:, ks], s_old.astype(BF16))
            kv = lax.dot_general(k_d[:, ks], v_h, (((0,), (0,)), ((), ())), preferred_element_type=F32)
            s_scr[h] = jnp.transpose(decay[:, ks]) * s_old + kv
            mu = jnp.mean(o, axis=-1, keepdims=True)
            oc = o - mu
            var = jnp.mean(oc * oc, axis=-1, keepdims=True)
            on = oc * lax.rsqrt(var + LN_EPS) * ng_ref[:, vs]
            g = g_ref[rows, vs].astype(F32)
            o_ref[rows, vs] = (on * (g * _sigmoid(g))).astype(BF16)

    @pl.when(t == pl.num_programs(1) - 1)
    def _():
        so_ref[0] = s_scr[...]


def _gla(qkvg, loga, norm_g, s0, layer, nb, rows, chunk, heads, dk, dv):
    t = qkvg.shape[0]
    tps = t // nb // rows
    kw, vw = heads * dk, heads * dv
    has_state = s0 is not None
    in_specs = [pl.BlockSpec((rows, kw), lambda b, i: (b * tps + i, 0)),
                pl.BlockSpec((rows, kw), lambda b, i: (b * tps + i, 1)),
                pl.BlockSpec((rows, vw), lambda b, i: (b * tps + i, (2 * kw) // vw)),
                pl.BlockSpec((rows, kw), lambda b, i: (b * tps + i, 0)),
                pl.BlockSpec((rows, vw), lambda b, i: (b * tps + i, (2 * kw) // vw + 1)),
                _const_spec(norm_g.shape)]
    args = [qkvg, qkvg, qkvg, loga, qkvg, norm_g]
    if has_state:
        in_specs.append(pl.BlockSpec((1, 1, heads, dk, dv), lambda b, i: (layer, b, 0, 0, 0)))
        args.append(s0)
    return pl.pallas_call(
        functools.partial(_gla_kernel, chunk=chunk, n_chunks=rows // chunk, heads=heads, has_state=has_state),
        grid=(nb, tps),
        in_specs=in_specs,
        out_specs=[pl.BlockSpec((rows, vw), lambda b, i: (b * tps + i, 0)),
                   pl.BlockSpec((1, heads, dk, dv), lambda b, i: (b, 0, 0, 0))],
        out_shape=[jax.ShapeDtypeStruct((t, vw), BF16),
                   jax.ShapeDtypeStruct((nb, heads, dk, dv), F32)],
        scratch_shapes=[pltpu.VMEM((heads, dk, dv), F32)],
        compiler_params=_params("arbitrary", "arbitrary"),
        name="gla",
    )(*args)


def _rg_gates(xc, wax_ref, ba_ref, bx_ref, lam_ref):
    n_blocks, bw, _ = wax_ref.shape
    r_parts, i_parts = [], []
    for n in range(n_blocks):
        cs = slice(n * bw, (n + 1) * bw)
        pre = _bdot(xc[:, cs].astype(BF16), wax_ref[n])
        r_parts.append(_sigmoid(pre[:, :bw] + ba_ref[:, cs]))
        i_parts.append(_sigmoid(pre[:, bw:] + bx_ref[:, cs]))
    r = jnp.concatenate(r_parts, axis=1)
    i = jnp.concatenate(i_parts, axis=1)
    log_a = -RG_C * r * _softplus(-lam_ref[...])
    a = jnp.exp(log_a)
    bx = jnp.sqrt(jnp.tanh(-log_a) * (1.0 + a * a)) * (i * xc)
    return a, bx


def _rglru_seq_kernel(xr_ref, cw_ref, cb_ref, wax_ref, ba_ref, bx_ref, lam_ref,
                      h_ref, hl_ref, cv_ref, cbuf, a_scr, b_scr, h_scr, hc):
    t = pl.program_id(1)
    rows = xr_ref.shape[0]
    taps = cw_ref.shape[0]
    head = 8

    @pl.when(t == 0)
    def _():
        cbuf[0:head, :] = jnp.zeros((head, cbuf.shape[1]), F32)
        hc[...] = jnp.zeros_like(hc)

    cbuf[head:head + rows, :] = xr_ref[...]
    first = head - (taps - 1)
    xc = cb_ref[...] + sum(cbuf[first + j:first + j + rows, :] * cw_ref[j:j + 1, :] for j in range(taps))
    tail = cbuf[first + rows:head + rows, :]
    cbuf[first:head, :] = tail

    a, bx = _rg_gates(xc, wax_ref, ba_ref, bx_ref, lam_ref)
    a_scr[...] = a
    b_scr[...] = bx

    def step(i, h):
        h = a_scr[pl.ds(i, 1), :] * h + b_scr[pl.ds(i, 1), :]
        h_scr[pl.ds(i, 1), :] = h
        return h

    h_last = lax.fori_loop(0, rows, step, hc[...], unroll=8)
    hc[...] = h_last
    h_ref[...] = h_scr[...].astype(BF16)

    @pl.when(t == pl.num_programs(1) - 1)
    def _():
        hl_ref[0] = h_last
        cv_ref[0] = tail


def _rglru_seq(xr, w, nb, rows):
    t, width = xr.shape
    tps = t // nb // rows
    taps = w["cw"].shape[0]
    consts = [w["cw"], w["cb"], w["wax"], w["rba"], w["rbx"], w["lam"]]
    return pl.pallas_call(
        _rglru_seq_kernel,
        grid=(nb, tps),
        in_specs=[pl.BlockSpec((rows, width), lambda b, i: (b * tps + i, 0))] + [_const_spec(c.shape) for c in consts],
        out_specs=[pl.BlockSpec((rows, width), lambda b, i: (b * tps + i, 0)),
                   pl.BlockSpec((1, 1, width), lambda b, i: (b, 0, 0)),
                   pl.BlockSpec((1, taps - 1, width), lambda b, i: (b, 0, 0))],
        out_shape=[jax.ShapeDtypeStruct((t, width), BF16),
                   jax.ShapeDtypeStruct((nb, 1, width), F32),
                   jax.ShapeDtypeStruct((nb, taps - 1, width), F32)],
        scratch_shapes=[pltpu.VMEM((rows + 8, width), F32), pltpu.VMEM((rows, width), F32),
                        pltpu.VMEM((rows, width), F32), pltpu.VMEM((rows, width), F32),
                        pltpu.VMEM((1, width), F32)],
        compiler_params=_params("arbitrary", "arbitrary"),
        name="rglru_seq",
    )(xr, *consts)


def _rglru_step_kernel(xr_ref, sc_ref, h0_ref, cw_ref, cb_ref, wax_ref, ba_ref, bx_ref, lam_ref,
                       h_ref, hn_ref, cv_ref):
    taps = cw_ref.shape[0]
    xr = xr_ref[...]
    xc = cb_ref[...] + sum(sc_ref[j] * cw_ref[j:j + 1, :] for j in range(taps - 1)) + xr * cw_ref[taps - 1:taps, :]
    a, bx = _rg_gates(xc, wax_ref, ba_ref, bx_ref, lam_ref)
    h = a * h0_ref[...] + bx
    h_ref[...] = h.astype(BF16)
    hn_ref[...] = h
    for j in range(taps - 2):
        cv_ref[j] = sc_ref[j + 1]
    cv_ref[taps - 2] = xr


def _rglru_step(xr, conv_t, h0, w):
    n, width = xr.shape
    consts = [w["cw"], w["cb"], w["wax"], w["rba"], w["rbx"], w["lam"]]
    args = [xr, conv_t, h0] + consts
    return pl.pallas_call(
        _rglru_step_kernel,
        grid=(1,),
        in_specs=[_const_spec(a.shape) for a in args],
        out_specs=[_const_spec((n, width)), _const_spec((n, width)), _const_spec(conv_t.shape)],
        out_shape=[jax.ShapeDtypeStruct((n, width), BF16),
                   jax.ShapeDtypeStruct((n, width), F32),
                   jax.ShapeDtypeStruct(conv_t.shape, F32)],
        compiler_params=_params("arbitrary"),
        name="rglru_step",
    )(*args)


def _post_kernel(x_ref, o_ref, h_ref, gate_ref, wb0_ref, wb1_ref, wo_ref, g1_ref, b1_ref, rw_ref, rb_ref,
                 x1_ref, idx_ref, wt_ref, *, alpha, n_experts):
    d = x_ref.shape[1]
    gate = gate_ref[...].astype(F32)
    merged = gate[:, :d] * _bdot(o_ref[...], wb0_ref[...]) + gate[:, d:] * _bdot(h_ref[...], wb1_ref[...])
    mix = _bdot(merged.astype(BF16), wo_ref[...])
    x1 = _layernorm(alpha * x_ref[...] + mix, g1_ref[...], b1_ref[...])
    _store_row_tiled(x1_ref, x1)

    xh, xm, xl = _split3(x1)
    wh, wm, wl = _split3(rw_ref[...])
    logits = (_bdot(xl, wh) + _bdot(xh, wl) + _bdot(xm, wm)) + (_bdot(xm, wh) + _bdot(xh, wm)) + _bdot(xh, wh)
    logits = logits + rb_ref[...]
    lane = lax.broadcasted_iota(jnp.int32, logits.shape, 1)
    lane_f = lane.astype(F32)
    neg_inf = jnp.float32(-jnp.inf)
    cur = jnp.where(lane < n_experts, logits, neg_inf)
    vals, idxs = [], []
    for _ in range(TOP_K):
        m = jnp.max(cur, axis=-1, keepdims=True)
        sel = jnp.min(jnp.where(cur == m, lane_f, float(LANES)), axis=-1, keepdims=True)
        vals.append(m)
        idxs.append(sel)
        cur = jnp.where(lane_f == sel, neg_inf, cur)
    exps = [jnp.exp(v - vals[0]) for v in vals]
    total = sum(exps)
    idx_out = jnp.zeros(logits.shape, F32)
    wt_out = jnp.zeros(logits.shape, F32)
    for j in range(TOP_K):
        idx_out = jnp.where(lane == j, idxs[j], idx_out)
        wt_out = jnp.where(lane == j, exps[j] / total, wt_out)
    idx_ref[...] = idx_out.astype(jnp.int32)
    wt_ref[...] = wt_out


def _post(x2d, row0, o, h, gate, w, tm, alpha, n_experts):
    t, d = o.shape
    consts = [w["wb0"], w["wb1"], w["wo"], w["ln1g"], w["ln1b"], w["rw"], w["rb"]]
    row = lambda width: pl.BlockSpec((tm, width), lambda i: (i, 0))
    return pl.pallas_call(
        functools.partial(_post_kernel, alpha=alpha, n_experts=n_experts),
        grid=(t // tm,),
        in_specs=[pl.BlockSpec((tm, d), lambda i: (row0 // tm + i, 0)), row(d), row(d), row(2 * d)] +
                 [_const_spec(c.shape) for c in consts],
        out_specs=[pl.BlockSpec((tm * d // LANES, LANES), lambda i: (i, 0)), row(LANES), row(LANES)],
        out_shape=[jax.ShapeDtypeStruct((t * d // LANES, LANES), F32),
                   jax.ShapeDtypeStruct((t, LANES), jnp.int32),
                   jax.ShapeDtypeStruct((t, LANES), F32)],
        compiler_params=_params("arbitrary"),
        name="post",
    )(x2d, o, h, gate, *consts)


def _ffn_kernel(be_ref, nu_ref, rows_ref, x_hbm, wgu_ref, bgu_ref, wd_ref, bd_ref, y_hbm,
                wgu_bf, wd_bf, xbuf, obuf, gsem, ssem, *, dst_bits, bm):
    i = pl.program_id(0)
    rt = xbuf.shape[1] // bm
    f = wd_ref.shape[2]
    n_used = nu_ref[0]
    slot = i % 2
    other = 1 - slot

    def gather_row(block, k, s):
        tok = lax.shift_right_logical(rows_ref[block * bm + k], dst_bits)
        src = x_hbm.at[pl.ds(pl.multiple_of(tok * rt, rt), rt)]
        return pltpu.make_async_copy(src, xbuf.at[s, pl.ds(k * rt, rt)], gsem.at[s])

    def scatter_row(block, k, s):
        row = rows_ref[block * bm + k] & ((1 << dst_bits) - 1)
        dst = y_hbm.at[pl.ds(pl.multiple_of(row * rt, rt), rt)]
        return pltpu.make_async_copy(obuf.at[s, pl.ds(k * rt, rt)], dst, ssem.at[s])

    def wait_gather(s):
        pltpu.make_async_copy(x_hbm.at[pl.ds(0, bm * rt)], xbuf.at[s], gsem.at[s]).wait()

    def wait_scatter(s):
        pltpu.make_async_copy(obuf.at[s], y_hbm.at[pl.ds(0, bm * rt)], ssem.at[s]).wait()

    @pl.when(i == 0)
    def _():
        obuf[1] = jnp.zeros(obuf.shape[1:], F32)
        for k in range(bm):
            gather_row(1, k, 0).start()

    new_expert = jnp.logical_or(i == 0, be_ref[i] != be_ref[jnp.maximum(i - 1, 0)])

    @pl.when(new_expert)
    def _():
        wgu_bf[...] = wgu_ref[0, 0].astype(BF16)
        wd_bf[...] = wd_ref[0, 0].astype(BF16)

    @pl.when(i < n_used)
    def _():
        wait_gather(slot)

        @pl.when(i >= 1)
        def _():
            wait_scatter(slot)

        x = _load_row_tiled(xbuf, bm, lead=(slot,)).astype(BF16)
        acc = jnp.zeros((bm, wd_ref.shape[3]), F32)
        n_chunks = f // FFN_COLS
        rows_per_chunk = bm // n_chunks
        for ci in range(n_chunks):
            c = ci * FFN_COLS
            for k in range(ci * rows_per_chunk, (ci + 1) * rows_per_chunk):
                gather_row(i + 2, k, other).start()
                scatter_row(i, k, other).start()
            gate = _bdot(x, wgu_bf[:, c:c + FFN_COLS]) + bgu_ref[0, 0, :, c:c + FFN_COLS]
            up = _bdot(x, wgu_bf[:, f + c:f + c + FFN_COLS]) + bgu_ref[0, 0, :, f + c:f + c + FFN_COLS]
            gate = jnp.minimum(gate, SWIGLU_LIMIT)
            up = jnp.clip(up, -SWIGLU_LIMIT, SWIGLU_LIMIT)
            act = (up + 1.0) * gate * _sigmoid(SWIGLU_ALPHA * gate)
            acc = acc + _bdot(act.astype(BF16), wd_bf[c:c + FFN_COLS, :])
        _store_row_tiled(obuf, acc + bd_ref[0, 0], lead=(slot,))

        @pl.when(i == n_used - 1)
        def _():
            for k in range(bm):
                scatter_row(i + 1, k, slot).start()
            wait_scatter(other)
            wait_scatter(slot)
            wait_gather(other)


def _ffn(block_e, n_used, rows, dst_bits, x1, layer, w_gu, b_gu, w_down, b_down, bm):
    n_layers, n_exp, d, f2 = w_gu.shape
    rt = d // LANES
    n_tok = x1.shape[0] // rt
    f = f2 // 2
    n_blocks = block_e.shape[0]
    expert = lambda i, be, nu, rows: (layer, be[i], 0, 0)
    grid_spec = pltpu.PrefetchScalarGridSpec(
        num_scalar_prefetch=3,
        grid=(n_blocks,),
        in_specs=[pl.BlockSpec(memory_space=pl.ANY),
                  pl.BlockSpec((1, 1, d, f2), expert),
                  pl.BlockSpec((1, 1, 1, f2), expert),
                  pl.BlockSpec((1, 1, f, d), expert),
                  pl.BlockSpec((1, 1, 1, d), expert)],
        out_specs=pl.BlockSpec(memory_space=pl.ANY),
        scratch_shapes=[pltpu.VMEM((d, f2), BF16), pltpu.VMEM((f, d), BF16),
                        pltpu.VMEM((2, bm * rt, LANES), F32), pltpu.VMEM((2, bm * rt, LANES), F32),
                        pltpu.SemaphoreType.DMA((2,)), pltpu.SemaphoreType.DMA((2,))],
    )
    return pl.pallas_call(
        functools.partial(_ffn_kernel, dst_bits=dst_bits, bm=bm),
        grid_spec=grid_spec,
        out_shape=jax.ShapeDtypeStruct(((n_tok * TOP_K + bm) * rt, LANES), F32),
        compiler_params=_params("arbitrary"),
        name="ffn",
    )(block_e, n_used, rows, x1, w_gu, b_gu.reshape(n_layers, n_exp, 1, f2),
      w_down, b_down.reshape(n_layers, n_exp, 1, d))


def _combine_kernel(x_ref, *refs, alpha):
    y_refs, (wt_ref, g_ref, b_ref, o_ref) = refs[:TOP_K], refs[TOP_K:]
    wt = wt_ref[...]
    tm = wt.shape[0]
    ffn = sum(wt[:, j:j + 1] * _load_row_tiled(y_refs[j], tm) for j in range(TOP_K))
    o_ref[...] = _layernorm(alpha * _load_row_tiled(x_ref, tm) + ffn, g_ref[...], b_ref[...])


def _combine(x1, y, wt, g, b, tm, alpha):
    d = g.shape[1]
    rt = d // LANES
    t = x1.shape[0] // rt
    planes = [pl.BlockSpec((tm * rt, LANES), lambda i, j=j: (j * (t // tm) + i, 0)) for j in range(TOP_K)]
    return pl.pallas_call(
        functools.partial(_combine_kernel, alpha=alpha),
        grid=(t // tm,),
        in_specs=[pl.BlockSpec((tm * rt, LANES), lambda i: (i, 0))] + planes +
                 [pl.BlockSpec((tm, LANES), lambda i: (i, 0)), _const_spec(g.shape), _const_spec(b.shape)],
        out_specs=pl.BlockSpec((tm, d), lambda i: (i, 0)),
        out_shape=jax.ShapeDtypeStruct((t, d), F32),
        compiler_params=_params("arbitrary"),
        name="combine",
    )(x1, *([y] * TOP_K), wt, g, b)


def _route(top_idx, n_experts, bm):
    n_tok = top_idx.shape[0]
    m = n_tok * TOP_K
    dst_bits = (m + bm - 1).bit_length()
    assert (n_tok - 1).bit_length() + dst_bits <= 32, "packed (token, row) word must fit in 32 bits"
    i32 = jnp.int32
    flat_e = top_idx.reshape(-1)
    order = jnp.argsort(flat_e).astype(i32)
    e_sorted = flat_e[order]
    bounds = jnp.searchsorted(e_sorted, jnp.arange(n_experts + 1, dtype=i32), side="left").astype(i32)
    start_sorted, counts = bounds[:-1], bounds[1:] - bounds[:-1]
    padded = (counts + bm - 1) // bm * bm
    ends_pad = jnp.cumsum(padded)
    start_pad = ends_pad - padded
    n_blocks = -(-m // bm) + n_experts
    block_e = jnp.minimum(jnp.searchsorted(ends_pad, jnp.arange(n_blocks, dtype=i32) * bm, side="right"),
                          n_experts - 1).astype(i32)
    k = jnp.arange(bm, dtype=i32)[None, :]
    off = jnp.arange(n_blocks, dtype=i32)[:, None] * bm + k - start_pad[block_e][:, None]
    src = jnp.clip(start_sorted[block_e][:, None] + off, 0, m - 1)
    dump = jnp.broadcast_to(m + k, (n_blocks, bm))
    asg = order[src]
    tok = asg // TOP_K
    word = (tok << dst_bits) | ((asg % TOP_K) * n_tok + tok)
    rows = jnp.where(off < counts[block_e][:, None], word, dump)
    rows = jnp.concatenate([dump[:1], rows, dump[:1]], axis=0).reshape(-1)
    n_used = (ends_pad[-1:] // bm).astype(i32)
    return block_e, n_used, rows, dst_bits


def _moe(x1, top_idx, top_w_pad, w, layer, moe_weights, alpha, tm):
    n_experts = moe_weights[0].shape[1]
    block_e, n_used, rows, dst_bits = _route(top_idx, n_experts, MOE_ROWS)
    y = _ffn(block_e, n_used, rows, dst_bits, x1, layer, *moe_weights, MOE_ROWS)
    return _combine(x1, y, top_w_pad, w["ln2g"], w["ln2b"], tm, alpha)


def _layer_weights(layer, w_in, gla_w_a2, gla_b_a, gla_norm_g, rg_conv_w, rg_conv_b, rg_w_a, rg_b_a, rg_w_x,
                   rg_b_x, rg_lambda, b_merge, w_branch, w_o, ln1_g, ln1_b, ln2_g, ln2_b, router_w, router_b,
                   kw, vw):
    d = w_in.shape[1]
    rank = gla_w_a2.shape[1]
    width = rg_conv_w.shape[2]
    n_exp = router_w.shape[2]
    c0 = 2 * kw + 2 * vw
    wi = w_in[layer]
    row = lambda v: v.reshape(1, -1)
    return {
        "wq": wi[:, :c0].astype(BF16),
        "wal": jnp.pad(wi[:, c0:c0 + rank], ((0, 0), (0, LANES - rank))).astype(BF16),
        "wa2": jnp.pad(gla_w_a2[layer], ((0, LANES - rank), (0, 0))).astype(BF16),
        "ba": row(gla_b_a[layer]),
        "wxr": wi[:, c0 + rank:c0 + rank + width].astype(BF16),
        "wmg": wi[:, c0 + rank + width:].astype(BF16),
        "bmg": row(b_merge[layer]),
        "ng": row(gla_norm_g[layer]),
        "cw": rg_conv_w[layer], "cb": row(rg_conv_b[layer]),
        "wax": jnp.concatenate([rg_w_a[layer], rg_w_x[layer]], axis=-1).astype(BF16),
        "rba": row(rg_b_a[layer]), "rbx": row(rg_b_x[layer]), "lam": row(rg_lambda[layer]),
        "wb0": w_branch[layer, 0].astype(BF16), "wb1": w_branch[layer, 1].astype(BF16),
        "wo": w_o[layer].astype(BF16),
        "ln1g": row(ln1_g[layer]), "ln1b": row(ln1_b[layer]),
        "ln2g": row(ln2_g[layer]), "ln2b": row(ln2_b[layer]),
        "rw": jnp.pad(router_w[layer], ((0, 0), (0, LANES - n_exp))),
        "rb": jnp.pad(row(router_b[layer]), ((0, 0), (0, LANES - n_exp))),
    }


def kernel(x_prompt, x_sample, state_gla, state_rglru, state_conv, ln1_g, ln1_b, w_in, gla_w_a2, gla_b_a, gla_norm_g, rg_conv_w, rg_conv_b, rg_w_a, rg_b_a, rg_w_x, rg_b_x, rg_lambda, b_merge, w_branch, w_o, ln2_g, ln2_b, router_w, router_b, moe_w_gu, moe_b_gu, moe_w_down, moe_b_down):
    n_p, seq, d = x_prompt.shape
    n_s, dec_seq, _ = x_sample.shape
    assert dec_seq == 1, "the sample group carries one new token per sequence"
    depth, _, heads, dk, dv = state_gla.shape
    kw, vw = heads * dk, heads * dv
    n_exp = router_w.shape[2]
    alpha = (2.0 * depth) ** 0.25
    t_p = n_p * seq
    pad_rows = BF16_SUBLANES

    xp, xp0 = x_prompt.reshape(t_p, d), 0
    xs, xs0 = x_sample.reshape(n_s, d), 0
    moe_weights = (moe_w_gu, moe_b_gu, moe_w_down, moe_b_down)
    outs = {k: [] for k in ("gla_p", "rg_p", "cv_p", "gla_s", "rg_s", "cv_s")}
    for layer in range(depth):
        w = _layer_weights(layer, w_in, gla_w_a2, gla_b_a, gla_norm_g, rg_conv_w, rg_conv_b, rg_w_a, rg_b_a,
                           rg_w_x, rg_b_x, rg_lambda, b_merge, w_branch, w_o, ln1_g, ln1_b, ln2_g, ln2_b,
                           router_w, router_b, kw, vw)
        qkvg, loga, xr, gate = _proj(xp, xp0, t_p, w, PROJ_ROWS)
        o_p, s_p = _gla(qkvg, loga, w["ng"], None, layer, n_p, SEQ_ROWS, GLA_CHUNK, heads, dk, dv)
        h_p, hl_p, cv_p = _rglru_seq(xr, w, n_p, SEQ_ROWS)
        x1_p, idx_p, wt_p = _post(xp, xp0, o_p, h_p, gate, w, PROJ_ROWS, alpha, n_exp)
        qkvg_s, loga_s, xr_s, gate_s = _proj(xs, xs0, n_s, w, n_s)
        pad = lambda a: jnp.pad(a[:, None, :], ((0, 0), (0, pad_rows - 1), (0, 0))).reshape(n_s * pad_rows, -1)
        o_s, s_s = _gla(pad(qkvg_s), pad(loga_s), w["ng"], state_gla, layer, n_s, pad_rows, pad_rows,
                        heads, dk, dv)
        o_s = o_s.reshape(n_s, pad_rows, vw)[:, 0]
        h_s, hn_s, cv_s = _rglru_step(xr_s, jnp.swapaxes(state_conv[layer], 0, 1), state_rglru[layer], w)
        x1_s, idx_s, wt_s = _post(xs, xs0, o_s, h_s, gate_s, w, n_s, alpha, n_exp)
        x1 = jnp.concatenate([x1_p, x1_s], axis=0)
        idx = jnp.concatenate([idx_p, idx_s], axis=0)[:, :TOP_K]
        wt = jnp.concatenate([wt_p, wt_s], axis=0)
        x2 = _moe(x1, idx, wt, w, layer, moe_weights, alpha, n_s)
        xp, xp0, xs, xs0 = x2, 0, x2, t_p
        outs["gla_p"].append(s_p)
        outs["rg_p"].append(hl_p.reshape(n_p, -1))
        outs["cv_p"].append(cv_p)
        outs["gla_s"].append(s_s)
        outs["rg_s"].append(hn_s)
        outs["cv_s"].append(jnp.swapaxes(cv_s, 0, 1))
    return (xp[:t_p].reshape(n_p, seq, d), xs[t_p:].reshape(n_s, dec_seq, d),
            jnp.stack(outs["gla_p"]), jnp.stack(outs["rg_p"]), jnp.stack(outs["cv_p"]),
            jnp.stack(outs["gla_s"]), jnp.stack(outs["rg_s"]), jnp.stack(outs["cv_s"]))
```

```python
import functools

import jax
import jax.numpy as jnp
from jax import lax
from jax.experimental import pallas as pl
from jax.experimental.pallas import tpu as pltpu
from jax.experimental.pallas import tpu_sc as plsc

F32 = jnp.float32
BF16 = jnp.bfloat16

TOP_K = 4
GLA_GATE_TAU = 16.0
GLA_CHUNK = 64
RG_C = 8.0
SWIGLU_LIMIT = 7.0
SWIGLU_ALPHA = 1.702
LN_EPS = 1e-5

LANES = 128
BF16_SUBLANES = 16
VMEM_LIMIT = 56 * 1024 * 1024

PROJ_ROWS = 256
SEQ_ROWS = 256
MOE_ROWS = 256
FFN_COLS = 512


def _params(*sem):
    return pltpu.CompilerParams(dimension_semantics=sem, vmem_limit_bytes=VMEM_LIMIT)


def _const_spec(shape):
    nd = len(shape)
    return pl.BlockSpec(shape, lambda *_: (0,) * nd)


def _bdot(a, b):
    return jnp.dot(a, b, preferred_element_type=F32)


def _split3(x):
    hi = x.astype(BF16)
    r1 = x - hi.astype(F32)
    mid = r1.astype(BF16)
    lo = (r1 - mid.astype(F32)).astype(BF16)
    return hi, mid, lo


def _sigmoid(x):
    return 1.0 / (1.0 + jnp.exp(-x))


def _log_sigmoid(x):
    return jnp.minimum(x, 0.0) - jnp.log1p(jnp.exp(-jnp.abs(x)))


def _softplus(x):
    return jnp.maximum(x, 0.0) + jnp.log1p(jnp.exp(-jnp.abs(x)))


def _layernorm(x, g, b):
    mu = jnp.mean(x, axis=-1, keepdims=True)
    xc = x - mu
    var = jnp.mean(xc * xc, axis=-1, keepdims=True)
    return xc * lax.rsqrt(var + LN_EPS) * g + b


def _proj_kernel(x_ref, wq_ref, wal_ref, wa2_ref, ba_ref, wxr_ref, wmg_ref, bmg_ref,
                 qkvg_ref, loga_ref, xr_ref, gate_ref):
    xb = x_ref[...].astype(BF16)
    d = x_ref.shape[1]
    for c in range(0, qkvg_ref.shape[1], d):
        qkvg_ref[:, c:c + d] = _bdot(xb, wq_ref[:, c:c + d]).astype(BF16)
    a_low = _bdot(xb, wal_ref[...])
    z = _bdot(a_low.astype(BF16), wa2_ref[...]) + ba_ref[...]
    loga_ref[...] = _log_sigmoid(z) * (1.0 / GLA_GATE_TAU)
    xr_ref[...] = _bdot(xb, wxr_ref[...])
    for c in range(0, gate_ref.shape[1], d):
        gate_ref[:, c:c + d] = _sigmoid(_bdot(xb, wmg_ref[:, c:c + d]) + bmg_ref[:, c:c + d]).astype(BF16)


def _proj(x2d, row0, t, w, tm):
    d = x2d.shape[1]
    kw = w["wq"].shape[1]
    nk = w["wa2"].shape[1]
    consts = [w["wq"], w["wal"], w["wa2"], w["ba"], w["wxr"], w["wmg"], w["bmg"]]
    return pl.pallas_call(
        _proj_kernel,
        grid=(t // tm,),
        in_specs=[pl.BlockSpec((tm, d), lambda i: (row0 // tm + i, 0))] + [_const_spec(c.shape) for c in consts],
        out_specs=[pl.BlockSpec((tm, kw), lambda i: (i, 0)),
                   pl.BlockSpec((tm, nk), lambda i: (i, 0)),
                   pl.BlockSpec((tm, d), lambda i: (i, 0)),
                   pl.BlockSpec((tm, 2 * d), lambda i: (i, 0))],
        out_shape=[jax.ShapeDtypeStruct((t, kw), BF16),
                   jax.ShapeDtypeStruct((t, nk), F32),
                   jax.ShapeDtypeStruct((t, d), F32),
                   jax.ShapeDtypeStruct((t, 2 * d), BF16)],
        compiler_params=_params("arbitrary"),
        name="proj",
    )(x2d, *consts)


def _gla_kernel(*refs, chunk, n_chunks, heads, has_state):
    q_ref, k_ref, v_ref, la_ref, g_ref, ng_ref = refs[:6]
    s0_ref = refs[6] if has_state else None
    o_ref, so_ref, s_scr = refs[-3:]
    t = pl.program_id(1)
    dk = q_ref.shape[1] // heads
    dv = v_ref.shape[1] // heads
    scale = dk ** -0.5

    @pl.when(t == 0)
    def _():
        if has_state:
            s_scr[...] = s0_ref[0, 0]
        else:
            s_scr[...] = jnp.zeros_like(s_scr)

    row = lax.broadcasted_iota(jnp.int32, (chunk, chunk), 0)
    col = lax.broadcasted_iota(jnp.int32, (chunk, chunk), 1)
    causal = row >= col
    tril = jnp.where(causal, 1.0, 0.0).astype(BF16)

    for ci in range(n_chunks):
        rows = slice(ci * chunk, (ci + 1) * chunk)
        la_hi, la_mid, la_lo = _split3(la_ref[rows, :])
        b = _bdot(tril, la_hi) + _bdot(tril, la_mid) + _bdot(tril, la_lo)
        b_last = b[chunk - 1:chunk, :]
        q_e = (q_ref[rows, :].astype(F32) * scale * jnp.exp(b)).astype(BF16)
        k = k_ref[rows, :].astype(F32)
        k_e = (k * jnp.exp(-b)).astype(BF16)
        k_d = (k * jnp.exp(b_last - b)).astype(BF16)
        decay = jnp.exp(b_last)
        for h in range(heads):
            ks = slice(h * dk, (h + 1) * dk)
            vs = slice(h * dv, (h + 1) * dv)
            v_h = v_ref[rows, vs]
            s_old = s_scr[h]
            scores = lax.dot_general(q_e[:, ks], k_e[:, ks], (((1,), (1,)), ((), ())),
                                     preferred_element_type=F32)
            scores = jnp.where(causal, scores, 0.0)
            o = _bdot(scores.astype(BF16), v_h) + _bdot(q_e[:, ks], s_old.astype(BF16))
            kv = lax.dot_general(k_d[:, ks], v_h, (((0,), (0,)), ((), ())), preferred_element_type=F32)
            s_scr[h] = jnp.transpose(decay[:, ks]) * s_old + kv
            mu = jnp.mean(o, axis=-1, keepdims=True)
            oc = o - mu
            var = jnp.mean(oc * oc, axis=-1, keepdims=True)
            on = oc * lax.rsqrt(var + LN_EPS) * ng_ref[:, vs]
            g = g_ref[rows, vs].astype(F32)
            o_ref[rows, vs] = (on * (g * _sigmoid(g))).astype(BF16)

    @pl.when(t == pl.num_programs(1) - 1)
    def _():
        so_ref[0] = s_scr[...]


def _gla(qkvg, loga, norm_g, s0, layer, nb, rows, chunk, heads, dk, dv):
    t = qkvg.shape[0]
    tps = t // nb // rows
    kw, vw = heads * dk, heads * dv
    has_state = s0 is not None
    in_specs = [pl.BlockSpec((rows, kw), lambda b, i: (b * tps + i, 0)),
                pl.BlockSpec((rows, kw), lambda b, i: (b * tps + i, 1)),
                pl.BlockSpec((rows, vw), lambda b, i: (b * tps + i, (2 * kw) // vw)),
                pl.BlockSpec((rows, kw), lambda b, i: (b * tps + i, 0)),
                pl.BlockSpec((rows, vw), lambda b, i: (b * tps + i, (2 * kw) // vw + 1)),
                _const_spec(norm_g.shape)]
    args = [qkvg, qkvg, qkvg, loga, qkvg, norm_g]
    if has_state:
        in_specs.append(pl.BlockSpec((1, 1, heads, dk, dv), lambda b, i: (layer, b, 0, 0, 0)))
        args.append(s0)
    return pl.pallas_call(
        functools.partial(_gla_kernel, chunk=chunk, n_chunks=rows // chunk, heads=heads, has_state=has_state),
        grid=(nb, tps),
        in_specs=in_specs,
        out_specs=[pl.BlockSpec((rows, vw), lambda b, i: (b * tps + i, 0)),
                   pl.BlockSpec((1, heads, dk, dv), lambda b, i: (b, 0, 0, 0))],
        out_shape=[jax.ShapeDtypeStruct((t, vw), BF16),
                   jax.ShapeDtypeStruct((nb, heads, dk, dv), F32)],
        scratch_shapes=[pltpu.VMEM((heads, dk, dv), F32)],
        compiler_params=_params("arbitrary", "arbitrary"),
        name="gla",
    )(*args)


def _rg_gates(xc, wax_ref, ba_ref, bx_ref, lam_ref):
    n_blocks, bw, _ = wax_ref.shape
    r_parts, i_parts = [], []
    for n in range(n_blocks):
        cs = slice(n * bw, (n + 1) * bw)
        pre = _bdot(xc[:, cs].astype(BF16), wax_ref[n])
        r_parts.append(_sigmoid(pre[:, :bw] + ba_ref[:, cs]))
        i_parts.append(_sigmoid(pre[:, bw:] + bx_ref[:, cs]))
    r = jnp.concatenate(r_parts, axis=1)
    i = jnp.concatenate(i_parts, axis=1)
    log_a = -RG_C * r * _softplus(-lam_ref[...])
    a = jnp.exp(log_a)
    bx = jnp.sqrt(jnp.tanh(-log_a) * (1.0 + a * a)) * (i * xc)
    return a, bx


def _rglru_seq_kernel(xr_ref, cw_ref, cb_ref, wax_ref, ba_ref, bx_ref, lam_ref,
                      h_ref, hl_ref, cv_ref, cbuf, a_scr, b_scr, h_scr, hc):
    t = pl.program_id(1)
    rows = xr_ref.shape[0]
    taps = cw_ref.shape[0]
    head = 8

    @pl.when(t == 0)
    def _():
        cbuf[0:head, :] = jnp.zeros((head, cbuf.shape[1]), F32)
        hc[...] = jnp.zeros_like(hc)

    cbuf[head:head + rows, :] = xr_ref[...]
    first = head - (taps - 1)
    xc = cb_ref[...] + sum(cbuf[first + j:first + j + rows, :] * cw_ref[j:j + 1, :] for j in range(taps))
    tail = cbuf[first + rows:head + rows, :]
    cbuf[first:head, :] = tail

    a, bx = _rg_gates(xc, wax_ref, ba_ref, bx_ref, lam_ref)
    a_scr[...] = a
    b_scr[...] = bx

    def step(i, h):
        h = a_scr[pl.ds(i, 1), :] * h + b_scr[pl.ds(i, 1), :]
        h_scr[pl.ds(i, 1), :] = h
        return h

    h_last = lax.fori_loop(0, rows, step, hc[...], unroll=8)
    hc[...] = h_last
    h_ref[...] = h_scr[...].astype(BF16)

    @pl.when(t == pl.num_programs(1) - 1)
    def _():
        hl_ref[0] = h_last
        cv_ref[0] = tail


def _rglru_seq(xr, w, nb, rows):
    t, width = xr.shape
    tps = t // nb // rows
    taps = w["cw"].shape[0]
    consts = [w["cw"], w["cb"], w["wax"], w["rba"], w["rbx"], w["lam"]]
    return pl.pallas_call(
        _rglru_seq_kernel,
        grid=(nb, tps),
        in_specs=[pl.BlockSpec((rows, width), lambda b, i: (b * tps + i, 0))] + [_const_spec(c.shape) for c in consts],
        out_specs=[pl.BlockSpec((rows, width), lambda b, i: (b * tps + i, 0)),
                   pl.BlockSpec((1, 1, width), lambda b, i: (b, 0, 0)),
                   pl.BlockSpec((1, taps - 1, width), lambda b, i: (b, 0, 0))],
        out_shape=[jax.ShapeDtypeStruct((t, width), BF16),
                   jax.ShapeDtypeStruct((nb, 1, width), F32),
                   jax.ShapeDtypeStruct((nb, taps - 1, width), F32)],
        scratch_shapes=[pltpu.VMEM((rows + 8, width), F32), pltpu.VMEM((rows, width), F32),
                        pltpu.VMEM((rows, width), F32), pltpu.VMEM((rows, width), F32),
                        pltpu.VMEM((1, width), F32)],
        compiler_params=_params("arbitrary", "arbitrary"),
        name="rglru_seq",
    )(xr, *consts)


def _rglru_step_kernel(xr_ref, sc_ref, h0_ref, cw_ref, cb_ref, wax_ref, ba_ref, bx_ref, lam_ref,
                       h_ref, hn_ref, cv_ref):
    taps = cw_ref.shape[0]
    xr = xr_ref[...]
    xc = cb_ref[...] + sum(sc_ref[j] * cw_ref[j:j + 1, :] for j in range(taps - 1)) + xr * cw_ref[taps - 1:taps, :]
    a, bx = _rg_gates(xc, wax_ref, ba_ref, bx_ref, lam_ref)
    h = a * h0_ref[...] + bx
    h_ref[...] = h.astype(BF16)
    hn_ref[...] = h
    for j in range(taps - 2):
        cv_ref[j] = sc_ref[j + 1]
    cv_ref[taps - 2] = xr


def _rglru_step(xr, conv_t, h0, w):
    n, width = xr.shape
    consts = [w["cw"], w["cb"], w["wax"], w["rba"], w["rbx"], w["lam"]]
    args = [xr, conv_t, h0] + consts
    return pl.pallas_call(
        _rglru_step_kernel,
        grid=(1,),
        in_specs=[_const_spec(a.shape) for a in args],
        out_specs=[_const_spec((n, width)), _const_spec((n, width)), _const_spec(conv_t.shape)],
        out_shape=[jax.ShapeDtypeStruct((n, width), BF16),
                   jax.ShapeDtypeStruct((n, width), F32),
                   jax.ShapeDtypeStruct(conv_t.shape, F32)],
        compiler_params=_params("arbitrary"),
        name="rglru_step",
    )(*args)


def _post_kernel(x_ref, o_ref, h_ref, gate_ref, wb0_ref, wb1_ref, wo_ref, g1_ref, b1_ref, rw_ref, rb_ref,
                 x1_ref, idx_ref, wt_ref, *, alpha, n_experts):
    d = x_ref.shape[1]
    gate = gate_ref[...].astype(F32)
    merged = gate[:, :d] * _bdot(o_ref[...], wb0_ref[...]) + gate[:, d:] * _bdot(h_ref[...], wb1_ref[...])
    mix = _bdot(merged.astype(BF16), wo_ref[...])
    x1 = _layernorm(alpha * x_ref[...] + mix, g1_ref[...], b1_ref[...])
    x1_ref[...] = x1

    xh, xm, xl = _split3(x1)
    wh, wm, wl = _split3(rw_ref[...])
    logits = (_bdot(xl, wh) + _bdot(xh, wl) + _bdot(xm, wm)) + (_bdot(xm, wh) + _bdot(xh, wm)) + _bdot(xh, wh)
    logits = logits + rb_ref[...]
    lane = lax.broadcasted_iota(jnp.int32, logits.shape, 1)
    lane_f = lane.astype(F32)
    neg_inf = jnp.float32(-jnp.inf)
    cur = jnp.where(lane < n_experts, logits, neg_inf)
    vals, idxs = [], []
    for _ in range(TOP_K):
        m = jnp.max(cur, axis=-1, keepdims=True)
        sel = jnp.min(jnp.where(cur == m, lane_f, float(LANES)), axis=-1, keepdims=True)
        vals.append(m)
        idxs.append(sel)
        cur = jnp.where(lane_f == sel, neg_inf, cur)
    exps = [jnp.exp(v - vals[0]) for v in vals]
    total = sum(exps)
    idx_out = jnp.zeros(logits.shape, F32)
    wt_out = jnp.zeros(logits.shape, F32)
    for j in range(TOP_K):
        idx_out = jnp.where(lane == j, idxs[j], idx_out)
        wt_out = jnp.where(lane == j, exps[j] / total, wt_out)
    idx_ref[...] = idx_out.astype(jnp.int32)
    wt_ref[...] = wt_out


def _post(x2d, row0, o, h, gate, w, tm, alpha, n_experts):
    t, d = o.shape
    consts = [w["wb0"], w["wb1"], w["wo"], w["ln1g"], w["ln1b"], w["rw"], w["rb"]]
    row = lambda width: pl.BlockSpec((tm, width), lambda i: (i, 0))
    return pl.pallas_call(
        functools.partial(_post_kernel, alpha=alpha, n_experts=n_experts),
        grid=(t // tm,),
        in_specs=[pl.BlockSpec((tm, d), lambda i: (row0 // tm + i, 0)), row(d), row(d), row(2 * d)] +
                 [_const_spec(c.shape) for c in consts],
        out_specs=[row(d), row(LANES), row(LANES)],
        out_shape=[jax.ShapeDtypeStruct((t, d), F32),
                   jax.ShapeDtypeStruct((t, LANES), jnp.int32),
                   jax.ShapeDtypeStruct((t, LANES), F32)],
        compiler_params=_params("arbitrary"),
        name="post",
    )(x2d, o, h, gate, *consts)


SC_GATHER_ROWS = 32


def _sc_gather(table, idx):
    n_rows = idx.shape[0]
    d = table.shape[1]
    info = plsc.get_sparse_core_info()
    n_workers = info.num_cores * info.num_subcores
    per_worker = n_rows // n_workers
    n_chunks = per_worker // SC_GATHER_ROWS
    assert n_chunks * SC_GATHER_ROWS * n_workers == n_rows, "pad the index list to whole chunks per subcore"
    mesh = plsc.VectorSubcoreMesh(core_axis_name="c", subcore_axis_name="s")

    @functools.partial(
        pl.kernel, mesh=mesh, out_type=jax.ShapeDtypeStruct((n_rows, d), table.dtype),
        scratch_types=[pltpu.VMEM((SC_GATHER_ROWS,), jnp.int32), pltpu.VMEM((SC_GATHER_ROWS, d), table.dtype),
                       pltpu.SemaphoreType.DMA])
    def gather(table_hbm, idx_hbm, out_hbm, idx_v, rows_v, sem):
        worker = lax.axis_index("s") * info.num_cores + lax.axis_index("c")
        base = worker * per_worker

        @pl.loop(0, n_chunks)
        def _(c):
            off = pl.multiple_of(base + c * SC_GATHER_ROWS, SC_GATHER_ROWS)
            pltpu.sync_copy(idx_hbm.at[pl.ds(off, SC_GATHER_ROWS)], idx_v)
            pltpu.async_copy(table_hbm.at[idx_v], rows_v, sem).wait()
            pltpu.sync_copy(rows_v, out_hbm.at[pl.ds(off, SC_GATHER_ROWS)])

    return gather(table, idx)


def _ffn_kernel(be_ref, nu_ref, x_ref, wgu_ref, bgu_ref, wd_ref, bd_ref, o_ref, wgu_bf, wd_bf):
    i = pl.program_id(0)
    f = wd_ref.shape[2]
    new_expert = jnp.logical_or(i == 0, be_ref[i] != be_ref[jnp.maximum(i - 1, 0)])

    @pl.when(new_expert)
    def _():
        wgu_bf[...] = wgu_ref[0, 0].astype(BF16)
        wd_bf[...] = wd_ref[0, 0].astype(BF16)

    @pl.when(i < nu_ref[0])
    def _():
        x = x_ref[...].astype(BF16)
        acc = jnp.zeros(o_ref.shape, F32)
        for c in range(0, f, FFN_COLS):
            gate = _bdot(x, wgu_bf[:, c:c + FFN_COLS]) + bgu_ref[0, 0, :, c:c + FFN_COLS]
            up = _bdot(x, wgu_bf[:, f + c:f + c + FFN_COLS]) + bgu_ref[0, 0, :, f + c:f + c + FFN_COLS]
            gate = jnp.minimum(gate, SWIGLU_LIMIT)
            up = jnp.clip(up, -SWIGLU_LIMIT, SWIGLU_LIMIT)
            act = (up + 1.0) * gate * _sigmoid(SWIGLU_ALPHA * gate)
            acc = acc + _bdot(act.astype(BF16), wd_bf[c:c + FFN_COLS, :])
        o_ref[...] = acc + bd_ref[0, 0]

    @pl.when(i >= nu_ref[0])
    def _():
        o_ref[...] = jnp.zeros_like(o_ref)


def _ffn(block_e, n_used, xs, layer, w_gu, b_gu, w_down, b_down, bm):
    n_layers, n_exp, d, f2 = w_gu.shape
    f = f2 // 2
    n_blocks = block_e.shape[0]
    expert = lambda i, be, nu: (layer, be[i], 0, 0)
    grid_spec = pltpu.PrefetchScalarGridSpec(
        num_scalar_prefetch=2,
        grid=(n_blocks,),
        in_specs=[pl.BlockSpec((bm, d), lambda i, be, nu: (i, 0)),
                  pl.BlockSpec((1, 1, d, f2), expert),
                  pl.BlockSpec((1, 1, 1, f2), expert),
                  pl.BlockSpec((1, 1, f, d), expert),
                  pl.BlockSpec((1, 1, 1, d), expert)],
        out_specs=pl.BlockSpec((bm, d), lambda i, be, nu: (i, 0)),
        scratch_shapes=[pltpu.VMEM((d, f2), BF16), pltpu.VMEM((f, d), BF16)],
    )
    return pl.pallas_call(
        _ffn_kernel,
        grid_spec=grid_spec,
        out_shape=jax.ShapeDtypeStruct((n_blocks * bm, d), F32),
        compiler_params=_params("arbitrary"),
        name="ffn",
    )(block_e, n_used, xs, w_gu, b_gu.reshape(n_layers, n_exp, 1, f2),
      w_down, b_down.reshape(n_layers, n_exp, 1, d))


def _combine_kernel(x_ref, *refs, alpha):
    y_refs, (wt_ref, g_ref, b_ref, o_ref) = refs[:TOP_K], refs[TOP_K:]
    wt = wt_ref[...]
    ffn = sum(wt[:, j:j + 1] * y_refs[j][...] for j in range(TOP_K))
    o_ref[...] = _layernorm(alpha * x_ref[...] + ffn, g_ref[...], b_ref[...])


def _combine(x1, y, wt, g, b, tm, alpha):
    t, d = x1.shape
    planes = [pl.BlockSpec((tm, d), lambda i, j=j: (j * (t // tm) + i, 0)) for j in range(TOP_K)]
    return pl.pallas_call(
        functools.partial(_combine_kernel, alpha=alpha),
        grid=(t // tm,),
        in_specs=[pl.BlockSpec((tm, d), lambda i: (i, 0))] + planes +
                 [pl.BlockSpec((tm, LANES), lambda i: (i, 0)), _const_spec(g.shape), _const_spec(b.shape)],
        out_specs=pl.BlockSpec((tm, d), lambda i: (i, 0)),
        out_shape=jax.ShapeDtypeStruct((t, d), F32),
        compiler_params=_params("arbitrary"),
        name="combine",
    )(x1, *([y] * TOP_K), wt, g, b)


def _pad_to(v, multiple):
    return jnp.pad(v, (0, -v.shape[0] % multiple))


def _route(top_idx, n_experts, bm):
    n_tok = top_idx.shape[0]
    m = n_tok * TOP_K
    i32 = jnp.int32
    flat_e = top_idx.reshape(-1)
    order = jnp.argsort(flat_e).astype(i32)
    e_sorted = flat_e[order]
    bounds = jnp.searchsorted(e_sorted, jnp.arange(n_experts + 1, dtype=i32), side="left").astype(i32)
    start_sorted, counts = bounds[:-1], bounds[1:] - bounds[:-1]
    padded = (counts + bm - 1) // bm * bm
    ends_pad = jnp.cumsum(padded)
    start_pad = ends_pad - padded
    n_blocks = -(-m // bm) + n_experts
    block_e = jnp.minimum(jnp.searchsorted(ends_pad, jnp.arange(n_blocks, dtype=i32) * bm, side="right"),
                          n_experts - 1).astype(i32)
    k = jnp.arange(bm, dtype=i32)[None, :]
    off = jnp.arange(n_blocks, dtype=i32)[:, None] * bm + k - start_pad[block_e][:, None]
    src = jnp.clip(start_sorted[block_e][:, None] + off, 0, m - 1)
    row_tok = jnp.where(off < counts[block_e][:, None], order[src] // TOP_K, 0).reshape(-1)
    dest = (start_pad[e_sorted] + jnp.arange(m, dtype=i32) - start_sorted[e_sorted]).astype(i32)
    pos = jnp.zeros((m,), i32).at[order].set(dest)
    pos_planes = pos.reshape(n_tok, TOP_K).T.reshape(-1)
    n_used = (ends_pad[-1:] // bm).astype(i32)
    return block_e, n_used, row_tok, pos_planes


def _moe(x1, top_idx, top_w_pad, w, layer, moe_weights, alpha, tm):
    n_experts = moe_weights[0].shape[1]
    info = plsc.get_sparse_core_info()
    chunk = info.num_cores * info.num_subcores * SC_GATHER_ROWS
    block_e, n_used, row_tok, pos_planes = _route(top_idx, n_experts, MOE_ROWS)
    xs = _sc_gather(x1, _pad_to(row_tok, chunk))
    out_rows = _ffn(block_e, n_used, xs, layer, *moe_weights, MOE_ROWS)
    y = _sc_gather(out_rows, _pad_to(pos_planes, chunk))
    return _combine(x1, y, top_w_pad, w["ln2g"], w["ln2b"], tm, alpha)


def _layer_weights(layer, w_in, gla_w_a2, gla_b_a, gla_norm_g, rg_conv_w, rg_conv_b, rg_w_a, rg_b_a, rg_w_x,
                   rg_b_x, rg_lambda, b_merge, w_branch, w_o, ln1_g, ln1_b, ln2_g, ln2_b, router_w, router_b,
                   kw, vw):
    d = w_in.shape[1]
    rank = gla_w_a2.shape[1]
    width = rg_conv_w.shape[2]
    n_exp = router_w.shape[2]
    c0 = 2 * kw + 2 * vw
    wi = w_in[layer]
    row = lambda v: v.reshape(1, -1)
    return {
        "wq": wi[:, :c0].astype(BF16),
        "wal": jnp.pad(wi[:, c0:c0 + rank], ((0, 0), (0, LANES - rank))).astype(BF16),
        "wa2": jnp.pad(gla_w_a2[layer], ((0, LANES - rank), (0, 0))).astype(BF16),
        "ba": row(gla_b_a[layer]),
        "wxr": wi[:, c0 + rank:c0 + rank + width].astype(BF16),
        "wmg": wi[:, c0 + rank + width:].astype(BF16),
        "bmg": row(b_merge[layer]),
        "ng": row(gla_norm_g[layer]),
        "cw": rg_conv_w[layer], "cb": row(rg_conv_b[layer]),
        "wax": jnp.concatenate([rg_w_a[layer], rg_w_x[layer]], axis=-1).astype(BF16),
        "rba": row(rg_b_a[layer]), "rbx": row(rg_b_x[layer]), "lam": row(rg_lambda[layer]),
        "wb0": w_branch[layer, 0].astype(BF16), "wb1": w_branch[layer, 1].astype(BF16),
        "wo": w_o[layer].astype(BF16),
        "ln1g": row(ln1_g[layer]), "ln1b": row(ln1_b[layer]),
        "ln2g": row(ln2_g[layer]), "ln2b": row(ln2_b[layer]),
        "rw": jnp.pad(router_w[layer], ((0, 0), (0, LANES - n_exp))),
        "rb": jnp.pad(row(router_b[layer]), ((0, 0), (0, LANES - n_exp))),
    }


def kernel(x_prompt, x_sample, state_gla, state_rglru, state_conv, ln1_g, ln1_b, w_in, gla_w_a2, gla_b_a, gla_norm_g, rg_conv_w, rg_conv_b, rg_w_a, rg_b_a, rg_w_x, rg_b_x, rg_lambda, b_merge, w_branch, w_o, ln2_g, ln2_b, router_w, router_b, moe_w_gu, moe_b_gu, moe_w_down, moe_b_down):
    n_p, seq, d = x_prompt.shape
    n_s, dec_seq, _ = x_sample.shape
    assert dec_seq == 1, "the sample group carries one new token per sequence"
    depth, _, heads, dk, dv = state_gla.shape
    kw, vw = heads * dk, heads * dv
    n_exp = router_w.shape[2]
    alpha = (2.0 * depth) ** 0.25
    t_p = n_p * seq
    pad_rows = BF16_SUBLANES

    xp, xp0 = x_prompt.reshape(t_p, d), 0
    xs, xs0 = x_sample.reshape(n_s, d), 0
    moe_weights = (moe_w_gu, moe_b_gu, moe_w_down, moe_b_down)
    outs = {k: [] for k in ("gla_p", "rg_p", "cv_p", "gla_s", "rg_s", "cv_s")}
    for layer in range(depth):
        w = _layer_weights(layer, w_in, gla_w_a2, gla_b_a, gla_norm_g, rg_conv_w, rg_conv_b, rg_w_a, rg_b_a,
                           rg_w_x, rg_b_x, rg_lambda, b_merge, w_branch, w_o, ln1_g, ln1_b, ln2_g, ln2_b,
                           router_w, router_b, kw, vw)
        qkvg, loga, xr, gate = _proj(xp, xp0, t_p, w, PROJ_ROWS)
        o_p, s_p = _gla(qkvg, loga, w["ng"], None, layer, n_p, SEQ_ROWS, GLA_CHUNK, heads, dk, dv)
        h_p, hl_p, cv_p = _rglru_seq(xr, w, n_p, SEQ_ROWS)
        x1_p, idx_p, wt_p = _post(xp, xp0, o_p, h_p, gate, w, PROJ_ROWS, alpha, n_exp)
        qkvg_s, loga_s, xr_s, gate_s = _proj(xs, xs0, n_s, w, n_s)
        pad = lambda a: jnp.pad(a[:, None, :], ((0, 0), (0, pad_rows - 1), (0, 0))).reshape(n_s * pad_rows, -1)
        o_s, s_s = _gla(pad(qkvg_s), pad(loga_s), w["ng"], state_gla, layer, n_s, pad_rows, pad_rows,
                        heads, dk, dv)
        o_s = o_s.reshape(n_s, pad_rows, vw)[:, 0]
        h_s, hn_s, cv_s = _rglru_step(xr_s, jnp.swapaxes(state_conv[layer], 0, 1), state_rglru[layer], w)
        x1_s, idx_s, wt_s = _post(xs, xs0, o_s, h_s, gate_s, w, n_s, alpha, n_exp)
        x1 = jnp.concatenate([x1_p, x1_s], axis=0)
        idx = jnp.concatenate([idx_p, idx_s], axis=0)[:, :TOP_K]
        wt = jnp.concatenate([wt_p, wt_s], axis=0)
        x2 = _moe(x1, idx, wt, w, layer, moe_weights, alpha, n_s)
        xp, xp0, xs, xs0 = x2, 0, x2, t_p
        outs["gla_p"].append(s_p)
        outs["rg_p"].append(hl_p.reshape(n_p, -1))
        outs["cv_p"].append(cv_p)
        outs["gla_s"].append(s_s)
        outs["rg_s"].append(hn_s)
        outs["cv_s"].append(jnp.swapaxes(cv_s, 0, 1))
    return (xp[:t_p].reshape(n_p, seq, d), xs[t_p:].reshape(n_s, dec_seq, d),
            jnp.stack(outs["gla_p"]), jnp.stack(outs["rg_p"]), jnp.stack(outs["cv_p"]),
            jnp.stack(outs["gla_s"]), jnp.stack(outs["rg_s"]), jnp.stack(outs["cv_s"]))
```

```python
import functools

import jax
import jax.numpy as jnp
from jax import lax
from jax.experimental import pallas as pl
from jax.experimental.pallas import tpu as pltpu
from jax.experimental.pallas import tpu_sc as plsc

F32 = jnp.float32
BF16 = jnp.bfloat16

TOP_K = 4
GLA_GATE_TAU = 16.0
GLA_CHUNK = 64
RG_C = 8.0
SWIGLU_LIMIT = 7.0
SWIGLU_ALPHA = 1.702
LN_EPS = 1e-5

LANES = 128
BF16_SUBLANES = 16
VMEM_LIMIT = 56 * 1024 * 1024

PROJ_ROWS = 256
SEQ_ROWS = 256
MOE_ROWS = 256
FFN_COLS = 512


def _params(*sem):
    return pltpu.CompilerParams(dimension_semantics=sem, vmem_limit_bytes=VMEM_LIMIT)


def _const_spec(shape):
    nd = len(shape)
    return pl.BlockSpec(shape, lambda *_: (0,) * nd)


def _bdot(a, b):
    return jnp.dot(a, b, preferred_element_type=F32)


def _split3(x):
    hi = x.astype(BF16)
    r1 = x - hi.astype(F32)
    mid = r1.astype(BF16)
    lo = (r1 - mid.astype(F32)).astype(BF16)
    return hi, mid, lo


def _sigmoid(x):
    return 1.0 / (1.0 + jnp.exp(-x))


def _log_sigmoid(x):
    return jnp.minimum(x, 0.0) - jnp.log1p(jnp.exp(-jnp.abs(x)))


def _softplus(x):
    return jnp.maximum(x, 0.0) + jnp.log1p(jnp.exp(-jnp.abs(x)))


def _layernorm(x, g, b):
    mu = jnp.mean(x, axis=-1, keepdims=True)
    xc = x - mu
    var = jnp.mean(xc * xc, axis=-1, keepdims=True)
    return xc * lax.rsqrt(var + LN_EPS) * g + b


def _proj_kernel(x_ref, wq_ref, wal_ref, wa2_ref, ba_ref, wxr_ref, wmg_ref, bmg_ref,
                 qkvg_ref, loga_ref, xr_ref, gate_ref):
    xb = x_ref[...].astype(BF16)
    d = x_ref.shape[1]
    for c in range(0, qkvg_ref.shape[1], d):
        qkvg_ref[:, c:c + d] = _bdot(xb, wq_ref[:, c:c + d]).astype(BF16)
    a_low = _bdot(xb, wal_ref[...])
    z = _bdot(a_low.astype(BF16), wa2_ref[...]) + ba_ref[...]
    loga_ref[...] = _log_sigmoid(z) * (1.0 / GLA_GATE_TAU)
    xr_ref[...] = _bdot(xb, wxr_ref[...])
    for c in range(0, gate_ref.shape[1], d):
        gate_ref[:, c:c + d] = _sigmoid(_bdot(xb, wmg_ref[:, c:c + d]) + bmg_ref[:, c:c + d]).astype(BF16)


def _proj(x2d, row0, t, w, tm):
    d = x2d.shape[1]
    kw = w["wq"].shape[1]
    nk = w["wa2"].shape[1]
    consts = [w["wq"], w["wal"], w["wa2"], w["ba"], w["wxr"], w["wmg"], w["bmg"]]
    return pl.pallas_call(
        _proj_kernel,
        grid=(t // tm,),
        in_specs=[pl.BlockSpec((tm, d), lambda i: (row0 // tm + i, 0))] + [_const_spec(c.shape) for c in consts],
        out_specs=[pl.BlockSpec((tm, kw), lambda i: (i, 0)),
                   pl.BlockSpec((tm, nk), lambda i: (i, 0)),
                   pl.BlockSpec((tm, d), lambda i: (i, 0)),
                   pl.BlockSpec((tm, 2 * d), lambda i: (i, 0))],
        out_shape=[jax.ShapeDtypeStruct((t, kw), BF16),
                   jax.ShapeDtypeStruct((t, nk), F32),
                   jax.ShapeDtypeStruct((t, d), F32),
                   jax.ShapeDtypeStruct((t, 2 * d), BF16)],
        compiler_params=_params("arbitrary"),
        name="proj",
    )(x2d, *consts)


def _gla_kernel(*refs, chunk, n_chunks, heads, has_state):
    q_ref, k_ref, v_ref, la_ref, g_ref, ng_ref = refs[:6]
    s0_ref = refs[6] if has_state else None
    o_ref, so_ref, s_scr = refs[-3:]
    t = pl.program_id(1)
    dk = q_ref.shape[1] // heads
    dv = v_ref.shape[1] // heads
    scale = dk ** -0.5

    @pl.when(t == 0)
    def _():
        if has_state:
            s_scr[...] = s0_ref[0, 0]
        else:
            s_scr[...] = jnp.zeros_like(s_scr)

    row = lax.broadcasted_iota(jnp.int32, (chunk, chunk), 0)
    col = lax.broadcasted_iota(jnp.int32, (chunk, chunk), 1)
    causal = row >= col
    tril = jnp.where(causal, 1.0, 0.0).astype(BF16)

    for ci in range(n_chunks):
        rows = slice(ci * chunk, (ci + 1) * chunk)
        la_hi, la_mid, la_lo = _split3(la_ref[rows, :])
        b = _bdot(tril, la_hi) + _bdot(tril, la_mid) + _bdot(tril, la_lo)
        b_last = b[chunk - 1:chunk, :]
        q_e = (q_ref[rows, :].astype(F32) * scale * jnp.exp(b)).astype(BF16)
        k = k_ref[rows, :].astype(F32)
        k_e = (k * jnp.exp(-b)).astype(BF16)
        k_d = (k * jnp.exp(b_last - b)).astype(BF16)
        decay = jnp.exp(b_last)
        for h in range(heads):
            ks = slice(h * dk, (h + 1) * dk)
            vs = slice(h * dv, (h + 1) * dv)
            v_h = v_ref[rows, vs]
            s_old = s_scr[h]
            scores = lax.dot_general(q_e[:, ks], k_e[:, ks], (((1,), (1,)), ((), ())),
                                     preferred_element_type=F32)
            scores = jnp.where(causal, scores, 0.0)
            o = _bdot(scores.astype(BF16), v_h) + _bdot(q_e[:, ks], s_old.astype(BF16))
            kv = lax.dot_general(k_d[:, ks], v_h, (((0,), (0,)), ((), ())), preferred_element_type=F32)
            s_scr[h] = jnp.transpose(decay[:, ks]) * s_old + kv
            mu = jnp.mean(o, axis=-1, keepdims=True)
            oc = o - mu
            var = jnp.mean(oc * oc, axis=-1, keepdims=True)
            on = oc * lax.rsqrt(var + LN_EPS) * ng_ref[:, vs]
            g = g_ref[rows, vs].astype(F32)
            o_ref[rows, vs] = (on * (g * _sigmoid(g))).astype(BF16)

    @pl.when(t == pl.num_programs(1) - 1)
    def _():
        so_ref[0] = s_scr[...]


def _gla(qkvg, loga, norm_g, s0, layer, nb, rows, chunk, heads, dk, dv):
    t = qkvg.shape[0]
    tps = t // nb // rows
    kw, vw = heads * dk, heads * dv
    has_state = s0 is not None
    in_specs = [pl.BlockSpec((rows, kw), lambda b, i: (b * tps + i, 0)),
                pl.BlockSpec((rows, kw), lambda b, i: (b * tps + i, 1)),
                pl.BlockSpec((rows, vw), lambda b, i: (b * tps + i, (2 * kw) // vw)),
                pl.BlockSpec((rows, kw), lambda b, i: (b * tps + i, 0)),
                pl.BlockSpec((rows, vw), lambda b, i: (b * tps + i, (2 * kw) // vw + 1)),
                _const_spec(norm_g.shape)]
    args = [qkvg, qkvg, qkvg, loga, qkvg, norm_g]
    if has_state:
        in_specs.append(pl.BlockSpec((1, 1, heads, dk, dv), lambda b, i: (layer, b, 0, 0, 0)))
        args.append(s0)
    return pl.pallas_call(
        functools.partial(_gla_kernel, chunk=chunk, n_chunks=rows // chunk, heads=heads, has_state=has_state),
        grid=(nb, tps),
        in_specs=in_specs,
        out_specs=[pl.BlockSpec((rows, vw), lambda b, i: (b * tps + i, 0)),
                   pl.BlockSpec((1, heads, dk, dv), lambda b, i: (b, 0, 0, 0))],
        out_shape=[jax.ShapeDtypeStruct((t, vw), BF16),
                   jax.ShapeDtypeStruct((nb, heads, dk, dv), F32)],
        scratch_shapes=[pltpu.VMEM((heads, dk, dv), F32)],
        compiler_params=_params("arbitrary", "arbitrary"),
        name="gla",
    )(*args)


def _rg_gates(xc, wax_ref, ba_ref, bx_ref, lam_ref):
    n_blocks, bw, _ = wax_ref.shape
    r_parts, i_parts = [], []
    for n in range(n_blocks):
        cs = slice(n * bw, (n + 1) * bw)
        pre = _bdot(xc[:, cs].astype(BF16), wax_ref[n])
        r_parts.append(_sigmoid(pre[:, :bw] + ba_ref[:, cs]))
        i_parts.append(_sigmoid(pre[:, bw:] + bx_ref[:, cs]))
    r = jnp.concatenate(r_parts, axis=1)
    i = jnp.concatenate(i_parts, axis=1)
    log_a = -RG_C * r * _softplus(-lam_ref[...])
    a = jnp.exp(log_a)
    bx = jnp.sqrt(jnp.tanh(-log_a) * (1.0 + a * a)) * (i * xc)
    return a, bx


def _rglru_seq_kernel(xr_ref, cw_ref, cb_ref, wax_ref, ba_ref, bx_ref, lam_ref,
                      h_ref, hl_ref, cv_ref, cbuf, a_scr, b_scr, h_scr, hc):
    t = pl.program_id(1)
    rows = xr_ref.shape[0]
    taps = cw_ref.shape[0]
    head = 8

    @pl.when(t == 0)
    def _():
        cbuf[0:head, :] = jnp.zeros((head, cbuf.shape[1]), F32)
        hc[...] = jnp.zeros_like(hc)

    cbuf[head:head + rows, :] = xr_ref[...]
    first = head - (taps - 1)
    xc = cb_ref[...] + sum(cbuf[first + j:first + j + rows, :] * cw_ref[j:j + 1, :] for j in range(taps))
    tail = cbuf[first + rows:head + rows, :]
    cbuf[first:head, :] = tail

    a, bx = _rg_gates(xc, wax_ref, ba_ref, bx_ref, lam_ref)
    a_scr[...] = a
    b_scr[...] = bx

    def step(i, h):
        h = a_scr[pl.ds(i, 1), :] * h + b_scr[pl.ds(i, 1), :]
        h_scr[pl.ds(i, 1), :] = h
        return h

    h_last = lax.fori_loop(0, rows, step, hc[...], unroll=8)
    hc[...] = h_last
    h_ref[...] = h_scr[...].astype(BF16)

    @pl.when(t == pl.num_programs(1) - 1)
    def _():
        hl_ref[0] = h_last
        cv_ref[0] = tail


def _rglru_seq(xr, w, nb, rows):
    t, width = xr.shape
    tps = t // nb // rows
    taps = w["cw"].shape[0]
    consts = [w["cw"], w["cb"], w["wax"], w["rba"], w["rbx"], w["lam"]]
    return pl.pallas_call(
        _rglru_seq_kernel,
        grid=(nb, tps),
        in_specs=[pl.BlockSpec((rows, width), lambda b, i: (b * tps + i, 0))] + [_const_spec(c.shape) for c in consts],
        out_specs=[pl.BlockSpec((rows, width), lambda b, i: (b * tps + i, 0)),
                   pl.BlockSpec((1, 1, width), lambda b, i: (b, 0, 0)),
                   pl.BlockSpec((1, taps - 1, width), lambda b, i: (b, 0, 0))],
        out_shape=[jax.ShapeDtypeStruct((t, width), BF16),
                   jax.ShapeDtypeStruct((nb, 1, width), F32),
                   jax.ShapeDtypeStruct((nb, taps - 1, width), F32)],
        scratch_shapes=[pltpu.VMEM((rows + 8, width), F32), pltpu.VMEM((rows, width), F32),
                        pltpu.VMEM((rows, width), F32), pltpu.VMEM((rows, width), F32),
                        pltpu.VMEM((1, width), F32)],
        compiler_params=_params("arbitrary", "arbitrary"),
        name="rglru_seq",
    )(xr, *consts)


def _rglru_step_kernel(xr_ref, sc_ref, h0_ref, cw_ref, cb_ref, wax_ref, ba_ref, bx_ref, lam_ref,
                       h_ref, hn_ref, cv_ref):
    taps = cw_ref.shape[0]
    xr = xr_ref[...]
    xc = cb_ref[...] + sum(sc_ref[j] * cw_ref[j:j + 1, :] for j in range(taps - 1)) + xr * cw_ref[taps - 1:taps, :]
    a, bx = _rg_gates(xc, wax_ref, ba_ref, bx_ref, lam_ref)
    h = a * h0_ref[...] + bx
    h_ref[...] = h.astype(BF16)
    hn_ref[...] = h
    for j in range(taps - 2):
        cv_ref[j] = sc_ref[j + 1]
    cv_ref[taps - 2] = xr


def _rglru_step(xr, conv_t, h0, w):
    n, width = xr.shape
    consts = [w["cw"], w["cb"], w["wax"], w["rba"], w["rbx"], w["lam"]]
    args = [xr, conv_t, h0] + consts
    return pl.pallas_call(
        _rglru_step_kernel,
        grid=(1,),
        in_specs=[_const_spec(a.shape) for a in args],
        out_specs=[_const_spec((n, width)), _const_spec((n, width)), _const_spec(conv_t.shape)],
        out_shape=[jax.ShapeDtypeStruct((n, width), BF16),
                   jax.ShapeDtypeStruct((n, width), F32),
                   jax.ShapeDtypeStruct(conv_t.shape, F32)],
        compiler_params=_params("arbitrary"),
        name="rglru_step",
    )(*args)


def _pack_bf16_pairs(x):
    half = x.shape[1] // 2
    hi = pltpu.bitcast(x[:, :half].astype(BF16).astype(F32), jnp.uint32)
    lo = pltpu.bitcast(x[:, half:].astype(BF16).astype(F32), jnp.uint32)
    return hi | (lo >> 16)


def _unpack_bf16_pairs(p):
    hi = pltpu.bitcast(p & jnp.uint32(0xFFFF0000), F32)
    lo = pltpu.bitcast(p << 16, F32)
    return jnp.concatenate([hi, lo], axis=1)


def _post_kernel(x_ref, o_ref, h_ref, gate_ref, wb0_ref, wb1_ref, wo_ref, g1_ref, b1_ref, rw_ref, rb_ref, cnt0_ref,
                 x1_ref, x1p_ref, idx_ref, wt_ref, cnt_ref, cnt_scr, *, alpha, n_experts):
    d = x_ref.shape[1]
    tm = x_ref.shape[0]

    @pl.when(pl.program_id(0) == 0)
    def _():
        cnt_scr[...] = cnt0_ref[...]

    gate = gate_ref[...].astype(F32)
    merged = gate[:, :d] * _bdot(o_ref[...], wb0_ref[...]) + gate[:, d:] * _bdot(h_ref[...], wb1_ref[...])
    mix = _bdot(merged.astype(BF16), wo_ref[...])
    x1 = _layernorm(alpha * x_ref[...] + mix, g1_ref[...], b1_ref[...])
    x1_ref[...] = x1
    x1p_ref[...] = _pack_bf16_pairs(x1)

    xh, xm, xl = _split3(x1)
    wh, wm, wl = _split3(rw_ref[...])
    logits = (_bdot(xl, wh) + _bdot(xh, wl) + _bdot(xm, wm)) + (_bdot(xm, wh) + _bdot(xh, wm)) + _bdot(xh, wh)
    logits = logits + rb_ref[...]
    lane = lax.broadcasted_iota(jnp.int32, logits.shape, 1)
    lane_f = lane.astype(F32)
    neg_inf = jnp.float32(-jnp.inf)
    cur = jnp.where(lane < n_experts, logits, neg_inf)
    vals, idxs = [], []
    for _ in range(TOP_K):
        m = jnp.max(cur, axis=-1, keepdims=True)
        sel = jnp.min(jnp.where(cur == m, lane_f, float(LANES)), axis=-1, keepdims=True)
        vals.append(m)
        idxs.append(sel)
        cur = jnp.where(lane_f == sel, neg_inf, cur)
    exps = [jnp.exp(v - vals[0]) for v in vals]
    total = sum(exps)

    onehots = [jnp.where(lane_f == idxs[j], 1.0, 0.0) for j in range(TOP_K)]
    chosen = sum(onehots)
    r = lax.broadcasted_iota(jnp.int32, (tm, tm), 0)
    c = lax.broadcasted_iota(jnp.int32, (tm, tm), 1)
    earlier = jnp.where(r > c, 1.0, 0.0).astype(BF16)
    before = _bdot(earlier, chosen.astype(BF16)) + cnt_scr[...]
    ranks = [jnp.sum(onehots[j] * before, axis=-1, keepdims=True) for j in range(TOP_K)]
    cnt_scr[...] += jnp.sum(chosen, axis=0, keepdims=True)
    cnt_ref[...] = cnt_scr[...]

    idx_out = jnp.zeros(logits.shape, F32)
    wt_out = jnp.zeros(logits.shape, F32)
    for j in range(TOP_K):
        idx_out = jnp.where(lane == j, idxs[j], idx_out)
        idx_out = jnp.where(lane == TOP_K + j, ranks[j], idx_out)
        wt_out = jnp.where(lane == j, exps[j] / total, wt_out)
    idx_ref[...] = idx_out.astype(jnp.int32)
    wt_ref[...] = wt_out


def _post(x2d, row0, o, h, gate, cnt0, w, tm, alpha, n_experts):
    t, d = o.shape
    consts = [w["wb0"], w["wb1"], w["wo"], w["ln1g"], w["ln1b"], w["rw"], w["rb"], cnt0]
    row = lambda width: pl.BlockSpec((tm, width), lambda i: (i, 0))
    return pl.pallas_call(
        functools.partial(_post_kernel, alpha=alpha, n_experts=n_experts),
        grid=(t // tm,),
        in_specs=[pl.BlockSpec((tm, d), lambda i: (row0 // tm + i, 0)), row(d), row(d), row(2 * d)] +
                 [_const_spec(c.shape) for c in consts],
        out_specs=[row(d), row(d // 2), row(LANES), row(LANES), _const_spec((1, LANES))],
        out_shape=[jax.ShapeDtypeStruct((t, d), F32),
                   jax.ShapeDtypeStruct((t, d // 2), jnp.uint32),
                   jax.ShapeDtypeStruct((t, LANES), jnp.int32),
                   jax.ShapeDtypeStruct((t, LANES), F32),
                   jax.ShapeDtypeStruct((1, LANES), F32)],
        scratch_shapes=[pltpu.VMEM((1, LANES), F32)],
        compiler_params=_params("arbitrary"),
        name="post",
    )(x2d, o, h, gate, *consts)


SC_ROWS = 32


def _sc_workers():
    info = plsc.get_sparse_core_info()
    return info.num_cores, info.num_cores * info.num_subcores


def _sc_gather(table, idx):
    n_rows = idx.shape[0]
    d = table.shape[1]
    n_cores, n_workers = _sc_workers()
    per_worker = n_rows // n_workers
    n_chunks = per_worker // SC_ROWS
    assert n_chunks * SC_ROWS * n_workers == n_rows, "pad the index list to whole chunks per subcore"
    mesh = plsc.VectorSubcoreMesh(core_axis_name="c", subcore_axis_name="s")

    @functools.partial(
        pl.kernel, mesh=mesh, out_type=jax.ShapeDtypeStruct((n_rows, d), table.dtype),
        scratch_types=[pltpu.VMEM((SC_ROWS,), jnp.int32), pltpu.VMEM((SC_ROWS, d), table.dtype),
                       pltpu.SemaphoreType.DMA])
    def gather(table_hbm, idx_hbm, out_hbm, idx_v, rows_v, sem):
        worker = lax.axis_index("s") * n_cores + lax.axis_index("c")
        base = worker * per_worker

        @pl.loop(0, n_chunks)
        def _(c):
            off = pl.multiple_of(base + c * SC_ROWS, SC_ROWS)
            pltpu.sync_copy(idx_hbm.at[pl.ds(off, SC_ROWS)], idx_v)
            pltpu.async_copy(table_hbm.at[idx_v], rows_v, sem).wait()
            pltpu.sync_copy(rows_v, out_hbm.at[pl.ds(off, SC_ROWS)])

    return gather(table, idx)


def _sc_scatter(x, pos, n_out):
    n, d = x.shape
    n_cores, n_workers = _sc_workers()
    per_worker = n // n_workers
    n_chunks = per_worker // SC_ROWS
    assert n_chunks * SC_ROWS * n_workers == n, "pad the rows to whole chunks per subcore"
    assert pos.shape == (TOP_K * n,)
    mesh = plsc.VectorSubcoreMesh(core_axis_name="c", subcore_axis_name="s")

    @functools.partial(
        pl.kernel, mesh=mesh, out_type=jax.ShapeDtypeStruct((n_out, d), x.dtype),
        scratch_types=[pltpu.VMEM((SC_ROWS, d), x.dtype)] + [pltpu.VMEM((SC_ROWS,), jnp.int32)] * TOP_K)
    def scatter(x_hbm, pos_hbm, out_hbm, rows_v, *idx_v):
        worker = lax.axis_index("s") * n_cores + lax.axis_index("c")
        base = worker * per_worker

        @pl.loop(0, n_chunks)
        def _(c):
            off = pl.multiple_of(base + c * SC_ROWS, SC_ROWS)
            pltpu.sync_copy(x_hbm.at[pl.ds(off, SC_ROWS)], rows_v)
            for k in range(TOP_K):
                pltpu.sync_copy(pos_hbm.at[pl.ds(k * n + off, SC_ROWS)], idx_v[k])
            for k in range(TOP_K):
                pltpu.sync_copy(rows_v, out_hbm.at[idx_v[k]])

    return scatter(x, pos)


def _ffn_kernel(be_ref, nu_ref, x_ref, wgu_ref, bgu_ref, wd_ref, bd_ref, o_ref, wgu_bf, wd_bf):
    i = pl.program_id(0)
    f = wd_ref.shape[2]
    new_expert = jnp.logical_or(i == 0, be_ref[i] != be_ref[jnp.maximum(i - 1, 0)])

    @pl.when(new_expert)
    def _():
        wgu_bf[...] = wgu_ref[0, 0].astype(BF16)
        wd_bf[...] = wd_ref[0, 0].astype(BF16)

    @pl.when(i < nu_ref[0])
    def _():
        x = _unpack_bf16_pairs(x_ref[...]).astype(BF16)
        acc = jnp.zeros((x.shape[0], wd_ref.shape[3]), F32)
        for c in range(0, f, FFN_COLS):
            gate = _bdot(x, wgu_bf[:, c:c + FFN_COLS]) + bgu_ref[0, 0, :, c:c + FFN_COLS]
            up = _bdot(x, wgu_bf[:, f + c:f + c + FFN_COLS]) + bgu_ref[0, 0, :, f + c:f + c + FFN_COLS]
            gate = jnp.minimum(gate, SWIGLU_LIMIT)
            up = jnp.clip(up, -SWIGLU_LIMIT, SWIGLU_LIMIT)
            act = (up + 1.0) * gate * _sigmoid(SWIGLU_ALPHA * gate)
            acc = acc + _bdot(act.astype(BF16), wd_bf[c:c + FFN_COLS, :])
        o_ref[...] = _pack_bf16_pairs(acc + bd_ref[0, 0])

    @pl.when(i >= nu_ref[0])
    def _():
        o_ref[...] = jnp.zeros_like(o_ref)


def _ffn(block_e, n_used, xs, layer, w_gu, b_gu, w_down, b_down, bm):
    n_layers, n_exp, d, f2 = w_gu.shape
    f = f2 // 2
    n_blocks = block_e.shape[0]
    expert = lambda i, be, nu: (layer, be[i], 0, 0)
    grid_spec = pltpu.PrefetchScalarGridSpec(
        num_scalar_prefetch=2,
        grid=(n_blocks,),
        in_specs=[pl.BlockSpec((bm, d // 2), lambda i, be, nu: (i, 0)),
                  pl.BlockSpec((1, 1, d, f2), expert),
                  pl.BlockSpec((1, 1, 1, f2), expert),
                  pl.BlockSpec((1, 1, f, d), expert),
                  pl.BlockSpec((1, 1, 1, d), expert)],
        out_specs=pl.BlockSpec((bm, d // 2), lambda i, be, nu: (i, 0)),
        scratch_shapes=[pltpu.VMEM((d, f2), BF16), pltpu.VMEM((f, d), BF16)],
    )
    return pl.pallas_call(
        _ffn_kernel,
        grid_spec=grid_spec,
        out_shape=jax.ShapeDtypeStruct((n_blocks * bm, d // 2), jnp.uint32),
        compiler_params=_params("arbitrary"),
        name="ffn",
    )(block_e, n_used, xs, w_gu, b_gu.reshape(n_layers, n_exp, 1, f2),
      w_down, b_down.reshape(n_layers, n_exp, 1, d))


def _combine_kernel(x_ref, *refs, alpha):
    y_refs, (wt_ref, g_ref, b_ref, o_ref) = refs[:TOP_K], refs[TOP_K:]
    wt = wt_ref[...]
    ffn = sum(wt[:, j:j + 1] * _unpack_bf16_pairs(y_refs[j][...]) for j in range(TOP_K))
    o_ref[...] = _layernorm(alpha * x_ref[...] + ffn, g_ref[...], b_ref[...])


def _combine(x1, y, wt, g, b, tm, alpha):
    t, d = x1.shape
    planes = [pl.BlockSpec((tm, d // 2), lambda i, j=j: (j * (t // tm) + i, 0)) for j in range(TOP_K)]
    return pl.pallas_call(
        functools.partial(_combine_kernel, alpha=alpha),
        grid=(t // tm,),
        in_specs=[pl.BlockSpec((tm, d), lambda i: (i, 0))] + planes +
                 [pl.BlockSpec((tm, LANES), lambda i: (i, 0)), _const_spec(g.shape), _const_spec(b.shape)],
        out_specs=pl.BlockSpec((tm, d), lambda i: (i, 0)),
        out_shape=jax.ShapeDtypeStruct((t, d), F32),
        compiler_params=_params("arbitrary"),
        name="combine",
    )(x1, *([y] * TOP_K), wt, g, b)


def _pad_to(v, multiple):
    return jnp.pad(v, (0, -v.shape[0] % multiple))


def _route(experts, ranks, counts, bm):
    n_experts = counts.shape[0]
    i32 = jnp.int32
    padded = (counts + bm - 1) // bm * bm
    ends_pad = jnp.cumsum(padded)
    start_pad = ends_pad - padded
    n_blocks = -(-experts.size // bm) + n_experts
    block_e = jnp.minimum(jnp.searchsorted(ends_pad, jnp.arange(n_blocks, dtype=i32) * bm, side="right"),
                          n_experts - 1).astype(i32)
    onehot = experts[..., None] == jnp.arange(n_experts, dtype=i32)
    pos = jnp.sum(jnp.where(onehot, start_pad, 0), axis=-1).astype(i32) + ranks
    n_used = (ends_pad[-1:] // bm).astype(i32)
    return block_e, n_used, pos


def _moe(x1, x1p, routing, counts, top_w_pad, w, layer, moe_weights, alpha, tm):
    n_tok = x1.shape[0]
    n_experts = moe_weights[0].shape[1]
    chunk = _sc_workers()[1] * SC_ROWS
    block_e, n_used, pos = _route(routing[:, :TOP_K], routing[:, TOP_K:2 * TOP_K], counts[0, :n_experts].astype(jnp.int32),
                                  MOE_ROWS)
    n_rows = block_e.shape[0] * MOE_ROWS
    n_pad = -n_tok % chunk
    dump = n_rows + jnp.arange(TOP_K * n_pad, dtype=jnp.int32).reshape(TOP_K, n_pad)
    pos_planes = jnp.concatenate([pos.T, dump], axis=1).reshape(-1)
    x_rows = jnp.pad(x1p, ((0, n_pad), (0, 0)))
    xs = _sc_scatter(x_rows, pos_planes, n_rows + TOP_K * n_pad)
    out_rows = _ffn(block_e, n_used, xs, layer, *moe_weights, MOE_ROWS)
    y = _sc_gather(out_rows, _pad_to(pos.T.reshape(-1), chunk))
    return _combine(x1, y, top_w_pad, w["ln2g"], w["ln2b"], tm, alpha)


def _layer_weights(layer, w_in, gla_w_a2, gla_b_a, gla_norm_g, rg_conv_w, rg_conv_b, rg_w_a, rg_b_a, rg_w_x,
                   rg_b_x, rg_lambda, b_merge, w_branch, w_o, ln1_g, ln1_b, ln2_g, ln2_b, router_w, router_b,
                   kw, vw):
    d = w_in.shape[1]
    rank = gla_w_a2.shape[1]
    width = rg_conv_w.shape[2]
    n_exp = router_w.shape[2]
    c0 = 2 * kw + 2 * vw
    wi = w_in[layer]
    row = lambda v: v.reshape(1, -1)
    return {
        "wq": wi[:, :c0].astype(BF16),
        "wal": jnp.pad(wi[:, c0:c0 + rank], ((0, 0), (0, LANES - rank))).astype(BF16),
        "wa2": jnp.pad(gla_w_a2[layer], ((0, LANES - rank), (0, 0))).astype(BF16),
        "ba": row(gla_b_a[layer]),
        "wxr": wi[:, c0 + rank:c0 + rank + width].astype(BF16),
        "wmg": wi[:, c0 + rank + width:].astype(BF16),
        "bmg": row(b_merge[layer]),
        "ng": row(gla_norm_g[layer]),
        "cw": rg_conv_w[layer], "cb": row(rg_conv_b[layer]),
        "wax": jnp.concatenate([rg_w_a[layer], rg_w_x[layer]], axis=-1).astype(BF16),
        "rba": row(rg_b_a[layer]), "rbx": row(rg_b_x[layer]), "lam": row(rg_lambda[layer]),
        "wb0": w_branch[layer, 0].astype(BF16), "wb1": w_branch[layer, 1].astype(BF16),
        "wo": w_o[layer].astype(BF16),
        "ln1g": row(ln1_g[layer]), "ln1b": row(ln1_b[layer]),
        "ln2g": row(ln2_g[layer]), "ln2b": row(ln2_b[layer]),
        "rw": jnp.pad(router_w[layer], ((0, 0), (0, LANES - n_exp))),
        "rb": jnp.pad(row(router_b[layer]), ((0, 0), (0, LANES - n_exp))),
    }


def kernel(x_prompt, x_sample, state_gla, state_rglru, state_conv, ln1_g, ln1_b, w_in, gla_w_a2, gla_b_a, gla_norm_g, rg_conv_w, rg_conv_b, rg_w_a, rg_b_a, rg_w_x, rg_b_x, rg_lambda, b_merge, w_branch, w_o, ln2_g, ln2_b, router_w, router_b, moe_w_gu, moe_b_gu, moe_w_down, moe_b_down):
    n_p, seq, d = x_prompt.shape
    n_s, dec_seq, _ = x_sample.shape
    assert dec_seq == 1, "the sample group carries one new token per sequence"
    depth, _, heads, dk, dv = state_gla.shape
    kw, vw = heads * dk, heads * dv
    n_exp = router_w.shape[2]
    alpha = (2.0 * depth) ** 0.25
    t_p = n_p * seq
    pad_rows = BF16_SUBLANES

    xp, xp0 = x_prompt.reshape(t_p, d), 0
    xs, xs0 = x_sample.reshape(n_s, d), 0
    moe_weights = (moe_w_gu, moe_b_gu, moe_w_down, moe_b_down)
    outs = {k: [] for k in ("gla_p", "rg_p", "cv_p", "gla_s", "rg_s", "cv_s")}
    for layer in range(depth):
        w = _layer_weights(layer, w_in, gla_w_a2, gla_b_a, gla_norm_g, rg_conv_w, rg_conv_b, rg_w_a, rg_b_a,
                           rg_w_x, rg_b_x, rg_lambda, b_merge, w_branch, w_o, ln1_g, ln1_b, ln2_g, ln2_b,
                           router_w, router_b, kw, vw)
        qkvg, loga, xr, gate = _proj(xp, xp0, t_p, w, PROJ_ROWS)
        o_p, s_p = _gla(qkvg, loga, w["ng"], None, layer, n_p, SEQ_ROWS, GLA_CHUNK, heads, dk, dv)
        h_p, hl_p, cv_p = _rglru_seq(xr, w, n_p, SEQ_ROWS)
        x1_p, x1p_p, idx_p, wt_p, cnt = _post(xp, xp0, o_p, h_p, gate, jnp.zeros((1, LANES), F32), w, PROJ_ROWS,
                                              alpha, n_exp)
        qkvg_s, loga_s, xr_s, gate_s = _proj(xs, xs0, n_s, w, n_s)
        pad = lambda a: jnp.pad(a[:, None, :], ((0, 0), (0, pad_rows - 1), (0, 0))).reshape(n_s * pad_rows, -1)
        o_s, s_s = _gla(pad(qkvg_s), pad(loga_s), w["ng"], state_gla, layer, n_s, pad_rows, pad_rows,
                        heads, dk, dv)
        o_s = o_s.reshape(n_s, pad_rows, vw)[:, 0]
        h_s, hn_s, cv_s = _rglru_step(xr_s, jnp.swapaxes(state_conv[layer], 0, 1), state_rglru[layer], w)
        x1_s, x1p_s, idx_s, wt_s, cnt = _post(xs, xs0, o_s, h_s, gate_s, cnt, w, n_s, alpha, n_exp)
        x1 = jnp.concatenate([x1_p, x1_s], axis=0)
        x1p = jnp.concatenate([x1p_p, x1p_s], axis=0)
        idx = jnp.concatenate([idx_p, idx_s], axis=0)
        wt = jnp.concatenate([wt_p, wt_s], axis=0)
        x2 = _moe(x1, x1p, idx, cnt, wt, w, layer, moe_weights, alpha, n_s)
        xp, xp0, xs, xs0 = x2, 0, x2, t_p
        outs["gla_p"].append(s_p)
        outs["rg_p"].append(hl_p.reshape(n_p, -1))
        outs["cv_p"].append(cv_p)
        outs["gla_s"].append(s_s)
        outs["rg_s"].append(hn_s)
        outs["cv_s"].append(jnp.swapaxes(cv_s, 0, 1))
    return (xp[:t_p].reshape(n_p, seq, d), xs[t_p:].reshape(n_s, dec_seq, d),
            jnp.stack(outs["gla_p"]), jnp.stack(outs["rg_p"]), jnp.stack(outs["cv_p"]),
            jnp.stack(outs["gla_s"]), jnp.stack(outs["rg_s"]), jnp.stack(outs["cv_s"]))
```

```python
import functools

import jax
import jax.numpy as jnp
from jax import lax
from jax.experimental import pallas as pl
from jax.experimental.pallas import tpu as pltpu
from jax.experimental.pallas import tpu_sc as plsc

F32 = jnp.float32
BF16 = jnp.bfloat16

TOP_K = 4
GLA_GATE_TAU = 16.0
GLA_CHUNK = 64
RG_C = 8.0
SWIGLU_LIMIT = 7.0
SWIGLU_ALPHA = 1.702
LN_EPS = 1e-5

LANES = 128
BF16_SUBLANES = 16
VMEM_LIMIT = 56 * 1024 * 1024

PROJ_ROWS = 256
SEQ_ROWS = 256
DECODE_SEQS = 8
MOE_ROWS = 256
FFN_COLS = 512


def _params(*sem):
    return pltpu.CompilerParams(dimension_semantics=sem, vmem_limit_bytes=VMEM_LIMIT)


def _const_spec(shape):
    nd = len(shape)
    return pl.BlockSpec(shape, lambda *_: (0,) * nd)


def _bdot(a, b):
    return jnp.dot(a, b, preferred_element_type=F32)


def _split3(x):
    hi = x.astype(BF16)
    r1 = x - hi.astype(F32)
    mid = r1.astype(BF16)
    lo = (r1 - mid.astype(F32)).astype(BF16)
    return hi, mid, lo


def _sigmoid(x):
    return 1.0 / (1.0 + jnp.exp(-x))


def _log_sigmoid(x):
    return jnp.minimum(x, 0.0) - jnp.log1p(jnp.exp(-jnp.abs(x)))


def _softplus(x):
    return jnp.maximum(x, 0.0) + jnp.log1p(jnp.exp(-jnp.abs(x)))


def _layernorm(x, g, b):
    mu = jnp.mean(x, axis=-1, keepdims=True)
    xc = x - mu
    var = jnp.mean(xc * xc, axis=-1, keepdims=True)
    return xc * lax.rsqrt(var + LN_EPS) * g + b


def _proj_kernel(x_ref, wq_ref, wal_ref, wa2_ref, ba_ref, wxr_ref, wmg_ref, bmg_ref,
                 qkvg_ref, loga_ref, xr_ref, gate_ref):
    xb = x_ref[...].astype(BF16)
    d = x_ref.shape[1]
    for c in range(0, qkvg_ref.shape[1], d):
        qkvg_ref[:, c:c + d] = _bdot(xb, wq_ref[:, c:c + d]).astype(BF16)
    a_low = _bdot(xb, wal_ref[...])
    z = _bdot(a_low.astype(BF16), wa2_ref[...]) + ba_ref[...]
    loga_ref[...] = _log_sigmoid(z) * (1.0 / GLA_GATE_TAU)
    xr_ref[...] = _bdot(xb, wxr_ref[...])
    for c in range(0, gate_ref.shape[1], d):
        gate_ref[:, c:c + d] = _sigmoid(_bdot(xb, wmg_ref[:, c:c + d]) + bmg_ref[:, c:c + d]).astype(BF16)


def _proj(x2d, w, tm):
    t, d = x2d.shape
    kw = w["wq"].shape[1]
    nk = w["wa2"].shape[1]
    consts = [w["wq"], w["wal"], w["wa2"], w["ba"], w["wxr"], w["wmg"], w["bmg"]]
    return pl.pallas_call(
        _proj_kernel,
        grid=(t // tm,),
        in_specs=[pl.BlockSpec((tm, d), lambda i: (i, 0))] + [_const_spec(c.shape) for c in consts],
        out_specs=[pl.BlockSpec((tm, kw), lambda i: (i, 0)),
                   pl.BlockSpec((tm, nk), lambda i: (i, 0)),
                   pl.BlockSpec((tm, d), lambda i: (i, 0)),
                   pl.BlockSpec((tm, 2 * d), lambda i: (i, 0))],
        out_shape=[jax.ShapeDtypeStruct((t, kw), BF16),
                   jax.ShapeDtypeStruct((t, nk), F32),
                   jax.ShapeDtypeStruct((t, d), F32),
                   jax.ShapeDtypeStruct((t, 2 * d), BF16)],
        compiler_params=_params("arbitrary"),
        name="proj",
    )(x2d, *consts)


def _gla_kernel(*refs, chunk, n_chunks, heads, has_state):
    q_ref, k_ref, v_ref, la_ref, g_ref, ng_ref = refs[:6]
    s0_ref = refs[6] if has_state else None
    o_ref, so_ref, s_scr = refs[-3:]
    t = pl.program_id(1)
    dk = q_ref.shape[1] // heads
    dv = v_ref.shape[1] // heads
    scale = dk ** -0.5
    chunks_per_seq = n_chunks // s_scr.shape[0]

    @pl.when(t == 0)
    def _():
        if has_state:
            s_scr[...] = s0_ref[0]
        else:
            s_scr[...] = jnp.zeros_like(s_scr)

    row = lax.broadcasted_iota(jnp.int32, (chunk, chunk), 0)
    col = lax.broadcasted_iota(jnp.int32, (chunk, chunk), 1)
    causal = row >= col
    tril = jnp.where(causal, 1.0, 0.0).astype(BF16)

    for ci in range(n_chunks):
        rows = slice(ci * chunk, (ci + 1) * chunk)
        seq = ci // chunks_per_seq
        la_hi, la_mid, la_lo = _split3(la_ref[rows, :])
        b = _bdot(tril, la_hi) + _bdot(tril, la_mid) + _bdot(tril, la_lo)
        b_last = b[chunk - 1:chunk, :]
        q_e = (q_ref[rows, :].astype(F32) * scale * jnp.exp(b)).astype(BF16)
        k = k_ref[rows, :].astype(F32)
        k_e = (k * jnp.exp(-b)).astype(BF16)
        k_d = (k * jnp.exp(b_last - b)).astype(BF16)
        decay = jnp.exp(b_last)
        for h in range(heads):
            ks = slice(h * dk, (h + 1) * dk)
            vs = slice(h * dv, (h + 1) * dv)
            v_h = v_ref[rows, vs]
            s_old = s_scr[seq, h]
            scores = lax.dot_general(q_e[:, ks], k_e[:, ks], (((1,), (1,)), ((), ())),
                                     preferred_element_type=F32)
            scores = jnp.where(causal, scores, 0.0)
            o = _bdot(scores.astype(BF16), v_h) + _bdot(q_e[:, ks], s_old.astype(BF16))
            kv = lax.dot_general(k_d[:, ks], v_h, (((0,), (0,)), ((), ())), preferred_element_type=F32)
            s_scr[seq, h] = jnp.transpose(decay[:, ks]) * s_old + kv
            mu = jnp.mean(o, axis=-1, keepdims=True)
            oc = o - mu
            var = jnp.mean(oc * oc, axis=-1, keepdims=True)
            on = oc * lax.rsqrt(var + LN_EPS) * ng_ref[:, vs]
            g = g_ref[rows, vs].astype(F32)
            o_ref[rows, vs] = (on * (g * _sigmoid(g))).astype(BF16)

    @pl.when(t == pl.num_programs(1) - 1)
    def _():
        so_ref[...] = s_scr[...]


def _gla(qkvg, loga, norm_g, s0, layer, nb, s_len, rows, chunk, heads, dk, dv):
    t = nb * s_len
    seqs = max(1, rows // s_len)
    assert s_len * seqs % rows == 0 and nb % seqs == 0 and s_len % chunk == 0
    nb, tps = nb // seqs, s_len * seqs // rows
    kw, vw = heads * dk, heads * dv
    has_state = s0 is not None
    in_specs = [pl.BlockSpec((rows, kw), lambda b, i: (b * tps + i, 0)),
                pl.BlockSpec((rows, kw), lambda b, i: (b * tps + i, 1)),
                pl.BlockSpec((rows, vw), lambda b, i: (b * tps + i, (2 * kw) // vw)),
                pl.BlockSpec((rows, kw), lambda b, i: (b * tps + i, 0)),
                pl.BlockSpec((rows, vw), lambda b, i: (b * tps + i, (2 * kw) // vw + 1)),
                _const_spec(norm_g.shape)]
    args = [qkvg, qkvg, qkvg, loga, qkvg, norm_g]
    if has_state:
        in_specs.append(pl.BlockSpec((1, seqs, heads, dk, dv), lambda b, i: (layer, b, 0, 0, 0)))
        args.append(s0)
    return pl.pallas_call(
        functools.partial(_gla_kernel, chunk=chunk, n_chunks=rows // chunk, heads=heads, has_state=has_state),
        grid=(nb, tps),
        in_specs=in_specs,
        out_specs=[pl.BlockSpec((rows, vw), lambda b, i: (b * tps + i, 0)),
                   pl.BlockSpec((seqs, heads, dk, dv), lambda b, i: (b, 0, 0, 0))],
        out_shape=[jax.ShapeDtypeStruct((t, vw), BF16),
                   jax.ShapeDtypeStruct((nb * seqs, heads, dk, dv), F32)],
        scratch_shapes=[pltpu.VMEM((seqs, heads, dk, dv), F32)],
        compiler_params=_params("arbitrary", "arbitrary"),
        name="gla",
    )(*args)


def _rg_gates(xc, wax_ref, ba_ref, bx_ref, lam_ref):
    n_blocks, bw, _ = wax_ref.shape
    r_parts, i_parts = [], []
    for n in range(n_blocks):
        cs = slice(n * bw, (n + 1) * bw)
        pre = _bdot(xc[:, cs].astype(BF16), wax_ref[n])
        r_parts.append(_sigmoid(pre[:, :bw] + ba_ref[:, cs]))
        i_parts.append(_sigmoid(pre[:, bw:] + bx_ref[:, cs]))
    r = jnp.concatenate(r_parts, axis=1)
    i = jnp.concatenate(i_parts, axis=1)
    log_a = -RG_C * r * _softplus(-lam_ref[...])
    a = jnp.exp(log_a)
    bx = jnp.sqrt(jnp.tanh(-log_a) * (1.0 + a * a)) * (i * xc)
    return a, bx


def _rglru_seq_kernel(xr_ref, cw_ref, cb_ref, wax_ref, ba_ref, bx_ref, lam_ref,
                      h_ref, hl_ref, cv_ref, cbuf, a_scr, b_scr, h_scr, hc):
    t = pl.program_id(1)
    rows = xr_ref.shape[0]
    taps = cw_ref.shape[0]
    head = 8

    @pl.when(t == 0)
    def _():
        cbuf[0:head, :] = jnp.zeros((head, cbuf.shape[1]), F32)
        hc[...] = jnp.zeros_like(hc)

    cbuf[head:head + rows, :] = xr_ref[...]
    first = head - (taps - 1)
    xc = cb_ref[...] + sum(cbuf[first + j:first + j + rows, :] * cw_ref[j:j + 1, :] for j in range(taps))
    tail = cbuf[first + rows:head + rows, :]
    cbuf[first:head, :] = tail

    a, bx = _rg_gates(xc, wax_ref, ba_ref, bx_ref, lam_ref)
    a_scr[...] = a
    b_scr[...] = bx

    def step(i, h):
        h = a_scr[pl.ds(i, 1), :] * h + b_scr[pl.ds(i, 1), :]
        h_scr[pl.ds(i, 1), :] = h
        return h

    h_last = lax.fori_loop(0, rows, step, hc[...], unroll=8)
    hc[...] = h_last
    h_ref[...] = h_scr[...].astype(BF16)

    @pl.when(t == pl.num_programs(1) - 1)
    def _():
        hl_ref[0] = h_last
        cv_ref[0] = tail


def _rglru_seq(xr, w, nb, s_len, rows):
    width = xr.shape[1]
    t = nb * s_len
    tps = s_len // rows
    taps = w["cw"].shape[0]
    consts = [w["cw"], w["cb"], w["wax"], w["rba"], w["rbx"], w["lam"]]
    return pl.pallas_call(
        _rglru_seq_kernel,
        grid=(nb, tps),
        in_specs=[pl.BlockSpec((rows, width), lambda b, i: (b * tps + i, 0))] + [_const_spec(c.shape) for c in consts],
        out_specs=[pl.BlockSpec((rows, width), lambda b, i: (b * tps + i, 0)),
                   pl.BlockSpec((1, 1, width), lambda b, i: (b, 0, 0)),
                   pl.BlockSpec((1, taps - 1, width), lambda b, i: (b, 0, 0))],
        out_shape=[jax.ShapeDtypeStruct((t, width), BF16),
                   jax.ShapeDtypeStruct((nb, 1, width), F32),
                   jax.ShapeDtypeStruct((nb, taps - 1, width), F32)],
        scratch_shapes=[pltpu.VMEM((rows + 8, width), F32), pltpu.VMEM((rows, width), F32),
                        pltpu.VMEM((rows, width), F32), pltpu.VMEM((rows, width), F32),
                        pltpu.VMEM((1, width), F32)],
        compiler_params=_params("arbitrary", "arbitrary"),
        name="rglru_seq",
    )(xr, *consts)


def _rglru_step_kernel(xr_ref, sc_ref, h0_ref, cw_ref, cb_ref, wax_ref, ba_ref, bx_ref, lam_ref,
                       h_ref, hn_ref, cv_ref):
    taps = cw_ref.shape[0]
    xr = xr_ref[...]
    xc = cb_ref[...] + sum(sc_ref[j] * cw_ref[j:j + 1, :] for j in range(taps - 1)) + xr * cw_ref[taps - 1:taps, :]
    a, bx = _rg_gates(xc, wax_ref, ba_ref, bx_ref, lam_ref)
    h = a * h0_ref[...] + bx
    h_ref[...] = h.astype(BF16)
    hn_ref[...] = h
    for j in range(taps - 2):
        cv_ref[j] = sc_ref[j + 1]
    cv_ref[taps - 2] = xr


def _rglru_step(xr, conv_t, h0, w):
    n, width = xr.shape
    consts = [w["cw"], w["cb"], w["wax"], w["rba"], w["rbx"], w["lam"]]
    args = [xr, conv_t, h0] + consts
    return pl.pallas_call(
        _rglru_step_kernel,
        grid=(1,),
        in_specs=[_const_spec(a.shape) for a in args],
        out_specs=[_const_spec((n, width)), _const_spec((n, width)), _const_spec(conv_t.shape)],
        out_shape=[jax.ShapeDtypeStruct((n, width), BF16),
                   jax.ShapeDtypeStruct((n, width), F32),
                   jax.ShapeDtypeStruct(conv_t.shape, F32)],
        compiler_params=_params("arbitrary"),
        name="rglru_step",
    )(*args)


def _pack_bf16_pairs(x):
    half = x.shape[1] // 2
    hi = pltpu.bitcast(x[:, :half].astype(BF16).astype(F32), jnp.uint32)
    lo = pltpu.bitcast(x[:, half:].astype(BF16).astype(F32), jnp.uint32)
    return hi | (lo >> 16)


def _unpack_bf16_pairs(p):
    hi = pltpu.bitcast(p & jnp.uint32(0xFFFF0000), F32)
    lo = pltpu.bitcast(p << 16, F32)
    return jnp.concatenate([hi, lo], axis=1)


def _post_kernel(x_ref, o_ref, h_ref, ot_ref, ht_ref, gate_ref, wb0_ref, wb1_ref, wo_ref, g1_ref, b1_ref, rw_ref,
                 rb_ref, x1_ref, x1p_ref, idx_ref, wt_ref, cnt_ref, cnt_scr, *, alpha, n_experts):
    d = x_ref.shape[1]
    tm = x_ref.shape[0]
    i = pl.program_id(0)

    @pl.when(i == 0)
    def _():
        cnt_scr[...] = jnp.zeros_like(cnt_scr)

    tail = i == pl.num_programs(0) - 1
    o = jnp.where(tail, ot_ref[...], o_ref[...])
    h = jnp.where(tail, ht_ref[...], h_ref[...])
    gate = gate_ref[...].astype(F32)
    merged = gate[:, :d] * _bdot(o, wb0_ref[...]) + gate[:, d:] * _bdot(h, wb1_ref[...])
    mix = _bdot(merged.astype(BF16), wo_ref[...])
    x1 = _layernorm(alpha * x_ref[...] + mix, g1_ref[...], b1_ref[...])
    x1_ref[...] = x1
    x1p_ref[...] = _pack_bf16_pairs(x1)

    xh, xm, xl = _split3(x1)
    wh, wm, wl = _split3(rw_ref[...])
    logits = (_bdot(xl, wh) + _bdot(xh, wl) + _bdot(xm, wm)) + (_bdot(xm, wh) + _bdot(xh, wm)) + _bdot(xh, wh)
    logits = logits + rb_ref[...]
    lane = lax.broadcasted_iota(jnp.int32, logits.shape, 1)
    lane_f = lane.astype(F32)
    neg_inf = jnp.float32(-jnp.inf)
    cur = jnp.where(lane < n_experts, logits, neg_inf)
    vals, idxs = [], []
    for _ in range(TOP_K):
        m = jnp.max(cur, axis=-1, keepdims=True)
        sel = jnp.min(jnp.where(cur == m, lane_f, float(LANES)), axis=-1, keepdims=True)
        vals.append(m)
        idxs.append(sel)
        cur = jnp.where(lane_f == sel, neg_inf, cur)
    exps = [jnp.exp(v - vals[0]) for v in vals]
    total = sum(exps)

    onehots = [jnp.where(lane_f == idxs[j], 1.0, 0.0) for j in range(TOP_K)]
    chosen = sum(onehots)
    r = lax.broadcasted_iota(jnp.int32, (tm, tm), 0)
    c = lax.broadcasted_iota(jnp.int32, (tm, tm), 1)
    earlier = jnp.where(r > c, 1.0, 0.0).astype(BF16)
    before = _bdot(earlier, chosen.astype(BF16)) + cnt_scr[...]
    ranks = [jnp.sum(onehots[j] * before, axis=-1, keepdims=True) for j in range(TOP_K)]
    cnt_scr[...] += jnp.sum(chosen, axis=0, keepdims=True)
    cnt_ref[...] = cnt_scr[...]

    idx_out = jnp.zeros(logits.shape, F32)
    wt_out = jnp.zeros(logits.shape, F32)
    for j in range(TOP_K):
        idx_out = jnp.where(lane == j, idxs[j], idx_out)
        idx_out = jnp.where(lane == TOP_K + j, ranks[j], idx_out)
        wt_out = jnp.where(lane == j, exps[j] / total, wt_out)
    idx_ref[...] = idx_out.astype(jnp.int32)
    wt_ref[...] = wt_out


def _post(x2d, o, h, o_tail, h_tail, gate, w, tm, alpha, n_experts):
    t, d = x2d.shape
    assert o.shape[0] == t - tm and o_tail.shape[0] == tm
    consts = [w["wb0"], w["wb1"], w["wo"], w["ln1g"], w["ln1b"], w["rw"], w["rb"]]
    row = lambda width: pl.BlockSpec((tm, width), lambda i: (i, 0))
    head = pl.BlockSpec((tm, d), lambda i: (jnp.minimum(i, t // tm - 2), 0))
    return pl.pallas_call(
        functools.partial(_post_kernel, alpha=alpha, n_experts=n_experts),
        grid=(t // tm,),
        in_specs=[row(d), head, head, _const_spec((tm, d)), _const_spec((tm, d)), row(2 * d)] +
                 [_const_spec(c.shape) for c in consts],
        out_specs=[row(d), row(d // 2), row(LANES), row(LANES), _const_spec((1, LANES))],
        out_shape=[jax.ShapeDtypeStruct((t, d), F32),
                   jax.ShapeDtypeStruct((t, d // 2), jnp.uint32),
                   jax.ShapeDtypeStruct((t, LANES), jnp.int32),
                   jax.ShapeDtypeStruct((t, LANES), F32),
                   jax.ShapeDtypeStruct((1, LANES), F32)],
        scratch_shapes=[pltpu.VMEM((1, LANES), F32)],
        compiler_params=_params("arbitrary"),
        name="post",
    )(x2d, o, h, o_tail, h_tail, gate, *consts)


SC_INDEX_MAX = 128
SC_ALIGN = 8
SC_BUFFER_BYTES = 224 * 1024


def _sc_plan(n_rows, row_bytes):
    info = plsc.get_sparse_core_info()
    n_workers = info.num_cores * info.num_subcores
    per_worker, rem = divmod(n_rows, n_workers)
    assert rem == 0 and per_worker % SC_ALIGN == 0, "rows must split into aligned equal shares per subcore"
    cap = min(SC_INDEX_MAX, SC_BUFFER_BYTES // row_bytes)
    chunk = max(c for c in range(SC_ALIGN, cap + 1, SC_ALIGN) if per_worker % c == 0)
    return info.num_cores, n_workers, per_worker, chunk


def _sc_gather(table, idx):
    n_rows = idx.shape[0]
    d = table.shape[1]
    n_cores, _, per_worker, chunk = _sc_plan(n_rows, d * table.dtype.itemsize)
    mesh = plsc.VectorSubcoreMesh(core_axis_name="c", subcore_axis_name="s")

    @functools.partial(
        pl.kernel, mesh=mesh, out_type=jax.ShapeDtypeStruct((n_rows, d), table.dtype),
        scratch_types=[pltpu.VMEM((chunk,), jnp.int32), pltpu.VMEM((chunk, d), table.dtype),
                       pltpu.SemaphoreType.DMA])
    def gather(table_hbm, idx_hbm, out_hbm, idx_v, rows_v, sem):
        worker = lax.axis_index("s") * n_cores + lax.axis_index("c")
        base = worker * per_worker

        @pl.loop(0, per_worker // chunk)
        def _(c):
            off = pl.multiple_of(base + c * chunk, SC_ALIGN)
            pltpu.sync_copy(idx_hbm.at[pl.ds(off, chunk)], idx_v)
            pltpu.async_copy(table_hbm.at[idx_v], rows_v, sem).wait()
            pltpu.sync_copy(rows_v, out_hbm.at[pl.ds(off, chunk)])

    return gather(table, idx)


def _sc_scatter(x, pos, n_out):
    n, d = x.shape
    assert pos.shape == (TOP_K * n,)
    n_cores, _, per_worker, chunk = _sc_plan(n, d * x.dtype.itemsize)
    mesh = plsc.VectorSubcoreMesh(core_axis_name="c", subcore_axis_name="s")

    @functools.partial(
        pl.kernel, mesh=mesh, out_type=jax.ShapeDtypeStruct((n_out, d), x.dtype),
        scratch_types=[pltpu.VMEM((chunk, d), x.dtype)] + [pltpu.VMEM((chunk,), jnp.int32)] * TOP_K)
    def scatter(x_hbm, pos_hbm, out_hbm, rows_v, *idx_v):
        worker = lax.axis_index("s") * n_cores + lax.axis_index("c")
        base = worker * per_worker

        @pl.loop(0, per_worker // chunk)
        def _(c):
            off = pl.multiple_of(base + c * chunk, SC_ALIGN)
            pltpu.sync_copy(x_hbm.at[pl.ds(off, chunk)], rows_v)
            for k in range(TOP_K):
                pltpu.sync_copy(pos_hbm.at[pl.ds(k * n + off, chunk)], idx_v[k])
            for k in range(TOP_K):
                pltpu.sync_copy(rows_v, out_hbm.at[idx_v[k]])

    return scatter(x, pos)


def _ffn_kernel(be_ref, nu_ref, x_ref, wgu_ref, bgu_ref, wd_ref, bd_ref, o_ref, wgu_bf, wd_bf):
    i = pl.program_id(0)
    f = wd_ref.shape[2]
    new_expert = jnp.logical_or(i == 0, be_ref[i] != be_ref[jnp.maximum(i - 1, 0)])

    @pl.when(new_expert)
    def _():
        wgu_bf[...] = wgu_ref[0, 0].astype(BF16)
        wd_bf[...] = wd_ref[0, 0].astype(BF16)

    @pl.when(i < nu_ref[0])
    def _():
        x = _unpack_bf16_pairs(x_ref[...]).astype(BF16)
        acc = jnp.zeros((x.shape[0], wd_ref.shape[3]), F32)
        for c in range(0, f, FFN_COLS):
            gate = _bdot(x, wgu_bf[:, c:c + FFN_COLS]) + bgu_ref[0, 0, :, c:c + FFN_COLS]
            up = _bdot(x, wgu_bf[:, f + c:f + c + FFN_COLS]) + bgu_ref[0, 0, :, f + c:f + c + FFN_COLS]
            gate = jnp.minimum(gate, SWIGLU_LIMIT)
            up = jnp.clip(up, -SWIGLU_LIMIT, SWIGLU_LIMIT)
            act = (up + 1.0) * gate * _sigmoid(SWIGLU_ALPHA * gate)
            acc = acc + _bdot(act.astype(BF16), wd_bf[c:c + FFN_COLS, :])
        o_ref[...] = _pack_bf16_pairs(acc + bd_ref[0, 0])

    @pl.when(i >= nu_ref[0])
    def _():
        o_ref[...] = jnp.zeros_like(o_ref)


def _ffn(block_e, n_used, xs, layer, w_gu, b_gu, w_down, b_down, bm):
    n_layers, n_exp, d, f2 = w_gu.shape
    f = f2 // 2
    n_blocks = block_e.shape[0]
    expert = lambda i, be, nu: (layer, be[i], 0, 0)
    grid_spec = pltpu.PrefetchScalarGridSpec(
        num_scalar_prefetch=2,
        grid=(n_blocks,),
        in_specs=[pl.BlockSpec((bm, d // 2), lambda i, be, nu: (i, 0)),
                  pl.BlockSpec((1, 1, d, f2), expert),
                  pl.BlockSpec((1, 1, 1, f2), expert),
                  pl.BlockSpec((1, 1, f, d), expert),
                  pl.BlockSpec((1, 1, 1, d), expert)],
        out_specs=pl.BlockSpec((bm, d // 2), lambda i, be, nu: (i, 0)),
        scratch_shapes=[pltpu.VMEM((d, f2), BF16), pltpu.VMEM((f, d), BF16)],
    )
    return pl.pallas_call(
        _ffn_kernel,
        grid_spec=grid_spec,
        out_shape=jax.ShapeDtypeStruct((n_blocks * bm, d // 2), jnp.uint32),
        compiler_params=_params("arbitrary"),
        name="ffn",
    )(block_e, n_used, xs, w_gu, b_gu.reshape(n_layers, n_exp, 1, f2),
      w_down, b_down.reshape(n_layers, n_exp, 1, d))


def _combine_kernel(x_ref, *refs, alpha):
    y_refs, (wt_ref, g_ref, b_ref, o_ref) = refs[:TOP_K], refs[TOP_K:]
    wt = wt_ref[...]
    ffn = sum(wt[:, j:j + 1] * _unpack_bf16_pairs(y_refs[j][...]) for j in range(TOP_K))
    o_ref[...] = _layernorm(alpha * x_ref[...] + ffn, g_ref[...], b_ref[...])


def _combine(x1, y, wt, g, b, tm, alpha):
    t, d = x1.shape
    planes = [pl.BlockSpec((tm, d // 2), lambda i, j=j: (j * (t // tm) + i, 0)) for j in range(TOP_K)]
    return pl.pallas_call(
        functools.partial(_combine_kernel, alpha=alpha),
        grid=(t // tm,),
        in_specs=[pl.BlockSpec((tm, d), lambda i: (i, 0))] + planes +
                 [pl.BlockSpec((tm, LANES), lambda i: (i, 0)), _const_spec(g.shape), _const_spec(b.shape)],
        out_specs=pl.BlockSpec((tm, d), lambda i: (i, 0)),
        out_shape=jax.ShapeDtypeStruct((t, d), F32),
        compiler_params=_params("arbitrary"),
        name="combine",
    )(x1, *([y] * TOP_K), wt, g, b)


def _route(experts, ranks, counts, bm):
    n_experts = counts.shape[0]
    i32 = jnp.int32
    padded = (counts + bm - 1) // bm * bm
    ends_pad = jnp.cumsum(padded)
    start_pad = ends_pad - padded
    n_blocks = -(-experts.size // bm) + n_experts
    first_row = jnp.arange(n_blocks, dtype=i32)[:, None] * bm
    block_e = jnp.minimum(jnp.sum(ends_pad[None, :] <= first_row, axis=1), n_experts - 1).astype(i32)
    onehot = experts[..., None] == jnp.arange(n_experts, dtype=i32)
    pos = jnp.sum(jnp.where(onehot, start_pad, 0), axis=-1).astype(i32) + ranks
    n_used = (ends_pad[-1:] // bm).astype(i32)
    return block_e, n_used, pos


def _moe(x1, x1p, routing, counts, top_w_pad, w, layer, moe_weights, alpha, tm):
    n_experts = moe_weights[0].shape[1]
    block_e, n_used, pos = _route(routing[:, :TOP_K], routing[:, TOP_K:2 * TOP_K],
                                  counts[0, :n_experts].astype(jnp.int32), MOE_ROWS)
    pos_planes = pos.T.reshape(-1)
    xs = _sc_scatter(x1p, pos_planes, block_e.shape[0] * MOE_ROWS)
    out_rows = _ffn(block_e, n_used, xs, layer, *moe_weights, MOE_ROWS)
    y = _sc_gather(out_rows, pos_planes)
    return _combine(x1, y, top_w_pad, w["ln2g"], w["ln2b"], tm, alpha)


def _layer_weights(layer, w_in, gla_w_a2, gla_b_a, gla_norm_g, rg_conv_w, rg_conv_b, rg_w_a, rg_b_a, rg_w_x,
                   rg_b_x, rg_lambda, b_merge, w_branch, w_o, ln1_g, ln1_b, ln2_g, ln2_b, router_w, router_b,
                   kw, vw):
    d = w_in.shape[1]
    rank = gla_w_a2.shape[1]
    width = rg_conv_w.shape[2]
    n_exp = router_w.shape[2]
    c0 = 2 * kw + 2 * vw
    wi = w_in[layer]
    row = lambda v: v.reshape(1, -1)
    return {
        "wq": wi[:, :c0].astype(BF16),
        "wal": jnp.pad(wi[:, c0:c0 + rank], ((0, 0), (0, LANES - rank))).astype(BF16),
        "wa2": jnp.pad(gla_w_a2[layer], ((0, LANES - rank), (0, 0))).astype(BF16),
        "ba": row(gla_b_a[layer]),
        "wxr": wi[:, c0 + rank:c0 + rank + width].astype(BF16),
        "wmg": wi[:, c0 + rank + width:].astype(BF16),
        "bmg": row(b_merge[layer]),
        "ng": row(gla_norm_g[layer]),
        "cw": rg_conv_w[layer], "cb": row(rg_conv_b[layer]),
        "wax": jnp.concatenate([rg_w_a[layer], rg_w_x[layer]], axis=-1).astype(BF16),
        "rba": row(rg_b_a[layer]), "rbx": row(rg_b_x[layer]), "lam": row(rg_lambda[layer]),
        "wb0": w_branch[layer, 0].astype(BF16), "wb1": w_branch[layer, 1].astype(BF16),
        "wo": w_o[layer].astype(BF16),
        "ln1g": row(ln1_g[layer]), "ln1b": row(ln1_b[layer]),
        "ln2g": row(ln2_g[layer]), "ln2b": row(ln2_b[layer]),
        "rw": jnp.pad(router_w[layer], ((0, 0), (0, LANES - n_exp))),
        "rb": jnp.pad(row(router_b[layer]), ((0, 0), (0, LANES - n_exp))),
    }


def kernel(x_prompt, x_sample, state_gla, state_rglru, state_conv, ln1_g, ln1_b, w_in, gla_w_a2, gla_b_a, gla_norm_g, rg_conv_w, rg_conv_b, rg_w_a, rg_b_a, rg_w_x, rg_b_x, rg_lambda, b_merge, w_branch, w_o, ln2_g, ln2_b, router_w, router_b, moe_w_gu, moe_b_gu, moe_w_down, moe_b_down):
    n_p, seq, d = x_prompt.shape
    n_s, dec_seq, _ = x_sample.shape
    assert dec_seq == 1, "the sample group carries one new token per sequence"
    depth, _, heads, dk, dv = state_gla.shape
    kw, vw = heads * dk, heads * dv
    n_exp = router_w.shape[2]
    alpha = (2.0 * depth) ** 0.25
    t_p = n_p * seq
    pad_rows = BF16_SUBLANES

    tile = PROJ_ROWS
    assert t_p % tile == 0 and n_s <= tile
    tail_pad = lambda a: jnp.pad(a, ((0, tile - n_s), (0, 0)))
    x = jnp.concatenate([x_prompt.reshape(t_p, d), tail_pad(x_sample.reshape(n_s, d))], axis=0)
    moe_weights = (moe_w_gu, moe_b_gu, moe_w_down, moe_b_down)
    outs = {k: [] for k in ("gla_p", "rg_p", "cv_p", "gla_s", "rg_s", "cv_s")}
    for layer in range(depth):
        w = _layer_weights(layer, w_in, gla_w_a2, gla_b_a, gla_norm_g, rg_conv_w, rg_conv_b, rg_w_a, rg_b_a,
                           rg_w_x, rg_b_x, rg_lambda, b_merge, w_branch, w_o, ln1_g, ln1_b, ln2_g, ln2_b,
                           router_w, router_b, kw, vw)
        qkvg, loga, xr, gate = _proj(x, w, tile)
        o_p, s_p = _gla(qkvg, loga, w["ng"], None, layer, n_p, seq, SEQ_ROWS, GLA_CHUNK, heads, dk, dv)
        h_p, hl_p, cv_p = _rglru_seq(xr, w, n_p, seq, SEQ_ROWS)
        pad = lambda a: jnp.pad(a[t_p:t_p + n_s, None, :],
                                ((0, 0), (0, pad_rows - 1), (0, 0))).reshape(n_s * pad_rows, -1)
        o_s, s_s = _gla(pad(qkvg), pad(loga), w["ng"], state_gla, layer, n_s, pad_rows,
                        DECODE_SEQS * pad_rows, pad_rows, heads, dk, dv)
        o_s = o_s.reshape(n_s, pad_rows, vw)[:, 0]
        h_s, hn_s, cv_s = _rglru_step(xr[t_p:t_p + n_s], jnp.swapaxes(state_conv[layer], 0, 1),
                                      state_rglru[layer], w)
        x1, x1p, routing, wt, cnt = _post(x, o_p, h_p, tail_pad(o_s), tail_pad(h_s), gate, w, tile, alpha, n_exp)
        x = _moe(x1, x1p, routing, cnt, wt, w, layer, moe_weights, alpha, tile)
        outs["gla_p"].append(s_p)
        outs["rg_p"].append(hl_p.reshape(n_p, -1))
        outs["cv_p"].append(cv_p)
        outs["gla_s"].append(s_s)
        outs["rg_s"].append(hn_s)
        outs["cv_s"].append(jnp.swapaxes(cv_s, 0, 1))
    return (x[:t_p].reshape(n_p, seq, d), x[t_p:t_p + n_s].reshape(n_s, dec_seq, d),
            jnp.stack(outs["gla_p"]), jnp.stack(outs["rg_p"]), jnp.stack(outs["cv_p"]),
            jnp.stack(outs["gla_s"]), jnp.stack(outs["rg_s"]), jnp.stack(outs["cv_s"]))
```

```python
import functools

import jax
import jax.numpy as jnp
from jax import lax
from jax.experimental import pallas as pl
from jax.experimental.pallas import tpu as pltpu
from jax.experimental.pallas import tpu_sc as plsc

F32 = jnp.float32
BF16 = jnp.bfloat16

TOP_K = 4
GLA_GATE_TAU = 16.0
GLA_CHUNK = 64
RG_C = 8.0
SWIGLU_LIMIT = 7.0
SWIGLU_ALPHA = 1.702
LN_EPS = 1e-5

LANES = 128
BF16_SUBLANES = 16
VMEM_LIMIT = 56 * 1024 * 1024

PROJ_ROWS = 512
SEQ_ROWS = 256
DECODE_SEQS = 8
MOE_ROWS = 1024
FFN_COLS = 512


def _params(*sem):
    return pltpu.CompilerParams(dimension_semantics=sem, vmem_limit_bytes=VMEM_LIMIT)


def _const_spec(shape):
    nd = len(shape)
    return pl.BlockSpec(shape, lambda *_: (0,) * nd)


def _weight_spec(shape):
    nd = len(shape)
    return pl.BlockSpec(shape, lambda *_: (0,) * nd, pipeline_mode=pl.Buffered(1))


def _bdot(a, b):
    return jnp.dot(a, b, preferred_element_type=F32)


def _split3(x):
    hi = x.astype(BF16)
    r1 = x - hi.astype(F32)
    mid = r1.astype(BF16)
    lo = (r1 - mid.astype(F32)).astype(BF16)
    return hi, mid, lo


def _sigmoid(x):
    return 0.5 * jnp.tanh(0.5 * x) + 0.5


def _log_sigmoid(x):
    return jnp.minimum(x, 0.0) - jnp.log1p(jnp.exp(-jnp.abs(x)))


def _softplus(x):
    return jnp.maximum(x, 0.0) + jnp.log1p(jnp.exp(-jnp.abs(x)))


def _layernorm(x, g, b):
    mu = jnp.mean(x, axis=-1, keepdims=True)
    xc = x - mu
    var = jnp.mean(xc * xc, axis=-1, keepdims=True)
    return xc * lax.rsqrt(var + LN_EPS) * g + b


def _head_tail_specs(n_head_tiles, tail_tile, tm, d):
    return [pl.BlockSpec((tm, d), lambda i: (jnp.minimum(i, n_head_tiles - 1), 0)),
            pl.BlockSpec((tm, d), lambda i: (tail_tile, 0))]


def _head_or_tail(head_ref, tail_ref):
    is_tail = pl.program_id(0) == pl.num_programs(0) - 1
    return jnp.where(is_tail, tail_ref[...], head_ref[...])


def _proj_kernel(x_ref, xt_ref, wq_ref, wal_ref, wa2_ref, ba_ref, wxr_ref, wmg_ref, bmg_ref,
                 qkvg_ref, loga_ref, xr_ref, gate_ref):
    xb = _head_or_tail(x_ref, xt_ref).astype(BF16)
    d = x_ref.shape[1]
    for c in range(0, qkvg_ref.shape[1], d):
        qkvg_ref[:, c:c + d] = _bdot(xb, wq_ref[:, c:c + d]).astype(BF16)
    a_low = _bdot(xb, wal_ref[...])
    z = _bdot(a_low.astype(BF16), wa2_ref[...]) + ba_ref[...]
    loga_ref[...] = _log_sigmoid(z) * (1.0 / GLA_GATE_TAU)
    xr_ref[...] = _bdot(xb, wxr_ref[...])
    for c in range(0, gate_ref.shape[1], d):
        gate_ref[:, c:c + d] = _sigmoid(_bdot(xb, wmg_ref[:, c:c + d]) + bmg_ref[:, c:c + d]).astype(BF16)


def _proj(x_head, x_tail, n_head_tiles, tail_tile, w, tm):
    d = x_head.shape[1]
    t = (n_head_tiles + 1) * tm
    kw = w["wq"].shape[1]
    nk = w["wa2"].shape[1]
    consts = [w["wq"], w["wal"], w["wa2"], w["ba"], w["wxr"], w["wmg"], w["bmg"]]
    return pl.pallas_call(
        _proj_kernel,
        grid=(t // tm,),
        in_specs=_head_tail_specs(n_head_tiles, tail_tile, tm, d) + [_weight_spec(c.shape) for c in consts],
        out_specs=[pl.BlockSpec((tm, kw), lambda i: (i, 0)),
                   pl.BlockSpec((tm, nk), lambda i: (i, 0)),
                   pl.BlockSpec((tm, d), lambda i: (i, 0)),
                   pl.BlockSpec((tm, 2 * d), lambda i: (i, 0))],
        out_shape=[jax.ShapeDtypeStruct((t, kw), BF16),
                   jax.ShapeDtypeStruct((t, nk), F32),
                   jax.ShapeDtypeStruct((t, d), F32),
                   jax.ShapeDtypeStruct((t, 2 * d), BF16)],
        compiler_params=_params("arbitrary"),
        name="proj",
    )(x_head, x_tail, *consts)


def _gla_kernel(*refs, chunk, n_chunks, heads, has_state):
    q_ref, k_ref, v_ref, la_ref, g_ref, ng_ref = refs[:6]
    s0_ref = refs[6] if has_state else None
    o_ref, so_ref, s_scr = refs[-3:]
    t = pl.program_id(1)
    dk = q_ref.shape[1] // heads
    dv = v_ref.shape[1] // heads
    scale = dk ** -0.5
    chunks_per_seq = n_chunks // s_scr.shape[0]

    @pl.when(t == 0)
    def _():
        if has_state:
            s_scr[...] = s0_ref[0]
        else:
            s_scr[...] = jnp.zeros_like(s_scr)

    row = lax.broadcasted_iota(jnp.int32, (chunk, chunk), 0)
    col = lax.broadcasted_iota(jnp.int32, (chunk, chunk), 1)
    causal = row >= col
    tril = jnp.where(causal, 1.0, 0.0).astype(BF16)

    for ci in range(n_chunks):
        rows = slice(ci * chunk, (ci + 1) * chunk)
        seq = ci // chunks_per_seq
        la_hi, la_mid, la_lo = _split3(la_ref[rows, :])
        b = _bdot(tril, la_hi) + _bdot(tril, la_mid) + _bdot(tril, la_lo)
        b_last = b[chunk - 1:chunk, :]
        q_e = (q_ref[rows, :].astype(F32) * scale * jnp.exp(b)).astype(BF16)
        k = k_ref[rows, :].astype(F32)
        k_e = (k * jnp.exp(-b)).astype(BF16)
        k_d = (k * jnp.exp(b_last - b)).astype(BF16)
        decay = jnp.exp(b_last)
        for h in range(heads):
            ks = slice(h * dk, (h + 1) * dk)
            vs = slice(h * dv, (h + 1) * dv)
            v_h = v_ref[rows, vs]
            s_old = s_scr[seq, h]
            scores = lax.dot_general(q_e[:, ks], k_e[:, ks], (((1,), (1,)), ((), ())),
                                     preferred_element_type=F32)
            scores = jnp.where(causal, scores, 0.0)
            o = _bdot(scores.astype(BF16), v_h) + _bdot(q_e[:, ks], s_old.astype(BF16))
            kv = lax.dot_general(k_d[:, ks], v_h, (((0,), (0,)), ((), ())), preferred_element_type=F32)
            s_scr[seq, h] = jnp.transpose(decay[:, ks]) * s_old + kv
            mu = jnp.mean(o, axis=-1, keepdims=True)
            oc = o - mu
            var = jnp.mean(oc * oc, axis=-1, keepdims=True)
            on = oc * lax.rsqrt(var + LN_EPS) * ng_ref[:, vs]
            g = g_ref[rows, vs].astype(F32)
            o_ref[rows, vs] = (on * (g * _sigmoid(g))).astype(BF16)

    @pl.when(t == pl.num_programs(1) - 1)
    def _():
        so_ref[...] = s_scr[...]


def _gla(qkvg, loga, norm_g, s0, layer, nb, s_len, rows, chunk, heads, dk, dv):
    t = nb * s_len
    seqs = max(1, rows // s_len)
    assert s_len * seqs % rows == 0 and nb % seqs == 0 and s_len % chunk == 0
    nb, tps = nb // seqs, s_len * seqs // rows
    kw, vw = heads * dk, heads * dv
    has_state = s0 is not None
    in_specs = [pl.BlockSpec((rows, kw), lambda b, i: (b * tps + i, 0)),
                pl.BlockSpec((rows, kw), lambda b, i: (b * tps + i, 1)),
                pl.BlockSpec((rows, vw), lambda b, i: (b * tps + i, (2 * kw) // vw)),
                pl.BlockSpec((rows, kw), lambda b, i: (b * tps + i, 0)),
                pl.BlockSpec((rows, vw), lambda b, i: (b * tps + i, (2 * kw) // vw + 1)),
                _const_spec(norm_g.shape)]
    args = [qkvg, qkvg, qkvg, loga, qkvg, norm_g]
    if has_state:
        in_specs.append(pl.BlockSpec((1, seqs, heads, dk, dv), lambda b, i: (layer, b, 0, 0, 0)))
        args.append(s0)
    return pl.pallas_call(
        functools.partial(_gla_kernel, chunk=chunk, n_chunks=rows // chunk, heads=heads, has_state=has_state),
        grid=(nb, tps),
        in_specs=in_specs,
        out_specs=[pl.BlockSpec((rows, vw), lambda b, i: (b * tps + i, 0)),
                   pl.BlockSpec((seqs, heads, dk, dv), lambda b, i: (b, 0, 0, 0))],
        out_shape=[jax.ShapeDtypeStruct((t, vw), BF16),
                   jax.ShapeDtypeStruct((nb * seqs, heads, dk, dv), F32)],
        scratch_shapes=[pltpu.VMEM((seqs, heads, dk, dv), F32)],
        compiler_params=_params("arbitrary", "arbitrary"),
        name="gla",
    )(*args)


def _rg_gates(xc, wax_ref, ba_ref, bx_ref, lam_ref):
    n_blocks, bw, _ = wax_ref.shape
    r_parts, i_parts = [], []
    for n in range(n_blocks):
        cs = slice(n * bw, (n + 1) * bw)
        pre = _bdot(xc[:, cs].astype(BF16), wax_ref[n])
        r_parts.append(_sigmoid(pre[:, :bw] + ba_ref[:, cs]))
        i_parts.append(_sigmoid(pre[:, bw:] + bx_ref[:, cs]))
    r = jnp.concatenate(r_parts, axis=1)
    i = jnp.concatenate(i_parts, axis=1)
    log_a = -RG_C * r * _softplus(-lam_ref[...])
    a = jnp.exp(log_a)
    bx = jnp.sqrt(jnp.tanh(-log_a) * (1.0 + a * a)) * (i * xc)
    return a, bx


def _rglru_seq_kernel(xr_ref, cw_ref, cb_ref, wax_ref, ba_ref, bx_ref, lam_ref,
                      h_ref, hl_ref, cv_ref, cbuf, a_scr, b_scr, h_scr, hc):
    t = pl.program_id(1)
    rows = xr_ref.shape[0]
    taps = cw_ref.shape[0]
    head = 8

    @pl.when(t == 0)
    def _():
        cbuf[0:head, :] = jnp.zeros((head, cbuf.shape[1]), F32)
        hc[...] = jnp.zeros_like(hc)

    cbuf[head:head + rows, :] = xr_ref[...]
    first = head - (taps - 1)
    xc = cb_ref[...] + sum(cbuf[first + j:first + j + rows, :] * cw_ref[j:j + 1, :] for j in range(taps))
    tail = cbuf[first + rows:head + rows, :]
    cbuf[first:head, :] = tail

    a, bx = _rg_gates(xc, wax_ref, ba_ref, bx_ref, lam_ref)
    a_scr[...] = a
    b_scr[...] = bx

    def step(i, h):
        h = a_scr[pl.ds(i, 1), :] * h + b_scr[pl.ds(i, 1), :]
        h_scr[pl.ds(i, 1), :] = h
        return h

    h_last = lax.fori_loop(0, rows, step, hc[...], unroll=8)
    hc[...] = h_last
    h_ref[...] = h_scr[...].astype(BF16)

    @pl.when(t == pl.num_programs(1) - 1)
    def _():
        hl_ref[0] = h_last
        cv_ref[0] = tail


def _rglru_seq(xr, w, nb, s_len, rows):
    width = xr.shape[1]
    t = nb * s_len
    tps = s_len // rows
    taps = w["cw"].shape[0]
    consts = [w["cw"], w["cb"], w["wax"], w["rba"], w["rbx"], w["lam"]]
    return pl.pallas_call(
        _rglru_seq_kernel,
        grid=(nb, tps),
        in_specs=[pl.BlockSpec((rows, width), lambda b, i: (b * tps + i, 0))] + [_const_spec(c.shape) for c in consts],
        out_specs=[pl.BlockSpec((rows, width), lambda b, i: (b * tps + i, 0)),
                   pl.BlockSpec((1, 1, width), lambda b, i: (b, 0, 0)),
                   pl.BlockSpec((1, taps - 1, width), lambda b, i: (b, 0, 0))],
        out_shape=[jax.ShapeDtypeStruct((t, width), BF16),
                   jax.ShapeDtypeStruct((nb, 1, width), F32),
                   jax.ShapeDtypeStruct((nb, taps - 1, width), F32)],
        scratch_shapes=[pltpu.VMEM((rows + 8, width), F32), pltpu.VMEM((rows, width), F32),
                        pltpu.VMEM((rows, width), F32), pltpu.VMEM((rows, width), F32),
                        pltpu.VMEM((1, width), F32)],
        compiler_params=_params("arbitrary", "arbitrary"),
        name="rglru_seq",
    )(xr, *consts)


def _rglru_step_kernel(xr_ref, sc_ref, h0_ref, cw_ref, cb_ref, wax_ref, ba_ref, bx_ref, lam_ref,
                       h_ref, hn_ref, cv_ref):
    taps = cw_ref.shape[0]
    xr = xr_ref[...]
    xc = cb_ref[...] + sum(sc_ref[j] * cw_ref[j:j + 1, :] for j in range(taps - 1)) + xr * cw_ref[taps - 1:taps, :]
    a, bx = _rg_gates(xc, wax_ref, ba_ref, bx_ref, lam_ref)
    h = a * h0_ref[...] + bx
    h_ref[...] = h.astype(BF16)
    hn_ref[...] = h
    for j in range(taps - 2):
        cv_ref[j] = sc_ref[j + 1]
    cv_ref[taps - 2] = xr


def _rglru_step(xr, conv_t, h0, w):
    n, width = xr.shape
    consts = [w["cw"], w["cb"], w["wax"], w["rba"], w["rbx"], w["lam"]]
    args = [xr, conv_t, h0] + consts
    return pl.pallas_call(
        _rglru_step_kernel,
        grid=(1,),
        in_specs=[_const_spec(a.shape) for a in args],
        out_specs=[_const_spec((n, width)), _const_spec((n, width)), _const_spec(conv_t.shape)],
        out_shape=[jax.ShapeDtypeStruct((n, width), BF16),
                   jax.ShapeDtypeStruct((n, width), F32),
                   jax.ShapeDtypeStruct(conv_t.shape, F32)],
        compiler_params=_params("arbitrary"),
        name="rglru_step",
    )(*args)


def _pack_bf16_pairs(x):
    half = x.shape[1] // 2
    hi = pltpu.bitcast(x[:, :half].astype(BF16).astype(F32), jnp.uint32)
    lo = pltpu.bitcast(x[:, half:].astype(BF16).astype(F32), jnp.uint32)
    return hi | (lo >> 16)


def _unpack_bf16_pairs(p):
    hi = pltpu.bitcast(p & jnp.uint32(0xFFFF0000), F32)
    lo = pltpu.bitcast(p << 16, F32)
    return jnp.concatenate([hi, lo], axis=1)


def _post_kernel(x_ref, xt_ref, o_ref, ot_ref, h_ref, ht_ref, gate_ref, wb0_ref, wb1_ref, wo_ref, g1_ref, b1_ref,
                 rw_ref, rb_ref, x1_ref, x1p_ref, idx_ref, wt_ref, cnt_ref, cnt_scr, *, alpha, n_experts):
    d = x_ref.shape[1]
    tm = x_ref.shape[0]

    @pl.when(pl.program_id(0) == 0)
    def _():
        cnt_scr[...] = jnp.zeros_like(cnt_scr)

    x = _head_or_tail(x_ref, xt_ref)
    o = _head_or_tail(o_ref, ot_ref)
    h = _head_or_tail(h_ref, ht_ref)
    gate = gate_ref[...].astype(F32)
    merged = gate[:, :d] * _bdot(o, wb0_ref[...]) + gate[:, d:] * _bdot(h, wb1_ref[...])
    mix = _bdot(merged.astype(BF16), wo_ref[...])
    x1 = _layernorm(alpha * x + mix, g1_ref[...], b1_ref[...])
    x1_ref[...] = x1
    x1p_ref[...] = _pack_bf16_pairs(x1)

    xh, xm, xl = _split3(x1)
    wh, wm, wl = _split3(rw_ref[...])
    logits = (_bdot(xl, wh) + _bdot(xh, wl) + _bdot(xm, wm)) + (_bdot(xm, wh) + _bdot(xh, wm)) + _bdot(xh, wh)
    logits = logits + rb_ref[...]
    lane = lax.broadcasted_iota(jnp.int32, logits.shape, 1)
    lane_f = lane.astype(F32)
    neg_inf = jnp.float32(-jnp.inf)
    cur = jnp.where(lane < n_experts, logits, neg_inf)
    vals, idxs = [], []
    for _ in range(TOP_K):
        m = jnp.max(cur, axis=-1, keepdims=True)
        sel = jnp.min(jnp.where(cur == m, lane_f, float(LANES)), axis=-1, keepdims=True)
        vals.append(m)
        idxs.append(sel)
        cur = jnp.where(lane_f == sel, neg_inf, cur)
    exps = [jnp.exp(v - vals[0]) for v in vals]
    total = sum(exps)

    onehots = [jnp.where(lane_f == idxs[j], 1.0, 0.0) for j in range(TOP_K)]
    chosen = sum(onehots)
    r = lax.broadcasted_iota(jnp.int32, (tm, tm), 0)
    c = lax.broadcasted_iota(jnp.int32, (tm, tm), 1)
    earlier = jnp.where(r > c, 1.0, 0.0).astype(BF16)
    before = _bdot(earlier, chosen.astype(BF16)) + cnt_scr[...]
    ranks = [jnp.sum(onehots[j] * before, axis=-1, keepdims=True) for j in range(TOP_K)]
    cnt_scr[...] += jnp.sum(chosen, axis=0, keepdims=True)
    cnt_ref[...] = cnt_scr[...]

    idx_out = jnp.zeros(logits.shape, F32)
    wt_out = jnp.zeros(logits.shape, F32)
    for j in range(TOP_K):
        idx_out = jnp.where(lane == j, idxs[j], idx_out)
        idx_out = jnp.where(lane == TOP_K + j, ranks[j], idx_out)
        wt_out = jnp.where(lane == j, exps[j] / total, wt_out)
    idx_ref[...] = idx_out.astype(jnp.int32)
    wt_ref[...] = wt_out


def _post(x_head, x_tail, n_head_tiles, tail_tile, o, h, o_tail, h_tail, gate, w, tm, alpha, n_experts):
    d = x_head.shape[1]
    t = (n_head_tiles + 1) * tm
    assert o.shape[0] == t - tm and o_tail.shape[0] == tm and gate.shape[0] == t
    consts = [w["wb0"], w["wb1"], w["wo"], w["ln1g"], w["ln1b"], w["rw"], w["rb"]]
    row = lambda width: pl.BlockSpec((tm, width), lambda i: (i, 0))
    return pl.pallas_call(
        functools.partial(_post_kernel, alpha=alpha, n_experts=n_experts),
        grid=(t // tm,),
        in_specs=_head_tail_specs(n_head_tiles, tail_tile, tm, d) + _head_tail_specs(n_head_tiles, 0, tm, d) +
                 _head_tail_specs(n_head_tiles, 0, tm, d) + [row(2 * d)] + [_weight_spec(c.shape) for c in consts],
        out_specs=[row(d), row(d // 2), row(LANES), row(LANES), _const_spec((1, LANES))],
        out_shape=[jax.ShapeDtypeStruct((t, d), F32),
                   jax.ShapeDtypeStruct((t, d // 2), jnp.uint32),
                   jax.ShapeDtypeStruct((t, LANES), jnp.int32),
                   jax.ShapeDtypeStruct((t, LANES), F32),
                   jax.ShapeDtypeStruct((1, LANES), F32)],
        scratch_shapes=[pltpu.VMEM((1, LANES), F32)],
        compiler_params=_params("arbitrary"),
        name="post",
    )(x_head, x_tail, o, o_tail, h, h_tail, gate, *consts)


SC_INDEX_MAX = 128
SC_ALIGN = 8
SC_BUFFER_BYTES = 224 * 1024


def _sc_plan(n_rows, row_bytes):
    info = plsc.get_sparse_core_info()
    n_workers = info.num_cores * info.num_subcores
    per_worker, rem = divmod(n_rows, n_workers)
    assert rem == 0 and per_worker % SC_ALIGN == 0, "rows must split into aligned equal shares per subcore"
    cap = min(SC_INDEX_MAX, SC_BUFFER_BYTES // row_bytes)
    chunk = max(c for c in range(SC_ALIGN, cap + 1, SC_ALIGN) if per_worker % c == 0)
    return info.num_cores, n_workers, per_worker, chunk


def _sc_gather(table, idx):
    n_rows = idx.shape[0]
    d = table.shape[1]
    n_cores, _, per_worker, chunk = _sc_plan(n_rows, d * table.dtype.itemsize)
    mesh = plsc.VectorSubcoreMesh(core_axis_name="c", subcore_axis_name="s")

    @functools.partial(
        pl.kernel, mesh=mesh, out_type=jax.ShapeDtypeStruct((n_rows, d), table.dtype),
        scratch_types=[pltpu.VMEM((chunk,), jnp.int32), pltpu.VMEM((chunk, d), table.dtype),
                       pltpu.SemaphoreType.DMA])
    def gather(table_hbm, idx_hbm, out_hbm, idx_v, rows_v, sem):
        worker = lax.axis_index("s") * n_cores + lax.axis_index("c")
        base = worker * per_worker

        @pl.loop(0, per_worker // chunk)
        def _(c):
            off = pl.multiple_of(base + c * chunk, SC_ALIGN)
            pltpu.sync_copy(idx_hbm.at[pl.ds(off, chunk)], idx_v)
            pltpu.async_copy(table_hbm.at[idx_v], rows_v, sem).wait()
            pltpu.sync_copy(rows_v, out_hbm.at[pl.ds(off, chunk)])

    return gather(table, idx)


def _sc_scatter(x, pos, n_out):
    n, d = x.shape
    assert pos.shape == (TOP_K * n,)
    n_cores, _, per_worker, chunk = _sc_plan(n, d * x.dtype.itemsize)
    mesh = plsc.VectorSubcoreMesh(core_axis_name="c", subcore_axis_name="s")

    @functools.partial(
        pl.kernel, mesh=mesh, out_type=jax.ShapeDtypeStruct((n_out, d), x.dtype),
        scratch_types=[pltpu.VMEM((chunk, d), x.dtype)] + [pltpu.VMEM((chunk,), jnp.int32)] * TOP_K)
    def scatter(x_hbm, pos_hbm, out_hbm, rows_v, *idx_v):
        worker = lax.axis_index("s") * n_cores + lax.axis_index("c")
        base = worker * per_worker

        @pl.loop(0, per_worker // chunk)
        def _(c):
            off = pl.multiple_of(base + c * chunk, SC_ALIGN)
            pltpu.sync_copy(x_hbm.at[pl.ds(off, chunk)], rows_v)
            for k in range(TOP_K):
                pltpu.sync_copy(pos_hbm.at[pl.ds(k * n + off, chunk)], idx_v[k])
            for k in range(TOP_K):
                pltpu.sync_copy(rows_v, out_hbm.at[idx_v[k]])

    return scatter(x, pos)


def _ffn_kernel(be_ref, nu_ref, x_ref, wgu_ref, bgu_ref, wd_ref, bd_ref, o_ref, wgu_bf, wd_bf):
    i = pl.program_id(0)
    f = wd_ref.shape[2]
    new_expert = jnp.logical_or(i == 0, be_ref[i] != be_ref[jnp.maximum(i - 1, 0)])

    @pl.when(new_expert)
    def _():
        wgu_bf[...] = wgu_ref[0, 0].astype(BF16)
        wd_bf[...] = wd_ref[0, 0].astype(BF16)

    @pl.when(i < nu_ref[0])
    def _():
        x = _unpack_bf16_pairs(x_ref[...]).astype(BF16)
        acc = jnp.zeros((x.shape[0], wd_ref.shape[3]), F32)
        for c in range(0, f, FFN_COLS):
            gate = _bdot(x, wgu_bf[:, c:c + FFN_COLS]) + bgu_ref[0, 0, :, c:c + FFN_COLS]
            up = _bdot(x, wgu_bf[:, f + c:f + c + FFN_COLS]) + bgu_ref[0, 0, :, f + c:f + c + FFN_COLS]
            gate = jnp.minimum(gate, SWIGLU_LIMIT)
            up = jnp.clip(up, -SWIGLU_LIMIT, SWIGLU_LIMIT)
            act = (up + 1.0) * gate * _sigmoid(SWIGLU_ALPHA * gate)
            acc = acc + _bdot(act.astype(BF16), wd_bf[c:c + FFN_COLS, :])
        o_ref[...] = _pack_bf16_pairs(acc + bd_ref[0, 0])

    @pl.when(i >= nu_ref[0])
    def _():
        o_ref[...] = jnp.zeros_like(o_ref)


def _ffn(block_e, n_used, xs, layer, w_gu, b_gu, w_down, b_down, bm):
    n_layers, n_exp, d, f2 = w_gu.shape
    f = f2 // 2
    n_blocks = block_e.shape[0]
    expert = lambda i, be, nu: (layer, be[i], 0, 0)
    grid_spec = pltpu.PrefetchScalarGridSpec(
        num_scalar_prefetch=2,
        grid=(n_blocks,),
        in_specs=[pl.BlockSpec((bm, d // 2), lambda i, be, nu: (i, 0)),
                  pl.BlockSpec((1, 1, d, f2), expert),
                  pl.BlockSpec((1, 1, 1, f2), expert),
                  pl.BlockSpec((1, 1, f, d), expert),
                  pl.BlockSpec((1, 1, 1, d), expert)],
        out_specs=pl.BlockSpec((bm, d // 2), lambda i, be, nu: (i, 0)),
        scratch_shapes=[pltpu.VMEM((d, f2), BF16), pltpu.VMEM((f, d), BF16)],
    )
    return pl.pallas_call(
        _ffn_kernel,
        grid_spec=grid_spec,
        out_shape=jax.ShapeDtypeStruct((n_blocks * bm, d // 2), jnp.uint32),
        compiler_params=_params("arbitrary"),
        name="ffn",
    )(block_e, n_used, xs, w_gu, b_gu.reshape(n_layers, n_exp, 1, f2),
      w_down, b_down.reshape(n_layers, n_exp, 1, d))


def _combine_kernel(x_ref, *refs, alpha):
    y_refs, (wt_ref, g_ref, b_ref, o_ref) = refs[:TOP_K], refs[TOP_K:]
    wt = wt_ref[...]
    ffn = sum(wt[:, j:j + 1] * _unpack_bf16_pairs(y_refs[j][...]) for j in range(TOP_K))
    o_ref[...] = _layernorm(alpha * x_ref[...] + ffn, g_ref[...], b_ref[...])


def _combine(x1, y, wt, g, b, tm, alpha):
    t, d = x1.shape
    planes = [pl.BlockSpec((tm, d // 2), lambda i, j=j: (j * (t // tm) + i, 0)) for j in range(TOP_K)]
    return pl.pallas_call(
        functools.partial(_combine_kernel, alpha=alpha),
        grid=(t // tm,),
        in_specs=[pl.BlockSpec((tm, d), lambda i: (i, 0))] + planes +
                 [pl.BlockSpec((tm, LANES), lambda i: (i, 0)), _const_spec(g.shape), _const_spec(b.shape)],
        out_specs=pl.BlockSpec((tm, d), lambda i: (i, 0)),
        out_shape=jax.ShapeDtypeStruct((t, d), F32),
        compiler_params=_params("arbitrary"),
        name="combine",
    )(x1, *([y] * TOP_K), wt, g, b)


def _route(experts, ranks, counts, bm):
    n_experts = counts.shape[0]
    i32 = jnp.int32
    padded = (counts + bm - 1) // bm * bm
    ends_pad = jnp.cumsum(padded)
    start_pad = ends_pad - padded
    n_blocks = -(-experts.size // bm) + n_experts
    first_row = jnp.arange(n_blocks, dtype=i32)[:, None] * bm
    block_e = jnp.minimum(jnp.sum(ends_pad[None, :] <= first_row, axis=1), n_experts - 1).astype(i32)
    onehot = experts[..., None] == jnp.arange(n_experts, dtype=i32)
    pos = jnp.sum(jnp.where(onehot, start_pad, 0), axis=-1).astype(i32) + ranks
    n_used = (ends_pad[-1:] // bm).astype(i32)
    return block_e, n_used, pos


def _moe(x1, x1p, routing, counts, top_w_pad, w, layer, moe_weights, alpha, tm):
    n_experts = moe_weights[0].shape[1]
    block_e, n_used, pos = _route(routing[:, :TOP_K], routing[:, TOP_K:2 * TOP_K],
                                  counts[0, :n_experts].astype(jnp.int32), MOE_ROWS)
    pos_planes = pos.T.reshape(-1)
    xs = _sc_scatter(x1p, pos_planes, block_e.shape[0] * MOE_ROWS)
    out_rows = _ffn(block_e, n_used, xs, layer, *moe_weights, MOE_ROWS)
    y = _sc_gather(out_rows, pos_planes)
    return _combine(x1, y, top_w_pad, w["ln2g"], w["ln2b"], tm, alpha)


def _layer_weights(layer, w_in, gla_w_a2, gla_b_a, gla_norm_g, rg_conv_w, rg_conv_b, rg_w_a, rg_b_a, rg_w_x,
                   rg_b_x, rg_lambda, b_merge, w_branch, w_o, ln1_g, ln1_b, ln2_g, ln2_b, router_w, router_b,
                   kw, vw):
    d = w_in.shape[1]
    rank = gla_w_a2.shape[1]
    width = rg_conv_w.shape[2]
    n_exp = router_w.shape[2]
    c0 = 2 * kw + 2 * vw
    wi = w_in[layer]
    row = lambda v: v.reshape(1, -1)
    return {
        "wq": wi[:, :c0].astype(BF16),
        "wal": jnp.pad(wi[:, c0:c0 + rank], ((0, 0), (0, LANES - rank))).astype(BF16),
        "wa2": jnp.pad(gla_w_a2[layer], ((0, LANES - rank), (0, 0))).astype(BF16),
        "ba": row(gla_b_a[layer]),
        "wxr": wi[:, c0 + rank:c0 + rank + width].astype(BF16),
        "wmg": wi[:, c0 + rank + width:].astype(BF16),
        "bmg": row(b_merge[layer]),
        "ng": row(gla_norm_g[layer]),
        "cw": rg_conv_w[layer], "cb": row(rg_conv_b[layer]),
        "wax": jnp.concatenate([rg_w_a[layer], rg_w_x[layer]], axis=-1).astype(BF16),
        "rba": row(rg_b_a[layer]), "rbx": row(rg_b_x[layer]), "lam": row(rg_lambda[layer]),
        "wb0": w_branch[layer, 0].astype(BF16), "wb1": w_branch[layer, 1].astype(BF16),
        "wo": w_o[layer].astype(BF16),
        "ln1g": row(ln1_g[layer]), "ln1b": row(ln1_b[layer]),
        "ln2g": row(ln2_g[layer]), "ln2b": row(ln2_b[layer]),
        "rw": jnp.pad(router_w[layer], ((0, 0), (0, LANES - n_exp))),
        "rb": jnp.pad(row(router_b[layer]), ((0, 0), (0, LANES - n_exp))),
    }


def kernel(x_prompt, x_sample, state_gla, state_rglru, state_conv, ln1_g, ln1_b, w_in, gla_w_a2, gla_b_a, gla_norm_g, rg_conv_w, rg_conv_b, rg_w_a, rg_b_a, rg_w_x, rg_b_x, rg_lambda, b_merge, w_branch, w_o, ln2_g, ln2_b, router_w, router_b, moe_w_gu, moe_b_gu, moe_w_down, moe_b_down):
    n_p, seq, d = x_prompt.shape
    n_s, dec_seq, _ = x_sample.shape
    assert dec_seq == 1, "the sample group carries one new token per sequence"
    depth, _, heads, dk, dv = state_gla.shape
    kw, vw = heads * dk, heads * dv
    n_exp = router_w.shape[2]
    alpha = (2.0 * depth) ** 0.25
    t_p = n_p * seq
    pad_rows = BF16_SUBLANES

    tile = PROJ_ROWS
    assert t_p % tile == 0 and n_s <= tile
    n_head_tiles = t_p // tile
    tail_pad = lambda a: jnp.pad(a, ((0, tile - n_s), (0, 0)))
    x_head, x_tail, tail_tile = x_prompt.reshape(t_p, d), tail_pad(x_sample.reshape(n_s, d)), 0
    moe_weights = (moe_w_gu, moe_b_gu, moe_w_down, moe_b_down)
    outs = {k: [] for k in ("gla_p", "rg_p", "cv_p", "gla_s", "rg_s", "cv_s")}
    for layer in range(depth):
        w = _layer_weights(layer, w_in, gla_w_a2, gla_b_a, gla_norm_g, rg_conv_w, rg_conv_b, rg_w_a, rg_b_a,
                           rg_w_x, rg_b_x, rg_lambda, b_merge, w_branch, w_o, ln1_g, ln1_b, ln2_g, ln2_b,
                           router_w, router_b, kw, vw)
        qkvg, loga, xr, gate = _proj(x_head, x_tail, n_head_tiles, tail_tile, w, tile)
        o_p, s_p = _gla(qkvg, loga, w["ng"], None, layer, n_p, seq, SEQ_ROWS, GLA_CHUNK, heads, dk, dv)
        h_p, hl_p, cv_p = _rglru_seq(xr, w, n_p, seq, SEQ_ROWS)
        pad = lambda a: jnp.pad(a[t_p:t_p + n_s, None, :],
                                ((0, 0), (0, pad_rows - 1), (0, 0))).reshape(n_s * pad_rows, -1)
        o_s, s_s = _gla(pad(qkvg), pad(loga), w["ng"], state_gla, layer, n_s, pad_rows,
                        DECODE_SEQS * pad_rows, pad_rows, heads, dk, dv)
        o_s = o_s.reshape(n_s, pad_rows, vw)[:, 0]
        h_s, hn_s, cv_s = _rglru_step(xr[t_p:t_p + n_s], jnp.swapaxes(state_conv[layer], 0, 1),
                                      state_rglru[layer], w)
        x1, x1p, routing, wt, cnt = _post(x_head, x_tail, n_head_tiles, tail_tile, o_p, h_p, tail_pad(o_s),
                                          tail_pad(h_s), gate, w, tile, alpha, n_exp)
        x = _moe(x1, x1p, routing, cnt, wt, w, layer, moe_weights, alpha, tile)
        x_head, x_tail, tail_tile = x, x, n_head_tiles
        outs["gla_p"].append(s_p)
        outs["rg_p"].append(hl_p.reshape(n_p, -1))
        outs["cv_p"].append(cv_p)
        outs["gla_s"].append(s_s)
        outs["rg_s"].append(hn_s)
        outs["cv_s"].append(jnp.swapaxes(cv_s, 0, 1))
    return (x[:t_p].reshape(n_p, seq, d), x[t_p:t_p + n_s].reshape(n_s, dec_seq, d),
            jnp.stack(outs["gla_p"]), jnp.stack(outs["rg_p"]), jnp.stack(outs["cv_p"]),
            jnp.stack(outs["gla_s"]), jnp.stack(outs["rg_s"]), jnp.stack(outs["cv_s"]))
```

```python
import functools

import jax
import jax.numpy as jnp
from jax import lax
from jax.experimental import pallas as pl
from jax.experimental.pallas import tpu as pltpu
from jax.experimental.pallas import tpu_sc as plsc

F32 = jnp.float32
BF16 = jnp.bfloat16

TOP_K = 4
GLA_GATE_TAU = 16.0
GLA_CHUNK = 64
RG_C = 8.0
SWIGLU_LIMIT = 7.0
SWIGLU_ALPHA = 1.702
LN_EPS = 1e-5

LANES = 128
BF16_SUBLANES = 16
VMEM_LIMIT = 56 * 1024 * 1024

PROJ_ROWS = 512
SEQ_ROWS = 256
DECODE_SEQS = 8
MOE_ROWS = 1024
FFN_COLS = 512


def _params(*sem):
    return pltpu.CompilerParams(dimension_semantics=sem, vmem_limit_bytes=VMEM_LIMIT)


def _const_spec(shape):
    nd = len(shape)
    return pl.BlockSpec(shape, lambda *_: (0,) * nd)


def _weight_spec(shape):
    nd = len(shape)
    return pl.BlockSpec(shape, lambda *_: (0,) * nd, pipeline_mode=pl.Buffered(1))


def _bdot(a, b):
    return jnp.dot(a, b, preferred_element_type=F32)


def _split3(x):
    hi = x.astype(BF16)
    r1 = x - hi.astype(F32)
    mid = r1.astype(BF16)
    lo = (r1 - mid.astype(F32)).astype(BF16)
    return hi, mid, lo


def _sigmoid(x):
    return 0.5 * jnp.tanh(0.5 * x) + 0.5


def _log_sigmoid(x):
    return jnp.minimum(x, 0.0) - jnp.log1p(jnp.exp(-jnp.abs(x)))


def _softplus(x):
    return jnp.maximum(x, 0.0) + jnp.log1p(jnp.exp(-jnp.abs(x)))


def _layernorm(x, g, b):
    mu = jnp.mean(x, axis=-1, keepdims=True)
    xc = x - mu
    var = jnp.mean(xc * xc, axis=-1, keepdims=True)
    return xc * lax.rsqrt(var + LN_EPS) * g + b


def _head_tail_specs(n_head_tiles, tail_tile, tm, d):
    return [pl.BlockSpec((tm, d), lambda i: (jnp.minimum(i, n_head_tiles - 1), 0)),
            pl.BlockSpec((tm, d), lambda i: (tail_tile, 0))]


def _head_or_tail(head_ref, tail_ref):
    is_tail = pl.program_id(0) == pl.num_programs(0) - 1
    return jnp.where(is_tail, tail_ref[...], head_ref[...])


def _proj_kernel(x_ref, xt_ref, wq_ref, wal_ref, wa2_ref, ba_ref, wxr_ref, wmg_ref, bmg_ref,
                 qkvg_ref, loga_ref, xr_ref, gate_ref):
    xb = _head_or_tail(x_ref, xt_ref).astype(BF16)
    d = x_ref.shape[1]
    for c in range(0, qkvg_ref.shape[1], d):
        qkvg_ref[:, c:c + d] = _bdot(xb, wq_ref[:, c:c + d]).astype(BF16)
    a_low = _bdot(xb, wal_ref[...])
    z = _bdot(a_low.astype(BF16), wa2_ref[...]) + ba_ref[...]
    loga_ref[...] = _log_sigmoid(z) * (1.0 / GLA_GATE_TAU)
    xr_ref[...] = _bdot(xb, wxr_ref[...])
    for c in range(0, gate_ref.shape[1], d):
        gate_ref[:, c:c + d] = _sigmoid(_bdot(xb, wmg_ref[:, c:c + d]) + bmg_ref[:, c:c + d]).astype(BF16)


def _proj(x_head, x_tail, n_head_tiles, tail_tile, w, tm):
    d = x_head.shape[1]
    t = (n_head_tiles + 1) * tm
    kw = w["wq"].shape[1]
    nk = w["wa2"].shape[1]
    consts = [w["wq"], w["wal"], w["wa2"], w["ba"], w["wxr"], w["wmg"], w["bmg"]]
    return pl.pallas_call(
        _proj_kernel,
        grid=(t // tm,),
        in_specs=_head_tail_specs(n_head_tiles, tail_tile, tm, d) + [_weight_spec(c.shape) for c in consts],
        out_specs=[pl.BlockSpec((tm, kw), lambda i: (i, 0)),
                   pl.BlockSpec((tm, nk), lambda i: (i, 0)),
                   pl.BlockSpec((tm, d), lambda i: (i, 0)),
                   pl.BlockSpec((tm, 2 * d), lambda i: (i, 0))],
        out_shape=[jax.ShapeDtypeStruct((t, kw), BF16),
                   jax.ShapeDtypeStruct((t, nk), F32),
                   jax.ShapeDtypeStruct((t, d), F32),
                   jax.ShapeDtypeStruct((t, 2 * d), BF16)],
        compiler_params=_params("arbitrary"),
        name="proj",
    )(x_head, x_tail, *consts)


def _gla_kernel(*refs, chunk, n_chunks, heads, has_state, n_inputs, state_slices, state_slot):
    q_ref, k_ref, v_ref, la_ref, g_ref, ng_ref = refs[:6]
    s0_ref = refs[6] if has_state else None
    o_ref, so_ref, s_scr, o_scr, p_scr, kv_scr = refs[n_inputs:]
    t = pl.program_id(1)
    dk = q_ref.shape[1] // heads
    dv = v_ref.shape[1] // heads
    scale = dk ** -0.5
    chunks_per_seq = n_chunks // s_scr.shape[0]

    @pl.when(t == 0)
    def _():
        if has_state:
            s_scr[...] = s0_ref[0]
        else:
            s_scr[...] = jnp.zeros_like(s_scr)

    n_rows = n_chunks * chunk
    r = lax.broadcasted_iota(jnp.int32, (n_rows, n_rows), 0)
    c = lax.broadcasted_iota(jnp.int32, (n_rows, n_rows), 1)
    shift = chunk.bit_length() - 1
    assert 1 << shift == chunk, "chunk must be a power of two"
    tril = jnp.where(jnp.logical_and(r >> shift == c >> shift, r >= c), 1.0, 0.0).astype(BF16)
    la_hi, la_mid, la_lo = _split3(la_ref[...])
    b_all = _bdot(tril, la_hi) + _bdot(tril, la_mid) + _bdot(tril, la_lo)
    q_all = (q_ref[...].astype(F32) * scale * jnp.exp(b_all)).astype(BF16)
    k_all = (k_ref[...].astype(F32) * jnp.exp(-b_all)).astype(BF16)
    causal = (lax.broadcasted_iota(jnp.int32, (chunk, chunk), 0) >=
              lax.broadcasted_iota(jnp.int32, (chunk, chunk), 1))

    decays = []
    for ci in range(n_chunks):
        rows = slice(ci * chunk, (ci + 1) * chunk)
        b = b_all[rows, :]
        b_last = b[chunk - 1:chunk, :]
        q_e, k_e = q_all[rows, :], k_all[rows, :]
        k_d = (k_ref[rows, :].astype(F32) * jnp.exp(b_last - b)).astype(BF16)
        decays.append(jnp.exp(b_last))
        for h in range(heads):
            ks = slice(h * dk, (h + 1) * dk)
            v_h = v_ref[rows, h * dv:(h + 1) * dv]
            scores = lax.dot_general(q_e[:, ks], k_e[:, ks], (((1,), (1,)), ((), ())),
                                     preferred_element_type=F32)
            p_scr[rows, h * chunk:(h + 1) * chunk] = jnp.where(causal, scores, 0.0).astype(BF16)
            kv_scr[ci, h] = lax.dot_general(k_d[:, ks], v_h, (((0,), (0,)), ((), ())),
                                            preferred_element_type=F32)

    for ci in range(n_chunks):
        rows = slice(ci * chunk, (ci + 1) * chunk)
        seq = ci // chunks_per_seq
        for h in range(heads):
            ks = slice(h * dk, (h + 1) * dk)
            vs = slice(h * dv, (h + 1) * dv)
            s_old = s_scr[seq, h]
            o_scr[rows, vs] = (_bdot(p_scr[rows, h * chunk:(h + 1) * chunk], v_ref[rows, vs]) +
                               _bdot(q_all[rows, ks], s_old.astype(BF16)))
            s_scr[seq, h] = jnp.transpose(decays[ci][:, ks]) * s_old + kv_scr[ci, h]

    for h in range(heads):
        vs = slice(h * dv, (h + 1) * dv)
        o = o_scr[:, vs]
        mu = jnp.mean(o, axis=-1, keepdims=True)
        oc = o - mu
        var = jnp.mean(oc * oc, axis=-1, keepdims=True)
        on = oc * lax.rsqrt(var + LN_EPS) * ng_ref[:, vs]
        g = g_ref[:, vs].astype(F32)
        o_ref[:, vs] = (on * (g * _sigmoid(g))).astype(BF16)

    @pl.when(t == pl.num_programs(1) - 1)
    def _():
        if state_slices == 0:
            so_ref[...] = s_scr[...]
        for layer_slice in range(state_slices):
            so_ref[layer_slice] = s_scr[...] if layer_slice == state_slot else jnp.zeros_like(s_scr)


def _gla(qkvg, loga, norm_g, s0, layer, nb, s_len, rows, chunk, heads, dk, dv, stack=None):
    t = nb * s_len
    seqs = max(1, rows // s_len)
    assert s_len * seqs % rows == 0 and nb % seqs == 0 and s_len % chunk == 0
    nb, tps = nb // seqs, s_len * seqs // rows
    kw, vw = heads * dk, heads * dv
    has_state = s0 is not None
    state_shape, state_block, state_index = (nb * seqs, heads, dk, dv), (seqs, heads, dk, dv), lambda b, i: (b, 0, 0, 0)
    state_slices, state_slot, aliases = 0, 0, {}
    if stack is not None:
        depth, prev = stack
        state_shape = (depth,) + state_shape
        if prev is None:
            state_slices, state_slot = depth, layer
            state_block, state_index = (depth,) + state_block, lambda b, i: (0, b, 0, 0, 0)
        else:
            state_slices = 1
            state_block, state_index = (1,) + state_block, lambda b, i: (layer, b, 0, 0, 0)
    in_specs = [pl.BlockSpec((rows, kw), lambda b, i: (b * tps + i, 0)),
                pl.BlockSpec((rows, kw), lambda b, i: (b * tps + i, 1)),
                pl.BlockSpec((rows, vw), lambda b, i: (b * tps + i, (2 * kw) // vw)),
                pl.BlockSpec((rows, kw), lambda b, i: (b * tps + i, 0)),
                pl.BlockSpec((rows, vw), lambda b, i: (b * tps + i, (2 * kw) // vw + 1)),
                _const_spec(norm_g.shape)]
    args = [qkvg, qkvg, qkvg, loga, qkvg, norm_g]
    if has_state:
        in_specs.append(pl.BlockSpec((1, seqs, heads, dk, dv), lambda b, i: (layer, b, 0, 0, 0)))
        args.append(s0)
    if stack is not None and stack[1] is not None:
        in_specs.append(pl.BlockSpec(memory_space=pl.ANY))
        aliases[len(args)] = 1
        args.append(stack[1])
    return pl.pallas_call(
        functools.partial(_gla_kernel, chunk=chunk, n_chunks=rows // chunk, heads=heads, has_state=has_state,
                          n_inputs=len(args), state_slices=state_slices, state_slot=state_slot),
        grid=(nb, tps),
        in_specs=in_specs,
        out_specs=[pl.BlockSpec((rows, vw), lambda b, i: (b * tps + i, 0)),
                   pl.BlockSpec(state_block, state_index)],
        out_shape=[jax.ShapeDtypeStruct((t, vw), BF16), jax.ShapeDtypeStruct(state_shape, F32)],
        scratch_shapes=[pltpu.VMEM((seqs, heads, dk, dv), F32), pltpu.VMEM((rows, vw), F32),
                        pltpu.VMEM((rows, heads * chunk), BF16), pltpu.VMEM((rows // chunk, heads, dk, dv), F32)],
        input_output_aliases=aliases,
        compiler_params=_params("arbitrary", "arbitrary"),
        name="gla",
    )(*args)


def _rg_gates(xc, wax_ref, ba_ref, bx_ref, lam_ref):
    n_blocks, bw, _ = wax_ref.shape
    r_parts, i_parts = [], []
    for n in range(n_blocks):
        cs = slice(n * bw, (n + 1) * bw)
        pre = _bdot(xc[:, cs].astype(BF16), wax_ref[n])
        r_parts.append(_sigmoid(pre[:, :bw] + ba_ref[:, cs]))
        i_parts.append(_sigmoid(pre[:, bw:] + bx_ref[:, cs]))
    r = jnp.concatenate(r_parts, axis=1)
    i = jnp.concatenate(i_parts, axis=1)
    log_a = -RG_C * r * _softplus(-lam_ref[...])
    a = jnp.exp(log_a)
    bx = jnp.sqrt(jnp.tanh(-log_a) * (1.0 + a * a)) * (i * xc)
    return a, bx


def _rglru_seq_kernel(xr_ref, cw_ref, cb_ref, wax_ref, ba_ref, bx_ref, lam_ref,
                      h_ref, hl_ref, cv_ref, cbuf, a_scr, b_scr, h_scr, hc):
    t = pl.program_id(1)
    rows = xr_ref.shape[0]
    taps = cw_ref.shape[0]
    head = 8

    @pl.when(t == 0)
    def _():
        cbuf[0:head, :] = jnp.zeros((head, cbuf.shape[1]), F32)
        hc[...] = jnp.zeros_like(hc)

    cbuf[head:head + rows, :] = xr_ref[...]
    first = head - (taps - 1)
    xc = cb_ref[...] + sum(cbuf[first + j:first + j + rows, :] * cw_ref[j:j + 1, :] for j in range(taps))
    tail = cbuf[first + rows:head + rows, :]
    cbuf[first:head, :] = tail

    a, bx = _rg_gates(xc, wax_ref, ba_ref, bx_ref, lam_ref)
    a_scr[...] = a
    b_scr[...] = bx

    def step(i, h):
        h = a_scr[pl.ds(i, 1), :] * h + b_scr[pl.ds(i, 1), :]
        h_scr[pl.ds(i, 1), :] = h
        return h

    h_last = lax.fori_loop(0, rows, step, hc[...], unroll=8)
    hc[...] = h_last
    h_ref[...] = h_scr[...].astype(BF16)

    @pl.when(t == pl.num_programs(1) - 1)
    def _():
        hl_ref[0] = h_last
        cv_ref[0] = tail


def _rglru_seq(xr, w, nb, s_len, rows):
    width = xr.shape[1]
    t = nb * s_len
    tps = s_len // rows
    taps = w["cw"].shape[0]
    consts = [w["cw"], w["cb"], w["wax"], w["rba"], w["rbx"], w["lam"]]
    return pl.pallas_call(
        _rglru_seq_kernel,
        grid=(nb, tps),
        in_specs=[pl.BlockSpec((rows, width), lambda b, i: (b * tps + i, 0))] + [_const_spec(c.shape) for c in consts],
        out_specs=[pl.BlockSpec((rows, width), lambda b, i: (b * tps + i, 0)),
                   pl.BlockSpec((1, 1, width), lambda b, i: (b, 0, 0)),
                   pl.BlockSpec((1, taps - 1, width), lambda b, i: (b, 0, 0))],
        out_shape=[jax.ShapeDtypeStruct((t, width), BF16),
                   jax.ShapeDtypeStruct((nb, 1, width), F32),
                   jax.ShapeDtypeStruct((nb, taps - 1, width), F32)],
        scratch_shapes=[pltpu.VMEM((rows + 8, width), F32), pltpu.VMEM((rows, width), F32),
                        pltpu.VMEM((rows, width), F32), pltpu.VMEM((rows, width), F32),
                        pltpu.VMEM((1, width), F32)],
        compiler_params=_params("arbitrary", "arbitrary"),
        name="rglru_seq",
    )(xr, *consts)


def _rglru_step_kernel(xr_ref, sc_ref, h0_ref, cw_ref, cb_ref, wax_ref, ba_ref, bx_ref, lam_ref,
                       h_ref, hn_ref, cv_ref):
    taps = cw_ref.shape[0]
    xr = xr_ref[...]
    xc = cb_ref[...] + sum(sc_ref[j] * cw_ref[j:j + 1, :] for j in range(taps - 1)) + xr * cw_ref[taps - 1:taps, :]
    a, bx = _rg_gates(xc, wax_ref, ba_ref, bx_ref, lam_ref)
    h = a * h0_ref[...] + bx
    h_ref[...] = h.astype(BF16)
    hn_ref[...] = h
    for j in range(taps - 2):
        cv_ref[j] = sc_ref[j + 1]
    cv_ref[taps - 2] = xr


def _rglru_step(xr, conv_t, h0, w):
    n, width = xr.shape
    consts = [w["cw"], w["cb"], w["wax"], w["rba"], w["rbx"], w["lam"]]
    args = [xr, conv_t, h0] + consts
    return pl.pallas_call(
        _rglru_step_kernel,
        grid=(1,),
        in_specs=[_const_spec(a.shape) for a in args],
        out_specs=[_const_spec((n, width)), _const_spec((n, width)), _const_spec(conv_t.shape)],
        out_shape=[jax.ShapeDtypeStruct((n, width), BF16),
                   jax.ShapeDtypeStruct((n, width), F32),
                   jax.ShapeDtypeStruct(conv_t.shape, F32)],
        compiler_params=_params("arbitrary"),
        name="rglru_step",
    )(*args)


def _pack_bf16_pairs(x):
    half = x.shape[1] // 2
    hi = pltpu.bitcast(x[:, :half].astype(BF16).astype(F32), jnp.uint32)
    lo = pltpu.bitcast(x[:, half:].astype(BF16).astype(F32), jnp.uint32)
    return hi | (lo >> 16)


def _unpack_bf16_pairs(p):
    hi = pltpu.bitcast(p & jnp.uint32(0xFFFF0000), F32)
    lo = pltpu.bitcast(p << 16, F32)
    return jnp.concatenate([hi, lo], axis=1)


def _post_kernel(x_ref, xt_ref, o_ref, ot_ref, h_ref, ht_ref, gate_ref, wb0_ref, wb1_ref, wo_ref, g1_ref, b1_ref,
                 rw_ref, rb_ref, x1_ref, x1p_ref, idx_ref, wt_ref, cnt_ref, cnt_scr, *, alpha, n_experts):
    d = x_ref.shape[1]
    tm = x_ref.shape[0]

    @pl.when(pl.program_id(0) == 0)
    def _():
        cnt_scr[...] = jnp.zeros_like(cnt_scr)

    x = _head_or_tail(x_ref, xt_ref)
    o = _head_or_tail(o_ref, ot_ref)
    h = _head_or_tail(h_ref, ht_ref)
    gate = gate_ref[...].astype(F32)
    merged = gate[:, :d] * _bdot(o, wb0_ref[...]) + gate[:, d:] * _bdot(h, wb1_ref[...])
    mix = _bdot(merged.astype(BF16), wo_ref[...])
    x1 = _layernorm(alpha * x + mix, g1_ref[...], b1_ref[...])
    x1_ref[...] = x1
    x1p_ref[...] = _pack_bf16_pairs(x1)

    xh, xm, xl = _split3(x1)
    wh, wm, wl = _split3(rw_ref[...])
    logits = (_bdot(xl, wh) + _bdot(xh, wl) + _bdot(xm, wm)) + (_bdot(xm, wh) + _bdot(xh, wm)) + _bdot(xh, wh)
    logits = logits + rb_ref[...]
    lane = lax.broadcasted_iota(jnp.int32, logits.shape, 1)
    lane_f = lane.astype(F32)
    neg_inf = jnp.float32(-jnp.inf)
    cur = jnp.where(lane < n_experts, logits, neg_inf)
    vals, idxs = [], []
    for _ in range(TOP_K):
        m = jnp.max(cur, axis=-1, keepdims=True)
        sel = jnp.min(jnp.where(cur == m, lane_f, float(LANES)), axis=-1, keepdims=True)
        vals.append(m)
        idxs.append(sel)
        cur = jnp.where(lane_f == sel, neg_inf, cur)
    exps = [jnp.exp(v - vals[0]) for v in vals]
    total = sum(exps)

    onehots = [jnp.where(lane_f == idxs[j], 1.0, 0.0) for j in range(TOP_K)]
    chosen = sum(onehots)
    r = lax.broadcasted_iota(jnp.int32, (tm, tm), 0)
    c = lax.broadcasted_iota(jnp.int32, (tm, tm), 1)
    earlier = jnp.where(r > c, 1.0, 0.0).astype(BF16)
    before = _bdot(earlier, chosen.astype(BF16)) + cnt_scr[...]
    ranks = [jnp.sum(onehots[j] * before, axis=-1, keepdims=True) for j in range(TOP_K)]
    cnt_scr[...] += jnp.sum(chosen, axis=0, keepdims=True)
    cnt_ref[...] = cnt_scr[...]

    idx_out = jnp.zeros(logits.shape, F32)
    wt_out = jnp.zeros(logits.shape, F32)
    for j in range(TOP_K):
        idx_out = jnp.where(lane == j, idxs[j], idx_out)
        idx_out = jnp.where(lane == TOP_K + j, ranks[j], idx_out)
        wt_out = jnp.where(lane == j, exps[j] / total, wt_out)
    idx_ref[...] = idx_out.astype(jnp.int32)
    wt_ref[...] = wt_out


def _post(x_head, x_tail, n_head_tiles, tail_tile, o, h, o_tail, h_tail, gate, w, tm, alpha, n_experts):
    d = x_head.shape[1]
    t = (n_head_tiles + 1) * tm
    assert o.shape[0] == t - tm and o_tail.shape[0] == tm and gate.shape[0] == t
    consts = [w["wb0"], w["wb1"], w["wo"], w["ln1g"], w["ln1b"], w["rw"], w["rb"]]
    row = lambda width: pl.BlockSpec((tm, width), lambda i: (i, 0))
    return pl.pallas_call(
        functools.partial(_post_kernel, alpha=alpha, n_experts=n_experts),
        grid=(t // tm,),
        in_specs=_head_tail_specs(n_head_tiles, tail_tile, tm, d) + _head_tail_specs(n_head_tiles, 0, tm, d) +
                 _head_tail_specs(n_head_tiles, 0, tm, d) + [row(2 * d)] + [_weight_spec(c.shape) for c in consts],
        out_specs=[row(d), row(d // 2), row(LANES), row(LANES), _const_spec((1, LANES))],
        out_shape=[jax.ShapeDtypeStruct((t, d), F32),
                   jax.ShapeDtypeStruct((t, d // 2), jnp.uint32),
                   jax.ShapeDtypeStruct((t, LANES), jnp.int32),
                   jax.ShapeDtypeStruct((t, LANES), F32),
                   jax.ShapeDtypeStruct((1, LANES), F32)],
        scratch_shapes=[pltpu.VMEM((1, LANES), F32)],
        compiler_params=_params("arbitrary"),
        name="post",
    )(x_head, x_tail, o, o_tail, h, h_tail, gate, *consts)


SC_INDEX_MAX = 128
SC_ALIGN = 8
SC_BUFFER_BYTES = 224 * 1024


def _sc_plan(n_rows, row_bytes):
    info = plsc.get_sparse_core_info()
    n_workers = info.num_cores * info.num_subcores
    per_worker, rem = divmod(n_rows, n_workers)
    assert rem == 0 and per_worker % SC_ALIGN == 0, "rows must split into aligned equal shares per subcore"
    cap = min(SC_INDEX_MAX, SC_BUFFER_BYTES // row_bytes)
    chunk = max(c for c in range(SC_ALIGN, cap + 1, SC_ALIGN) if per_worker % c == 0)
    return info.num_cores, n_workers, per_worker, chunk


def _sc_gather(table, idx):
    n_rows = idx.shape[0]
    d = table.shape[1]
    n_cores, _, per_worker, chunk = _sc_plan(n_rows, d * table.dtype.itemsize)
    mesh = plsc.VectorSubcoreMesh(core_axis_name="c", subcore_axis_name="s")

    @functools.partial(
        pl.kernel, mesh=mesh, out_type=jax.ShapeDtypeStruct((n_rows, d), table.dtype),
        scratch_types=[pltpu.VMEM((chunk,), jnp.int32), pltpu.VMEM((chunk, d), table.dtype),
                       pltpu.SemaphoreType.DMA])
    def gather(table_hbm, idx_hbm, out_hbm, idx_v, rows_v, sem):
        worker = lax.axis_index("s") * n_cores + lax.axis_index("c")
        base = worker * per_worker

        @pl.loop(0, per_worker // chunk)
        def _(c):
            off = pl.multiple_of(base + c * chunk, SC_ALIGN)
            pltpu.sync_copy(idx_hbm.at[pl.ds(off, chunk)], idx_v)
            pltpu.async_copy(table_hbm.at[idx_v], rows_v, sem).wait()
            pltpu.sync_copy(rows_v, out_hbm.at[pl.ds(off, chunk)])

    return gather(table, idx)


def _sc_scatter(x, pos, n_out):
    n, d = x.shape
    assert pos.shape == (TOP_K * n,)
    n_cores, _, per_worker, chunk = _sc_plan(n, d * x.dtype.itemsize)
    mesh = plsc.VectorSubcoreMesh(core_axis_name="c", subcore_axis_name="s")

    @functools.partial(
        pl.kernel, mesh=mesh, out_type=jax.ShapeDtypeStruct((n_out, d), x.dtype),
        scratch_types=[pltpu.VMEM((chunk, d), x.dtype)] + [pltpu.VMEM((chunk,), jnp.int32)] * TOP_K)
    def scatter(x_hbm, pos_hbm, out_hbm, rows_v, *idx_v):
        worker = lax.axis_index("s") * n_cores + lax.axis_index("c")
        base = worker * per_worker

        @pl.loop(0, per_worker // chunk)
        def _(c):
            off = pl.multiple_of(base + c * chunk, SC_ALIGN)
            pltpu.sync_copy(x_hbm.at[pl.ds(off, chunk)], rows_v)
            for k in range(TOP_K):
                pltpu.sync_copy(pos_hbm.at[pl.ds(k * n + off, chunk)], idx_v[k])
            for k in range(TOP_K):
                pltpu.sync_copy(rows_v, out_hbm.at[idx_v[k]])

    return scatter(x, pos)


def _ffn_kernel(be_ref, nu_ref, x_ref, wgu_ref, bgu_ref, wd_ref, bd_ref, o_ref, wgu_bf, wd_bf):
    i = pl.program_id(0)
    f = wd_ref.shape[2]
    new_expert = jnp.logical_or(i == 0, be_ref[i] != be_ref[jnp.maximum(i - 1, 0)])

    @pl.when(new_expert)
    def _():
        wgu_bf[...] = wgu_ref[0, 0].astype(BF16)
        wd_bf[...] = wd_ref[0, 0].astype(BF16)

    @pl.when(i < nu_ref[0])
    def _():
        x = _unpack_bf16_pairs(x_ref[...]).astype(BF16)
        acc = jnp.zeros((x.shape[0], wd_ref.shape[3]), F32)
        for c in range(0, f, FFN_COLS):
            gate = _bdot(x, wgu_bf[:, c:c + FFN_COLS]) + bgu_ref[0, 0, :, c:c + FFN_COLS]
            up = _bdot(x, wgu_bf[:, f + c:f + c + FFN_COLS]) + bgu_ref[0, 0, :, f + c:f + c + FFN_COLS]
            gate = jnp.minimum(gate, SWIGLU_LIMIT)
            up = jnp.clip(up, -SWIGLU_LIMIT, SWIGLU_LIMIT)
            act = (up + 1.0) * gate * _sigmoid(SWIGLU_ALPHA * gate)
            acc = acc + _bdot(act.astype(BF16), wd_bf[c:c + FFN_COLS, :])
        o_ref[...] = _pack_bf16_pairs(acc + bd_ref[0, 0])

    @pl.when(i >= nu_ref[0])
    def _():
        o_ref[...] = jnp.zeros_like(o_ref)


def _ffn(block_e, n_used, xs, layer, w_gu, b_gu, w_down, b_down, bm):
    n_layers, n_exp, d, f2 = w_gu.shape
    f = f2 // 2
    n_blocks = block_e.shape[0]
    expert = lambda i, be, nu: (layer, be[i], 0, 0)
    grid_spec = pltpu.PrefetchScalarGridSpec(
        num_scalar_prefetch=2,
        grid=(n_blocks,),
        in_specs=[pl.BlockSpec((bm, d // 2), lambda i, be, nu: (i, 0)),
                  pl.BlockSpec((1, 1, d, f2), expert),
                  pl.BlockSpec((1, 1, 1, f2), expert),
                  pl.BlockSpec((1, 1, f, d), expert),
                  pl.BlockSpec((1, 1, 1, d), expert)],
        out_specs=pl.BlockSpec((bm, d // 2), lambda i, be, nu: (i, 0)),
        scratch_shapes=[pltpu.VMEM((d, f2), BF16), pltpu.VMEM((f, d), BF16)],
    )
    return pl.pallas_call(
        _ffn_kernel,
        grid_spec=grid_spec,
        out_shape=jax.ShapeDtypeStruct((n_blocks * bm, d // 2), jnp.uint32),
        compiler_params=_params("arbitrary"),
        name="ffn",
    )(block_e, n_used, xs, w_gu, b_gu.reshape(n_layers, n_exp, 1, f2),
      w_down, b_down.reshape(n_layers, n_exp, 1, d))


def _combine_kernel(x_ref, *refs, alpha, split):
    y_refs, (wt_ref, g_ref, b_ref), out_refs = refs[:TOP_K], refs[TOP_K:TOP_K + 3], refs[TOP_K + 3:]
    wt = wt_ref[...]
    ffn = sum(wt[:, j:j + 1] * _unpack_bf16_pairs(y_refs[j][...]) for j in range(TOP_K))
    res = _layernorm(alpha * x_ref[...] + ffn, g_ref[...], b_ref[...])
    if not split:
        out_refs[0][...] = res
    else:
        is_tail = pl.program_id(0) == pl.num_programs(0) - 1

        @pl.when(jnp.logical_not(is_tail))
        def _():
            out_refs[0][...] = res

        @pl.when(is_tail)
        def _():
            out_refs[1][...] = res


def _combine(x1, y, wt, g, b, tm, alpha, split=False):
    t, d = x1.shape
    n_tiles = t // tm
    planes = [pl.BlockSpec((tm, d // 2), lambda i, j=j: (j * n_tiles + i, 0)) for j in range(TOP_K)]
    if split:
        out_specs = [pl.BlockSpec((tm, d), lambda i: (jnp.minimum(i, n_tiles - 2), 0)), _const_spec((tm, d))]
        out_shape = [jax.ShapeDtypeStruct((t - tm, d), F32), jax.ShapeDtypeStruct((tm, d), F32)]
    else:
        out_specs = pl.BlockSpec((tm, d), lambda i: (i, 0))
        out_shape = jax.ShapeDtypeStruct((t, d), F32)
    return pl.pallas_call(
        functools.partial(_combine_kernel, alpha=alpha, split=split),
        grid=(n_tiles,),
        in_specs=[pl.BlockSpec((tm, d), lambda i: (i, 0))] + planes +
                 [pl.BlockSpec((tm, LANES), lambda i: (i, 0)), _const_spec(g.shape), _const_spec(b.shape)],
        out_specs=out_specs,
        out_shape=out_shape,
        compiler_params=_params("arbitrary"),
        name="combine",
    )(x1, *([y] * TOP_K), wt, g, b)


def _route(experts, ranks, counts, bm):
    n_experts = counts.shape[0]
    i32 = jnp.int32
    padded = (counts + bm - 1) // bm * bm
    ends_pad = jnp.cumsum(padded)
    start_pad = ends_pad - padded
    n_blocks = -(-experts.size // bm) + n_experts
    first_row = jnp.arange(n_blocks, dtype=i32)[:, None] * bm
    block_e = jnp.minimum(jnp.sum(ends_pad[None, :] <= first_row, axis=1), n_experts - 1).astype(i32)
    onehot = experts[..., None] == jnp.arange(n_experts, dtype=i32)
    pos = jnp.sum(jnp.where(onehot, start_pad, 0), axis=-1).astype(i32) + ranks
    n_used = (ends_pad[-1:] // bm).astype(i32)
    return block_e, n_used, pos


def _moe(x1, x1p, routing, counts, top_w_pad, w, layer, moe_weights, alpha, tm, split):
    n_experts = moe_weights[0].shape[1]
    block_e, n_used, pos = _route(routing[:, :TOP_K], routing[:, TOP_K:2 * TOP_K],
                                  counts[0, :n_experts].astype(jnp.int32), MOE_ROWS)
    pos_planes = pos.T.reshape(-1)
    xs = _sc_scatter(x1p, pos_planes, block_e.shape[0] * MOE_ROWS)
    out_rows = _ffn(block_e, n_used, xs, layer, *moe_weights, MOE_ROWS)
    y = _sc_gather(out_rows, pos_planes)
    return _combine(x1, y, top_w_pad, w["ln2g"], w["ln2b"], tm, alpha, split)


def _layer_weights(layer, w_in, gla_w_a2, gla_b_a, gla_norm_g, rg_conv_w, rg_conv_b, rg_w_a, rg_b_a, rg_w_x,
                   rg_b_x, rg_lambda, b_merge, w_branch, w_o, ln1_g, ln1_b, ln2_g, ln2_b, router_w, router_b,
                   kw, vw):
    d = w_in.shape[1]
    rank = gla_w_a2.shape[1]
    width = rg_conv_w.shape[2]
    n_exp = router_w.shape[2]
    c0 = 2 * kw + 2 * vw
    wi = w_in[layer]
    row = lambda v: v.reshape(1, -1)
    return {
        "wq": wi[:, :c0].astype(BF16),
        "wal": jnp.pad(wi[:, c0:c0 + rank], ((0, 0), (0, LANES - rank))).astype(BF16),
        "wa2": jnp.pad(gla_w_a2[layer], ((0, LANES - rank), (0, 0))).astype(BF16),
        "ba": row(gla_b_a[layer]),
        "wxr": wi[:, c0 + rank:c0 + rank + width].astype(BF16),
        "wmg": wi[:, c0 + rank + width:].astype(BF16),
        "bmg": row(b_merge[layer]),
        "ng": row(gla_norm_g[layer]),
        "cw": rg_conv_w[layer], "cb": row(rg_conv_b[layer]),
        "wax": jnp.concatenate([rg_w_a[layer], rg_w_x[layer]], axis=-1).astype(BF16),
        "rba": row(rg_b_a[layer]), "rbx": row(rg_b_x[layer]), "lam": row(rg_lambda[layer]),
        "wb0": w_branch[layer, 0].astype(BF16), "wb1": w_branch[layer, 1].astype(BF16),
        "wo": w_o[layer].astype(BF16),
        "ln1g": row(ln1_g[layer]), "ln1b": row(ln1_b[layer]),
        "ln2g": row(ln2_g[layer]), "ln2b": row(ln2_b[layer]),
        "rw": jnp.pad(router_w[layer], ((0, 0), (0, LANES - n_exp))),
        "rb": jnp.pad(row(router_b[layer]), ((0, 0), (0, LANES - n_exp))),
    }


def kernel(x_prompt, x_sample, state_gla, state_rglru, state_conv, ln1_g, ln1_b, w_in, gla_w_a2, gla_b_a, gla_norm_g, rg_conv_w, rg_conv_b, rg_w_a, rg_b_a, rg_w_x, rg_b_x, rg_lambda, b_merge, w_branch, w_o, ln2_g, ln2_b, router_w, router_b, moe_w_gu, moe_b_gu, moe_w_down, moe_b_down):
    n_p, seq, d = x_prompt.shape
    n_s, dec_seq, _ = x_sample.shape
    assert dec_seq == 1, "the sample group carries one new token per sequence"
    depth, _, heads, dk, dv = state_gla.shape
    kw, vw = heads * dk, heads * dv
    n_exp = router_w.shape[2]
    alpha = (2.0 * depth) ** 0.25
    t_p = n_p * seq
    pad_rows = BF16_SUBLANES

    tile = PROJ_ROWS
    assert t_p % tile == 0 and n_s <= tile
    n_head_tiles = t_p // tile
    tail_pad = lambda a: jnp.pad(a, ((0, tile - n_s), (0, 0)))
    x_head, x_tail, tail_tile = x_prompt.reshape(t_p, d), tail_pad(x_sample.reshape(n_s, d)), 0
    moe_weights = (moe_w_gu, moe_b_gu, moe_w_down, moe_b_down)
    outs = {k: [] for k in ("gla_p", "rg_p", "cv_p", "rg_s", "cv_s")}
    gla_s = None
    for layer in range(depth):
        w = _layer_weights(layer, w_in, gla_w_a2, gla_b_a, gla_norm_g, rg_conv_w, rg_conv_b, rg_w_a, rg_b_a,
                           rg_w_x, rg_b_x, rg_lambda, b_merge, w_branch, w_o, ln1_g, ln1_b, ln2_g, ln2_b,
                           router_w, router_b, kw, vw)
        qkvg, loga, xr, gate = _proj(x_head, x_tail, n_head_tiles, tail_tile, w, tile)
        o_p, s_p = _gla(qkvg, loga, w["ng"], None, layer, n_p, seq, SEQ_ROWS, GLA_CHUNK, heads, dk, dv)
        h_p, hl_p, cv_p = _rglru_seq(xr, w, n_p, seq, SEQ_ROWS)
        pad = lambda a: jnp.pad(a[t_p:t_p + n_s, None, :],
                                ((0, 0), (0, pad_rows - 1), (0, 0))).reshape(n_s * pad_rows, -1)
        o_s, gla_s = _gla(pad(qkvg), pad(loga), w["ng"], state_gla, layer, n_s, pad_rows,
                          DECODE_SEQS * pad_rows, pad_rows, heads, dk, dv, stack=(depth, gla_s))
        o_s = o_s.reshape(n_s, pad_rows, vw)[:, 0]
        h_s, hn_s, cv_s = _rglru_step(xr[t_p:t_p + n_s], jnp.swapaxes(state_conv[layer], 0, 1),
                                      state_rglru[layer], w)
        x1, x1p, routing, wt, cnt = _post(x_head, x_tail, n_head_tiles, tail_tile, o_p, h_p, tail_pad(o_s),
                                          tail_pad(h_s), gate, w, tile, alpha, n_exp)
        last = layer == depth - 1
        x = _moe(x1, x1p, routing, cnt, wt, w, layer, moe_weights, alpha, tile, split=last)
        if not last:
            x_head, x_tail, tail_tile = x, x, n_head_tiles
        outs["gla_p"].append(s_p)
        outs["rg_p"].append(hl_p.reshape(n_p, -1))
        outs["cv_p"].append(cv_p)
        outs["rg_s"].append(hn_s)
        outs["cv_s"].append(jnp.swapaxes(cv_s, 0, 1))
    y_head, y_tail = x
    return (y_head.reshape(n_p, seq, d), y_tail[:n_s].reshape(n_s, dec_seq, d),
            jnp.stack(outs["gla_p"]), jnp.stack(outs["rg_p"]), jnp.stack(outs["cv_p"]),
            gla_s, jnp.stack(outs["rg_s"]), jnp.stack(outs["cv_s"]))
```

```python
import functools

import jax
import jax.numpy as jnp
from jax import lax
from jax.experimental import pallas as pl
from jax.experimental.pallas import tpu as pltpu
from jax.experimental.pallas import tpu_sc as plsc

F32 = jnp.float32
BF16 = jnp.bfloat16

TOP_K = 4
GLA_GATE_TAU = 16.0
GLA_CHUNK = 64
RG_C = 8.0
SWIGLU_LIMIT = 7.0
SWIGLU_ALPHA = 1.702
LN_EPS = 1e-5

LANES = 128
BF16_SUBLANES = 16
VMEM_LIMIT = 56 * 1024 * 1024

PROJ_ROWS = 512
SEQ_ROWS = 256
DECODE_SEQS = 8
MOE_ROWS = 1024
FFN_COLS = 512
FFN_SUB_ROWS = 256


def _params(*sem):
    return pltpu.CompilerParams(dimension_semantics=sem, vmem_limit_bytes=VMEM_LIMIT)


def _const_spec(shape):
    nd = len(shape)
    return pl.BlockSpec(shape, lambda *_: (0,) * nd)


def _weight_spec(shape):
    nd = len(shape)
    return pl.BlockSpec(shape, lambda *_: (0,) * nd, pipeline_mode=pl.Buffered(1))


def _bdot(a, b):
    return jnp.dot(a, b, preferred_element_type=F32)


def _split3(x):
    hi = x.astype(BF16)
    r1 = x - hi.astype(F32)
    mid = r1.astype(BF16)
    lo = (r1 - mid.astype(F32)).astype(BF16)
    return hi, mid, lo


def _sigmoid(x):
    return 0.5 * jnp.tanh(0.5 * x) + 0.5


def _log_sigmoid(x):
    return jnp.minimum(x, 0.0) - jnp.log1p(jnp.exp(-jnp.abs(x)))


def _softplus(x):
    return jnp.maximum(x, 0.0) + jnp.log1p(jnp.exp(-jnp.abs(x)))


def _layernorm(x, g, b):
    mu = jnp.mean(x, axis=-1, keepdims=True)
    xc = x - mu
    var = jnp.mean(xc * xc, axis=-1, keepdims=True)
    return xc * lax.rsqrt(var + LN_EPS) * g + b


def _head_tail_specs(n_head_tiles, tail_tile, tm, d):
    return [pl.BlockSpec((tm, d), lambda i: (jnp.minimum(i, n_head_tiles - 1), 0)),
            pl.BlockSpec((tm, d), lambda i: (tail_tile, 0))]


def _head_or_tail(head_ref, tail_ref):
    is_tail = pl.program_id(0) == pl.num_programs(0) - 1
    return jnp.where(is_tail, tail_ref[...], head_ref[...])


def _proj_kernel(x_ref, xt_ref, wq_ref, wal_ref, wa2_ref, ba_ref, wxr_ref, wmg_ref, bmg_ref,
                 qkvg_ref, loga_ref, xr_ref, gate_ref):
    xb = _head_or_tail(x_ref, xt_ref).astype(BF16)
    d = x_ref.shape[1]
    for c in range(0, qkvg_ref.shape[1], d):
        qkvg_ref[:, c:c + d] = _bdot(xb, wq_ref[:, c:c + d]).astype(BF16)
    a_low = _bdot(xb, wal_ref[...])
    z = _bdot(a_low.astype(BF16), wa2_ref[...]) + ba_ref[...]
    loga_ref[...] = _log_sigmoid(z) * (1.0 / GLA_GATE_TAU)
    xr_ref[...] = _bdot(xb, wxr_ref[...])
    for c in range(0, gate_ref.shape[1], d):
        gate_ref[:, c:c + d] = _sigmoid(_bdot(xb, wmg_ref[:, c:c + d]) + bmg_ref[:, c:c + d]).astype(BF16)


def _proj(x_head, x_tail, n_head_tiles, tail_tile, w, tm):
    d = x_head.shape[1]
    t = (n_head_tiles + 1) * tm
    kw = w["wq"].shape[1]
    nk = w["wa2"].shape[1]
    consts = [w["wq"], w["wal"], w["wa2"], w["ba"], w["wxr"], w["wmg"], w["bmg"]]
    return pl.pallas_call(
        _proj_kernel,
        grid=(t // tm,),
        in_specs=_head_tail_specs(n_head_tiles, tail_tile, tm, d) + [_weight_spec(c.shape) for c in consts],
        out_specs=[pl.BlockSpec((tm, kw), lambda i: (i, 0)),
                   pl.BlockSpec((tm, nk), lambda i: (i, 0)),
                   pl.BlockSpec((tm, d), lambda i: (i, 0)),
                   pl.BlockSpec((tm, 2 * d), lambda i: (i, 0))],
        out_shape=[jax.ShapeDtypeStruct((t, kw), BF16),
                   jax.ShapeDtypeStruct((t, nk), F32),
                   jax.ShapeDtypeStruct((t, d), F32),
                   jax.ShapeDtypeStruct((t, 2 * d), BF16)],
        compiler_params=_params("arbitrary"),
        name="proj",
    )(x_head, x_tail, *consts)


def _gla_kernel(*refs, chunk, n_chunks, heads, has_state, n_inputs, state_slices, state_slot):
    q_ref, k_ref, v_ref, la_ref, g_ref, ng_ref = refs[:6]
    s0_ref = refs[6] if has_state else None
    o_ref, so_ref, s_scr, o_scr, p_scr, kv_scr = refs[n_inputs:]
    t = pl.program_id(1)
    dk = q_ref.shape[1] // heads
    dv = v_ref.shape[1] // heads
    scale = dk ** -0.5
    chunks_per_seq = n_chunks // s_scr.shape[0]

    @pl.when(t == 0)
    def _():
        if has_state:
            s_scr[...] = s0_ref[0]
        else:
            s_scr[...] = jnp.zeros_like(s_scr)

    n_rows = n_chunks * chunk
    r = lax.broadcasted_iota(jnp.int32, (n_rows, n_rows), 0)
    c = lax.broadcasted_iota(jnp.int32, (n_rows, n_rows), 1)
    shift = chunk.bit_length() - 1
    assert 1 << shift == chunk, "chunk must be a power of two"
    tril = jnp.where(jnp.logical_and(r >> shift == c >> shift, r >= c), 1.0, 0.0).astype(BF16)
    la_hi, la_mid, la_lo = _split3(la_ref[...])
    b_all = _bdot(tril, la_hi) + _bdot(tril, la_mid) + _bdot(tril, la_lo)
    q_all = (q_ref[...].astype(F32) * scale * jnp.exp(b_all)).astype(BF16)
    k_all = (k_ref[...].astype(F32) * jnp.exp(-b_all)).astype(BF16)
    causal = (lax.broadcasted_iota(jnp.int32, (chunk, chunk), 0) >=
              lax.broadcasted_iota(jnp.int32, (chunk, chunk), 1))

    decays = []
    for ci in range(n_chunks):
        rows = slice(ci * chunk, (ci + 1) * chunk)
        b = b_all[rows, :]
        b_last = b[chunk - 1:chunk, :]
        q_e, k_e = q_all[rows, :], k_all[rows, :]
        k_d = (k_ref[rows, :].astype(F32) * jnp.exp(b_last - b)).astype(BF16)
        decays.append(jnp.exp(b_last))
        for h in range(heads):
            ks = slice(h * dk, (h + 1) * dk)
            v_h = v_ref[rows, h * dv:(h + 1) * dv]
            scores = lax.dot_general(q_e[:, ks], k_e[:, ks], (((1,), (1,)), ((), ())),
                                     preferred_element_type=F32)
            p_scr[rows, h * chunk:(h + 1) * chunk] = jnp.where(causal, scores, 0.0).astype(BF16)
            kv_scr[ci, h] = lax.dot_general(k_d[:, ks], v_h, (((0,), (0,)), ((), ())),
                                            preferred_element_type=F32)

    for ci in range(n_chunks):
        rows = slice(ci * chunk, (ci + 1) * chunk)
        seq = ci // chunks_per_seq
        for h in range(heads):
            ks = slice(h * dk, (h + 1) * dk)
            vs = slice(h * dv, (h + 1) * dv)
            s_old = s_scr[seq, h]
            o_scr[rows, vs] = (_bdot(p_scr[rows, h * chunk:(h + 1) * chunk], v_ref[rows, vs]) +
                               _bdot(q_all[rows, ks], s_old.astype(BF16)))
            s_scr[seq, h] = jnp.transpose(decays[ci][:, ks]) * s_old + kv_scr[ci, h]

    for h in range(heads):
        vs = slice(h * dv, (h + 1) * dv)
        o = o_scr[:, vs]
        mu = jnp.mean(o, axis=-1, keepdims=True)
        oc = o - mu
        var = jnp.mean(oc * oc, axis=-1, keepdims=True)
        on = oc * lax.rsqrt(var + LN_EPS) * ng_ref[:, vs]
        g = g_ref[:, vs].astype(F32)
        o_ref[:, vs] = (on * (g * _sigmoid(g))).astype(BF16)

    @pl.when(t == pl.num_programs(1) - 1)
    def _():
        if state_slices == 0:
            so_ref[...] = s_scr[...]
        for layer_slice in range(state_slices):
            so_ref[layer_slice] = s_scr[...] if layer_slice == state_slot else jnp.zeros_like(s_scr)


def _gla(qkvg, loga, norm_g, s0, layer, nb, s_len, rows, chunk, heads, dk, dv, stack=None):
    t = nb * s_len
    seqs = max(1, rows // s_len)
    assert s_len * seqs % rows == 0 and nb % seqs == 0 and s_len % chunk == 0
    nb, tps = nb // seqs, s_len * seqs // rows
    kw, vw = heads * dk, heads * dv
    has_state = s0 is not None
    state_shape, state_block, state_index = (nb * seqs, heads, dk, dv), (seqs, heads, dk, dv), lambda b, i: (b, 0, 0, 0)
    state_slices, state_slot, aliases = 0, 0, {}
    if stack is not None:
        depth, prev = stack
        state_shape = (depth,) + state_shape
        if prev is None:
            state_slices, state_slot = depth, layer
            state_block, state_index = (depth,) + state_block, lambda b, i: (0, b, 0, 0, 0)
        else:
            state_slices = 1
            state_block, state_index = (1,) + state_block, lambda b, i: (layer, b, 0, 0, 0)
    in_specs = [pl.BlockSpec((rows, kw), lambda b, i: (b * tps + i, 0)),
                pl.BlockSpec((rows, kw), lambda b, i: (b * tps + i, 1)),
                pl.BlockSpec((rows, vw), lambda b, i: (b * tps + i, (2 * kw) // vw)),
                pl.BlockSpec((rows, kw), lambda b, i: (b * tps + i, 0)),
                pl.BlockSpec((rows, vw), lambda b, i: (b * tps + i, (2 * kw) // vw + 1)),
                _const_spec(norm_g.shape)]
    args = [qkvg, qkvg, qkvg, loga, qkvg, norm_g]
    if has_state:
        in_specs.append(pl.BlockSpec((1, seqs, heads, dk, dv), lambda b, i: (layer, b, 0, 0, 0)))
        args.append(s0)
    if stack is not None and stack[1] is not None:
        in_specs.append(pl.BlockSpec(memory_space=pl.ANY))
        aliases[len(args)] = 1
        args.append(stack[1])
    return pl.pallas_call(
        functools.partial(_gla_kernel, chunk=chunk, n_chunks=rows // chunk, heads=heads, has_state=has_state,
                          n_inputs=len(args), state_slices=state_slices, state_slot=state_slot),
        grid=(nb, tps),
        in_specs=in_specs,
        out_specs=[pl.BlockSpec((rows, vw), lambda b, i: (b * tps + i, 0)),
                   pl.BlockSpec(state_block, state_index)],
        out_shape=[jax.ShapeDtypeStruct((t, vw), BF16), jax.ShapeDtypeStruct(state_shape, F32)],
        scratch_shapes=[pltpu.VMEM((seqs, heads, dk, dv), F32), pltpu.VMEM((rows, vw), F32),
                        pltpu.VMEM((rows, heads * chunk), BF16), pltpu.VMEM((rows // chunk, heads, dk, dv), F32)],
        input_output_aliases=aliases,
        compiler_params=_params("arbitrary", "arbitrary"),
        name="gla",
    )(*args)


def _rg_gates(xc, wax_ref, ba_ref, bx_ref, lam_ref):
    n_blocks, bw, _ = wax_ref.shape
    r_parts, i_parts = [], []
    for n in range(n_blocks):
        cs = slice(n * bw, (n + 1) * bw)
        pre = _bdot(xc[:, cs].astype(BF16), wax_ref[n])
        r_parts.append(_sigmoid(pre[:, :bw] + ba_ref[:, cs]))
        i_parts.append(_sigmoid(pre[:, bw:] + bx_ref[:, cs]))
    r = jnp.concatenate(r_parts, axis=1)
    i = jnp.concatenate(i_parts, axis=1)
    log_a = -RG_C * r * _softplus(-lam_ref[...])
    a = jnp.exp(log_a)
    bx = jnp.sqrt(jnp.tanh(-log_a) * (1.0 + a * a)) * (i * xc)
    return a, bx


def _rglru_seq_kernel(xr_ref, cw_ref, cb_ref, wax_ref, ba_ref, bx_ref, lam_ref,
                      h_ref, hl_ref, cv_ref, cbuf, a_scr, b_scr, h_scr, hc):
    t = pl.program_id(1)
    rows = xr_ref.shape[0]
    taps = cw_ref.shape[0]
    head = 8

    @pl.when(t == 0)
    def _():
        cbuf[0:head, :] = jnp.zeros((head, cbuf.shape[1]), F32)
        hc[...] = jnp.zeros_like(hc)

    cbuf[head:head + rows, :] = xr_ref[...]
    first = head - (taps - 1)
    xc = cb_ref[...] + sum(cbuf[first + j:first + j + rows, :] * cw_ref[j:j + 1, :] for j in range(taps))
    tail = cbuf[first + rows:head + rows, :]
    cbuf[first:head, :] = tail

    a, bx = _rg_gates(xc, wax_ref, ba_ref, bx_ref, lam_ref)
    a_scr[...] = a
    b_scr[...] = bx

    def step(i, h):
        h = a_scr[pl.ds(i, 1), :] * h + b_scr[pl.ds(i, 1), :]
        h_scr[pl.ds(i, 1), :] = h
        return h

    h_last = lax.fori_loop(0, rows, step, hc[...], unroll=8)
    hc[...] = h_last
    h_ref[...] = h_scr[...].astype(BF16)

    @pl.when(t == pl.num_programs(1) - 1)
    def _():
        hl_ref[0] = h_last
        cv_ref[0] = tail


def _rglru_seq(xr, w, nb, s_len, rows):
    width = xr.shape[1]
    t = nb * s_len
    tps = s_len // rows
    taps = w["cw"].shape[0]
    consts = [w["cw"], w["cb"], w["wax"], w["rba"], w["rbx"], w["lam"]]
    return pl.pallas_call(
        _rglru_seq_kernel,
        grid=(nb, tps),
        in_specs=[pl.BlockSpec((rows, width), lambda b, i: (b * tps + i, 0))] + [_const_spec(c.shape) for c in consts],
        out_specs=[pl.BlockSpec((rows, width), lambda b, i: (b * tps + i, 0)),
                   pl.BlockSpec((1, 1, width), lambda b, i: (b, 0, 0)),
                   pl.BlockSpec((1, taps - 1, width), lambda b, i: (b, 0, 0))],
        out_shape=[jax.ShapeDtypeStruct((t, width), BF16),
                   jax.ShapeDtypeStruct((nb, 1, width), F32),
                   jax.ShapeDtypeStruct((nb, taps - 1, width), F32)],
        scratch_shapes=[pltpu.VMEM((rows + 8, width), F32), pltpu.VMEM((rows, width), F32),
                        pltpu.VMEM((rows, width), F32), pltpu.VMEM((rows, width), F32),
                        pltpu.VMEM((1, width), F32)],
        compiler_params=_params("arbitrary", "arbitrary"),
        name="rglru_seq",
    )(xr, *consts)


def _rglru_step_kernel(xr_ref, sc_ref, h0_ref, cw_ref, cb_ref, wax_ref, ba_ref, bx_ref, lam_ref,
                       h_ref, hn_ref, cv_ref):
    taps = cw_ref.shape[0]
    xr = xr_ref[...]
    xc = cb_ref[...] + sum(sc_ref[j] * cw_ref[j:j + 1, :] for j in range(taps - 1)) + xr * cw_ref[taps - 1:taps, :]
    a, bx = _rg_gates(xc, wax_ref, ba_ref, bx_ref, lam_ref)
    h = a * h0_ref[...] + bx
    h_ref[...] = h.astype(BF16)
    hn_ref[...] = h
    for j in range(taps - 2):
        cv_ref[j] = sc_ref[j + 1]
    cv_ref[taps - 2] = xr


def _rglru_step(xr, conv_t, h0, w):
    n, width = xr.shape
    consts = [w["cw"], w["cb"], w["wax"], w["rba"], w["rbx"], w["lam"]]
    args = [xr, conv_t, h0] + consts
    return pl.pallas_call(
        _rglru_step_kernel,
        grid=(1,),
        in_specs=[_const_spec(a.shape) for a in args],
        out_specs=[_const_spec((n, width)), _const_spec((n, width)), _const_spec(conv_t.shape)],
        out_shape=[jax.ShapeDtypeStruct((n, width), BF16),
                   jax.ShapeDtypeStruct((n, width), F32),
                   jax.ShapeDtypeStruct(conv_t.shape, F32)],
        compiler_params=_params("arbitrary"),
        name="rglru_step",
    )(*args)


def _pack_bf16_pairs(x):
    half = x.shape[1] // 2
    hi = pltpu.bitcast(x[:, :half].astype(BF16).astype(F32), jnp.uint32)
    lo = pltpu.bitcast(x[:, half:].astype(BF16).astype(F32), jnp.uint32)
    return hi | (lo >> 16)


def _unpack_bf16_pairs(p):
    hi = pltpu.bitcast(p & jnp.uint32(0xFFFF0000), F32)
    lo = pltpu.bitcast(p << 16, F32)
    return jnp.concatenate([hi, lo], axis=1)


def _post_kernel(x_ref, xt_ref, o_ref, ot_ref, h_ref, ht_ref, gate_ref, wb0_ref, wb1_ref, wo_ref, g1_ref, b1_ref,
                 rw_ref, rb_ref, x1_ref, x1p_ref, idx_ref, wt_ref, cnt_ref, cnt_scr, *, alpha, n_experts):
    d = x_ref.shape[1]
    tm = x_ref.shape[0]

    @pl.when(pl.program_id(0) == 0)
    def _():
        cnt_scr[...] = jnp.zeros_like(cnt_scr)

    x = _head_or_tail(x_ref, xt_ref)
    o = _head_or_tail(o_ref, ot_ref)
    h = _head_or_tail(h_ref, ht_ref)
    gate = gate_ref[...].astype(F32)
    merged = gate[:, :d] * _bdot(o, wb0_ref[...]) + gate[:, d:] * _bdot(h, wb1_ref[...])
    mix = _bdot(merged.astype(BF16), wo_ref[...])
    x1 = _layernorm(alpha * x + mix, g1_ref[...], b1_ref[...])
    x1_ref[...] = x1
    x1p_ref[...] = _pack_bf16_pairs(x1)

    xh = x1.astype(BF16)
    xl = (x1 - xh.astype(F32)).astype(BF16)
    n_lanes = rw_ref.shape[1] // 2
    hi_terms = _bdot(xh, rw_ref[...])
    logits = (hi_terms[:, n_lanes:] + _bdot(xl, rw_ref[:, :n_lanes])) + hi_terms[:, :n_lanes]
    logits = logits + rb_ref[...]
    lane = lax.broadcasted_iota(jnp.int32, logits.shape, 1)
    lane_f = lane.astype(F32)
    neg_inf = jnp.float32(-jnp.inf)
    cur = jnp.where(lane < n_experts, logits, neg_inf)
    vals, idxs = [], []
    for _ in range(TOP_K):
        m = jnp.max(cur, axis=-1, keepdims=True)
        sel = jnp.min(jnp.where(cur == m, lane_f, float(LANES)), axis=-1, keepdims=True)
        vals.append(m)
        idxs.append(sel)
        cur = jnp.where(lane_f == sel, neg_inf, cur)
    exps = [jnp.exp(v - vals[0]) for v in vals]
    total = sum(exps)

    onehots = [jnp.where(lane_f == idxs[j], 1.0, 0.0) for j in range(TOP_K)]
    chosen = sum(onehots)
    r = lax.broadcasted_iota(jnp.int32, (tm, tm), 0)
    c = lax.broadcasted_iota(jnp.int32, (tm, tm), 1)
    earlier = jnp.where(r > c, 1.0, 0.0).astype(BF16)
    before = _bdot(earlier, chosen.astype(BF16)) + cnt_scr[...]
    ranks = [jnp.sum(onehots[j] * before, axis=-1, keepdims=True) for j in range(TOP_K)]
    cnt_scr[...] += jnp.sum(chosen, axis=0, keepdims=True)
    cnt_ref[...] = cnt_scr[...]

    idx_out = jnp.zeros(logits.shape, F32)
    wt_out = jnp.zeros(logits.shape, F32)
    for j in range(TOP_K):
        idx_out = jnp.where(lane == j, idxs[j], idx_out)
        idx_out = jnp.where(lane == TOP_K + j, ranks[j], idx_out)
        wt_out = jnp.where(lane == j, exps[j] / total, wt_out)
    idx_ref[...] = idx_out.astype(jnp.int32)
    wt_ref[...] = wt_out


def _post(x_head, x_tail, n_head_tiles, tail_tile, o, h, o_tail, h_tail, gate, w, tm, alpha, n_experts):
    d = x_head.shape[1]
    t = (n_head_tiles + 1) * tm
    assert o.shape[0] == t - tm and o_tail.shape[0] == tm and gate.shape[0] == t
    consts = [w["wb0"], w["wb1"], w["wo"], w["ln1g"], w["ln1b"], w["rw"], w["rb"]]
    row = lambda width: pl.BlockSpec((tm, width), lambda i: (i, 0))
    return pl.pallas_call(
        functools.partial(_post_kernel, alpha=alpha, n_experts=n_experts),
        grid=(t // tm,),
        in_specs=_head_tail_specs(n_head_tiles, tail_tile, tm, d) + _head_tail_specs(n_head_tiles, 0, tm, d) +
                 _head_tail_specs(n_head_tiles, 0, tm, d) + [row(2 * d)] + [_weight_spec(c.shape) for c in consts],
        out_specs=[row(d), row(d // 2), row(LANES), row(LANES), _const_spec((1, LANES))],
        out_shape=[jax.ShapeDtypeStruct((t, d), F32),
                   jax.ShapeDtypeStruct((t, d // 2), jnp.uint32),
                   jax.ShapeDtypeStruct((t, LANES), jnp.int32),
                   jax.ShapeDtypeStruct((t, LANES), F32),
                   jax.ShapeDtypeStruct((1, LANES), F32)],
        scratch_shapes=[pltpu.VMEM((1, LANES), F32)],
        compiler_params=_params("arbitrary"),
        name="post",
    )(x_head, x_tail, o, o_tail, h, h_tail, gate, *consts)


SC_INDEX_MAX = 128
SC_ALIGN = 8
SC_BUFFER_BYTES = 224 * 1024


def _sc_plan(n_rows, row_bytes):
    info = plsc.get_sparse_core_info()
    n_workers = info.num_cores * info.num_subcores
    per_worker, rem = divmod(n_rows, n_workers)
    assert rem == 0 and per_worker % SC_ALIGN == 0, "rows must split into aligned equal shares per subcore"
    cap = min(SC_INDEX_MAX, SC_BUFFER_BYTES // row_bytes)
    chunk = max(c for c in range(SC_ALIGN, cap + 1, SC_ALIGN) if per_worker % c == 0)
    return info.num_cores, n_workers, per_worker, chunk


def _sc_gather(table, idx):
    n_rows = idx.shape[0]
    d = table.shape[1]
    n_cores, _, per_worker, chunk = _sc_plan(n_rows, d * table.dtype.itemsize)
    mesh = plsc.VectorSubcoreMesh(core_axis_name="c", subcore_axis_name="s")

    @functools.partial(
        pl.kernel, mesh=mesh, out_type=jax.ShapeDtypeStruct((n_rows, d), table.dtype),
        scratch_types=[pltpu.VMEM((chunk,), jnp.int32), pltpu.VMEM((chunk, d), table.dtype),
                       pltpu.SemaphoreType.DMA])
    def gather(table_hbm, idx_hbm, out_hbm, idx_v, rows_v, sem):
        worker = lax.axis_index("s") * n_cores + lax.axis_index("c")
        base = worker * per_worker

        @pl.loop(0, per_worker // chunk)
        def _(c):
            off = pl.multiple_of(base + c * chunk, SC_ALIGN)
            pltpu.sync_copy(idx_hbm.at[pl.ds(off, chunk)], idx_v)
            pltpu.async_copy(table_hbm.at[idx_v], rows_v, sem).wait()
            pltpu.sync_copy(rows_v, out_hbm.at[pl.ds(off, chunk)])

    return gather(table, idx)


def _sc_scatter(x, pos, n_out):
    n, d = x.shape
    assert pos.shape == (TOP_K * n,)
    n_cores, _, per_worker, chunk = _sc_plan(n, d * x.dtype.itemsize)
    mesh = plsc.VectorSubcoreMesh(core_axis_name="c", subcore_axis_name="s")

    @functools.partial(
        pl.kernel, mesh=mesh, out_type=jax.ShapeDtypeStruct((n_out, d), x.dtype),
        scratch_types=[pltpu.VMEM((chunk, d), x.dtype)] + [pltpu.VMEM((chunk,), jnp.int32)] * TOP_K)
    def scatter(x_hbm, pos_hbm, out_hbm, rows_v, *idx_v):
        worker = lax.axis_index("s") * n_cores + lax.axis_index("c")
        base = worker * per_worker

        @pl.loop(0, per_worker // chunk)
        def _(c):
            off = pl.multiple_of(base + c * chunk, SC_ALIGN)
            pltpu.sync_copy(x_hbm.at[pl.ds(off, chunk)], rows_v)
            for k in range(TOP_K):
                pltpu.sync_copy(pos_hbm.at[pl.ds(k * n + off, chunk)], idx_v[k])
            for k in range(TOP_K):
                pltpu.sync_copy(rows_v, out_hbm.at[idx_v[k]])

    return scatter(x, pos)


def _ffn_kernel(be_ref, valid_ref, x_ref, wgu_ref, bgu_ref, wd_ref, bd_ref, o_ref, wgu_bf, wd_bf):
    i = pl.program_id(0)
    bm = x_ref.shape[0]
    f = wd_ref.shape[2]
    valid = valid_ref[i]
    new_expert = jnp.logical_or(i == 0, be_ref[i] != be_ref[jnp.maximum(i - 1, 0)])

    @pl.when(new_expert)
    def _():
        wgu_bf[...] = wgu_ref[0, 0].astype(BF16)
        wd_bf[...] = wd_ref[0, 0].astype(BF16)

    def expert_rows(rows):
        x = _unpack_bf16_pairs(x_ref[rows, :]).astype(BF16)
        acc = jnp.zeros((x.shape[0], wd_ref.shape[3]), F32)
        for c in range(0, f, FFN_COLS):
            gate = _bdot(x, wgu_bf[:, c:c + FFN_COLS]) + bgu_ref[0, 0, :, c:c + FFN_COLS]
            up = _bdot(x, wgu_bf[:, f + c:f + c + FFN_COLS]) + bgu_ref[0, 0, :, f + c:f + c + FFN_COLS]
            gate = jnp.minimum(gate, SWIGLU_LIMIT)
            up = jnp.clip(up, -SWIGLU_LIMIT, SWIGLU_LIMIT)
            act = (up + 1.0) * gate * _sigmoid(SWIGLU_ALPHA * gate)
            acc = acc + _bdot(act.astype(BF16), wd_bf[c:c + FFN_COLS, :])
        o_ref[rows, :] = _pack_bf16_pairs(acc + bd_ref[0, 0])

    @pl.when(valid == bm)
    def _():
        expert_rows(slice(0, bm))

    @pl.when(valid < bm)
    def _():
        for s in range(0, bm, FFN_SUB_ROWS):
            rows = slice(s, s + FFN_SUB_ROWS)

            @pl.when(s < valid)
            def _():
                expert_rows(rows)

            @pl.when(s >= valid)
            def _():
                o_ref[rows, :] = jnp.zeros((FFN_SUB_ROWS, o_ref.shape[1]), o_ref.dtype)


def _ffn(block_e, valid, xs, layer, w_gu, b_gu, w_down, b_down, bm):
    n_layers, n_exp, d, f2 = w_gu.shape
    f = f2 // 2
    n_blocks = block_e.shape[0]
    expert = lambda i, be, nu: (layer, be[i], 0, 0)
    grid_spec = pltpu.PrefetchScalarGridSpec(
        num_scalar_prefetch=2,
        grid=(n_blocks,),
        in_specs=[pl.BlockSpec((bm, d // 2), lambda i, be, nu: (i, 0)),
                  pl.BlockSpec((1, 1, d, f2), expert),
                  pl.BlockSpec((1, 1, 1, f2), expert),
                  pl.BlockSpec((1, 1, f, d), expert),
                  pl.BlockSpec((1, 1, 1, d), expert)],
        out_specs=pl.BlockSpec((bm, d // 2), lambda i, be, nu: (i, 0)),
        scratch_shapes=[pltpu.VMEM((d, f2), BF16), pltpu.VMEM((f, d), BF16)],
    )
    return pl.pallas_call(
        _ffn_kernel,
        grid_spec=grid_spec,
        out_shape=jax.ShapeDtypeStruct((n_blocks * bm, d // 2), jnp.uint32),
        compiler_params=_params("arbitrary"),
        name="ffn",
    )(block_e, valid, xs, w_gu, b_gu.reshape(n_layers, n_exp, 1, f2),
      w_down, b_down.reshape(n_layers, n_exp, 1, d))


def _combine_kernel(x_ref, *refs, alpha, split):
    y_refs, (wt_ref, g_ref, b_ref), out_refs = refs[:TOP_K], refs[TOP_K:TOP_K + 3], refs[TOP_K + 3:]
    wt = wt_ref[...]
    ffn = sum(wt[:, j:j + 1] * _unpack_bf16_pairs(y_refs[j][...]) for j in range(TOP_K))
    res = _layernorm(alpha * x_ref[...] + ffn, g_ref[...], b_ref[...])
    if not split:
        out_refs[0][...] = res
    else:
        is_tail = pl.program_id(0) == pl.num_programs(0) - 1

        @pl.when(jnp.logical_not(is_tail))
        def _():
            out_refs[0][...] = res

        @pl.when(is_tail)
        def _():
            out_refs[1][...] = res


def _combine(x1, y, wt, g, b, tm, alpha, split=False):
    t, d = x1.shape
    n_tiles = t // tm
    planes = [pl.BlockSpec((tm, d // 2), lambda i, j=j: (j * n_tiles + i, 0)) for j in range(TOP_K)]
    if split:
        out_specs = [pl.BlockSpec((tm, d), lambda i: (jnp.minimum(i, n_tiles - 2), 0)), _const_spec((tm, d))]
        out_shape = [jax.ShapeDtypeStruct((t - tm, d), F32), jax.ShapeDtypeStruct((tm, d), F32)]
    else:
        out_specs = pl.BlockSpec((tm, d), lambda i: (i, 0))
        out_shape = jax.ShapeDtypeStruct((t, d), F32)
    return pl.pallas_call(
        functools.partial(_combine_kernel, alpha=alpha, split=split),
        grid=(n_tiles,),
        in_specs=[pl.BlockSpec((tm, d), lambda i: (i, 0))] + planes +
                 [pl.BlockSpec((tm, LANES), lambda i: (i, 0)), _const_spec(g.shape), _const_spec(b.shape)],
        out_specs=out_specs,
        out_shape=out_shape,
        compiler_params=_params("arbitrary"),
        name="combine",
    )(x1, *([y] * TOP_K), wt, g, b)


def _route(experts, ranks, counts, bm):
    n_experts = counts.shape[0]
    i32 = jnp.int32
    padded = (counts + bm - 1) // bm * bm
    ends_pad = jnp.cumsum(padded)
    start_pad = ends_pad - padded
    n_blocks = -(-experts.size // bm) + n_experts
    first_row = jnp.arange(n_blocks, dtype=i32)[:, None] * bm
    block_e = jnp.minimum(jnp.sum(ends_pad[None, :] <= first_row, axis=1), n_experts - 1).astype(i32)
    valid = jnp.clip(counts[block_e] - (first_row[:, 0] - start_pad[block_e]), 0, bm).astype(i32)
    onehot = experts[..., None] == jnp.arange(n_experts, dtype=i32)
    pos = jnp.sum(jnp.where(onehot, start_pad, 0), axis=-1).astype(i32) + ranks
    return block_e, valid, pos


def _moe(x1, x1p, routing, counts, top_w_pad, w, layer, moe_weights, alpha, tm, split):
    n_experts = moe_weights[0].shape[1]
    block_e, valid, pos = _route(routing[:, :TOP_K], routing[:, TOP_K:2 * TOP_K],
                                 counts[0, :n_experts].astype(jnp.int32), MOE_ROWS)
    pos_planes = pos.T.reshape(-1)
    xs = _sc_scatter(x1p, pos_planes, block_e.shape[0] * MOE_ROWS)
    out_rows = _ffn(block_e, valid, xs, layer, *moe_weights, MOE_ROWS)
    y = _sc_gather(out_rows, pos_planes)
    return _combine(x1, y, top_w_pad, w["ln2g"], w["ln2b"], tm, alpha, split)


def _hi_lo(w):
    hi = w.astype(BF16)
    return jnp.concatenate([hi, (w - hi.astype(F32)).astype(BF16)], axis=1)


def _layer_weights(layer, w_in, gla_w_a2, gla_b_a, gla_norm_g, rg_conv_w, rg_conv_b, rg_w_a, rg_b_a, rg_w_x,
                   rg_b_x, rg_lambda, b_merge, w_branch, w_o, ln1_g, ln1_b, ln2_g, ln2_b, router_w, router_b,
                   kw, vw):
    d = w_in.shape[1]
    rank = gla_w_a2.shape[1]
    width = rg_conv_w.shape[2]
    n_exp = router_w.shape[2]
    c0 = 2 * kw + 2 * vw
    wi = w_in[layer]
    row = lambda v: v.reshape(1, -1)
    return {
        "wq": wi[:, :c0].astype(BF16),
        "wal": jnp.pad(wi[:, c0:c0 + rank], ((0, 0), (0, LANES - rank))).astype(BF16),
        "wa2": jnp.pad(gla_w_a2[layer], ((0, LANES - rank), (0, 0))).astype(BF16),
        "ba": row(gla_b_a[layer]),
        "wxr": wi[:, c0 + rank:c0 + rank + width].astype(BF16),
        "wmg": wi[:, c0 + rank + width:].astype(BF16),
        "bmg": row(b_merge[layer]),
        "ng": row(gla_norm_g[layer]),
        "cw": rg_conv_w[layer], "cb": row(rg_conv_b[layer]),
        "wax": jnp.concatenate([rg_w_a[layer], rg_w_x[layer]], axis=-1).astype(BF16),
        "rba": row(rg_b_a[layer]), "rbx": row(rg_b_x[layer]), "lam": row(rg_lambda[layer]),
        "wb0": w_branch[layer, 0].astype(BF16), "wb1": w_branch[layer, 1].astype(BF16),
        "wo": w_o[layer].astype(BF16),
        "ln1g": row(ln1_g[layer]), "ln1b": row(ln1_b[layer]),
        "ln2g": row(ln2_g[layer]), "ln2b": row(ln2_b[layer]),
        "rw": _hi_lo(jnp.pad(router_w[layer], ((0, 0), (0, LANES - n_exp)))),
        "rb": jnp.pad(row(router_b[layer]), ((0, 0), (0, LANES - n_exp))),
    }


def kernel(x_prompt, x_sample, state_gla, state_rglru, state_conv, ln1_g, ln1_b, w_in, gla_w_a2, gla_b_a, gla_norm_g, rg_conv_w, rg_conv_b, rg_w_a, rg_b_a, rg_w_x, rg_b_x, rg_lambda, b_merge, w_branch, w_o, ln2_g, ln2_b, router_w, router_b, moe_w_gu, moe_b_gu, moe_w_down, moe_b_down):
    n_p, seq, d = x_prompt.shape
    n_s, dec_seq, _ = x_sample.shape
    assert dec_seq == 1, "the sample group carries one new token per sequence"
    depth, _, heads, dk, dv = state_gla.shape
    kw, vw = heads * dk, heads * dv
    n_exp = router_w.shape[2]
    alpha = (2.0 * depth) ** 0.25
    t_p = n_p * seq
    pad_rows = BF16_SUBLANES

    tile = PROJ_ROWS
    assert t_p % tile == 0 and n_s <= tile
    n_head_tiles = t_p // tile
    tail_pad = lambda a: jnp.pad(a, ((0, tile - n_s), (0, 0)))
    x_head, x_tail, tail_tile = x_prompt.reshape(t_p, d), tail_pad(x_sample.reshape(n_s, d)), 0
    moe_weights = (moe_w_gu, moe_b_gu, moe_w_down, moe_b_down)
    outs = {k: [] for k in ("gla_p", "rg_p", "cv_p", "rg_s", "cv_s")}
    gla_s = None
    for layer in range(depth):
        w = _layer_weights(layer, w_in, gla_w_a2, gla_b_a, gla_norm_g, rg_conv_w, rg_conv_b, rg_w_a, rg_b_a,
                           rg_w_x, rg_b_x, rg_lambda, b_merge, w_branch, w_o, ln1_g, ln1_b, ln2_g, ln2_b,
                           router_w, router_b, kw, vw)
        qkvg, loga, xr, gate = _proj(x_head, x_tail, n_head_tiles, tail_tile, w, tile)
        o_p, s_p = _gla(qkvg, loga, w["ng"], None, layer, n_p, seq, SEQ_ROWS, GLA_CHUNK, heads, dk, dv)
        h_p, hl_p, cv_p = _rglru_seq(xr, w, n_p, seq, SEQ_ROWS)
        pad = lambda a: jnp.pad(a[t_p:t_p + n_s, None, :],
                                ((0, 0), (0, pad_rows - 1), (0, 0))).reshape(n_s * pad_rows, -1)
        o_s, gla_s = _gla(pad(qkvg), pad(loga), w["ng"], state_gla, layer, n_s, pad_rows,
                          DECODE_SEQS * pad_rows, pad_rows, heads, dk, dv, stack=(depth, gla_s))
        o_s = o_s.reshape(n_s, pad_rows, vw)[:, 0]
        h_s, hn_s, cv_s = _rglru_step(xr[t_p:t_p + n_s], jnp.swapaxes(state_conv[layer], 0, 1),
                                      state_rglru[layer], w)
        x1, x1p, routing, wt, cnt = _post(x_head, x_tail, n_head_tiles, tail_tile, o_p, h_p, tail_pad(o_s),
                                          tail_pad(h_s), gate, w, tile, alpha, n_exp)
        last = layer == depth - 1
        x = _moe(x1, x1p, routing, cnt, wt, w, layer, moe_weights, alpha, tile, split=last)
        if not last:
            x_head, x_tail, tail_tile = x, x, n_head_tiles
        outs["gla_p"].append(s_p)
        outs["rg_p"].append(hl_p.reshape(n_p, -1))
        outs["cv_p"].append(cv_p)
        outs["rg_s"].append(hn_s)
        outs["cv_s"].append(jnp.swapaxes(cv_s, 0, 1))
    y_head, y_tail = x
    return (y_head.reshape(n_p, seq, d), y_tail[:n_s].reshape(n_s, dec_seq, d),
            jnp.stack(outs["gla_p"]), jnp.stack(outs["rg_p"]), jnp.stack(outs["cv_p"]),
            gla_s, jnp.stack(outs["rg_s"]), jnp.stack(outs["cv_s"]))
```

```python
import functools

import jax
import jax.numpy as jnp
from jax import lax
from jax.experimental import pallas as pl
from jax.experimental.pallas import tpu as pltpu
from jax.experimental.pallas import tpu_sc as plsc

F32 = jnp.float32
BF16 = jnp.bfloat16

TOP_K = 4
GLA_GATE_TAU = 16.0
GLA_CHUNK = 64
RG_C = 8.0
SWIGLU_LIMIT = 7.0
SWIGLU_ALPHA = 1.702
LN_EPS = 1e-5

LANES = 128
BF16_SUBLANES = 16
VMEM_LIMIT = 56 * 1024 * 1024

PROJ_ROWS = 512
SEQ_ROWS = 256
DECODE_SEQS = 8
MOE_ROWS = 1024
FFN_COLS = 512
FFN_SUB_ROWS = 256


def _params(*sem):
    return pltpu.CompilerParams(dimension_semantics=sem, vmem_limit_bytes=VMEM_LIMIT)


def _const_spec(shape):
    nd = len(shape)
    return pl.BlockSpec(shape, lambda *_: (0,) * nd)


def _weight_spec(shape):
    nd = len(shape)
    return pl.BlockSpec(shape, lambda *_: (0,) * nd, pipeline_mode=pl.Buffered(1))


def _bdot(a, b):
    return jnp.dot(a, b, preferred_element_type=F32)


def _split3(x):
    hi = x.astype(BF16)
    r1 = x - hi.astype(F32)
    mid = r1.astype(BF16)
    lo = (r1 - mid.astype(F32)).astype(BF16)
    return hi, mid, lo


def _sigmoid(x):
    return 0.5 * jnp.tanh(0.5 * x) + 0.5


def _log_sigmoid(x):
    return jnp.minimum(x, 0.0) - jnp.log1p(jnp.exp(-jnp.abs(x)))


def _softplus(x):
    return jnp.maximum(x, 0.0) + jnp.log1p(jnp.exp(-jnp.abs(x)))


def _layernorm(x, g, b):
    mu = jnp.mean(x, axis=-1, keepdims=True)
    xc = x - mu
    var = jnp.mean(xc * xc, axis=-1, keepdims=True)
    return xc * lax.rsqrt(var + LN_EPS) * g + b


def _head_tail_specs(n_head_tiles, tail_tile, tm, d):
    return [pl.BlockSpec((tm, d), lambda i: (jnp.minimum(i, n_head_tiles - 1), 0)),
            pl.BlockSpec((tm, d), lambda i: (tail_tile, 0))]


def _head_or_tail(head_ref, tail_ref):
    is_tail = pl.program_id(0) == pl.num_programs(0) - 1
    return jnp.where(is_tail, tail_ref[...], head_ref[...])


def _proj_kernel(x_ref, xt_ref, wq_ref, wal_ref, wa2_ref, ba_ref, wxr_ref, wmg_ref, bmg_ref,
                 qkvg_ref, loga_ref, xr_ref, gate_ref):
    xb = _head_or_tail(x_ref, xt_ref).astype(BF16)
    d = x_ref.shape[1]
    for c in range(0, qkvg_ref.shape[1], d):
        qkvg_ref[:, c:c + d] = _bdot(xb, wq_ref[:, c:c + d]).astype(BF16)
    a_low = _bdot(xb, wal_ref[...])
    z = _bdot(a_low.astype(BF16), wa2_ref[...]) + ba_ref[...]
    loga_ref[...] = _log_sigmoid(z) * (1.0 / GLA_GATE_TAU)
    xr_ref[...] = _bdot(xb, wxr_ref[...])
    for c in range(0, gate_ref.shape[1], d):
        gate_ref[:, c:c + d] = _sigmoid(_bdot(xb, wmg_ref[:, c:c + d]) + bmg_ref[:, c:c + d]).astype(BF16)


def _proj(x_head, x_tail, n_head_tiles, tail_tile, w, tm):
    d = x_head.shape[1]
    t = (n_head_tiles + 1) * tm
    kw = w["wq"].shape[1]
    nk = w["wa2"].shape[1]
    consts = [w["wq"], w["wal"], w["wa2"], w["ba"], w["wxr"], w["wmg"], w["bmg"]]
    return pl.pallas_call(
        _proj_kernel,
        grid=(t // tm,),
        in_specs=_head_tail_specs(n_head_tiles, tail_tile, tm, d) + [_weight_spec(c.shape) for c in consts],
        out_specs=[pl.BlockSpec((tm, kw), lambda i: (i, 0)),
                   pl.BlockSpec((tm, nk), lambda i: (i, 0)),
                   pl.BlockSpec((tm, d), lambda i: (i, 0)),
                   pl.BlockSpec((tm, 2 * d), lambda i: (i, 0))],
        out_shape=[jax.ShapeDtypeStruct((t, kw), BF16),
                   jax.ShapeDtypeStruct((t, nk), F32),
                   jax.ShapeDtypeStruct((t, d), F32),
                   jax.ShapeDtypeStruct((t, 2 * d), BF16)],
        compiler_params=_params("arbitrary"),
        name="proj",
    )(x_head, x_tail, *consts)


def _gla_kernel(*refs, chunk, n_chunks, heads, has_state, n_inputs, state_slices, state_slot):
    q_ref, k_ref, v_ref, la_ref, g_ref, ng_ref = refs[:6]
    s0_ref = refs[6] if has_state else None
    o_ref, so_ref, s_scr, o_scr, p_scr, kv_scr = refs[n_inputs:]
    t = pl.program_id(1)
    dk = q_ref.shape[1] // heads
    dv = v_ref.shape[1] // heads
    scale = dk ** -0.5
    chunks_per_seq = n_chunks // s_scr.shape[0]

    @pl.when(t == 0)
    def _():
        if has_state:
            s_scr[...] = s0_ref[0]
        else:
            s_scr[...] = jnp.zeros_like(s_scr)

    n_rows = n_chunks * chunk
    r = lax.broadcasted_iota(jnp.int32, (n_rows, n_rows), 0)
    c = lax.broadcasted_iota(jnp.int32, (n_rows, n_rows), 1)
    shift = chunk.bit_length() - 1
    assert 1 << shift == chunk, "chunk must be a power of two"
    tril = jnp.where(jnp.logical_and(r >> shift == c >> shift, r >= c), 1.0, 0.0).astype(BF16)
    la_hi, la_mid, la_lo = _split3(la_ref[...])
    b_all = _bdot(tril, la_hi) + _bdot(tril, la_mid) + _bdot(tril, la_lo)
    q_all = (q_ref[...].astype(F32) * scale * jnp.exp(b_all)).astype(BF16)
    k_all = (k_ref[...].astype(F32) * jnp.exp(-b_all)).astype(BF16)
    causal = (lax.broadcasted_iota(jnp.int32, (chunk, chunk), 0) >=
              lax.broadcasted_iota(jnp.int32, (chunk, chunk), 1))

    decays = []
    for ci in range(n_chunks):
        rows = slice(ci * chunk, (ci + 1) * chunk)
        b = b_all[rows, :]
        b_last = b[chunk - 1:chunk, :]
        q_e, k_e = q_all[rows, :], k_all[rows, :]
        k_d = (k_ref[rows, :].astype(F32) * jnp.exp(b_last - b)).astype(BF16)
        decays.append(jnp.exp(b_last))
        for h in range(heads):
            ks = slice(h * dk, (h + 1) * dk)
            v_h = v_ref[rows, h * dv:(h + 1) * dv]
            scores = lax.dot_general(q_e[:, ks], k_e[:, ks], (((1,), (1,)), ((), ())),
                                     preferred_element_type=F32)
            p_scr[rows, h * chunk:(h + 1) * chunk] = jnp.where(causal, scores, 0.0).astype(BF16)
            kv_scr[ci, h] = lax.dot_general(k_d[:, ks], v_h, (((0,), (0,)), ((), ())),
                                            preferred_element_type=F32)

    for ci in range(n_chunks):
        rows = slice(ci * chunk, (ci + 1) * chunk)
        seq = ci // chunks_per_seq
        for h in range(heads):
            ks = slice(h * dk, (h + 1) * dk)
            vs = slice(h * dv, (h + 1) * dv)
            s_old = s_scr[seq, h]
            o_scr[rows, vs] = (_bdot(p_scr[rows, h * chunk:(h + 1) * chunk], v_ref[rows, vs]) +
                               _bdot(q_all[rows, ks], s_old.astype(BF16)))
            s_scr[seq, h] = jnp.transpose(decays[ci][:, ks]) * s_old + kv_scr[ci, h]

    for h in range(heads):
        vs = slice(h * dv, (h + 1) * dv)
        o = o_scr[:, vs]
        mu = jnp.mean(o, axis=-1, keepdims=True)
        oc = o - mu
        var = jnp.mean(oc * oc, axis=-1, keepdims=True)
        on = oc * lax.rsqrt(var + LN_EPS) * ng_ref[:, vs]
        g = g_ref[:, vs].astype(F32)
        o_ref[:, vs] = (on * (g * _sigmoid(g))).astype(BF16)

    @pl.when(t == pl.num_programs(1) - 1)
    def _():
        if state_slices == 0:
            so_ref[...] = s_scr[...]
        for layer_slice in range(state_slices):
            so_ref[layer_slice] = s_scr[...] if layer_slice == state_slot else jnp.zeros_like(s_scr)


def _gla(qkvg, loga, norm_g, s0, layer, nb, s_len, rows, chunk, heads, dk, dv, stack=None):
    t = nb * s_len
    seqs = max(1, rows // s_len)
    assert s_len * seqs % rows == 0 and nb % seqs == 0 and s_len % chunk == 0
    nb, tps = nb // seqs, s_len * seqs // rows
    kw, vw = heads * dk, heads * dv
    has_state = s0 is not None
    state_shape, state_block, state_index = (nb * seqs, heads, dk, dv), (seqs, heads, dk, dv), lambda b, i: (b, 0, 0, 0)
    state_slices, state_slot, aliases = 0, 0, {}
    if stack is not None:
        depth, prev = stack
        state_shape = (depth,) + state_shape
        if prev is None:
            state_slices, state_slot = depth, layer
            state_block, state_index = (depth,) + state_block, lambda b, i: (0, b, 0, 0, 0)
        else:
            state_slices = 1
            state_block, state_index = (1,) + state_block, lambda b, i: (layer, b, 0, 0, 0)
    in_specs = [pl.BlockSpec((rows, kw), lambda b, i: (b * tps + i, 0)),
                pl.BlockSpec((rows, kw), lambda b, i: (b * tps + i, 1)),
                pl.BlockSpec((rows, vw), lambda b, i: (b * tps + i, (2 * kw) // vw)),
                pl.BlockSpec((rows, kw), lambda b, i: (b * tps + i, 0)),
                pl.BlockSpec((rows, vw), lambda b, i: (b * tps + i, (2 * kw) // vw + 1)),
                _const_spec(norm_g.shape)]
    args = [qkvg, qkvg, qkvg, loga, qkvg, norm_g]
    if has_state:
        in_specs.append(pl.BlockSpec((1, seqs, heads, dk, dv), lambda b, i: (layer, b, 0, 0, 0)))
        args.append(s0)
    if stack is not None and stack[1] is not None:
        in_specs.append(pl.BlockSpec(memory_space=pl.ANY))
        aliases[len(args)] = 1
        args.append(stack[1])
    return pl.pallas_call(
        functools.partial(_gla_kernel, chunk=chunk, n_chunks=rows // chunk, heads=heads, has_state=has_state,
                          n_inputs=len(args), state_slices=state_slices, state_slot=state_slot),
        grid=(nb, tps),
        in_specs=in_specs,
        out_specs=[pl.BlockSpec((rows, vw), lambda b, i: (b * tps + i, 0)),
                   pl.BlockSpec(state_block, state_index)],
        out_shape=[jax.ShapeDtypeStruct((t, vw), BF16), jax.ShapeDtypeStruct(state_shape, F32)],
        scratch_shapes=[pltpu.VMEM((seqs, heads, dk, dv), F32), pltpu.VMEM((rows, vw), F32),
                        pltpu.VMEM((rows, heads * chunk), BF16), pltpu.VMEM((rows // chunk, heads, dk, dv), F32)],
        input_output_aliases=aliases,
        compiler_params=_params("arbitrary", "arbitrary"),
        name="gla",
    )(*args)


def _rg_gates(xc, wax_ref, ba_ref, bx_ref, lam_ref):
    n_blocks, bw, _ = wax_ref.shape
    r_parts, i_parts = [], []
    for n in range(n_blocks):
        cs = slice(n * bw, (n + 1) * bw)
        pre = _bdot(xc[:, cs].astype(BF16), wax_ref[n])
        r_parts.append(_sigmoid(pre[:, :bw] + ba_ref[:, cs]))
        i_parts.append(_sigmoid(pre[:, bw:] + bx_ref[:, cs]))
    r = jnp.concatenate(r_parts, axis=1)
    i = jnp.concatenate(i_parts, axis=1)
    log_a = -RG_C * r * _softplus(-lam_ref[...])
    a = jnp.exp(log_a)
    bx = jnp.sqrt(jnp.tanh(-log_a) * (1.0 + a * a)) * (i * xc)
    return a, bx


def _rglru_seq_kernel(xr_ref, cw_ref, cb_ref, wax_ref, ba_ref, bx_ref, lam_ref,
                      h_ref, hl_ref, cv_ref, cbuf, a_scr, b_scr, h_scr, hc):
    t = pl.program_id(1)
    rows = xr_ref.shape[0]
    taps = cw_ref.shape[0]
    head = 8

    @pl.when(t == 0)
    def _():
        cbuf[0:head, :] = jnp.zeros((head, cbuf.shape[1]), F32)
        hc[...] = jnp.zeros_like(hc)

    cbuf[head:head + rows, :] = xr_ref[...]
    first = head - (taps - 1)
    xc = cb_ref[...] + sum(cbuf[first + j:first + j + rows, :] * cw_ref[j:j + 1, :] for j in range(taps))
    tail = cbuf[first + rows:head + rows, :]
    cbuf[first:head, :] = tail

    a, bx = _rg_gates(xc, wax_ref, ba_ref, bx_ref, lam_ref)
    a_scr[...] = a
    b_scr[...] = bx

    def step(i, h):
        h = a_scr[pl.ds(i, 1), :] * h + b_scr[pl.ds(i, 1), :]
        h_scr[pl.ds(i, 1), :] = h
        return h

    h_last = lax.fori_loop(0, rows, step, hc[...], unroll=8)
    hc[...] = h_last
    h_ref[...] = h_scr[...].astype(BF16)

    @pl.when(t == pl.num_programs(1) - 1)
    def _():
        hl_ref[0] = h_last
        cv_ref[0] = tail


def _rglru_seq(xr, w, nb, s_len, rows):
    width = xr.shape[1]
    t = nb * s_len
    tps = s_len // rows
    taps = w["cw"].shape[0]
    consts = [w["cw"], w["cb"], w["wax"], w["rba"], w["rbx"], w["lam"]]
    return pl.pallas_call(
        _rglru_seq_kernel,
        grid=(nb, tps),
        in_specs=[pl.BlockSpec((rows, width), lambda b, i: (b * tps + i, 0))] + [_const_spec(c.shape) for c in consts],
        out_specs=[pl.BlockSpec((rows, width), lambda b, i: (b * tps + i, 0)),
                   pl.BlockSpec((1, 1, width), lambda b, i: (b, 0, 0)),
                   pl.BlockSpec((1, taps - 1, width), lambda b, i: (b, 0, 0))],
        out_shape=[jax.ShapeDtypeStruct((t, width), BF16),
                   jax.ShapeDtypeStruct((nb, 1, width), F32),
                   jax.ShapeDtypeStruct((nb, taps - 1, width), F32)],
        scratch_shapes=[pltpu.VMEM((rows + 8, width), F32), pltpu.VMEM((rows, width), F32),
                        pltpu.VMEM((rows, width), F32), pltpu.VMEM((rows, width), F32),
                        pltpu.VMEM((1, width), F32)],
        compiler_params=_params("arbitrary", "arbitrary"),
        name="rglru_seq",
    )(xr, *consts)


def _rglru_step_kernel(xr_ref, sc_ref, h0_ref, cw_ref, cb_ref, wax_ref, ba_ref, bx_ref, lam_ref,
                       h_ref, hn_ref, cv_ref):
    taps = cw_ref.shape[0]
    xr = xr_ref[...]
    xc = cb_ref[...] + sum(sc_ref[j] * cw_ref[j:j + 1, :] for j in range(taps - 1)) + xr * cw_ref[taps - 1:taps, :]
    a, bx = _rg_gates(xc, wax_ref, ba_ref, bx_ref, lam_ref)
    h = a * h0_ref[...] + bx
    h_ref[...] = h.astype(BF16)
    hn_ref[...] = h
    for j in range(taps - 2):
        cv_ref[j] = sc_ref[j + 1]
    cv_ref[taps - 2] = xr


def _rglru_step(xr, conv_t, h0, w):
    n, width = xr.shape
    consts = [w["cw"], w["cb"], w["wax"], w["rba"], w["rbx"], w["lam"]]
    args = [xr, conv_t, h0] + consts
    return pl.pallas_call(
        _rglru_step_kernel,
        grid=(1,),
        in_specs=[_const_spec(a.shape) for a in args],
        out_specs=[_const_spec((n, width)), _const_spec((n, width)), _const_spec(conv_t.shape)],
        out_shape=[jax.ShapeDtypeStruct((n, width), BF16),
                   jax.ShapeDtypeStruct((n, width), F32),
                   jax.ShapeDtypeStruct(conv_t.shape, F32)],
        compiler_params=_params("arbitrary"),
        name="rglru_step",
    )(*args)


def _pack_bf16_pairs(x):
    half = x.shape[1] // 2
    hi = pltpu.bitcast(x[:, :half].astype(BF16).astype(F32), jnp.uint32)
    lo = pltpu.bitcast(x[:, half:].astype(BF16).astype(F32), jnp.uint32)
    return hi | (lo >> 16)


def _unpack_bf16_pairs(p):
    hi = pltpu.bitcast(p & jnp.uint32(0xFFFF0000), F32)
    lo = pltpu.bitcast(p << 16, F32)
    return jnp.concatenate([hi, lo], axis=1)


def _post_kernel(x_ref, xt_ref, o_ref, ot_ref, h_ref, ht_ref, gate_ref, wb0_ref, wb1_ref, wo_ref, g1_ref, b1_ref,
                 rw_ref, rb_ref, x1_ref, x1p_ref, idx_ref, wt_ref, cnt_ref, cnt_scr, *, alpha, n_experts):
    d = x_ref.shape[1]
    tm = x_ref.shape[0]

    @pl.when(pl.program_id(0) == 0)
    def _():
        cnt_scr[...] = jnp.zeros_like(cnt_scr)

    x = _head_or_tail(x_ref, xt_ref)
    o = _head_or_tail(o_ref, ot_ref)
    h = _head_or_tail(h_ref, ht_ref)
    gate = gate_ref[...].astype(F32)
    merged = gate[:, :d] * _bdot(o, wb0_ref[...]) + gate[:, d:] * _bdot(h, wb1_ref[...])
    mix = _bdot(merged.astype(BF16), wo_ref[...])
    x1 = _layernorm(alpha * x + mix, g1_ref[...], b1_ref[...])
    x1_ref[...] = x1
    x1p_ref[...] = _pack_bf16_pairs(x1)

    xh = x1.astype(BF16)
    xl = (x1 - xh.astype(F32)).astype(BF16)
    n_lanes = rw_ref.shape[1] // 2
    hi_terms = _bdot(xh, rw_ref[...])
    logits = (hi_terms[:, n_lanes:] + _bdot(xl, rw_ref[:, :n_lanes])) + hi_terms[:, :n_lanes]
    logits = logits + rb_ref[...]
    lane = lax.broadcasted_iota(jnp.int32, logits.shape, 1)
    lane_f = lane.astype(F32)
    neg_inf = jnp.float32(-jnp.inf)
    cur = jnp.where(lane < n_experts, logits, neg_inf)
    vals, idxs = [], []
    for _ in range(TOP_K):
        m = jnp.max(cur, axis=-1, keepdims=True)
        sel = jnp.min(jnp.where(cur == m, lane_f, float(LANES)), axis=-1, keepdims=True)
        vals.append(m)
        idxs.append(sel)
        cur = jnp.where(lane_f == sel, neg_inf, cur)
    exps = [jnp.exp(v - vals[0]) for v in vals]
    total = sum(exps)

    onehots = [jnp.where(lane_f == idxs[j], 1.0, 0.0) for j in range(TOP_K)]
    chosen = sum(onehots)
    r = lax.broadcasted_iota(jnp.int32, (tm, tm), 0)
    c = lax.broadcasted_iota(jnp.int32, (tm, tm), 1)
    earlier = jnp.where(r > c, 1.0, 0.0).astype(BF16)
    before = _bdot(earlier, chosen.astype(BF16)) + cnt_scr[...]
    ranks = [jnp.sum(onehots[j] * before, axis=-1, keepdims=True) for j in range(TOP_K)]
    cnt_scr[...] += jnp.sum(chosen, axis=0, keepdims=True)
    cnt_ref[...] = cnt_scr[...]

    idx_out = jnp.zeros(logits.shape, F32)
    wt_out = jnp.zeros(logits.shape, F32)
    for j in range(TOP_K):
        idx_out = jnp.where(lane == j, idxs[j], idx_out)
        idx_out = jnp.where(lane == TOP_K + j, ranks[j], idx_out)
        wt_out = jnp.where(lane == j, exps[j] / total, wt_out)
    idx_ref[...] = idx_out.astype(jnp.int32)
    wt_ref[...] = wt_out


def _post(x_head, x_tail, n_head_tiles, tail_tile, o, h, o_tail, h_tail, gate, w, tm, alpha, n_experts):
    d = x_head.shape[1]
    t = (n_head_tiles + 1) * tm
    assert o.shape[0] == t - tm and o_tail.shape[0] == tm and gate.shape[0] == t
    consts = [w["wb0"], w["wb1"], w["wo"], w["ln1g"], w["ln1b"], w["rw"], w["rb"]]
    row = lambda width: pl.BlockSpec((tm, width), lambda i: (i, 0))
    return pl.pallas_call(
        functools.partial(_post_kernel, alpha=alpha, n_experts=n_experts),
        grid=(t // tm,),
        in_specs=_head_tail_specs(n_head_tiles, tail_tile, tm, d) + _head_tail_specs(n_head_tiles, 0, tm, d) +
                 _head_tail_specs(n_head_tiles, 0, tm, d) + [row(2 * d)] + [_weight_spec(c.shape) for c in consts],
        out_specs=[row(d), row(d // 2), row(LANES), row(LANES), _const_spec((1, LANES))],
        out_shape=[jax.ShapeDtypeStruct((t, d), F32),
                   jax.ShapeDtypeStruct((t, d // 2), jnp.uint32),
                   jax.ShapeDtypeStruct((t, LANES), jnp.int32),
                   jax.ShapeDtypeStruct((t, LANES), F32),
                   jax.ShapeDtypeStruct((1, LANES), F32)],
        scratch_shapes=[pltpu.VMEM((1, LANES), F32)],
        compiler_params=_params("arbitrary"),
        name="post",
    )(x_head, x_tail, o, o_tail, h, h_tail, gate, *consts)


SC_INDEX_MAX = 128
SC_ALIGN = 8
SC_BUFFER_BYTES = 208 * 1024


def _sc_plan(n_rows, row_bytes):
    info = plsc.get_sparse_core_info()
    n_workers = info.num_cores * info.num_subcores
    per_worker, rem = divmod(n_rows, n_workers)
    assert rem == 0 and per_worker % SC_ALIGN == 0, "rows must split into aligned equal shares per subcore"
    cap = min(SC_INDEX_MAX, SC_BUFFER_BYTES // row_bytes)
    chunk = max(c for c in range(SC_ALIGN, cap + 1, SC_ALIGN) if per_worker % c == 0)
    return info.num_cores, n_workers, per_worker, chunk


def _sc_gather(table, idx):
    n_rows = idx.shape[0]
    d = table.shape[1]
    n_cores, _, per_worker, chunk = _sc_plan(n_rows, d * table.dtype.itemsize)
    mesh = plsc.VectorSubcoreMesh(core_axis_name="c", subcore_axis_name="s")

    n_chunks = per_worker // chunk

    @functools.partial(
        pl.kernel, mesh=mesh, out_type=jax.ShapeDtypeStruct((n_rows, d), table.dtype),
        scratch_types=[pltpu.VMEM((chunk,), jnp.int32)] * 2 + [pltpu.VMEM((chunk, d), table.dtype)] * 2 +
                      [pltpu.SemaphoreType.DMA] * 4)
    def gather(table_hbm, idx_hbm, out_hbm, idx0, idx1, rows0, rows1, g0, g1, w0, w1):
        idx_v, rows_v, gsem, wsem = (idx0, idx1), (rows0, rows1), (g0, g1), (w0, w1)
        worker = lax.axis_index("s") * n_cores + lax.axis_index("c")
        base = worker * per_worker

        def rows_of(c):
            return pl.ds(pl.multiple_of(base + c * chunk, SC_ALIGN), chunk)

        def start_gather(c):
            s = c % 2
            pltpu.sync_copy(idx_hbm.at[rows_of(c)], idx_v[s])
            return pltpu.async_copy(table_hbm.at[idx_v[s]], rows_v[s], gsem[s])

        gathers = {c: start_gather(c) for c in range(min(2, n_chunks))}
        writes = {}
        for c in range(n_chunks):
            s = c % 2
            gathers[c].wait()
            writes[c] = pltpu.async_copy(rows_v[s], out_hbm.at[rows_of(c)], wsem[s])
            if c + 2 < n_chunks:
                writes.pop(c).wait()
                gathers[c + 2] = start_gather(c + 2)
        for write in writes.values():
            write.wait()

    return gather(table, idx)


def _sc_scatter(x, pos, n_out):
    n, d = x.shape
    assert pos.shape == (TOP_K * n,)
    n_cores, _, per_worker, chunk = _sc_plan(n, d * x.dtype.itemsize)
    mesh = plsc.VectorSubcoreMesh(core_axis_name="c", subcore_axis_name="s")

    n_chunks = per_worker // chunk

    @functools.partial(
        pl.kernel, mesh=mesh, out_type=jax.ShapeDtypeStruct((n_out, d), x.dtype),
        scratch_types=[pltpu.VMEM((chunk, d), x.dtype)] * 2 + [pltpu.VMEM((chunk,), jnp.int32)] * TOP_K +
                      [pltpu.SemaphoreType.DMA] * 3)
    def scatter(x_hbm, pos_hbm, out_hbm, rows0, rows1, *rest):
        rows_v, idx_v, (l0, l1, ssem) = (rows0, rows1), rest[:TOP_K], rest[TOP_K:]
        lsem = (l0, l1)
        worker = lax.axis_index("s") * n_cores + lax.axis_index("c")
        base = worker * per_worker

        def first_row(c):
            return pl.multiple_of(base + c * chunk, SC_ALIGN)

        def start_load(c):
            return pltpu.async_copy(x_hbm.at[pl.ds(first_row(c), chunk)], rows_v[c % 2], lsem[c % 2])

        load = start_load(0)
        for c in range(n_chunks):
            next_load = start_load(c + 1) if c + 1 < n_chunks else None
            for k in range(TOP_K):
                pltpu.sync_copy(pos_hbm.at[pl.ds(k * n + first_row(c), chunk)], idx_v[k])
            load.wait()
            scatters = [pltpu.async_copy(rows_v[c % 2], out_hbm.at[idx_v[k]], ssem)
                        for k in range(TOP_K)]
            for scatter_k in scatters:
                scatter_k.wait()
            load = next_load

    return scatter(x, pos)


def _ffn_kernel(be_ref, valid_ref, x_ref, wgu_ref, bgu_ref, wd_ref, bd_ref, o_ref, wgu_bf, wd_bf):
    i = pl.program_id(0)
    bm = x_ref.shape[0]
    f = wd_ref.shape[2]
    valid = valid_ref[i]
    new_expert = jnp.logical_or(i == 0, be_ref[i] != be_ref[jnp.maximum(i - 1, 0)])

    @pl.when(new_expert)
    def _():
        wgu_bf[...] = wgu_ref[0, 0].astype(BF16)
        wd_bf[...] = wd_ref[0, 0].astype(BF16)

    def expert_rows(rows):
        x = _unpack_bf16_pairs(x_ref[rows, :]).astype(BF16)
        acc = jnp.zeros((x.shape[0], wd_ref.shape[3]), F32)
        for c in range(0, f, FFN_COLS):
            gate = _bdot(x, wgu_bf[:, c:c + FFN_COLS]) + bgu_ref[0, 0, :, c:c + FFN_COLS]
            up = _bdot(x, wgu_bf[:, f + c:f + c + FFN_COLS]) + bgu_ref[0, 0, :, f + c:f + c + FFN_COLS]
            gate = jnp.minimum(gate, SWIGLU_LIMIT)
            up = jnp.clip(up, -SWIGLU_LIMIT, SWIGLU_LIMIT)
            act = (up + 1.0) * gate * _sigmoid(SWIGLU_ALPHA * gate)
            acc = acc + _bdot(act.astype(BF16), wd_bf[c:c + FFN_COLS, :])
        o_ref[rows, :] = _pack_bf16_pairs(acc + bd_ref[0, 0])

    @pl.when(valid == bm)
    def _():
        expert_rows(slice(0, bm))

    @pl.when(valid < bm)
    def _():
        for s in range(0, bm, FFN_SUB_ROWS):
            rows = slice(s, s + FFN_SUB_ROWS)

            @pl.when(s < valid)
            def _():
                expert_rows(rows)

            @pl.when(s >= valid)
            def _():
                o_ref[rows, :] = jnp.zeros((FFN_SUB_ROWS, o_ref.shape[1]), o_ref.dtype)


def _ffn(block_e, valid, xs, layer, w_gu, b_gu, w_down, b_down, bm):
    n_layers, n_exp, d, f2 = w_gu.shape
    f = f2 // 2
    n_blocks = block_e.shape[0]
    expert = lambda i, be, nu: (layer, be[i], 0, 0)
    grid_spec = pltpu.PrefetchScalarGridSpec(
        num_scalar_prefetch=2,
        grid=(n_blocks,),
        in_specs=[pl.BlockSpec((bm, d // 2), lambda i, be, nu: (i, 0)),
                  pl.BlockSpec((1, 1, d, f2), expert),
                  pl.BlockSpec((1, 1, 1, f2), expert),
                  pl.BlockSpec((1, 1, f, d), expert),
                  pl.BlockSpec((1, 1, 1, d), expert)],
        out_specs=pl.BlockSpec((bm, d // 2), lambda i, be, nu: (i, 0)),
        scratch_shapes=[pltpu.VMEM((d, f2), BF16), pltpu.VMEM((f, d), BF16)],
    )
    return pl.pallas_call(
        _ffn_kernel,
        grid_spec=grid_spec,
        out_shape=jax.ShapeDtypeStruct((n_blocks * bm, d // 2), jnp.uint32),
        compiler_params=_params("arbitrary"),
        name="ffn",
    )(block_e, valid, xs, w_gu, b_gu.reshape(n_layers, n_exp, 1, f2),
      w_down, b_down.reshape(n_layers, n_exp, 1, d))


def _combine_kernel(x_ref, *refs, alpha, split):
    y_refs, (wt_ref, g_ref, b_ref), out_refs = refs[:TOP_K], refs[TOP_K:TOP_K + 3], refs[TOP_K + 3:]
    wt = wt_ref[...]
    ffn = sum(wt[:, j:j + 1] * _unpack_bf16_pairs(y_refs[j][...]) for j in range(TOP_K))
    res = _layernorm(alpha * x_ref[...] + ffn, g_ref[...], b_ref[...])
    if not split:
        out_refs[0][...] = res
    else:
        is_tail = pl.program_id(0) == pl.num_programs(0) - 1

        @pl.when(jnp.logical_not(is_tail))
        def _():
            out_refs[0][...] = res

        @pl.when(is_tail)
        def _():
            out_refs[1][...] = res


def _combine(x1, y, wt, g, b, tm, alpha, split=False):
    t, d = x1.shape
    n_tiles = t // tm
    planes = [pl.BlockSpec((tm, d // 2), lambda i, j=j: (j * n_tiles + i, 0)) for j in range(TOP_K)]
    if split:
        out_specs = [pl.BlockSpec((tm, d), lambda i: (jnp.minimum(i, n_tiles - 2), 0)), _const_spec((tm, d))]
        out_shape = [jax.ShapeDtypeStruct((t - tm, d), F32), jax.ShapeDtypeStruct((tm, d), F32)]
    else:
        out_specs = pl.BlockSpec((tm, d), lambda i: (i, 0))
        out_shape = jax.ShapeDtypeStruct((t, d), F32)
    return pl.pallas_call(
        functools.partial(_combine_kernel, alpha=alpha, split=split),
        grid=(n_tiles,),
        in_specs=[pl.BlockSpec((tm, d), lambda i: (i, 0))] + planes +
                 [pl.BlockSpec((tm, LANES), lambda i: (i, 0)), _const_spec(g.shape), _const_spec(b.shape)],
        out_specs=out_specs,
        out_shape=out_shape,
        compiler_params=_params("arbitrary"),
        name="combine",
    )(x1, *([y] * TOP_K), wt, g, b)


def _route(experts, ranks, counts, bm):
    n_experts = counts.shape[0]
    i32 = jnp.int32
    padded = (counts + bm - 1) // bm * bm
    ends_pad = jnp.cumsum(padded)
    start_pad = ends_pad - padded
    n_blocks = -(-experts.size // bm) + n_experts
    first_row = jnp.arange(n_blocks, dtype=i32)[:, None] * bm
    block_e = jnp.minimum(jnp.sum(ends_pad[None, :] <= first_row, axis=1), n_experts - 1).astype(i32)
    valid = jnp.clip(counts[block_e] - (first_row[:, 0] - start_pad[block_e]), 0, bm).astype(i32)
    onehot = experts[..., None] == jnp.arange(n_experts, dtype=i32)
    pos = jnp.sum(jnp.where(onehot, start_pad, 0), axis=-1).astype(i32) + ranks
    return block_e, valid, pos


def _moe(x1, x1p, routing, counts, top_w_pad, w, layer, moe_weights, alpha, tm, split):
    n_experts = moe_weights[0].shape[1]
    block_e, valid, pos = _route(routing[:, :TOP_K], routing[:, TOP_K:2 * TOP_K],
                                 counts[0, :n_experts].astype(jnp.int32), MOE_ROWS)
    pos_planes = pos.T.reshape(-1)
    xs = _sc_scatter(x1p, pos_planes, block_e.shape[0] * MOE_ROWS)
    out_rows = _ffn(block_e, valid, xs, layer, *moe_weights, MOE_ROWS)
    y = _sc_gather(out_rows, pos_planes)
    return _combine(x1, y, top_w_pad, w["ln2g"], w["ln2b"], tm, alpha, split)


def _hi_lo(w):
    hi = w.astype(BF16)
    return jnp.concatenate([hi, (w - hi.astype(F32)).astype(BF16)], axis=1)


def _layer_weights(layer, w_in, gla_w_a2, gla_b_a, gla_norm_g, rg_conv_w, rg_conv_b, rg_w_a, rg_b_a, rg_w_x,
                   rg_b_x, rg_lambda, b_merge, w_branch, w_o, ln1_g, ln1_b, ln2_g, ln2_b, router_w, router_b,
                   kw, vw):
    d = w_in.shape[1]
    rank = gla_w_a2.shape[1]
    width = rg_conv_w.shape[2]
    n_exp = router_w.shape[2]
    c0 = 2 * kw + 2 * vw
    wi = w_in[layer]
    row = lambda v: v.reshape(1, -1)
    return {
        "wq": wi[:, :c0].astype(BF16),
        "wal": jnp.pad(wi[:, c0:c0 + rank], ((0, 0), (0, LANES - rank))).astype(BF16),
        "wa2": jnp.pad(gla_w_a2[layer], ((0, LANES - rank), (0, 0))).astype(BF16),
        "ba": row(gla_b_a[layer]),
        "wxr": wi[:, c0 + rank:c0 + rank + width].astype(BF16),
        "wmg": wi[:, c0 + rank + width:].astype(BF16),
        "bmg": row(b_merge[layer]),
        "ng": row(gla_norm_g[layer]),
        "cw": rg_conv_w[layer], "cb": row(rg_conv_b[layer]),
        "wax": jnp.concatenate([rg_w_a[layer], rg_w_x[layer]], axis=-1).astype(BF16),
        "rba": row(rg_b_a[layer]), "rbx": row(rg_b_x[layer]), "lam": row(rg_lambda[layer]),
        "wb0": w_branch[layer, 0].astype(BF16), "wb1": w_branch[layer, 1].astype(BF16),
        "wo": w_o[layer].astype(BF16),
        "ln1g": row(ln1_g[layer]), "ln1b": row(ln1_b[layer]),
        "ln2g": row(ln2_g[layer]), "ln2b": row(ln2_b[layer]),
        "rw": _hi_lo(jnp.pad(router_w[layer], ((0, 0), (0, LANES - n_exp)))),
        "rb": jnp.pad(row(router_b[layer]), ((0, 0), (0, LANES - n_exp))),
    }


def kernel(x_prompt, x_sample, state_gla, state_rglru, state_conv, ln1_g, ln1_b, w_in, gla_w_a2, gla_b_a, gla_norm_g, rg_conv_w, rg_conv_b, rg_w_a, rg_b_a, rg_w_x, rg_b_x, rg_lambda, b_merge, w_branch, w_o, ln2_g, ln2_b, router_w, router_b, moe_w_gu, moe_b_gu, moe_w_down, moe_b_down):
    n_p, seq, d = x_prompt.shape
    n_s, dec_seq, _ = x_sample.shape
    assert dec_seq == 1, "the sample group carries one new token per sequence"
    depth, _, heads, dk, dv = state_gla.shape
    kw, vw = heads * dk, heads * dv
    n_exp = router_w.shape[2]
    alpha = (2.0 * depth) ** 0.25
    t_p = n_p * seq
    pad_rows = BF16_SUBLANES

    tile = PROJ_ROWS
    assert t_p % tile == 0 and n_s <= tile
    n_head_tiles = t_p // tile
    tail_pad = lambda a: jnp.pad(a, ((0, tile - n_s), (0, 0)))
    x_head, x_tail, tail_tile = x_prompt.reshape(t_p, d), tail_pad(x_sample.reshape(n_s, d)), 0
    moe_weights = (moe_w_gu, moe_b_gu, moe_w_down, moe_b_down)
    outs = {k: [] for k in ("gla_p", "rg_p", "cv_p", "rg_s", "cv_s")}
    gla_s = None
    for layer in range(depth):
        w = _layer_weights(layer, w_in, gla_w_a2, gla_b_a, gla_norm_g, rg_conv_w, rg_conv_b, rg_w_a, rg_b_a,
                           rg_w_x, rg_b_x, rg_lambda, b_merge, w_branch, w_o, ln1_g, ln1_b, ln2_g, ln2_b,
                           router_w, router_b, kw, vw)
        qkvg, loga, xr, gate = _proj(x_head, x_tail, n_head_tiles, tail_tile, w, tile)
        o_p, s_p = _gla(qkvg, loga, w["ng"], None, layer, n_p, seq, SEQ_ROWS, GLA_CHUNK, heads, dk, dv)
        h_p, hl_p, cv_p = _rglru_seq(xr, w, n_p, seq, SEQ_ROWS)
        pad = lambda a: jnp.pad(a[t_p:t_p + n_s, None, :],
                                ((0, 0), (0, pad_rows - 1), (0, 0))).reshape(n_s * pad_rows, -1)
        o_s, gla_s = _gla(pad(qkvg), pad(loga), w["ng"], state_gla, layer, n_s, pad_rows,
                          DECODE_SEQS * pad_rows, pad_rows, heads, dk, dv, stack=(depth, gla_s))
        o_s = o_s.reshape(n_s, pad_rows, vw)[:, 0]
        h_s, hn_s, cv_s = _rglru_step(xr[t_p:t_p + n_s], jnp.swapaxes(state_conv[layer], 0, 1),
                                      state_rglru[layer], w)
        x1, x1p, routing, wt, cnt = _post(x_head, x_tail, n_head_tiles, tail_tile, o_p, h_p, tail_pad(o_s),
                                          tail_pad(h_s), gate, w, tile, alpha, n_exp)
        last = layer == depth - 1
        x = _moe(x1, x1p, routing, cnt, wt, w, layer, moe_weights, alpha, tile, split=last)
        if not last:
            x_head, x_tail, tail_tile = x, x, n_head_tiles
        outs["gla_p"].append(s_p)
        outs["rg_p"].append(hl_p.reshape(n_p, -1))
        outs["cv_p"].append(cv_p)
        outs["rg_s"].append(hn_s)
        outs["cv_s"].append(jnp.swapaxes(cv_s, 0, 1))
    y_head, y_tail = x
    return (y_head.reshape(n_p, seq, d), y_tail[:n_s].reshape(n_s, dec_seq, d),
            jnp.stack(outs["gla_p"]), jnp.stack(outs["rg_p"]), jnp.stack(outs["cv_p"]),
            gla_s, jnp.stack(outs["rg_s"]), jnp.stack(outs["cv_s"]))
```

```python
import functools

import jax
import jax.numpy as jnp
from jax import lax
from jax.experimental import pallas as pl
from jax.experimental.pallas import tpu as pltpu
from jax.experimental.pallas import tpu_sc as plsc

F32 = jnp.float32
BF16 = jnp.bfloat16

TOP_K = 4
GLA_GATE_TAU = 16.0
GLA_CHUNK = 64
RG_C = 8.0
SWIGLU_LIMIT = 7.0
SWIGLU_ALPHA = 1.702
LN_EPS = 1e-5

LANES = 128
BF16_SUBLANES = 16
VMEM_LIMIT = 56 * 1024 * 1024

PROJ_ROWS = 512
SEQ_ROWS = 256
DECODE_SEQS = 8
MOE_ROWS = 1024
FFN_COLS = 512
FFN_SUB_ROWS = 256


def _params(*sem):
    return pltpu.CompilerParams(dimension_semantics=sem, vmem_limit_bytes=VMEM_LIMIT)


def _const_spec(shape):
    nd = len(shape)
    return pl.BlockSpec(shape, lambda *_: (0,) * nd)


def _weight_spec(shape):
    nd = len(shape)
    return pl.BlockSpec(shape, lambda *_: (0,) * nd, pipeline_mode=pl.Buffered(1))


def _bdot(a, b):
    return jnp.dot(a, b, preferred_element_type=F32)


def _split3(x):
    hi = x.astype(BF16)
    r1 = x - hi.astype(F32)
    mid = r1.astype(BF16)
    lo = (r1 - mid.astype(F32)).astype(BF16)
    return hi, mid, lo


def _sigmoid(x):
    return 0.5 * jnp.tanh(0.5 * x) + 0.5


def _log_sigmoid(x):
    return jnp.minimum(x, 0.0) - jnp.log1p(jnp.exp(-jnp.abs(x)))


def _softplus(x):
    return jnp.maximum(x, 0.0) + jnp.log1p(jnp.exp(-jnp.abs(x)))


def _layernorm(x, g, b):
    mu = jnp.mean(x, axis=-1, keepdims=True)
    xc = x - mu
    var = jnp.mean(xc * xc, axis=-1, keepdims=True)
    return xc * lax.rsqrt(var + LN_EPS) * g + b


def _head_tail_specs(n_head_tiles, tail_tile, tm, d):
    return [pl.BlockSpec((tm, d), lambda i: (jnp.minimum(i, n_head_tiles - 1), 0)),
            pl.BlockSpec((tm, d), lambda i: (tail_tile, 0))]


def _head_or_tail(head_ref, tail_ref):
    is_tail = pl.program_id(0) == pl.num_programs(0) - 1
    return jnp.where(is_tail, tail_ref[...], head_ref[...])


def _proj_kernel(x_ref, xt_ref, wq_ref, wal_ref, wa2_ref, ba_ref, wxr_ref, wmg_ref, bmg_ref,
                 qkvg_ref, loga_ref, xr_ref, gate_ref):
    xb = _head_or_tail(x_ref, xt_ref).astype(BF16)
    d = x_ref.shape[1]
    for c in range(0, qkvg_ref.shape[1], d):
        qkvg_ref[:, c:c + d] = _bdot(xb, wq_ref[:, c:c + d]).astype(BF16)
    a_low = _bdot(xb, wal_ref[...])
    z = _bdot(a_low.astype(BF16), wa2_ref[...]) + ba_ref[...]
    loga_ref[...] = _log_sigmoid(z) * (1.0 / GLA_GATE_TAU)
    xr_ref[...] = _bdot(xb, wxr_ref[...])
    for c in range(0, gate_ref.shape[1], d):
        gate_ref[:, c:c + d] = _sigmoid(_bdot(xb, wmg_ref[:, c:c + d]) + bmg_ref[:, c:c + d]).astype(BF16)


def _proj(x_head, x_tail, n_head_tiles, tail_tile, w, tm):
    d = x_head.shape[1]
    t = (n_head_tiles + 1) * tm
    kw = w["wq"].shape[1]
    nk = w["wa2"].shape[1]
    consts = [w["wq"], w["wal"], w["wa2"], w["ba"], w["wxr"], w["wmg"], w["bmg"]]
    return pl.pallas_call(
        _proj_kernel,
        grid=(t // tm,),
        in_specs=_head_tail_specs(n_head_tiles, tail_tile, tm, d) + [_weight_spec(c.shape) for c in consts],
        out_specs=[pl.BlockSpec((tm, kw), lambda i: (i, 0)),
                   pl.BlockSpec((tm, nk), lambda i: (i, 0)),
                   pl.BlockSpec((tm, d), lambda i: (i, 0)),
                   pl.BlockSpec((tm, 2 * d), lambda i: (i, 0))],
        out_shape=[jax.ShapeDtypeStruct((t, kw), BF16),
                   jax.ShapeDtypeStruct((t, nk), F32),
                   jax.ShapeDtypeStruct((t, d), F32),
                   jax.ShapeDtypeStruct((t, 2 * d), BF16)],
        compiler_params=_params("arbitrary"),
        name="proj",
    )(x_head, x_tail, *consts)


def _gla_kernel(*refs, chunk, n_chunks, heads, has_state, n_inputs, state_slices, state_slot):
    q_ref, k_ref, v_ref, la_ref, g_ref, ng_ref = refs[:6]
    s0_ref = refs[6] if has_state else None
    o_ref, so_ref, s_scr, o_scr, p_scr, kv_scr = refs[n_inputs:]
    t = pl.program_id(1)
    dk = q_ref.shape[1] // heads
    dv = v_ref.shape[1] // heads
    scale = dk ** -0.5
    chunks_per_seq = n_chunks // s_scr.shape[0]

    @pl.when(t == 0)
    def _():
        if has_state:
            s_scr[...] = s0_ref[0]
        else:
            s_scr[...] = jnp.zeros_like(s_scr)

    n_rows = n_chunks * chunk
    r = lax.broadcasted_iota(jnp.int32, (n_rows, n_rows), 0)
    c = lax.broadcasted_iota(jnp.int32, (n_rows, n_rows), 1)
    shift = chunk.bit_length() - 1
    assert 1 << shift == chunk, "chunk must be a power of two"
    tril = jnp.where(jnp.logical_and(r >> shift == c >> shift, r >= c), 1.0, 0.0).astype(BF16)
    la_hi, la_mid, la_lo = _split3(la_ref[...])
    b_all = _bdot(tril, la_hi) + _bdot(tril, la_mid) + _bdot(tril, la_lo)
    q_all = (q_ref[...].astype(F32) * scale * jnp.exp(b_all)).astype(BF16)
    k_all = (k_ref[...].astype(F32) * jnp.exp(-b_all)).astype(BF16)
    causal = (lax.broadcasted_iota(jnp.int32, (chunk, chunk), 0) >=
              lax.broadcasted_iota(jnp.int32, (chunk, chunk), 1))

    decays = []
    for ci in range(n_chunks):
        rows = slice(ci * chunk, (ci + 1) * chunk)
        b = b_all[rows, :]
        b_last = b[chunk - 1:chunk, :]
        q_e, k_e = q_all[rows, :], k_all[rows, :]
        k_d = (k_ref[rows, :].astype(F32) * jnp.exp(b_last - b)).astype(BF16)
        decays.append(jnp.exp(b_last))
        for h in range(heads):
            ks = slice(h * dk, (h + 1) * dk)
            v_h = v_ref[rows, h * dv:(h + 1) * dv]
            scores = lax.dot_general(q_e[:, ks], k_e[:, ks], (((1,), (1,)), ((), ())),
                                     preferred_element_type=F32)
            p_scr[rows, h * chunk:(h + 1) * chunk] = jnp.where(causal, scores, 0.0).astype(BF16)
            kv_scr[ci, h] = lax.dot_general(k_d[:, ks], v_h, (((0,), (0,)), ((), ())),
                                            preferred_element_type=F32)

    for ci in range(n_chunks):
        rows = slice(ci * chunk, (ci + 1) * chunk)
        seq = ci // chunks_per_seq
        for h in range(heads):
            ks = slice(h * dk, (h + 1) * dk)
            vs = slice(h * dv, (h + 1) * dv)
            s_old = s_scr[seq, h]
            o_scr[rows, vs] = (_bdot(p_scr[rows, h * chunk:(h + 1) * chunk], v_ref[rows, vs]) +
                               _bdot(q_all[rows, ks], s_old.astype(BF16)))
            s_scr[seq, h] = jnp.transpose(decays[ci][:, ks]) * s_old + kv_scr[ci, h]

    for h in range(heads):
        vs = slice(h * dv, (h + 1) * dv)
        o = o_scr[:, vs]
        mu = jnp.mean(o, axis=-1, keepdims=True)
        oc = o - mu
        var = jnp.mean(oc * oc, axis=-1, keepdims=True)
        on = oc * lax.rsqrt(var + LN_EPS) * ng_ref[:, vs]
        g = g_ref[:, vs].astype(F32)
        o_ref[:, vs] = (on * (g * _sigmoid(g))).astype(BF16)

    @pl.when(t == pl.num_programs(1) - 1)
    def _():
        if state_slices == 0:
            so_ref[...] = s_scr[...]
        for layer_slice in range(state_slices):
            so_ref[layer_slice] = s_scr[...] if layer_slice == state_slot else jnp.zeros_like(s_scr)


def _gla(qkvg, loga, norm_g, s0, layer, nb, s_len, rows, chunk, heads, dk, dv, stack=None):
    t = nb * s_len
    seqs = max(1, rows // s_len)
    assert s_len * seqs % rows == 0 and nb % seqs == 0 and s_len % chunk == 0
    nb, tps = nb // seqs, s_len * seqs // rows
    kw, vw = heads * dk, heads * dv
    has_state = s0 is not None
    state_shape, state_block, state_index = (nb * seqs, heads, dk, dv), (seqs, heads, dk, dv), lambda b, i: (b, 0, 0, 0)
    state_slices, state_slot, aliases = 0, 0, {}
    if stack is not None:
        depth, prev = stack
        state_shape = (depth,) + state_shape
        if prev is None:
            state_slices, state_slot = depth, layer
            state_block, state_index = (depth,) + state_block, lambda b, i: (0, b, 0, 0, 0)
        else:
            state_slices = 1
            state_block, state_index = (1,) + state_block, lambda b, i: (layer, b, 0, 0, 0)
    in_specs = [pl.BlockSpec((rows, kw), lambda b, i: (b * tps + i, 0)),
                pl.BlockSpec((rows, kw), lambda b, i: (b * tps + i, 1)),
                pl.BlockSpec((rows, vw), lambda b, i: (b * tps + i, (2 * kw) // vw)),
                pl.BlockSpec((rows, kw), lambda b, i: (b * tps + i, 0)),
                pl.BlockSpec((rows, vw), lambda b, i: (b * tps + i, (2 * kw) // vw + 1)),
                _const_spec(norm_g.shape)]
    args = [qkvg, qkvg, qkvg, loga, qkvg, norm_g]
    if has_state:
        in_specs.append(pl.BlockSpec((1, seqs, heads, dk, dv), lambda b, i: (layer, b, 0, 0, 0)))
        args.append(s0)
    if stack is not None and stack[1] is not None:
        in_specs.append(pl.BlockSpec(memory_space=pl.ANY))
        aliases[len(args)] = 1
        args.append(stack[1])
    return pl.pallas_call(
        functools.partial(_gla_kernel, chunk=chunk, n_chunks=rows // chunk, heads=heads, has_state=has_state,
                          n_inputs=len(args), state_slices=state_slices, state_slot=state_slot),
        grid=(nb, tps),
        in_specs=in_specs,
        out_specs=[pl.BlockSpec((rows, vw), lambda b, i: (b * tps + i, 0)),
                   pl.BlockSpec(state_block, state_index)],
        out_shape=[jax.ShapeDtypeStruct((t, vw), BF16), jax.ShapeDtypeStruct(state_shape, F32)],
        scratch_shapes=[pltpu.VMEM((seqs, heads, dk, dv), F32), pltpu.VMEM((rows, vw), F32),
                        pltpu.VMEM((rows, heads * chunk), BF16), pltpu.VMEM((rows // chunk, heads, dk, dv), F32)],
        input_output_aliases=aliases,
        compiler_params=_params("arbitrary", "arbitrary"),
        name="gla",
    )(*args)


def _rg_gates(xc, wax_ref, ba_ref, bx_ref, lam_ref):
    n_blocks, bw, _ = wax_ref.shape
    r_parts, i_parts = [], []
    for n in range(n_blocks):
        cs = slice(n * bw, (n + 1) * bw)
        pre = _bdot(xc[:, cs].astype(BF16), wax_ref[n])
        r_parts.append(_sigmoid(pre[:, :bw] + ba_ref[:, cs]))
        i_parts.append(_sigmoid(pre[:, bw:] + bx_ref[:, cs]))
    r = jnp.concatenate(r_parts, axis=1)
    i = jnp.concatenate(i_parts, axis=1)
    log_a = -RG_C * r * _softplus(-lam_ref[...])
    a = jnp.exp(log_a)
    bx = jnp.sqrt(jnp.tanh(-log_a) * (1.0 + a * a)) * (i * xc)
    return a, bx


def _rglru_seq_kernel(xr_ref, cw_ref, cb_ref, wax_ref, ba_ref, bx_ref, lam_ref,
                      h_ref, hl_ref, cv_ref, cbuf, a_scr, b_scr, h_scr, hc):
    t = pl.program_id(1)
    rows = xr_ref.shape[0]
    taps = cw_ref.shape[0]
    head = 8

    @pl.when(t == 0)
    def _():
        cbuf[0:head, :] = jnp.zeros((head, cbuf.shape[1]), F32)
        hc[...] = jnp.zeros_like(hc)

    cbuf[head:head + rows, :] = xr_ref[...]
    first = head - (taps - 1)
    xc = cb_ref[...] + sum(cbuf[first + j:first + j + rows, :] * cw_ref[j:j + 1, :] for j in range(taps))
    tail = cbuf[first + rows:head + rows, :]
    cbuf[first:head, :] = tail

    a, bx = _rg_gates(xc, wax_ref, ba_ref, bx_ref, lam_ref)
    a_scr[...] = a
    b_scr[...] = bx

    def step(i, h):
        h = a_scr[pl.ds(i, 1), :] * h + b_scr[pl.ds(i, 1), :]
        h_scr[pl.ds(i, 1), :] = h
        return h

    h_last = lax.fori_loop(0, rows, step, hc[...], unroll=8)
    hc[...] = h_last
    h_ref[...] = h_scr[...].astype(BF16)

    @pl.when(t == pl.num_programs(1) - 1)
    def _():
        hl_ref[0] = h_last
        cv_ref[0] = tail


def _rglru_seq(xr, w, nb, s_len, rows):
    width = xr.shape[1]
    t = nb * s_len
    tps = s_len // rows
    taps = w["cw"].shape[0]
    consts = [w["cw"], w["cb"], w["wax"], w["rba"], w["rbx"], w["lam"]]
    return pl.pallas_call(
        _rglru_seq_kernel,
        grid=(nb, tps),
        in_specs=[pl.BlockSpec((rows, width), lambda b, i: (b * tps + i, 0))] + [_const_spec(c.shape) for c in consts],
        out_specs=[pl.BlockSpec((rows, width), lambda b, i: (b * tps + i, 0)),
                   pl.BlockSpec((1, 1, width), lambda b, i: (b, 0, 0)),
                   pl.BlockSpec((1, taps - 1, width), lambda b, i: (b, 0, 0))],
        out_shape=[jax.ShapeDtypeStruct((t, width), BF16),
                   jax.ShapeDtypeStruct((nb, 1, width), F32),
                   jax.ShapeDtypeStruct((nb, taps - 1, width), F32)],
        scratch_shapes=[pltpu.VMEM((rows + 8, width), F32), pltpu.VMEM((rows, width), F32),
                        pltpu.VMEM((rows, width), F32), pltpu.VMEM((rows, width), F32),
                        pltpu.VMEM((1, width), F32)],
        compiler_params=_params("arbitrary", "arbitrary"),
        name="rglru_seq",
    )(xr, *consts)


def _rglru_step_kernel(xr_ref, sc_ref, h0_ref, cw_ref, cb_ref, wax_ref, ba_ref, bx_ref, lam_ref,
                       h_ref, hn_ref, cv_ref):
    taps = cw_ref.shape[0]
    xr = xr_ref[...]
    xc = cb_ref[...] + sum(sc_ref[j] * cw_ref[j:j + 1, :] for j in range(taps - 1)) + xr * cw_ref[taps - 1:taps, :]
    a, bx = _rg_gates(xc, wax_ref, ba_ref, bx_ref, lam_ref)
    h = a * h0_ref[...] + bx
    h_ref[...] = h.astype(BF16)
    hn_ref[...] = h
    for j in range(taps - 2):
        cv_ref[j] = sc_ref[j + 1]
    cv_ref[taps - 2] = xr


def _rglru_step(xr, conv_t, h0, w):
    n, width = xr.shape
    consts = [w["cw"], w["cb"], w["wax"], w["rba"], w["rbx"], w["lam"]]
    args = [xr, conv_t, h0] + consts
    return pl.pallas_call(
        _rglru_step_kernel,
        grid=(1,),
        in_specs=[_const_spec(a.shape) for a in args],
        out_specs=[_const_spec((n, width)), _const_spec((n, width)), _const_spec(conv_t.shape)],
        out_shape=[jax.ShapeDtypeStruct((n, width), BF16),
                   jax.ShapeDtypeStruct((n, width), F32),
                   jax.ShapeDtypeStruct(conv_t.shape, F32)],
        compiler_params=_params("arbitrary"),
        name="rglru_step",
    )(*args)


def _pack_bf16_pairs(x):
    half = x.shape[1] // 2
    hi = pltpu.bitcast(x[:, :half].astype(BF16).astype(F32), jnp.uint32)
    lo = pltpu.bitcast(x[:, half:].astype(BF16).astype(F32), jnp.uint32)
    return hi | (lo >> 16)


def _unpack_bf16_pairs(p):
    hi = pltpu.bitcast(p & jnp.uint32(0xFFFF0000), F32)
    lo = pltpu.bitcast(p << 16, F32)
    return jnp.concatenate([hi, lo], axis=1)


def _post_kernel(x_ref, xt_ref, o_ref, ot_ref, h_ref, ht_ref, gate_ref, wb0_ref, wb1_ref, wo_ref, g1_ref, b1_ref,
                 rw_ref, rb_ref, x1_ref, x1p_ref, idx_ref, wt_ref, cnt_ref, cnt_scr, *, alpha, n_experts):
    d = x_ref.shape[1]
    tm = x_ref.shape[0]

    @pl.when(pl.program_id(0) == 0)
    def _():
        cnt_scr[...] = jnp.zeros_like(cnt_scr)

    x = _head_or_tail(x_ref, xt_ref)
    o = _head_or_tail(o_ref, ot_ref)
    h = _head_or_tail(h_ref, ht_ref)
    gate = gate_ref[...].astype(F32)
    merged = gate[:, :d] * _bdot(o, wb0_ref[...]) + gate[:, d:] * _bdot(h, wb1_ref[...])
    mix = _bdot(merged.astype(BF16), wo_ref[...])
    x1 = _layernorm(alpha * x + mix, g1_ref[...], b1_ref[...])
    x1_ref[...] = x1
    x1p_ref[...] = _pack_bf16_pairs(x1)

    xh = x1.astype(BF16)
    xl = (x1 - xh.astype(F32)).astype(BF16)
    n_lanes = rw_ref.shape[1] // 2
    hi_terms = _bdot(xh, rw_ref[...])
    logits = (hi_terms[:, n_lanes:] + _bdot(xl, rw_ref[:, :n_lanes])) + hi_terms[:, :n_lanes]
    logits = logits + rb_ref[...]
    lane = lax.broadcasted_iota(jnp.int32, logits.shape, 1)
    lane_f = lane.astype(F32)
    neg_inf = jnp.float32(-jnp.inf)
    cur = jnp.where(lane < n_experts, logits, neg_inf)
    vals, idxs = [], []
    for _ in range(TOP_K):
        m = jnp.max(cur, axis=-1, keepdims=True)
        sel = jnp.min(jnp.where(cur == m, lane_f, float(LANES)), axis=-1, keepdims=True)
        vals.append(m)
        idxs.append(sel)
        cur = jnp.where(lane_f == sel, neg_inf, cur)
    exps = [jnp.exp(v - vals[0]) for v in vals]
    total = sum(exps)

    onehots = [jnp.where(lane_f == idxs[j], 1.0, 0.0) for j in range(TOP_K)]
    chosen = sum(onehots)
    r = lax.broadcasted_iota(jnp.int32, (tm, tm), 0)
    c = lax.broadcasted_iota(jnp.int32, (tm, tm), 1)
    earlier = jnp.where(r > c, 1.0, 0.0).astype(BF16)
    before = _bdot(earlier, chosen.astype(BF16)) + cnt_scr[...]
    ranks = [jnp.sum(onehots[j] * before, axis=-1, keepdims=True) for j in range(TOP_K)]
    cnt_scr[...] += jnp.sum(chosen, axis=0, keepdims=True)
    cnt_ref[...] = cnt_scr[...]

    idx_out = jnp.zeros(logits.shape, F32)
    wt_out = jnp.zeros(logits.shape, F32)
    for j in range(TOP_K):
        idx_out = jnp.where(lane == j, idxs[j], idx_out)
        idx_out = jnp.where(lane == TOP_K + j, ranks[j], idx_out)
        wt_out = jnp.where(lane == j, exps[j] / total, wt_out)
    idx_ref[...] = idx_out.astype(jnp.int32)
    wt_ref[...] = wt_out


def _post(x_head, x_tail, n_head_tiles, tail_tile, o, h, o_tail, h_tail, gate, w, tm, alpha, n_experts):
    d = x_head.shape[1]
    t = (n_head_tiles + 1) * tm
    assert o.shape[0] == t - tm and o_tail.shape[0] == tm and gate.shape[0] == t
    consts = [w["wb0"], w["wb1"], w["wo"], w["ln1g"], w["ln1b"], w["rw"], w["rb"]]
    row = lambda width: pl.BlockSpec((tm, width), lambda i: (i, 0))
    return pl.pallas_call(
        functools.partial(_post_kernel, alpha=alpha, n_experts=n_experts),
        grid=(t // tm,),
        in_specs=_head_tail_specs(n_head_tiles, tail_tile, tm, d) + _head_tail_specs(n_head_tiles, 0, tm, d) +
                 _head_tail_specs(n_head_tiles, 0, tm, d) + [row(2 * d)] + [_weight_spec(c.shape) for c in consts],
        out_specs=[row(d), row(d // 2), row(LANES), row(LANES), _const_spec((1, LANES))],
        out_shape=[jax.ShapeDtypeStruct((t, d), F32),
                   jax.ShapeDtypeStruct((t, d // 2), jnp.uint32),
                   jax.ShapeDtypeStruct((t, LANES), jnp.int32),
                   jax.ShapeDtypeStruct((t, LANES), F32),
                   jax.ShapeDtypeStruct((1, LANES), F32)],
        scratch_shapes=[pltpu.VMEM((1, LANES), F32)],
        compiler_params=_params("arbitrary"),
        name="post",
    )(x_head, x_tail, o, o_tail, h, h_tail, gate, *consts)


SC_INDEX_MAX = 128
SC_ALIGN = 8
SC_BUFFER_BYTES = 208 * 1024


def _sc_plan(n_rows, row_bytes):
    info = plsc.get_sparse_core_info()
    n_workers = info.num_cores * info.num_subcores
    per_worker, rem = divmod(n_rows, n_workers)
    assert rem == 0 and per_worker % SC_ALIGN == 0, "rows must split into aligned equal shares per subcore"
    cap = min(SC_INDEX_MAX, SC_BUFFER_BYTES // row_bytes)
    chunk = max(c for c in range(SC_ALIGN, cap + 1, SC_ALIGN) if per_worker % c == 0)
    return info.num_cores, n_workers, per_worker, chunk


def _sc_gather(table, idx):
    n_rows = idx.shape[0]
    d = table.shape[1]
    n_cores, _, per_worker, chunk = _sc_plan(n_rows, d * table.dtype.itemsize)
    mesh = plsc.VectorSubcoreMesh(core_axis_name="c", subcore_axis_name="s")

    n_chunks = per_worker // chunk

    @functools.partial(
        pl.kernel, mesh=mesh, out_type=jax.ShapeDtypeStruct((n_rows, d), table.dtype),
        scratch_types=[pltpu.VMEM((chunk,), jnp.int32)] * 2 + [pltpu.VMEM((chunk, d), table.dtype)] * 2 +
                      [pltpu.SemaphoreType.DMA] * 4)
    def gather(table_hbm, idx_hbm, out_hbm, idx0, idx1, rows0, rows1, g0, g1, w0, w1):
        idx_v, rows_v, gsem, wsem = (idx0, idx1), (rows0, rows1), (g0, g1), (w0, w1)
        worker = lax.axis_index("s") * n_cores + lax.axis_index("c")
        base = worker * per_worker

        def rows_of(c):
            return pl.ds(pl.multiple_of(base + c * chunk, SC_ALIGN), chunk)

        def start_gather(c):
            s = c % 2
            pltpu.sync_copy(idx_hbm.at[rows_of(c)], idx_v[s])
            return pltpu.async_copy(table_hbm.at[idx_v[s]], rows_v[s], gsem[s])

        gathers = {c: start_gather(c) for c in range(min(2, n_chunks))}
        writes = {}
        for c in range(n_chunks):
            s = c % 2
            gathers[c].wait()
            writes[c] = pltpu.async_copy(rows_v[s], out_hbm.at[rows_of(c)], wsem[s])
            if c + 2 < n_chunks:
                writes.pop(c).wait()
                gathers[c + 2] = start_gather(c + 2)
        for write in writes.values():
            write.wait()

    return gather(table, idx)


def _sc_scatter(x, pos, n_out):
    n, d = x.shape
    assert pos.shape == (TOP_K * n,)
    n_cores, _, per_worker, chunk = _sc_plan(n, d * x.dtype.itemsize)
    mesh = plsc.VectorSubcoreMesh(core_axis_name="c", subcore_axis_name="s")

    n_chunks = per_worker // chunk

    @functools.partial(
        pl.kernel, mesh=mesh, out_type=jax.ShapeDtypeStruct((n_out, d), x.dtype),
        scratch_types=[pltpu.VMEM((chunk, d), x.dtype)] * 2 + [pltpu.VMEM((chunk,), jnp.int32)] * TOP_K +
                      [pltpu.SemaphoreType.DMA] * 3)
    def scatter(x_hbm, pos_hbm, out_hbm, rows0, rows1, *rest):
        rows_v, idx_v, (l0, l1, ssem) = (rows0, rows1), rest[:TOP_K], rest[TOP_K:]
        lsem = (l0, l1)
        worker = lax.axis_index("s") * n_cores + lax.axis_index("c")
        base = worker * per_worker

        def first_row(c):
            return pl.multiple_of(base + c * chunk, SC_ALIGN)

        def start_load(c):
            return pltpu.async_copy(x_hbm.at[pl.ds(first_row(c), chunk)], rows_v[c % 2], lsem[c % 2])

        load = start_load(0)
        for c in range(n_chunks):
            next_load = start_load(c + 1) if c + 1 < n_chunks else None
            for k in range(TOP_K):
                pltpu.sync_copy(pos_hbm.at[pl.ds(k * n + first_row(c), chunk)], idx_v[k])
            load.wait()
            scatters = [pltpu.async_copy(rows_v[c % 2], out_hbm.at[idx_v[k]], ssem)
                        for k in range(TOP_K)]
            for scatter_k in scatters:
                scatter_k.wait()
            load = next_load

    return scatter(x, pos)


def _ffn_kernel(be_ref, valid_ref, x_ref, wgu_ref, bgu_ref, wd_ref, bd_ref, o_ref, wgu_bf, wd_bf):
    i = pl.program_id(0)
    bm = x_ref.shape[0]
    f = wd_ref.shape[2]
    valid = valid_ref[i]
    new_expert = jnp.logical_or(i == 0, be_ref[i] != be_ref[jnp.maximum(i - 1, 0)])

    @pl.when(new_expert)
    def _():
        wgu_bf[...] = wgu_ref[0, 0].astype(BF16)
        wd_bf[...] = wd_ref[0, 0].astype(BF16)

    def expert_rows(rows):
        x = _unpack_bf16_pairs(x_ref[rows, :]).astype(BF16)
        acc = jnp.zeros((x.shape[0], wd_ref.shape[3]), F32)
        for c in range(0, f, FFN_COLS):
            gate = _bdot(x, wgu_bf[:, c:c + FFN_COLS]) + bgu_ref[0, 0, :, c:c + FFN_COLS]
            up = _bdot(x, wgu_bf[:, f + c:f + c + FFN_COLS]) + bgu_ref[0, 0, :, f + c:f + c + FFN_COLS]
            gate = jnp.minimum(gate, SWIGLU_LIMIT)
            up = jnp.clip(up, -SWIGLU_LIMIT, SWIGLU_LIMIT)
            act = (up + 1.0) * gate * _sigmoid(SWIGLU_ALPHA * gate)
            acc = acc + _bdot(act.astype(BF16), wd_bf[c:c + FFN_COLS, :])
        o_ref[rows, :] = _pack_bf16_pairs(acc + bd_ref[0, 0])

    @pl.when(valid == bm)
    def _():
        expert_rows(slice(0, bm))

    @pl.when(valid < bm)
    def _():
        for s in range(0, bm, FFN_SUB_ROWS):
            rows = slice(s, s + FFN_SUB_ROWS)

            @pl.when(s < valid)
            def _():
                expert_rows(rows)

            @pl.when(s >= valid)
            def _():
                o_ref[rows, :] = jnp.zeros((FFN_SUB_ROWS, o_ref.shape[1]), o_ref.dtype)


def _ffn(block_e, valid, xs, layer, w_gu, b_gu, w_down, b_down, bm):
    n_layers, n_exp, d, f2 = w_gu.shape
    f = f2 // 2
    n_blocks = block_e.shape[0]
    expert = lambda i, be, nu: (layer, be[i], 0, 0)
    grid_spec = pltpu.PrefetchScalarGridSpec(
        num_scalar_prefetch=2,
        grid=(n_blocks,),
        in_specs=[pl.BlockSpec((bm, d // 2), lambda i, be, nu: (i, 0)),
                  pl.BlockSpec((1, 1, d, f2), expert),
                  pl.BlockSpec((1, 1, 1, f2), expert),
                  pl.BlockSpec((1, 1, f, d), expert),
                  pl.BlockSpec((1, 1, 1, d), expert)],
        out_specs=pl.BlockSpec((bm, d // 2), lambda i, be, nu: (i, 0)),
        scratch_shapes=[pltpu.VMEM((d, f2), BF16), pltpu.VMEM((f, d), BF16)],
    )
    return pl.pallas_call(
        _ffn_kernel,
        grid_spec=grid_spec,
        out_shape=jax.ShapeDtypeStruct((n_blocks * bm, d // 2), jnp.uint32),
        compiler_params=_params("arbitrary"),
        name="ffn",
    )(block_e, valid, xs, w_gu, b_gu.reshape(n_layers, n_exp, 1, f2),
      w_down, b_down.reshape(n_layers, n_exp, 1, d))


def _combine_kernel(x_ref, *refs, alpha, n_inputs, own_tail):
    y_refs, (wt_ref, g_ref, b_ref), out_refs = refs[:TOP_K], refs[TOP_K:TOP_K + 3], refs[n_inputs - 1:]
    wt = wt_ref[...]
    ffn = sum(wt[:, j:j + 1] * _unpack_bf16_pairs(y_refs[j][...]) for j in range(TOP_K))
    res = _layernorm(alpha * x_ref[...] + ffn, g_ref[...], b_ref[...])
    if not own_tail:
        out_refs[0][...] = res
    else:
        is_tail = pl.program_id(0) == pl.num_programs(0) - 1

        @pl.when(jnp.logical_not(is_tail))
        def _():
            out_refs[0][...] = res

        @pl.when(is_tail)
        def _():
            out_refs[1][...] = res


def _combine(x1, y, wt, g, b, tm, alpha, tile0, n_tiles, prev=None, split=False):
    t, d = x1.shape
    n_all = t // tm
    own_tail = split and tile0 + n_tiles == n_all
    row = lambda width: pl.BlockSpec((tm, width), lambda i: (tile0 + i, 0))
    planes = [pl.BlockSpec((tm, d // 2), lambda i, j=j: (j * n_tiles + i, 0)) for j in range(TOP_K)]
    in_specs = [row(d)] + planes + [row(LANES), _const_spec(g.shape), _const_spec(b.shape)]
    args = [x1] + [y] * TOP_K + [wt, g, b]
    aliases = {}
    if prev is not None:
        in_specs.append(pl.BlockSpec(memory_space=pl.ANY))
        aliases[len(args)] = 0
        args.append(prev)
    if split:
        out_specs = [pl.BlockSpec((tm, d), lambda i: (jnp.minimum(tile0 + i, n_all - 2), 0))]
        out_shape = [jax.ShapeDtypeStruct((t - tm, d), F32)]
        if own_tail:
            out_specs.append(_const_spec((tm, d)))
            out_shape.append(jax.ShapeDtypeStruct((tm, d), F32))
    else:
        out_specs = [row(d)]
        out_shape = [jax.ShapeDtypeStruct((t, d), F32)]
    return pl.pallas_call(
        functools.partial(_combine_kernel, alpha=alpha, n_inputs=len(args), own_tail=own_tail),
        grid=(n_tiles,),
        in_specs=in_specs,
        out_specs=out_specs,
        out_shape=out_shape,
        input_output_aliases=aliases,
        compiler_params=_params("arbitrary"),
        name="combine",
    )(*args)


def _route(experts, ranks, counts, bm):
    n_experts = counts.shape[0]
    i32 = jnp.int32
    padded = (counts + bm - 1) // bm * bm
    ends_pad = jnp.cumsum(padded)
    start_pad = ends_pad - padded
    n_blocks = -(-experts.size // bm) + n_experts
    first_row = jnp.arange(n_blocks, dtype=i32)[:, None] * bm
    block_e = jnp.minimum(jnp.sum(ends_pad[None, :] <= first_row, axis=1), n_experts - 1).astype(i32)
    valid = jnp.clip(counts[block_e] - (first_row[:, 0] - start_pad[block_e]), 0, bm).astype(i32)
    onehot = experts[..., None] == jnp.arange(n_experts, dtype=i32)
    pos = jnp.sum(jnp.where(onehot, start_pad, 0), axis=-1).astype(i32) + ranks
    return block_e, valid, pos


def _moe(x1, x1p, routing, counts, top_w_pad, w, layer, moe_weights, alpha, tm, split):
    n_experts = moe_weights[0].shape[1]
    block_e, valid, pos = _route(routing[:, :TOP_K], routing[:, TOP_K:2 * TOP_K],
                                 counts[0, :n_experts].astype(jnp.int32), MOE_ROWS)
    pos_planes = pos.T
    xs = _sc_scatter(x1p, pos_planes.reshape(-1), block_e.shape[0] * MOE_ROWS)
    out_rows = _ffn(block_e, valid, xs, layer, *moe_weights, MOE_ROWS)
    n_all = x1.shape[0] // tm
    n_first = (n_all + 1) // 2
    y_first = _sc_gather(out_rows, pos_planes[:, :n_first * tm].reshape(-1))
    y_second = _sc_gather(out_rows, pos_planes[:, n_first * tm:].reshape(-1))
    first = _combine(x1, y_first, top_w_pad, w["ln2g"], w["ln2b"], tm, alpha, 0, n_first, None, split)
    return _combine(x1, y_second, top_w_pad, w["ln2g"], w["ln2b"], tm, alpha, n_first, n_all - n_first,
                    first[0], split)


def _hi_lo(w):
    hi = w.astype(BF16)
    return jnp.concatenate([hi, (w - hi.astype(F32)).astype(BF16)], axis=1)


def _layer_weights(layer, w_in, gla_w_a2, gla_b_a, gla_norm_g, rg_conv_w, rg_conv_b, rg_w_a, rg_b_a, rg_w_x,
                   rg_b_x, rg_lambda, b_merge, w_branch, w_o, ln1_g, ln1_b, ln2_g, ln2_b, router_w, router_b,
                   kw, vw):
    d = w_in.shape[1]
    rank = gla_w_a2.shape[1]
    width = rg_conv_w.shape[2]
    n_exp = router_w.shape[2]
    c0 = 2 * kw + 2 * vw
    wi = w_in[layer]
    row = lambda v: v.reshape(1, -1)
    return {
        "wq": wi[:, :c0].astype(BF16),
        "wal": jnp.pad(wi[:, c0:c0 + rank], ((0, 0), (0, LANES - rank))).astype(BF16),
        "wa2": jnp.pad(gla_w_a2[layer], ((0, LANES - rank), (0, 0))).astype(BF16),
        "ba": row(gla_b_a[layer]),
        "wxr": wi[:, c0 + rank:c0 + rank + width].astype(BF16),
        "wmg": wi[:, c0 + rank + width:].astype(BF16),
        "bmg": row(b_merge[layer]),
        "ng": row(gla_norm_g[layer]),
        "cw": rg_conv_w[layer], "cb": row(rg_conv_b[layer]),
        "wax": jnp.concatenate([rg_w_a[layer], rg_w_x[layer]], axis=-1).astype(BF16),
        "rba": row(rg_b_a[layer]), "rbx": row(rg_b_x[layer]), "lam": row(rg_lambda[layer]),
        "wb0": w_branch[layer, 0].astype(BF16), "wb1": w_branch[layer, 1].astype(BF16),
        "wo": w_o[layer].astype(BF16),
        "ln1g": row(ln1_g[layer]), "ln1b": row(ln1_b[layer]),
        "ln2g": row(ln2_g[layer]), "ln2b": row(ln2_b[layer]),
        "rw": _hi_lo(jnp.pad(router_w[layer], ((0, 0), (0, LANES - n_exp)))),
        "rb": jnp.pad(row(router_b[layer]), ((0, 0), (0, LANES - n_exp))),
    }


def kernel(x_prompt, x_sample, state_gla, state_rglru, state_conv, ln1_g, ln1_b, w_in, gla_w_a2, gla_b_a, gla_norm_g, rg_conv_w, rg_conv_b, rg_w_a, rg_b_a, rg_w_x, rg_b_x, rg_lambda, b_merge, w_branch, w_o, ln2_g, ln2_b, router_w, router_b, moe_w_gu, moe_b_gu, moe_w_down, moe_b_down):
    n_p, seq, d = x_prompt.shape
    n_s, dec_seq, _ = x_sample.shape
    assert dec_seq == 1, "the sample group carries one new token per sequence"
    depth, _, heads, dk, dv = state_gla.shape
    kw, vw = heads * dk, heads * dv
    n_exp = router_w.shape[2]
    alpha = (2.0 * depth) ** 0.25
    t_p = n_p * seq
    pad_rows = BF16_SUBLANES

    tile = PROJ_ROWS
    assert t_p % tile == 0 and n_s <= tile
    n_head_tiles = t_p // tile
    tail_pad = lambda a: jnp.pad(a, ((0, tile - n_s), (0, 0)))
    x_head, x_tail, tail_tile = x_prompt.reshape(t_p, d), tail_pad(x_sample.reshape(n_s, d)), 0
    moe_weights = (moe_w_gu, moe_b_gu, moe_w_down, moe_b_down)
    outs = {k: [] for k in ("gla_p", "rg_p", "cv_p", "rg_s", "cv_s")}
    gla_s = None
    for layer in range(depth):
        w = _layer_weights(layer, w_in, gla_w_a2, gla_b_a, gla_norm_g, rg_conv_w, rg_conv_b, rg_w_a, rg_b_a,
                           rg_w_x, rg_b_x, rg_lambda, b_merge, w_branch, w_o, ln1_g, ln1_b, ln2_g, ln2_b,
                           router_w, router_b, kw, vw)
        qkvg, loga, xr, gate = _proj(x_head, x_tail, n_head_tiles, tail_tile, w, tile)
        o_p, s_p = _gla(qkvg, loga, w["ng"], None, layer, n_p, seq, SEQ_ROWS, GLA_CHUNK, heads, dk, dv)
        h_p, hl_p, cv_p = _rglru_seq(xr, w, n_p, seq, SEQ_ROWS)
        pad = lambda a: jnp.pad(a[t_p:t_p + n_s, None, :],
                                ((0, 0), (0, pad_rows - 1), (0, 0))).reshape(n_s * pad_rows, -1)
        o_s, gla_s = _gla(pad(qkvg), pad(loga), w["ng"], state_gla, layer, n_s, pad_rows,
                          DECODE_SEQS * pad_rows, pad_rows, heads, dk, dv, stack=(depth, gla_s))
        o_s = o_s.reshape(n_s, pad_rows, vw)[:, 0]
        h_s, hn_s, cv_s = _rglru_step(xr[t_p:t_p + n_s], jnp.swapaxes(state_conv[layer], 0, 1),
                                      state_rglru[layer], w)
        x1, x1p, routing, wt, cnt = _post(x_head, x_tail, n_head_tiles, tail_tile, o_p, h_p, tail_pad(o_s),
                                          tail_pad(h_s), gate, w, tile, alpha, n_exp)
        last = layer == depth - 1
        x = _moe(x1, x1p, routing, cnt, wt, w, layer, moe_weights, alpha, tile, split=last)
        if not last:
            x_head, x_tail, tail_tile = x[0], x[0], n_head_tiles
        outs["gla_p"].append(s_p)
        outs["rg_p"].append(hl_p.reshape(n_p, -1))
        outs["cv_p"].append(cv_p)
        outs["rg_s"].append(hn_s)
        outs["cv_s"].append(jnp.swapaxes(cv_s, 0, 1))
    y_head, y_tail = x
    return (y_head.reshape(n_p, seq, d), y_tail[:n_s].reshape(n_s, dec_seq, d),
            jnp.stack(outs["gla_p"]), jnp.stack(outs["rg_p"]), jnp.stack(outs["cv_p"]),
            gla_s, jnp.stack(outs["rg_s"]), jnp.stack(outs["cv_s"]))
```

```python
import functools

import jax
import jax.numpy as jnp
from jax import lax
from jax.experimental import pallas as pl
from jax.experimental.pallas import tpu as pltpu
from jax.experimental.pallas import tpu_sc as plsc

F32 = jnp.float32
BF16 = jnp.bfloat16

TOP_K = 4
GLA_GATE_TAU = 16.0
GLA_CHUNK = 64
RG_C = 8.0
SWIGLU_LIMIT = 7.0
SWIGLU_ALPHA = 1.702
LN_EPS = 1e-5

LANES = 128
BF16_SUBLANES = 16
VMEM_LIMIT = 56 * 1024 * 1024

PROJ_ROWS = 512
SEQ_ROWS = 256
DECODE_SEQS = 8
MOE_ROWS = 1024
FFN_COLS = 512
FFN_SUB_ROWS = 256


def _params(*sem):
    return pltpu.CompilerParams(dimension_semantics=sem, vmem_limit_bytes=VMEM_LIMIT)


def _const_spec(shape):
    nd = len(shape)
    return pl.BlockSpec(shape, lambda *_: (0,) * nd)


def _weight_spec(shape):
    nd = len(shape)
    return pl.BlockSpec(shape, lambda *_: (0,) * nd, pipeline_mode=pl.Buffered(1))


def _bdot(a, b):
    return jnp.dot(a, b, preferred_element_type=F32)


def _split3(x):
    hi = x.astype(BF16)
    r1 = x - hi.astype(F32)
    mid = r1.astype(BF16)
    lo = (r1 - mid.astype(F32)).astype(BF16)
    return hi, mid, lo


def _sigmoid(x):
    return 0.5 * jnp.tanh(0.5 * x) + 0.5


def _log_sigmoid(x):
    return jnp.minimum(x, 0.0) - jnp.log1p(jnp.exp(-jnp.abs(x)))


def _softplus(x):
    return jnp.maximum(x, 0.0) + jnp.log1p(jnp.exp(-jnp.abs(x)))


def _layernorm(x, g, b):
    mu = jnp.mean(x, axis=-1, keepdims=True)
    xc = x - mu
    var = jnp.mean(xc * xc, axis=-1, keepdims=True)
    return xc * lax.rsqrt(var + LN_EPS) * g + b


def _head_tail_specs(n_head_tiles, tail_tile, tm, d):
    return [pl.BlockSpec((tm, d), lambda i: (jnp.minimum(i, n_head_tiles - 1), 0)),
            pl.BlockSpec((tm, d), lambda i: (tail_tile, 0))]


def _head_or_tail(head_ref, tail_ref):
    is_tail = pl.program_id(0) == pl.num_programs(0) - 1
    return jnp.where(is_tail, tail_ref[...], head_ref[...])


def _token_inputs(x_head, x_tail, n_head_tiles, tail_tile, tm):
    d = x_head.shape[1]
    if x_tail is x_head:
        assert tail_tile == n_head_tiles
        return [x_head], [pl.BlockSpec((tm, d), lambda i: (i, 0))]
    return [x_head, x_tail], _head_tail_specs(n_head_tiles, tail_tile, tm, d)


def _token_tile(x_refs):
    return x_refs[0][...] if len(x_refs) == 1 else _head_or_tail(*x_refs)


def _proj_kernel(*refs, n_x):
    x_refs = refs[:n_x]
    wq_ref, wal_ref, wa2_ref, ba_ref, wxr_ref, wmg_ref, bmg_ref, qkvg_ref, loga_ref, xr_ref, gate_ref = refs[n_x:]
    xb = _token_tile(x_refs).astype(BF16)
    d = xb.shape[1]
    for c in range(0, qkvg_ref.shape[1], d):
        qkvg_ref[:, c:c + d] = _bdot(xb, wq_ref[:, c:c + d]).astype(BF16)
    a_low = _bdot(xb, wal_ref[...])
    z = _bdot(a_low.astype(BF16), wa2_ref[...]) + ba_ref[...]
    loga_ref[...] = _log_sigmoid(z) * (1.0 / GLA_GATE_TAU)
    xr_ref[...] = _bdot(xb, wxr_ref[...])
    for c in range(0, gate_ref.shape[1], d):
        gate_ref[:, c:c + d] = _sigmoid(_bdot(xb, wmg_ref[:, c:c + d]) + bmg_ref[:, c:c + d]).astype(BF16)


def _proj(x_head, x_tail, n_head_tiles, tail_tile, w, tm):
    d = x_head.shape[1]
    t = (n_head_tiles + 1) * tm
    kw = w["wq"].shape[1]
    nk = w["wa2"].shape[1]
    consts = [w["wq"], w["wal"], w["wa2"], w["ba"], w["wxr"], w["wmg"], w["bmg"]]
    xs, x_specs = _token_inputs(x_head, x_tail, n_head_tiles, tail_tile, tm)
    return pl.pallas_call(
        functools.partial(_proj_kernel, n_x=len(xs)),
        grid=(t // tm,),
        in_specs=x_specs + [_weight_spec(c.shape) for c in consts],
        out_specs=[pl.BlockSpec((tm, kw), lambda i: (i, 0)),
                   pl.BlockSpec((tm, nk), lambda i: (i, 0)),
                   pl.BlockSpec((tm, d), lambda i: (i, 0)),
                   pl.BlockSpec((tm, 2 * d), lambda i: (i, 0))],
        out_shape=[jax.ShapeDtypeStruct((t, kw), BF16),
                   jax.ShapeDtypeStruct((t, nk), F32),
                   jax.ShapeDtypeStruct((t, d), F32),
                   jax.ShapeDtypeStruct((t, 2 * d), BF16)],
        compiler_params=_params("arbitrary"),
        name="proj",
    )(*xs, *consts)


def _gla_kernel(*refs, chunk, n_chunks, heads, has_state, n_inputs, state_slices, state_slot):
    q_ref, k_ref, v_ref, la_ref, g_ref, ng_ref = refs[:6]
    s0_ref = refs[6] if has_state else None
    o_ref, so_ref, s_scr, o_scr, p_scr, kv_scr = refs[n_inputs:]
    t = pl.program_id(1)
    dk = q_ref.shape[1] // heads
    dv = v_ref.shape[1] // heads
    scale = dk ** -0.5
    chunks_per_seq = n_chunks // s_scr.shape[0]

    @pl.when(t == 0)
    def _():
        if has_state:
            s_scr[...] = s0_ref[0]
        else:
            s_scr[...] = jnp.zeros_like(s_scr)

    n_rows = n_chunks * chunk
    r = lax.broadcasted_iota(jnp.int32, (n_rows, n_rows), 0)
    c = lax.broadcasted_iota(jnp.int32, (n_rows, n_rows), 1)
    shift = chunk.bit_length() - 1
    assert 1 << shift == chunk, "chunk must be a power of two"
    tril = jnp.where(jnp.logical_and(r >> shift == c >> shift, r >= c), 1.0, 0.0).astype(BF16)
    la_hi, la_mid, la_lo = _split3(la_ref[...])
    b_all = _bdot(tril, la_hi) + _bdot(tril, la_mid) + _bdot(tril, la_lo)
    q_all = (q_ref[...].astype(F32) * scale * jnp.exp(b_all)).astype(BF16)
    k_all = (k_ref[...].astype(F32) * jnp.exp(-b_all)).astype(BF16)
    causal = (lax.broadcasted_iota(jnp.int32, (chunk, chunk), 0) >=
              lax.broadcasted_iota(jnp.int32, (chunk, chunk), 1))

    decays = []
    for ci in range(n_chunks):
        rows = slice(ci * chunk, (ci + 1) * chunk)
        b = b_all[rows, :]
        b_last = b[chunk - 1:chunk, :]
        q_e, k_e = q_all[rows, :], k_all[rows, :]
        k_d = (k_ref[rows, :].astype(F32) * jnp.exp(b_last - b)).astype(BF16)
        decays.append(jnp.exp(b_last))
        for h in range(heads):
            ks = slice(h * dk, (h + 1) * dk)
            v_h = v_ref[rows, h * dv:(h + 1) * dv]
            scores = lax.dot_general(q_e[:, ks], k_e[:, ks], (((1,), (1,)), ((), ())),
                                     preferred_element_type=F32)
            p_scr[rows, h * chunk:(h + 1) * chunk] = jnp.where(causal, scores, 0.0).astype(BF16)
            kv_scr[ci, h] = lax.dot_general(k_d[:, ks], v_h, (((0,), (0,)), ((), ())),
                                            preferred_element_type=F32)

    for ci in range(n_chunks):
        rows = slice(ci * chunk, (ci + 1) * chunk)
        seq = ci // chunks_per_seq
        for h in range(heads):
            ks = slice(h * dk, (h + 1) * dk)
            vs = slice(h * dv, (h + 1) * dv)
            s_old = s_scr[seq, h]
            o_scr[rows, vs] = (_bdot(p_scr[rows, h * chunk:(h + 1) * chunk], v_ref[rows, vs]) +
                               _bdot(q_all[rows, ks], s_old.astype(BF16)))
            s_scr[seq, h] = jnp.transpose(decays[ci][:, ks]) * s_old + kv_scr[ci, h]

    for h in range(heads):
        vs = slice(h * dv, (h + 1) * dv)
        o = o_scr[:, vs]
        mu = jnp.mean(o, axis=-1, keepdims=True)
        oc = o - mu
        var = jnp.mean(oc * oc, axis=-1, keepdims=True)
        on = oc * lax.rsqrt(var + LN_EPS) * ng_ref[:, vs]
        g = g_ref[:, vs].astype(F32)
        o_ref[:, vs] = (on * (g * _sigmoid(g))).astype(BF16)

    @pl.when(t == pl.num_programs(1) - 1)
    def _():
        if state_slices == 0:
            so_ref[...] = s_scr[...]
        for layer_slice in range(state_slices):
            so_ref[layer_slice] = s_scr[...] if layer_slice == state_slot else jnp.zeros_like(s_scr)


def _gla(qkvg, loga, norm_g, s0, layer, nb, s_len, rows, chunk, heads, dk, dv, stack=None):
    t = nb * s_len
    seqs = max(1, rows // s_len)
    assert s_len * seqs % rows == 0 and nb % seqs == 0 and s_len % chunk == 0
    nb, tps = nb // seqs, s_len * seqs // rows
    kw, vw = heads * dk, heads * dv
    has_state = s0 is not None
    state_shape, state_block, state_index = (nb * seqs, heads, dk, dv), (seqs, heads, dk, dv), lambda b, i: (b, 0, 0, 0)
    state_slices, state_slot, aliases = 0, 0, {}
    if stack is not None:
        depth, prev = stack
        state_shape = (depth,) + state_shape
        if prev is None:
            state_slices, state_slot = depth, layer
            state_block, state_index = (depth,) + state_block, lambda b, i: (0, b, 0, 0, 0)
        else:
            state_slices = 1
            state_block, state_index = (1,) + state_block, lambda b, i: (layer, b, 0, 0, 0)
    in_specs = [pl.BlockSpec((rows, kw), lambda b, i: (b * tps + i, 0)),
                pl.BlockSpec((rows, kw), lambda b, i: (b * tps + i, 1)),
                pl.BlockSpec((rows, vw), lambda b, i: (b * tps + i, (2 * kw) // vw)),
                pl.BlockSpec((rows, kw), lambda b, i: (b * tps + i, 0)),
                pl.BlockSpec((rows, vw), lambda b, i: (b * tps + i, (2 * kw) // vw + 1)),
                _const_spec(norm_g.shape)]
    args = [qkvg, qkvg, qkvg, loga, qkvg, norm_g]
    if has_state:
        in_specs.append(pl.BlockSpec((1, seqs, heads, dk, dv), lambda b, i: (layer, b, 0, 0, 0)))
        args.append(s0)
    if stack is not None and stack[1] is not None:
        in_specs.append(pl.BlockSpec(memory_space=pl.ANY))
        aliases[len(args)] = 1
        args.append(stack[1])
    return pl.pallas_call(
        functools.partial(_gla_kernel, chunk=chunk, n_chunks=rows // chunk, heads=heads, has_state=has_state,
                          n_inputs=len(args), state_slices=state_slices, state_slot=state_slot),
        grid=(nb, tps),
        in_specs=in_specs,
        out_specs=[pl.BlockSpec((rows, vw), lambda b, i: (b * tps + i, 0)),
                   pl.BlockSpec(state_block, state_index)],
        out_shape=[jax.ShapeDtypeStruct((t, vw), BF16), jax.ShapeDtypeStruct(state_shape, F32)],
        scratch_shapes=[pltpu.VMEM((seqs, heads, dk, dv), F32), pltpu.VMEM((rows, vw), F32),
                        pltpu.VMEM((rows, heads * chunk), BF16), pltpu.VMEM((rows // chunk, heads, dk, dv), F32)],
        input_output_aliases=aliases,
        compiler_params=_params("arbitrary", "arbitrary"),
        name="gla",
    )(*args)


def _rg_gates(xc, wax_ref, ba_ref, bx_ref, lam_ref):
    n_blocks, bw, _ = wax_ref.shape
    r_parts, i_parts = [], []
    for n in range(n_blocks):
        cs = slice(n * bw, (n + 1) * bw)
        pre = _bdot(xc[:, cs].astype(BF16), wax_ref[n])
        r_parts.append(_sigmoid(pre[:, :bw] + ba_ref[:, cs]))
        i_parts.append(_sigmoid(pre[:, bw:] + bx_ref[:, cs]))
    r = jnp.concatenate(r_parts, axis=1)
    i = jnp.concatenate(i_parts, axis=1)
    log_a = r * (-RG_C * _softplus(-lam_ref[...]))
    a = jnp.exp(log_a)
    bx = jnp.sqrt(jnp.tanh(-log_a) * (1.0 + a * a)) * (i * xc)
    return a, bx


def _rglru_seq_kernel(xr_ref, cw_ref, cb_ref, wax_ref, ba_ref, bx_ref, lam_ref,
                      h_ref, hl_ref, cv_ref, cbuf, a_scr, b_scr, h_scr, hc):
    t = pl.program_id(1)
    rows = xr_ref.shape[0]
    taps = cw_ref.shape[0]
    head = 8

    @pl.when(t == 0)
    def _():
        cbuf[0:head, :] = jnp.zeros((head, cbuf.shape[1]), F32)
        hc[...] = jnp.zeros_like(hc)

    cbuf[head:head + rows, :] = xr_ref[...]
    first = head - (taps - 1)
    xc = cb_ref[...] + sum(cbuf[first + j:first + j + rows, :] * cw_ref[j:j + 1, :] for j in range(taps))
    tail = cbuf[first + rows:head + rows, :]
    cbuf[first:head, :] = tail

    a, bx = _rg_gates(xc, wax_ref, ba_ref, bx_ref, lam_ref)
    a_scr[...] = a
    b_scr[...] = bx

    def step(i, h):
        h = a_scr[pl.ds(i, 1), :] * h + b_scr[pl.ds(i, 1), :]
        h_scr[pl.ds(i, 1), :] = h
        return h

    h_last = lax.fori_loop(0, rows, step, hc[...], unroll=8)
    hc[...] = h_last
    h_ref[...] = h_scr[...].astype(BF16)

    @pl.when(t == pl.num_programs(1) - 1)
    def _():
        hl_ref[0] = h_last
        cv_ref[0] = tail


def _rglru_seq(xr, w, nb, s_len, rows):
    width = xr.shape[1]
    t = nb * s_len
    tps = s_len // rows
    taps = w["cw"].shape[0]
    consts = [w["cw"], w["cb"], w["wax"], w["rba"], w["rbx"], w["lam"]]
    return pl.pallas_call(
        _rglru_seq_kernel,
        grid=(nb, tps),
        in_specs=[pl.BlockSpec((rows, width), lambda b, i: (b * tps + i, 0))] + [_const_spec(c.shape) for c in consts],
        out_specs=[pl.BlockSpec((rows, width), lambda b, i: (b * tps + i, 0)),
                   pl.BlockSpec((1, 1, width), lambda b, i: (b, 0, 0)),
                   pl.BlockSpec((1, taps - 1, width), lambda b, i: (b, 0, 0))],
        out_shape=[jax.ShapeDtypeStruct((t, width), BF16),
                   jax.ShapeDtypeStruct((nb, 1, width), F32),
                   jax.ShapeDtypeStruct((nb, taps - 1, width), F32)],
        scratch_shapes=[pltpu.VMEM((rows + 8, width), F32), pltpu.VMEM((rows, width), F32),
                        pltpu.VMEM((rows, width), F32), pltpu.VMEM((rows, width), F32),
                        pltpu.VMEM((1, width), F32)],
        compiler_params=_params("arbitrary", "arbitrary"),
        name="rglru_seq",
    )(xr, *consts)


def _rglru_step_kernel(xr_ref, sc_ref, h0_ref, cw_ref, cb_ref, wax_ref, ba_ref, bx_ref, lam_ref,
                       h_ref, hn_ref, cv_ref):
    taps = cw_ref.shape[0]
    xr = xr_ref[...]
    xc = (cb_ref[...] + sum(sc_ref[:, j, :] * cw_ref[j:j + 1, :] for j in range(taps - 1)) +
          xr * cw_ref[taps - 1:taps, :])
    a, bx = _rg_gates(xc, wax_ref, ba_ref, bx_ref, lam_ref)
    h = a * h0_ref[...] + bx
    h_ref[...] = h.astype(BF16)
    hn_ref[...] = h
    for j in range(taps - 2):
        cv_ref[:, j, :] = sc_ref[:, j + 1, :]
    cv_ref[:, taps - 2, :] = xr


def _rglru_step(xr, conv, h0, w):
    n, width = xr.shape
    consts = [w["cw"], w["cb"], w["wax"], w["rba"], w["rbx"], w["lam"]]
    args = [xr, conv, h0] + consts
    return pl.pallas_call(
        _rglru_step_kernel,
        grid=(1,),
        in_specs=[_const_spec(a.shape) for a in args],
        out_specs=[_const_spec((n, width)), _const_spec((n, width)), _const_spec(conv.shape)],
        out_shape=[jax.ShapeDtypeStruct((n, width), BF16),
                   jax.ShapeDtypeStruct((n, width), F32),
                   jax.ShapeDtypeStruct(conv.shape, F32)],
        compiler_params=_params("arbitrary"),
        name="rglru_step",
    )(*args)


def _pack_bf16_pairs(x):
    half = x.shape[1] // 2
    hi = pltpu.bitcast(x[:, :half].astype(BF16).astype(F32), jnp.uint32)
    lo = pltpu.bitcast(x[:, half:].astype(BF16).astype(F32), jnp.uint32)
    return hi | (lo >> 16)


def _unpack_bf16_pairs(p):
    hi = pltpu.bitcast(p & jnp.uint32(0xFFFF0000), F32)
    lo = pltpu.bitcast(p << 16, F32)
    return jnp.concatenate([hi, lo], axis=1)


def _post_kernel(*refs, n_x, alpha, n_experts):
    x_refs = refs[:n_x]
    (o_ref, ot_ref, h_ref, ht_ref, gate_ref, wb0_ref, wb1_ref, wo_ref, g1_ref, b1_ref, rw_ref, rb_ref,
     x1_ref, x1p_ref, idx_ref, wt_ref, cnt_ref, cnt_scr) = refs[n_x:]
    tm, d = o_ref.shape

    @pl.when(pl.program_id(0) == 0)
    def _():
        cnt_scr[...] = jnp.zeros_like(cnt_scr)

    x = _token_tile(x_refs)
    o = _head_or_tail(o_ref, ot_ref)
    h = _head_or_tail(h_ref, ht_ref)
    gate = gate_ref[...].astype(F32)
    merged = gate[:, :d] * _bdot(o, wb0_ref[...]) + gate[:, d:] * _bdot(h, wb1_ref[...])
    mix = _bdot(merged.astype(BF16), wo_ref[...])
    x1 = _layernorm(alpha * x + mix, g1_ref[...], b1_ref[...])
    x1_ref[...] = x1
    x1p_ref[...] = _pack_bf16_pairs(x1)

    xh = x1.astype(BF16)
    xl = (x1 - xh.astype(F32)).astype(BF16)
    n_lanes = rw_ref.shape[1] // 2
    hi_terms = _bdot(xh, rw_ref[...])
    logits = (hi_terms[:, n_lanes:] + _bdot(xl, rw_ref[:, :n_lanes])) + hi_terms[:, :n_lanes]
    logits = logits + rb_ref[...]
    lane = lax.broadcasted_iota(jnp.int32, logits.shape, 1)
    lane_f = lane.astype(F32)
    neg_inf = jnp.float32(-jnp.inf)
    cur = jnp.where(lane < n_experts, logits, neg_inf)
    vals, idxs = [], []
    for _ in range(TOP_K):
        m = jnp.max(cur, axis=-1, keepdims=True)
        sel = jnp.min(jnp.where(cur == m, lane_f, float(LANES)), axis=-1, keepdims=True)
        vals.append(m)
        idxs.append(sel)
        cur = jnp.where(lane_f == sel, neg_inf, cur)
    exps = [jnp.exp(v - vals[0]) for v in vals]
    total = sum(exps)

    onehots = [jnp.where(lane_f == idxs[j], 1.0, 0.0) for j in range(TOP_K)]
    chosen = sum(onehots)
    r = lax.broadcasted_iota(jnp.int32, (tm, tm), 0)
    c = lax.broadcasted_iota(jnp.int32, (tm, tm), 1)
    earlier = jnp.where(r > c, 1.0, 0.0).astype(BF16)
    before = _bdot(earlier, chosen.astype(BF16)) + cnt_scr[...]
    ranks = [jnp.sum(onehots[j] * before, axis=-1, keepdims=True) for j in range(TOP_K)]
    cnt_scr[...] += jnp.sum(chosen, axis=0, keepdims=True)
    cnt_ref[...] = cnt_scr[...]

    idx_out = jnp.zeros(logits.shape, F32)
    wt_out = jnp.zeros(logits.shape, F32)
    for j in range(TOP_K):
        idx_out = jnp.where(lane == j, idxs[j], idx_out)
        idx_out = jnp.where(lane == TOP_K + j, ranks[j], idx_out)
        wt_out = jnp.where(lane == j, exps[j] / total, wt_out)
    idx_ref[...] = idx_out.astype(jnp.int32)
    wt_ref[...] = wt_out


def _post(x_head, x_tail, n_head_tiles, tail_tile, o, h, o_tail, h_tail, gate, w, tm, alpha, n_experts):
    d = x_head.shape[1]
    t = (n_head_tiles + 1) * tm
    assert o.shape[0] == t - tm and o_tail.shape[0] == tm and gate.shape[0] == t
    consts = [w["wb0"], w["wb1"], w["wo"], w["ln1g"], w["ln1b"], w["rw"], w["rb"]]
    row = lambda width: pl.BlockSpec((tm, width), lambda i: (i, 0))
    xs, x_specs = _token_inputs(x_head, x_tail, n_head_tiles, tail_tile, tm)
    return pl.pallas_call(
        functools.partial(_post_kernel, n_x=len(xs), alpha=alpha, n_experts=n_experts),
        grid=(t // tm,),
        in_specs=x_specs + _head_tail_specs(n_head_tiles, 0, tm, d) + _head_tail_specs(n_head_tiles, 0, tm, d) +
                 [row(2 * d)] + [_weight_spec(c.shape) for c in consts],
        out_specs=[row(d), row(d // 2), row(LANES), row(LANES), _const_spec((1, LANES))],
        out_shape=[jax.ShapeDtypeStruct((t, d), F32),
                   jax.ShapeDtypeStruct((t, d // 2), jnp.uint32),
                   jax.ShapeDtypeStruct((t, LANES), jnp.int32),
                   jax.ShapeDtypeStruct((t, LANES), F32),
                   jax.ShapeDtypeStruct((1, LANES), F32)],
        scratch_shapes=[pltpu.VMEM((1, LANES), F32)],
        compiler_params=_params("arbitrary"),
        name="post",
    )(*xs, o, o_tail, h, h_tail, gate, *consts)


SC_INDEX_MAX = 128
SC_ALIGN = 8
SC_BUFFER_BYTES = 208 * 1024


def _sc_plan(n_rows, row_bytes):
    info = plsc.get_sparse_core_info()
    n_workers = info.num_cores * info.num_subcores
    per_worker, rem = divmod(n_rows, n_workers)
    assert rem == 0 and per_worker % SC_ALIGN == 0, "rows must split into aligned equal shares per subcore"
    cap = min(SC_INDEX_MAX, SC_BUFFER_BYTES // row_bytes)
    chunk = max(c for c in range(SC_ALIGN, cap + 1, SC_ALIGN) if per_worker % c == 0)
    return info.num_cores, n_workers, per_worker, chunk


def _sc_gather(table, idx):
    n_rows = idx.shape[0]
    d = table.shape[1]
    n_cores, _, per_worker, chunk = _sc_plan(n_rows, d * table.dtype.itemsize)
    mesh = plsc.VectorSubcoreMesh(core_axis_name="c", subcore_axis_name="s")

    n_chunks = per_worker // chunk

    @functools.partial(
        pl.kernel, mesh=mesh, out_type=jax.ShapeDtypeStruct((n_rows, d), table.dtype),
        scratch_types=[pltpu.VMEM((chunk,), jnp.int32)] * 2 + [pltpu.VMEM((chunk, d), table.dtype)] * 2 +
                      [pltpu.SemaphoreType.DMA] * 4)
    def gather(table_hbm, idx_hbm, out_hbm, idx0, idx1, rows0, rows1, g0, g1, w0, w1):
        idx_v, rows_v, gsem, wsem = (idx0, idx1), (rows0, rows1), (g0, g1), (w0, w1)
        worker = lax.axis_index("s") * n_cores + lax.axis_index("c")
        base = worker * per_worker

        def rows_of(c):
            return pl.ds(pl.multiple_of(base + c * chunk, SC_ALIGN), chunk)

        def start_gather(c):
            s = c % 2
            pltpu.sync_copy(idx_hbm.at[rows_of(c)], idx_v[s])
            return pltpu.async_copy(table_hbm.at[idx_v[s]], rows_v[s], gsem[s])

        gathers = {c: start_gather(c) for c in range(min(2, n_chunks))}
        writes = {}
        for c in range(n_chunks):
            s = c % 2
            gathers[c].wait()
            writes[c] = pltpu.async_copy(rows_v[s], out_hbm.at[rows_of(c)], wsem[s])
            if c + 2 < n_chunks:
                writes.pop(c).wait()
                gathers[c + 2] = start_gather(c + 2)
        for write in writes.values():
            write.wait()

    return gather(table, idx)


def _sc_scatter(x, pos, n_out):
    n, d = x.shape
    assert pos.shape == (TOP_K * n,)
    n_cores, _, per_worker, chunk = _sc_plan(n, d * x.dtype.itemsize)
    mesh = plsc.VectorSubcoreMesh(core_axis_name="c", subcore_axis_name="s")

    n_chunks = per_worker // chunk

    @functools.partial(
        pl.kernel, mesh=mesh, out_type=jax.ShapeDtypeStruct((n_out, d), x.dtype),
        scratch_types=[pltpu.VMEM((chunk, d), x.dtype)] * 2 + [pltpu.VMEM((chunk,), jnp.int32)] * TOP_K +
                      [pltpu.SemaphoreType.DMA] * 3)
    def scatter(x_hbm, pos_hbm, out_hbm, rows0, rows1, *rest):
        rows_v, idx_v, (l0, l1, ssem) = (rows0, rows1), rest[:TOP_K], rest[TOP_K:]
        lsem = (l0, l1)
        worker = lax.axis_index("s") * n_cores + lax.axis_index("c")
        base = worker * per_worker

        def first_row(c):
            return pl.multiple_of(base + c * chunk, SC_ALIGN)

        def start_load(c):
            return pltpu.async_copy(x_hbm.at[pl.ds(first_row(c), chunk)], rows_v[c % 2], lsem[c % 2])

        load = start_load(0)
        for c in range(n_chunks):
            next_load = start_load(c + 1) if c + 1 < n_chunks else None
            for k in range(TOP_K):
                pltpu.sync_copy(pos_hbm.at[pl.ds(k * n + first_row(c), chunk)], idx_v[k])
            load.wait()
            scatters = [pltpu.async_copy(rows_v[c % 2], out_hbm.at[idx_v[k]], ssem)
                        for k in range(TOP_K)]
            for scatter_k in scatters:
                scatter_k.wait()
            load = next_load

    return scatter(x, pos)


def _ffn_kernel(be_ref, valid_ref, x_ref, wgu_ref, bgu_ref, wd_ref, bd_ref, o_ref, wgu_bf, wd_bf):
    i = pl.program_id(0)
    bm = x_ref.shape[0]
    f = wd_ref.shape[2]
    valid = valid_ref[i]
    new_expert = jnp.logical_or(i == 0, be_ref[i] != be_ref[jnp.maximum(i - 1, 0)])

    @pl.when(new_expert)
    def _():
        wgu_bf[...] = wgu_ref[0, 0].astype(BF16)
        wd_bf[...] = wd_ref[0, 0].astype(BF16)

    def expert_rows(rows):
        x = _unpack_bf16_pairs(x_ref[rows, :]).astype(BF16)
        acc = jnp.zeros((x.shape[0], wd_ref.shape[3]), F32)
        for c in range(0, f, FFN_COLS):
            gate = _bdot(x, wgu_bf[:, c:c + FFN_COLS]) + bgu_ref[0, 0, :, c:c + FFN_COLS]
            up = _bdot(x, wgu_bf[:, f + c:f + c + FFN_COLS]) + bgu_ref[0, 0, :, f + c:f + c + FFN_COLS]
            gate = jnp.minimum(gate, SWIGLU_LIMIT)
            up = jnp.clip(up, -SWIGLU_LIMIT, SWIGLU_LIMIT)
            act = (up + 1.0) * gate * _sigmoid(SWIGLU_ALPHA * gate)
            acc = acc + _bdot(act.astype(BF16), wd_bf[c:c + FFN_COLS, :])
        o_ref[rows, :] = _pack_bf16_pairs(acc + bd_ref[0, 0])

    @pl.when(valid == bm)
    def _():
        expert_rows(slice(0, bm))

    @pl.when(valid < bm)
    def _():
        for s in range(0, bm, FFN_SUB_ROWS):
            rows = slice(s, s + FFN_SUB_ROWS)

            @pl.when(s < valid)
            def _():
                expert_rows(rows)

            @pl.when(s >= valid)
            def _():
                o_ref[rows, :] = jnp.zeros((FFN_SUB_ROWS, o_ref.shape[1]), o_ref.dtype)


def _ffn(block_e, valid, xs, layer, w_gu, b_gu, w_down, b_down, bm):
    n_layers, n_exp, d, f2 = w_gu.shape
    f = f2 // 2
    n_blocks = block_e.shape[0]
    expert = lambda i, be, nu: (layer, be[i], 0, 0)
    grid_spec = pltpu.PrefetchScalarGridSpec(
        num_scalar_prefetch=2,
        grid=(n_blocks,),
        in_specs=[pl.BlockSpec((bm, d // 2), lambda i, be, nu: (i, 0)),
                  pl.BlockSpec((1, 1, d, f2), expert),
                  pl.BlockSpec((1, 1, 1, f2), expert),
                  pl.BlockSpec((1, 1, f, d), expert),
                  pl.BlockSpec((1, 1, 1, d), expert)],
        out_specs=pl.BlockSpec((bm, d // 2), lambda i, be, nu: (i, 0)),
        scratch_shapes=[pltpu.VMEM((d, f2), BF16), pltpu.VMEM((f, d), BF16)],
    )
    return pl.pallas_call(
        _ffn_kernel,
        grid_spec=grid_spec,
        out_shape=jax.ShapeDtypeStruct((n_blocks * bm, d // 2), jnp.uint32),
        compiler_params=_params("arbitrary"),
        name="ffn",
    )(block_e, valid, xs, w_gu, b_gu.reshape(n_layers, n_exp, 1, f2),
      w_down, b_down.reshape(n_layers, n_exp, 1, d))


def _combine_kernel(x_ref, *refs, alpha, split):
    y_refs, (wt_ref, g_ref, b_ref), out_refs = refs[:TOP_K], refs[TOP_K:TOP_K + 3], refs[TOP_K + 3:]
    wt = wt_ref[...]
    ffn = sum(wt[:, j:j + 1] * _unpack_bf16_pairs(y_refs[j][...]) for j in range(TOP_K))
    res = _layernorm(alpha * x_ref[...] + ffn, g_ref[...], b_ref[...])
    if not split:
        out_refs[0][...] = res
    else:
        is_tail = pl.program_id(0) == pl.num_programs(0) - 1

        @pl.when(jnp.logical_not(is_tail))
        def _():
            out_refs[0][...] = res

        @pl.when(is_tail)
        def _():
            out_refs[1][...] = res


def _combine(x1, y, wt, g, b, tm, alpha, split=False):
    t, d = x1.shape
    n_tiles = t // tm
    planes = [pl.BlockSpec((tm, d // 2), lambda i, j=j: (j * n_tiles + i, 0)) for j in range(TOP_K)]
    if split:
        out_specs = [pl.BlockSpec((tm, d), lambda i: (jnp.minimum(i, n_tiles - 2), 0)), _const_spec((tm, d))]
        out_shape = [jax.ShapeDtypeStruct((t - tm, d), F32), jax.ShapeDtypeStruct((tm, d), F32)]
    else:
        out_specs = pl.BlockSpec((tm, d), lambda i: (i, 0))
        out_shape = jax.ShapeDtypeStruct((t, d), F32)
    return pl.pallas_call(
        functools.partial(_combine_kernel, alpha=alpha, split=split),
        grid=(n_tiles,),
        in_specs=[pl.BlockSpec((tm, d), lambda i: (i, 0))] + planes +
                 [pl.BlockSpec((tm, LANES), lambda i: (i, 0)), _const_spec(g.shape), _const_spec(b.shape)],
        out_specs=out_specs,
        out_shape=out_shape,
        compiler_params=_params("arbitrary"),
        name="combine",
    )(x1, *([y] * TOP_K), wt, g, b)


def _route(experts, ranks, counts, bm):
    n_experts = counts.shape[0]
    i32 = jnp.int32
    padded = (counts + bm - 1) // bm * bm
    ends_pad = jnp.cumsum(padded)
    start_pad = ends_pad - padded
    n_blocks = -(-experts.size // bm) + n_experts
    first_row = jnp.arange(n_blocks, dtype=i32)[:, None] * bm
    block_e = jnp.minimum(jnp.sum(ends_pad[None, :] <= first_row, axis=1), n_experts - 1).astype(i32)
    valid = jnp.clip(counts[block_e] - (first_row[:, 0] - start_pad[block_e]), 0, bm).astype(i32)
    onehot = experts[..., None] == jnp.arange(n_experts, dtype=i32)
    pos = jnp.sum(jnp.where(onehot, start_pad, 0), axis=-1).astype(i32) + ranks
    return block_e, valid, pos


def _moe(x1, x1p, routing, counts, top_w_pad, w, layer, moe_weights, alpha, tm, split):
    n_experts = moe_weights[0].shape[1]
    block_e, valid, pos = _route(routing[:, :TOP_K], routing[:, TOP_K:2 * TOP_K],
                                 counts[0, :n_experts].astype(jnp.int32), MOE_ROWS)
    pos_planes = pos.T.reshape(-1)
    xs = _sc_scatter(x1p, pos_planes, block_e.shape[0] * MOE_ROWS)
    out_rows = _ffn(block_e, valid, xs, layer, *moe_weights, MOE_ROWS)
    y = _sc_gather(out_rows, pos_planes)
    return _combine(x1, y, top_w_pad, w["ln2g"], w["ln2b"], tm, alpha, split)


def _hi_lo(w):
    hi = w.astype(BF16)
    return jnp.concatenate([hi, (w - hi.astype(F32)).astype(BF16)], axis=1)


def _layer_weights(layer, w_in, gla_w_a2, gla_b_a, gla_norm_g, rg_conv_w, rg_conv_b, rg_w_a, rg_b_a, rg_w_x,
                   rg_b_x, rg_lambda, b_merge, w_branch, w_o, ln1_g, ln1_b, ln2_g, ln2_b, router_w, router_b,
                   kw, vw):
    d = w_in.shape[1]
    rank = gla_w_a2.shape[1]
    width = rg_conv_w.shape[2]
    n_exp = router_w.shape[2]
    c0 = 2 * kw + 2 * vw
    wi = w_in[layer]
    row = lambda v: v.reshape(1, -1)
    return {
        "wq": wi[:, :c0].astype(BF16),
        "wal": jnp.pad(wi[:, c0:c0 + rank], ((0, 0), (0, LANES - rank))).astype(BF16),
        "wa2": jnp.pad(gla_w_a2[layer], ((0, LANES - rank), (0, 0))).astype(BF16),
        "ba": row(gla_b_a[layer]),
        "wxr": wi[:, c0 + rank:c0 + rank + width].astype(BF16),
        "wmg": wi[:, c0 + rank + width:].astype(BF16),
        "bmg": row(b_merge[layer]),
        "ng": row(gla_norm_g[layer]),
        "cw": rg_conv_w[layer], "cb": row(rg_conv_b[layer]),
        "wax": jnp.concatenate([rg_w_a[layer], rg_w_x[layer]], axis=-1).astype(BF16),
        "rba": row(rg_b_a[layer]), "rbx": row(rg_b_x[layer]), "lam": row(rg_lambda[layer]),
        "wb0": w_branch[layer, 0].astype(BF16), "wb1": w_branch[layer, 1].astype(BF16),
        "wo": w_o[layer].astype(BF16),
        "ln1g": row(ln1_g[layer]), "ln1b": row(ln1_b[layer]),
        "ln2g": row(ln2_g[layer]), "ln2b": row(ln2_b[layer]),
        "rw": _hi_lo(jnp.pad(router_w[layer], ((0, 0), (0, LANES - n_exp)))),
        "rb": jnp.pad(row(router_b[layer]), ((0, 0), (0, LANES - n_exp))),
    }


def kernel(x_prompt, x_sample, state_gla, state_rglru, state_conv, ln1_g, ln1_b, w_in, gla_w_a2, gla_b_a, gla_norm_g, rg_conv_w, rg_conv_b, rg_w_a, rg_b_a, rg_w_x, rg_b_x, rg_lambda, b_merge, w_branch, w_o, ln2_g, ln2_b, router_w, router_b, moe_w_gu, moe_b_gu, moe_w_down, moe_b_down):
    n_p, seq, d = x_prompt.shape
    n_s, dec_seq, _ = x_sample.shape
    assert dec_seq == 1, "the sample group carries one new token per sequence"
    depth, _, heads, dk, dv = state_gla.shape
    kw, vw = heads * dk, heads * dv
    n_exp = router_w.shape[2]
    alpha = (2.0 * depth) ** 0.25
    t_p = n_p * seq
    pad_rows = BF16_SUBLANES

    tile = PROJ_ROWS
    assert t_p % tile == 0 and n_s <= tile
    n_head_tiles = t_p // tile
    tail_pad = lambda a: jnp.pad(a, ((0, tile - n_s), (0, 0)))
    x_head, x_tail, tail_tile = x_prompt.reshape(t_p, d), tail_pad(x_sample.reshape(n_s, d)), 0
    moe_weights = (moe_w_gu, moe_b_gu, moe_w_down, moe_b_down)
    outs = {k: [] for k in ("gla_p", "rg_p", "cv_p", "rg_s", "cv_s")}
    gla_s = None
    for layer in range(depth):
        w = _layer_weights(layer, w_in, gla_w_a2, gla_b_a, gla_norm_g, rg_conv_w, rg_conv_b, rg_w_a, rg_b_a,
                           rg_w_x, rg_b_x, rg_lambda, b_merge, w_branch, w_o, ln1_g, ln1_b, ln2_g, ln2_b,
                           router_w, router_b, kw, vw)
        qkvg, loga, xr, gate = _proj(x_head, x_tail, n_head_tiles, tail_tile, w, tile)
        o_p, s_p = _gla(qkvg, loga, w["ng"], None, layer, n_p, seq, SEQ_ROWS, GLA_CHUNK, heads, dk, dv)
        h_p, hl_p, cv_p = _rglru_seq(xr, w, n_p, seq, SEQ_ROWS)
        pad = lambda a: jnp.pad(a[t_p:t_p + n_s, None, :],
                                ((0, 0), (0, pad_rows - 1), (0, 0))).reshape(n_s * pad_rows, -1)
        o_s, gla_s = _gla(pad(qkvg), pad(loga), w["ng"], state_gla, layer, n_s, pad_rows,
                          DECODE_SEQS * pad_rows, pad_rows, heads, dk, dv, stack=(depth, gla_s))
        o_s = o_s.reshape(n_s, pad_rows, vw)[:, 0]
        h_s, hn_s, cv_s = _rglru_step(xr[t_p:t_p + n_s], state_conv[layer], state_rglru[layer], w)
        x1, x1p, routing, wt, cnt = _post(x_head, x_tail, n_head_tiles, tail_tile, o_p, h_p, tail_pad(o_s),
                                          tail_pad(h_s), gate, w, tile, alpha, n_exp)
        last = layer == depth - 1
        x = _moe(x1, x1p, routing, cnt, wt, w, layer, moe_weights, alpha, tile, split=last)
        if not last:
            x_head, x_tail, tail_tile = x, x, n_head_tiles
        outs["gla_p"].append(s_p)
        outs["rg_p"].append(hl_p.reshape(n_p, -1))
        outs["cv_p"].append(cv_p)
        outs["rg_s"].append(hn_s)
        outs["cv_s"].append(cv_s)
    y_head, y_tail = x
    return (y_head.reshape(n_p, seq, d), y_tail[:n_s].reshape(n_s, dec_seq, d),
            jnp.stack(outs["gla_p"]), jnp.stack(outs["rg_p"]), jnp.stack(outs["cv_p"]),
            gla_s, jnp.stack(outs["rg_s"]), jnp.stack(outs["cv_s"]))
```

```python
import functools

import jax
import jax.numpy as jnp
from jax import lax
from jax.experimental import pallas as pl
from jax.experimental.pallas import tpu as pltpu
from jax.experimental.pallas import tpu_sc as plsc

F32 = jnp.float32
BF16 = jnp.bfloat16

TOP_K = 4
GLA_GATE_TAU = 16.0
GLA_CHUNK = 64
RG_C = 8.0
SWIGLU_LIMIT = 7.0
SWIGLU_ALPHA = 1.702
LN_EPS = 1e-5

LANES = 128
BF16_SUBLANES = 16
VMEM_LIMIT = 56 * 1024 * 1024

PROJ_ROWS = 512
SEQ_ROWS = 256
DECODE_SEQS = 8
MOE_ROWS = 1024
FFN_COLS = 512
FFN_SUB_ROWS = 256


def _params(*sem):
    return pltpu.CompilerParams(dimension_semantics=sem, vmem_limit_bytes=VMEM_LIMIT)


def _const_spec(shape):
    nd = len(shape)
    return pl.BlockSpec(shape, lambda *_: (0,) * nd)


def _weight_spec(shape):
    nd = len(shape)
    return pl.BlockSpec(shape, lambda *_: (0,) * nd, pipeline_mode=pl.Buffered(1))


def _bdot(a, b):
    return jnp.dot(a, b, preferred_element_type=F32)


def _split3(x):
    hi = x.astype(BF16)
    r1 = x - hi.astype(F32)
    mid = r1.astype(BF16)
    lo = (r1 - mid.astype(F32)).astype(BF16)
    return hi, mid, lo


def _sigmoid(x):
    return 0.5 * jnp.tanh(0.5 * x) + 0.5


def _log_sigmoid(x):
    return jnp.minimum(x, 0.0) - jnp.log1p(jnp.exp(-jnp.abs(x)))


def _softplus(x):
    return jnp.maximum(x, 0.0) + jnp.log1p(jnp.exp(-jnp.abs(x)))


def _layernorm(x, g, b):
    mu = jnp.mean(x, axis=-1, keepdims=True)
    xc = x - mu
    var = jnp.mean(xc * xc, axis=-1, keepdims=True)
    return xc * lax.rsqrt(var + LN_EPS) * g + b


def _head_tail_specs(n_head_tiles, tail_tile, tm, d):
    return [pl.BlockSpec((tm, d), lambda i: (jnp.minimum(i, n_head_tiles - 1), 0)),
            pl.BlockSpec((tm, d), lambda i: (tail_tile, 0))]


def _head_or_tail(head_ref, tail_ref):
    is_tail = pl.program_id(0) == pl.num_programs(0) - 1
    return jnp.where(is_tail, tail_ref[...], head_ref[...])


def _token_inputs(x_head, x_tail, n_head_tiles, tail_tile, tm):
    d = x_head.shape[1]
    if x_tail is x_head:
        assert tail_tile == n_head_tiles
        return [x_head], [pl.BlockSpec((tm, d), lambda i: (i, 0))]
    return [x_head, x_tail], _head_tail_specs(n_head_tiles, tail_tile, tm, d)


def _token_tile(x_refs):
    return x_refs[0][...] if len(x_refs) == 1 else _head_or_tail(*x_refs)


def _proj_kernel(*refs, n_x):
    x_refs = refs[:n_x]
    wq_ref, wal_ref, wa2_ref, ba_ref, wxr_ref, wmg_ref, bmg_ref, qkvg_ref, loga_ref, xr_ref, gate_ref = refs[n_x:]
    xb = _token_tile(x_refs).astype(BF16)
    d = xb.shape[1]
    for c in range(0, qkvg_ref.shape[1], d):
        qkvg_ref[:, c:c + d] = _bdot(xb, wq_ref[:, c:c + d]).astype(BF16)
    a_low = _bdot(xb, wal_ref[...])
    z = _bdot(a_low.astype(BF16), wa2_ref[...]) + ba_ref[...]
    loga_ref[...] = _log_sigmoid(z) * (1.0 / GLA_GATE_TAU)
    xr_ref[...] = _bdot(xb, wxr_ref[...])
    for c in range(0, gate_ref.shape[1], d):
        gate_ref[:, c:c + d] = _sigmoid(_bdot(xb, wmg_ref[:, c:c + d]) + bmg_ref[:, c:c + d]).astype(BF16)


def _proj(x_head, x_tail, n_head_tiles, tail_tile, w, tm):
    d = x_head.shape[1]
    t = (n_head_tiles + 1) * tm
    kw = w["wq"].shape[1]
    nk = w["wa2"].shape[1]
    consts = [w["wq"], w["wal"], w["wa2"], w["ba"], w["wxr"], w["wmg"], w["bmg"]]
    xs, x_specs = _token_inputs(x_head, x_tail, n_head_tiles, tail_tile, tm)
    return pl.pallas_call(
        functools.partial(_proj_kernel, n_x=len(xs)),
        grid=(t // tm,),
        in_specs=x_specs + [_weight_spec(c.shape) for c in consts],
        out_specs=[pl.BlockSpec((tm, kw), lambda i: (i, 0)),
                   pl.BlockSpec((tm, nk), lambda i: (i, 0)),
                   pl.BlockSpec((tm, d), lambda i: (i, 0)),
                   pl.BlockSpec((tm, 2 * d), lambda i: (i, 0))],
        out_shape=[jax.ShapeDtypeStruct((t, kw), BF16),
                   jax.ShapeDtypeStruct((t, nk), F32),
                   jax.ShapeDtypeStruct((t, d), F32),
                   jax.ShapeDtypeStruct((t, 2 * d), BF16)],
        compiler_params=_params("arbitrary"),
        name="proj",
    )(*xs, *consts)


def _gla_kernel(*refs, chunk, n_chunks, heads, has_state, n_inputs, state_slices, state_slot):
    q_ref, k_ref, v_ref, la_ref, g_ref, ng_ref = refs[:6]
    s0_ref = refs[6] if has_state else None
    o_ref, so_ref, s_scr, o_scr, p_scr, kv_scr = refs[n_inputs:]
    t = pl.program_id(1)
    dk = q_ref.shape[1] // heads
    dv = v_ref.shape[1] // heads
    scale = dk ** -0.5
    chunks_per_seq = n_chunks // s_scr.shape[0]

    @pl.when(t == 0)
    def _():
        if has_state:
            s_scr[...] = s0_ref[0]
        else:
            s_scr[...] = jnp.zeros_like(s_scr)

    n_rows = n_chunks * chunk
    r = lax.broadcasted_iota(jnp.int32, (n_rows, n_rows), 0)
    c = lax.broadcasted_iota(jnp.int32, (n_rows, n_rows), 1)
    shift = chunk.bit_length() - 1
    assert 1 << shift == chunk, "chunk must be a power of two"
    tril = jnp.where(jnp.logical_and(r >> shift == c >> shift, r >= c), 1.0, 0.0).astype(BF16)
    la_hi, la_mid, la_lo = _split3(la_ref[...])
    b_all = _bdot(tril, la_hi) + _bdot(tril, la_mid) + _bdot(tril, la_lo)
    q_all = (q_ref[...].astype(F32) * scale * jnp.exp(b_all)).astype(BF16)
    k_all = (k_ref[...].astype(F32) * jnp.exp(-b_all)).astype(BF16)
    causal = (lax.broadcasted_iota(jnp.int32, (chunk, chunk), 0) >=
              lax.broadcasted_iota(jnp.int32, (chunk, chunk), 1))

    decays = []
    for ci in range(n_chunks):
        rows = slice(ci * chunk, (ci + 1) * chunk)
        b = b_all[rows, :]
        b_last = b[chunk - 1:chunk, :]
        q_e, k_e = q_all[rows, :], k_all[rows, :]
        k_d = (k_ref[rows, :].astype(F32) * jnp.exp(b_last - b)).astype(BF16)
        decays.append(jnp.exp(b_last))
        for h in range(heads):
            ks = slice(h * dk, (h + 1) * dk)
            v_h = v_ref[rows, h * dv:(h + 1) * dv]
            scores = lax.dot_general(q_e[:, ks], k_e[:, ks], (((1,), (1,)), ((), ())),
                                     preferred_element_type=F32)
            p_scr[rows, h * chunk:(h + 1) * chunk] = jnp.where(causal, scores, 0.0).astype(BF16)
            kv_scr[ci, h] = lax.dot_general(k_d[:, ks], v_h, (((0,), (0,)), ((), ())),
                                            preferred_element_type=F32)

    for ci in range(n_chunks):
        rows = slice(ci * chunk, (ci + 1) * chunk)
        seq = ci // chunks_per_seq
        for h in range(heads):
            ks = slice(h * dk, (h + 1) * dk)
            vs = slice(h * dv, (h + 1) * dv)
            s_old = s_scr[seq, h]
            o_scr[rows, vs] = (_bdot(p_scr[rows, h * chunk:(h + 1) * chunk], v_ref[rows, vs]) +
                               _bdot(q_all[rows, ks], s_old.astype(BF16)))
            s_scr[seq, h] = jnp.transpose(decays[ci][:, ks]) * s_old + kv_scr[ci, h]

    for h in range(heads):
        vs = slice(h * dv, (h + 1) * dv)
        o = o_scr[:, vs]
        mu = jnp.mean(o, axis=-1, keepdims=True)
        oc = o - mu
        var = jnp.mean(oc * oc, axis=-1, keepdims=True)
        on = oc * lax.rsqrt(var + LN_EPS) * ng_ref[:, vs]
        g = g_ref[:, vs].astype(F32)
        o_ref[:, vs] = (on * (g * _sigmoid(g))).astype(BF16)

    @pl.when(t == pl.num_programs(1) - 1)
    def _():
        if state_slices == 0:
            so_ref[...] = s_scr[...]
        for layer_slice in range(state_slices):
            so_ref[layer_slice] = s_scr[...] if layer_slice == state_slot else jnp.zeros_like(s_scr)


def _gla(qkvg, loga, norm_g, s0, layer, nb, s_len, rows, chunk, heads, dk, dv, stack=None):
    t = nb * s_len
    seqs = max(1, rows // s_len)
    assert s_len * seqs % rows == 0 and nb % seqs == 0 and s_len % chunk == 0
    nb, tps = nb // seqs, s_len * seqs // rows
    kw, vw = heads * dk, heads * dv
    has_state = s0 is not None
    state_shape, state_block, state_index = (nb * seqs, heads, dk, dv), (seqs, heads, dk, dv), lambda b, i: (b, 0, 0, 0)
    state_slices, state_slot, aliases = 0, 0, {}
    if stack is not None:
        depth, prev = stack
        state_shape = (depth,) + state_shape
        if prev is None:
            state_slices, state_slot = depth, layer
            state_block, state_index = (depth,) + state_block, lambda b, i: (0, b, 0, 0, 0)
        else:
            state_slices = 1
            state_block, state_index = (1,) + state_block, lambda b, i: (layer, b, 0, 0, 0)
    in_specs = [pl.BlockSpec((rows, kw), lambda b, i: (b * tps + i, 0)),
                pl.BlockSpec((rows, kw), lambda b, i: (b * tps + i, 1)),
                pl.BlockSpec((rows, vw), lambda b, i: (b * tps + i, (2 * kw) // vw)),
                pl.BlockSpec((rows, kw), lambda b, i: (b * tps + i, 0)),
                pl.BlockSpec((rows, vw), lambda b, i: (b * tps + i, (2 * kw) // vw + 1)),
                _const_spec(norm_g.shape)]
    args = [qkvg, qkvg, qkvg, loga, qkvg, norm_g]
    if has_state:
        in_specs.append(pl.BlockSpec((1, seqs, heads, dk, dv), lambda b, i: (layer, b, 0, 0, 0)))
        args.append(s0)
    if stack is not None and stack[1] is not None:
        in_specs.append(pl.BlockSpec(memory_space=pl.ANY))
        aliases[len(args)] = 1
        args.append(stack[1])
    return pl.pallas_call(
        functools.partial(_gla_kernel, chunk=chunk, n_chunks=rows // chunk, heads=heads, has_state=has_state,
                          n_inputs=len(args), state_slices=state_slices, state_slot=state_slot),
        grid=(nb, tps),
        in_specs=in_specs,
        out_specs=[pl.BlockSpec((rows, vw), lambda b, i: (b * tps + i, 0)),
                   pl.BlockSpec(state_block, state_index)],
        out_shape=[jax.ShapeDtypeStruct((t, vw), BF16), jax.ShapeDtypeStruct(state_shape, F32)],
        scratch_shapes=[pltpu.VMEM((seqs, heads, dk, dv), F32), pltpu.VMEM((rows, vw), F32),
                        pltpu.VMEM((rows, heads * chunk), BF16), pltpu.VMEM((rows // chunk, heads, dk, dv), F32)],
        input_output_aliases=aliases,
        compiler_params=_params("arbitrary", "arbitrary"),
        name="gla",
    )(*args)


def _rg_gates(xc, wax_ref, ba_ref, bx_ref, lam_ref):
    n_blocks, bw, _ = wax_ref.shape
    r_parts, i_parts = [], []
    for n in range(n_blocks):
        cs = slice(n * bw, (n + 1) * bw)
        pre = _bdot(xc[:, cs].astype(BF16), wax_ref[n])
        r_parts.append(_sigmoid(pre[:, :bw] + ba_ref[:, cs]))
        i_parts.append(_sigmoid(pre[:, bw:] + bx_ref[:, cs]))
    r = jnp.concatenate(r_parts, axis=1)
    i = jnp.concatenate(i_parts, axis=1)
    log_a = r * (-RG_C * _softplus(-lam_ref[...]))
    a = jnp.exp(log_a)
    bx = jnp.sqrt(jnp.tanh(-log_a) * (1.0 + a * a)) * (i * xc)
    return a, bx


def _rglru_seq_kernel(xr_ref, cw_ref, cb_ref, wax_ref, ba_ref, bx_ref, lam_ref,
                      h_ref, hl_ref, cv_ref, cbuf, a_scr, b_scr, h_scr, hc):
    t = pl.program_id(1)
    rows = xr_ref.shape[0]
    taps = cw_ref.shape[0]
    head = 8

    @pl.when(t == 0)
    def _():
        cbuf[0:head, :] = jnp.zeros((head, cbuf.shape[1]), F32)
        hc[...] = jnp.zeros_like(hc)

    cbuf[head:head + rows, :] = xr_ref[...]
    first = head - (taps - 1)
    xc = cb_ref[...] + sum(cbuf[first + j:first + j + rows, :] * cw_ref[j:j + 1, :] for j in range(taps))
    tail = cbuf[first + rows:head + rows, :]
    cbuf[first:head, :] = tail

    a, bx = _rg_gates(xc, wax_ref, ba_ref, bx_ref, lam_ref)
    a_scr[...] = a
    b_scr[...] = bx

    def step(i, h):
        h = a_scr[pl.ds(i, 1), :] * h + b_scr[pl.ds(i, 1), :]
        h_scr[pl.ds(i, 1), :] = h
        return h

    h_last = lax.fori_loop(0, rows, step, hc[...], unroll=16)
    hc[...] = h_last
    h_ref[...] = h_scr[...].astype(BF16)

    @pl.when(t == pl.num_programs(1) - 1)
    def _():
        hl_ref[0] = h_last
        cv_ref[0] = tail


def _rglru_seq(xr, w, nb, s_len, rows):
    width = xr.shape[1]
    t = nb * s_len
    tps = s_len // rows
    taps = w["cw"].shape[0]
    consts = [w["cw"], w["cb"], w["wax"], w["rba"], w["rbx"], w["lam"]]
    return pl.pallas_call(
        _rglru_seq_kernel,
        grid=(nb, tps),
        in_specs=[pl.BlockSpec((rows, width), lambda b, i: (b * tps + i, 0))] + [_const_spec(c.shape) for c in consts],
        out_specs=[pl.BlockSpec((rows, width), lambda b, i: (b * tps + i, 0)),
                   pl.BlockSpec((1, 1, width), lambda b, i: (b, 0, 0)),
                   pl.BlockSpec((1, taps - 1, width), lambda b, i: (b, 0, 0))],
        out_shape=[jax.ShapeDtypeStruct((t, width), BF16),
                   jax.ShapeDtypeStruct((nb, 1, width), F32),
                   jax.ShapeDtypeStruct((nb, taps - 1, width), F32)],
        scratch_shapes=[pltpu.VMEM((rows + 8, width), F32), pltpu.VMEM((rows, width), F32),
                        pltpu.VMEM((rows, width), F32), pltpu.VMEM((rows, width), F32),
                        pltpu.VMEM((1, width), F32)],
        compiler_params=_params("arbitrary", "arbitrary"),
        name="rglru_seq",
    )(xr, *consts)


def _rglru_step_kernel(xr_ref, sc_ref, h0_ref, cw_ref, cb_ref, wax_ref, ba_ref, bx_ref, lam_ref,
                       h_ref, hn_ref, cv_ref):
    taps = cw_ref.shape[0]
    xr = xr_ref[...]
    xc = (cb_ref[...] + sum(sc_ref[:, j, :] * cw_ref[j:j + 1, :] for j in range(taps - 1)) +
          xr * cw_ref[taps - 1:taps, :])
    a, bx = _rg_gates(xc, wax_ref, ba_ref, bx_ref, lam_ref)
    h = a * h0_ref[...] + bx
    h_ref[...] = h.astype(BF16)
    hn_ref[...] = h
    for j in range(taps - 2):
        cv_ref[:, j, :] = sc_ref[:, j + 1, :]
    cv_ref[:, taps - 2, :] = xr


def _rglru_step(xr, conv, h0, w):
    n, width = xr.shape
    consts = [w["cw"], w["cb"], w["wax"], w["rba"], w["rbx"], w["lam"]]
    args = [xr, conv, h0] + consts
    return pl.pallas_call(
        _rglru_step_kernel,
        grid=(1,),
        in_specs=[_const_spec(a.shape) for a in args],
        out_specs=[_const_spec((n, width)), _const_spec((n, width)), _const_spec(conv.shape)],
        out_shape=[jax.ShapeDtypeStruct((n, width), BF16),
                   jax.ShapeDtypeStruct((n, width), F32),
                   jax.ShapeDtypeStruct(conv.shape, F32)],
        compiler_params=_params("arbitrary"),
        name="rglru_step",
    )(*args)


def _pack_bf16_pairs(x):
    half = x.shape[1] // 2
    hi = pltpu.bitcast(x[:, :half].astype(BF16).astype(F32), jnp.uint32)
    lo = pltpu.bitcast(x[:, half:].astype(BF16).astype(F32), jnp.uint32)
    return hi | (lo >> 16)


def _unpack_bf16_pairs(p):
    hi = pltpu.bitcast(p & jnp.uint32(0xFFFF0000), F32)
    lo = pltpu.bitcast(p << 16, F32)
    return jnp.concatenate([hi, lo], axis=1)


def _post_kernel(*refs, n_x, alpha, n_experts, n_tail_real):
    x_refs = refs[:n_x]
    (o_ref, ot_ref, h_ref, ht_ref, gate_ref, wb0_ref, wb1_ref, wo_ref, g1_ref, b1_ref, rw_ref, rb_ref,
     x1_ref, x1p_ref, idx_ref, wt_ref, cnt_ref, cnt_scr) = refs[n_x:]
    tm, d = o_ref.shape

    @pl.when(pl.program_id(0) == 0)
    def _():
        cnt_scr[...] = jnp.zeros_like(cnt_scr)

    x = _token_tile(x_refs)
    o = _head_or_tail(o_ref, ot_ref)
    h = _head_or_tail(h_ref, ht_ref)
    gate = gate_ref[...].astype(F32)
    merged = gate[:, :d] * _bdot(o, wb0_ref[...]) + gate[:, d:] * _bdot(h, wb1_ref[...])
    mix = _bdot(merged.astype(BF16), wo_ref[...])
    x1 = _layernorm(alpha * x + mix, g1_ref[...], b1_ref[...])
    x1_ref[...] = x1
    x1p_ref[...] = _pack_bf16_pairs(x1)

    xh = x1.astype(BF16)
    xl = (x1 - xh.astype(F32)).astype(BF16)
    n_lanes = rw_ref.shape[1] // 2
    hi_terms = _bdot(xh, rw_ref[...])
    logits = (hi_terms[:, n_lanes:] + _bdot(xl, rw_ref[:, :n_lanes])) + hi_terms[:, :n_lanes]
    logits = logits + rb_ref[...]
    lane = lax.broadcasted_iota(jnp.int32, logits.shape, 1)
    lane_f = lane.astype(F32)
    neg_inf = jnp.float32(-jnp.inf)
    cur = jnp.where(lane < n_experts, logits, neg_inf)
    vals, idxs = [], []
    for _ in range(TOP_K):
        m = jnp.max(cur, axis=-1, keepdims=True)
        sel = jnp.min(jnp.where(cur == m, lane_f, float(LANES)), axis=-1, keepdims=True)
        vals.append(m)
        idxs.append(sel)
        cur = jnp.where(lane_f == sel, neg_inf, cur)
    exps = [jnp.exp(v - vals[0]) for v in vals]
    total = sum(exps)

    onehots = [jnp.where(lane_f == idxs[j], 1.0, 0.0) for j in range(TOP_K)]
    chosen = sum(onehots)
    if n_tail_real < tm:
        is_fill = jnp.logical_and(pl.program_id(0) == pl.num_programs(0) - 1,
                                  lax.broadcasted_iota(jnp.int32, chosen.shape, 0) >= n_tail_real)
        chosen = jnp.where(is_fill, 0.0, chosen)
    r = lax.broadcasted_iota(jnp.int32, (tm, tm), 0)
    c = lax.broadcasted_iota(jnp.int32, (tm, tm), 1)
    earlier = jnp.where(r > c, 1.0, 0.0).astype(BF16)
    before = _bdot(earlier, chosen.astype(BF16)) + cnt_scr[...]
    ranks = [jnp.sum(onehots[j] * before, axis=-1, keepdims=True) for j in range(TOP_K)]
    cnt_scr[...] += jnp.sum(chosen, axis=0, keepdims=True)
    cnt_ref[...] = cnt_scr[...]

    idx_out = jnp.zeros(logits.shape, F32)
    wt_out = jnp.zeros(logits.shape, F32)
    for j in range(TOP_K):
        idx_out = jnp.where(lane == j, idxs[j], idx_out)
        idx_out = jnp.where(lane == TOP_K + j, ranks[j], idx_out)
        wt_out = jnp.where(lane == j, exps[j] / total, wt_out)
    idx_ref[...] = idx_out.astype(jnp.int32)
    wt_ref[...] = wt_out


def _post(x_head, x_tail, n_head_tiles, tail_tile, o, h, o_tail, h_tail, gate, w, tm, alpha, n_experts,
          n_tail_real):
    d = x_head.shape[1]
    t = (n_head_tiles + 1) * tm
    assert o.shape[0] == t - tm and o_tail.shape[0] == tm and gate.shape[0] == t
    consts = [w["wb0"], w["wb1"], w["wo"], w["ln1g"], w["ln1b"], w["rw"], w["rb"]]
    row = lambda width: pl.BlockSpec((tm, width), lambda i: (i, 0))
    xs, x_specs = _token_inputs(x_head, x_tail, n_head_tiles, tail_tile, tm)
    return pl.pallas_call(
        functools.partial(_post_kernel, n_x=len(xs), alpha=alpha, n_experts=n_experts, n_tail_real=n_tail_real),
        grid=(t // tm,),
        in_specs=x_specs + _head_tail_specs(n_head_tiles, 0, tm, d) + _head_tail_specs(n_head_tiles, 0, tm, d) +
                 [row(2 * d)] + [_weight_spec(c.shape) for c in consts],
        out_specs=[row(d), row(d // 2), row(LANES), row(LANES), _const_spec((1, LANES))],
        out_shape=[jax.ShapeDtypeStruct((t, d), F32),
                   jax.ShapeDtypeStruct((t, d // 2), jnp.uint32),
                   jax.ShapeDtypeStruct((t, LANES), jnp.int32),
                   jax.ShapeDtypeStruct((t, LANES), F32),
                   jax.ShapeDtypeStruct((1, LANES), F32)],
        scratch_shapes=[pltpu.VMEM((1, LANES), F32)],
        compiler_params=_params("arbitrary"),
        name="post",
    )(*xs, o, o_tail, h, h_tail, gate, *consts)


SC_INDEX_MAX = 128
SC_ALIGN = 8
SC_BUFFER_BYTES = 208 * 1024


def _sc_plan(n_rows, row_bytes):
    info = plsc.get_sparse_core_info()
    n_workers = info.num_cores * info.num_subcores
    per_worker, rem = divmod(n_rows, n_workers)
    assert rem == 0 and per_worker % SC_ALIGN == 0, "rows must split into aligned equal shares per subcore"
    cap = min(SC_INDEX_MAX, SC_BUFFER_BYTES // row_bytes)
    chunk = max(c for c in range(SC_ALIGN, cap + 1, SC_ALIGN) if per_worker % c == 0)
    return info.num_cores, n_workers, per_worker, chunk


def _sc_gather(table, idx):
    n_rows = idx.shape[0]
    d = table.shape[1]
    n_cores, _, per_worker, chunk = _sc_plan(n_rows, d * table.dtype.itemsize)
    mesh = plsc.VectorSubcoreMesh(core_axis_name="c", subcore_axis_name="s")

    n_chunks = per_worker // chunk

    @functools.partial(
        pl.kernel, mesh=mesh, out_type=jax.ShapeDtypeStruct((n_rows, d), table.dtype),
        scratch_types=[pltpu.VMEM((chunk,), jnp.int32)] * 2 + [pltpu.VMEM((chunk, d), table.dtype)] * 2 +
                      [pltpu.SemaphoreType.DMA] * 4)
    def gather(table_hbm, idx_hbm, out_hbm, idx0, idx1, rows0, rows1, g0, g1, w0, w1):
        idx_v, rows_v, gsem, wsem = (idx0, idx1), (rows0, rows1), (g0, g1), (w0, w1)
        worker = lax.axis_index("s") * n_cores + lax.axis_index("c")
        base = worker * per_worker

        def rows_of(c):
            return pl.ds(pl.multiple_of(base + c * chunk, SC_ALIGN), chunk)

        def start_gather(c):
            s = c % 2
            pltpu.sync_copy(idx_hbm.at[rows_of(c)], idx_v[s])
            return pltpu.async_copy(table_hbm.at[idx_v[s]], rows_v[s], gsem[s])

        gathers = {c: start_gather(c) for c in range(min(2, n_chunks))}
        writes = {}
        for c in range(n_chunks):
            s = c % 2
            gathers[c].wait()
            writes[c] = pltpu.async_copy(rows_v[s], out_hbm.at[rows_of(c)], wsem[s])
            if c + 2 < n_chunks:
                writes.pop(c).wait()
                gathers[c + 2] = start_gather(c + 2)
        for write in writes.values():
            write.wait()

    return gather(table, idx)


def _sc_scatter(x, pos, n_out):
    n, d = x.shape
    assert pos.shape == (TOP_K * n,)
    n_cores, _, per_worker, chunk = _sc_plan(n, d * x.dtype.itemsize)
    mesh = plsc.VectorSubcoreMesh(core_axis_name="c", subcore_axis_name="s")

    n_chunks = per_worker // chunk

    @functools.partial(
        pl.kernel, mesh=mesh, out_type=jax.ShapeDtypeStruct((n_out, d), x.dtype),
        scratch_types=[pltpu.VMEM((chunk, d), x.dtype)] * 2 + [pltpu.VMEM((chunk,), jnp.int32)] * TOP_K +
                      [pltpu.SemaphoreType.DMA] * 3)
    def scatter(x_hbm, pos_hbm, out_hbm, rows0, rows1, *rest):
        rows_v, idx_v, (l0, l1, ssem) = (rows0, rows1), rest[:TOP_K], rest[TOP_K:]
        lsem = (l0, l1)
        worker = lax.axis_index("s") * n_cores + lax.axis_index("c")
        base = worker * per_worker

        def first_row(c):
            return pl.multiple_of(base + c * chunk, SC_ALIGN)

        def start_load(c):
            return pltpu.async_copy(x_hbm.at[pl.ds(first_row(c), chunk)], rows_v[c % 2], lsem[c % 2])

        load = start_load(0)
        for c in range(n_chunks):
            next_load = start_load(c + 1) if c + 1 < n_chunks else None
            for k in range(TOP_K):
                pltpu.sync_copy(pos_hbm.at[pl.ds(k * n + first_row(c), chunk)], idx_v[k])
            load.wait()
            scatters = [pltpu.async_copy(rows_v[c % 2], out_hbm.at[idx_v[k]], ssem)
                        for k in range(TOP_K)]
            for scatter_k in scatters:
                scatter_k.wait()
            load = next_load

    return scatter(x, pos)


def _ffn_kernel(be_ref, valid_ref, x_ref, wgu_ref, bgu_ref, wd_ref, bd_ref, o_ref, wgu_bf, wd_bf):
    i = pl.program_id(0)
    bm = x_ref.shape[0]
    f = wd_ref.shape[2]
    valid = valid_ref[i]
    new_expert = jnp.logical_or(i == 0, be_ref[i] != be_ref[jnp.maximum(i - 1, 0)])

    @pl.when(new_expert)
    def _():
        wgu_bf[...] = wgu_ref[0, 0].astype(BF16)
        wd_bf[...] = wd_ref[0, 0].astype(BF16)

    def expert_rows(rows):
        x = _unpack_bf16_pairs(x_ref[rows, :]).astype(BF16)
        acc = jnp.zeros((x.shape[0], wd_ref.shape[3]), F32)
        for c in range(0, f, FFN_COLS):
            gate = _bdot(x, wgu_bf[:, c:c + FFN_COLS]) + bgu_ref[0, 0, :, c:c + FFN_COLS]
            up = _bdot(x, wgu_bf[:, f + c:f + c + FFN_COLS]) + bgu_ref[0, 0, :, f + c:f + c + FFN_COLS]
            gate = jnp.minimum(gate, SWIGLU_LIMIT)
            up = jnp.clip(up, -SWIGLU_LIMIT, SWIGLU_LIMIT)
            act = (up + 1.0) * gate * _sigmoid(SWIGLU_ALPHA * gate)
            acc = acc + _bdot(act.astype(BF16), wd_bf[c:c + FFN_COLS, :])
        o_ref[rows, :] = _pack_bf16_pairs(acc + bd_ref[0, 0])

    @pl.when(valid == bm)
    def _():
        expert_rows(slice(0, bm))

    @pl.when(valid < bm)
    def _():
        for s in range(0, bm, FFN_SUB_ROWS):
            rows = slice(s, s + FFN_SUB_ROWS)

            @pl.when(s < valid)
            def _():
                expert_rows(rows)

            @pl.when(s >= valid)
            def _():
                o_ref[rows, :] = jnp.zeros((FFN_SUB_ROWS, o_ref.shape[1]), o_ref.dtype)


def _ffn(block_e, valid, xs, layer, w_gu, b_gu, w_down, b_down, bm):
    n_layers, n_exp, d, f2 = w_gu.shape
    f = f2 // 2
    n_blocks = block_e.shape[0]
    expert = lambda i, be, nu: (layer, be[i], 0, 0)
    grid_spec = pltpu.PrefetchScalarGridSpec(
        num_scalar_prefetch=2,
        grid=(n_blocks,),
        in_specs=[pl.BlockSpec((bm, d // 2), lambda i, be, nu: (i, 0)),
                  pl.BlockSpec((1, 1, d, f2), expert),
                  pl.BlockSpec((1, 1, 1, f2), expert),
                  pl.BlockSpec((1, 1, f, d), expert),
                  pl.BlockSpec((1, 1, 1, d), expert)],
        out_specs=pl.BlockSpec((bm, d // 2), lambda i, be, nu: (i, 0)),
        scratch_shapes=[pltpu.VMEM((d, f2), BF16), pltpu.VMEM((f, d), BF16)],
    )
    return pl.pallas_call(
        _ffn_kernel,
        grid_spec=grid_spec,
        out_shape=jax.ShapeDtypeStruct((n_blocks * bm, d // 2), jnp.uint32),
        compiler_params=_params("arbitrary"),
        name="ffn",
    )(block_e, valid, xs, w_gu, b_gu.reshape(n_layers, n_exp, 1, f2),
      w_down, b_down.reshape(n_layers, n_exp, 1, d))


def _combine_kernel(x_ref, *refs, alpha, split):
    y_refs, (wt_ref, g_ref, b_ref), out_refs = refs[:TOP_K], refs[TOP_K:TOP_K + 3], refs[TOP_K + 3:]
    wt = wt_ref[...]
    ffn = sum(wt[:, j:j + 1] * _unpack_bf16_pairs(y_refs[j][...]) for j in range(TOP_K))
    res = _layernorm(alpha * x_ref[...] + ffn, g_ref[...], b_ref[...])
    if not split:
        out_refs[0][...] = res
    else:
        is_tail = pl.program_id(0) == pl.num_programs(0) - 1

        @pl.when(jnp.logical_not(is_tail))
        def _():
            out_refs[0][...] = res

        @pl.when(is_tail)
        def _():
            out_refs[1][...] = res


def _combine(x1, y, wt, g, b, tm, alpha, split=False):
    t, d = x1.shape
    n_tiles = t // tm
    planes = [pl.BlockSpec((tm, d // 2), lambda i, j=j: (j * n_tiles + i, 0)) for j in range(TOP_K)]
    if split:
        out_specs = [pl.BlockSpec((tm, d), lambda i: (jnp.minimum(i, n_tiles - 2), 0)), _const_spec((tm, d))]
        out_shape = [jax.ShapeDtypeStruct((t - tm, d), F32), jax.ShapeDtypeStruct((tm, d), F32)]
    else:
        out_specs = pl.BlockSpec((tm, d), lambda i: (i, 0))
        out_shape = jax.ShapeDtypeStruct((t, d), F32)
    return pl.pallas_call(
        functools.partial(_combine_kernel, alpha=alpha, split=split),
        grid=(n_tiles,),
        in_specs=[pl.BlockSpec((tm, d), lambda i: (i, 0))] + planes +
                 [pl.BlockSpec((tm, LANES), lambda i: (i, 0)), _const_spec(g.shape), _const_spec(b.shape)],
        out_specs=out_specs,
        out_shape=out_shape,
        compiler_params=_params("arbitrary"),
        name="combine",
    )(x1, *([y] * TOP_K), wt, g, b)


def _route(experts, ranks, counts, bm):
    n_experts = counts.shape[0]
    i32 = jnp.int32
    padded = (counts + bm - 1) // bm * bm
    ends_pad = jnp.cumsum(padded)
    start_pad = ends_pad - padded
    n_blocks = -(-experts.size // bm) + n_experts
    first_row = jnp.arange(n_blocks, dtype=i32)[:, None] * bm
    block_e = jnp.minimum(jnp.sum(ends_pad[None, :] <= first_row, axis=1), n_experts - 1).astype(i32)
    valid = jnp.clip(counts[block_e] - (first_row[:, 0] - start_pad[block_e]), 0, bm).astype(i32)
    onehot = experts[..., None] == jnp.arange(n_experts, dtype=i32)
    pos = jnp.sum(jnp.where(onehot, start_pad, 0), axis=-1).astype(i32) + ranks
    return block_e, valid, pos


def _moe(x1, x1p, routing, counts, top_w_pad, w, layer, moe_weights, alpha, tm, n_real, split):
    n_tok = x1.shape[0]
    n_experts = moe_weights[0].shape[1]
    block_e, valid, pos = _route(routing[:n_real, :TOP_K], routing[:n_real, TOP_K:2 * TOP_K],
                                 counts[0, :n_experts].astype(jnp.int32), MOE_ROWS)
    n_rows = block_e.shape[0] * MOE_ROWS
    n_fill = n_tok - n_real
    spare = n_rows + jnp.arange(n_fill * TOP_K, dtype=jnp.int32).reshape(n_fill, TOP_K)
    xs = _sc_scatter(x1p, jnp.concatenate([pos, spare]).T.reshape(-1), n_rows + n_fill * TOP_K)
    out_rows = _ffn(block_e, valid, xs, layer, *moe_weights, MOE_ROWS)
    y = _sc_gather(out_rows, jnp.concatenate([pos, jnp.zeros_like(spare)]).T.reshape(-1))
    return _combine(x1, y, top_w_pad, w["ln2g"], w["ln2b"], tm, alpha, split)


def _hi_lo(w):
    hi = w.astype(BF16)
    return jnp.concatenate([hi, (w - hi.astype(F32)).astype(BF16)], axis=1)


def _layer_weights(layer, w_in, gla_w_a2, gla_b_a, gla_norm_g, rg_conv_w, rg_conv_b, rg_w_a, rg_b_a, rg_w_x,
                   rg_b_x, rg_lambda, b_merge, w_branch, w_o, ln1_g, ln1_b, ln2_g, ln2_b, router_w, router_b,
                   kw, vw):
    d = w_in.shape[1]
    rank = gla_w_a2.shape[1]
    width = rg_conv_w.shape[2]
    n_exp = router_w.shape[2]
    c0 = 2 * kw + 2 * vw
    wi = w_in[layer]
    row = lambda v: v.reshape(1, -1)
    return {
        "wq": wi[:, :c0].astype(BF16),
        "wal": jnp.pad(wi[:, c0:c0 + rank], ((0, 0), (0, LANES - rank))).astype(BF16),
        "wa2": jnp.pad(gla_w_a2[layer], ((0, LANES - rank), (0, 0))).astype(BF16),
        "ba": row(gla_b_a[layer]),
        "wxr": wi[:, c0 + rank:c0 + rank + width].astype(BF16),
        "wmg": wi[:, c0 + rank + width:].astype(BF16),
        "bmg": row(b_merge[layer]),
        "ng": row(gla_norm_g[layer]),
        "cw": rg_conv_w[layer], "cb": row(rg_conv_b[layer]),
        "wax": jnp.concatenate([rg_w_a[layer], rg_w_x[layer]], axis=-1).astype(BF16),
        "rba": row(rg_b_a[layer]), "rbx": row(rg_b_x[layer]), "lam": row(rg_lambda[layer]),
        "wb0": w_branch[layer, 0].astype(BF16), "wb1": w_branch[layer, 1].astype(BF16),
        "wo": w_o[layer].astype(BF16),
        "ln1g": row(ln1_g[layer]), "ln1b": row(ln1_b[layer]),
        "ln2g": row(ln2_g[layer]), "ln2b": row(ln2_b[layer]),
        "rw": _hi_lo(jnp.pad(router_w[layer], ((0, 0), (0, LANES - n_exp)))),
        "rb": jnp.pad(row(router_b[layer]), ((0, 0), (0, LANES - n_exp))),
    }


def kernel(x_prompt, x_sample, state_gla, state_rglru, state_conv, ln1_g, ln1_b, w_in, gla_w_a2, gla_b_a, gla_norm_g, rg_conv_w, rg_conv_b, rg_w_a, rg_b_a, rg_w_x, rg_b_x, rg_lambda, b_merge, w_branch, w_o, ln2_g, ln2_b, router_w, router_b, moe_w_gu, moe_b_gu, moe_w_down, moe_b_down):
    n_p, seq, d = x_prompt.shape
    n_s, dec_seq, _ = x_sample.shape
    assert dec_seq == 1, "the sample group carries one new token per sequence"
    depth, _, heads, dk, dv = state_gla.shape
    kw, vw = heads * dk, heads * dv
    n_exp = router_w.shape[2]
    alpha = (2.0 * depth) ** 0.25
    t_p = n_p * seq
    pad_rows = BF16_SUBLANES

    tile = PROJ_ROWS
    assert t_p % tile == 0 and n_s <= tile
    n_head_tiles = t_p // tile
    tail_pad = lambda a: jnp.pad(a, ((0, tile - n_s), (0, 0)))
    x_head, x_tail, tail_tile = x_prompt.reshape(t_p, d), tail_pad(x_sample.reshape(n_s, d)), 0
    moe_weights = (moe_w_gu, moe_b_gu, moe_w_down, moe_b_down)
    outs = {k: [] for k in ("gla_p", "rg_p", "cv_p", "rg_s", "cv_s")}
    gla_s = None
    for layer in range(depth):
        w = _layer_weights(layer, w_in, gla_w_a2, gla_b_a, gla_norm_g, rg_conv_w, rg_conv_b, rg_w_a, rg_b_a,
                           rg_w_x, rg_b_x, rg_lambda, b_merge, w_branch, w_o, ln1_g, ln1_b, ln2_g, ln2_b,
                           router_w, router_b, kw, vw)
        qkvg, loga, xr, gate = _proj(x_head, x_tail, n_head_tiles, tail_tile, w, tile)
        o_p, s_p = _gla(qkvg, loga, w["ng"], None, layer, n_p, seq, SEQ_ROWS, GLA_CHUNK, heads, dk, dv)
        h_p, hl_p, cv_p = _rglru_seq(xr, w, n_p, seq, SEQ_ROWS)
        pad = lambda a: jnp.pad(a[t_p:t_p + n_s, None, :],
                                ((0, 0), (0, pad_rows - 1), (0, 0))).reshape(n_s * pad_rows, -1)
        o_s, gla_s = _gla(pad(qkvg), pad(loga), w["ng"], state_gla, layer, n_s, pad_rows,
                          DECODE_SEQS * pad_rows, pad_rows, heads, dk, dv, stack=(depth, gla_s))
        o_s = o_s.reshape(n_s, pad_rows, vw)[:, 0]
        h_s, hn_s, cv_s = _rglru_step(xr[t_p:t_p + n_s], state_conv[layer], state_rglru[layer], w)
        x1, x1p, routing, wt, cnt = _post(x_head, x_tail, n_head_tiles, tail_tile, o_p, h_p, tail_pad(o_s),
                                          tail_pad(h_s), gate, w, tile, alpha, n_exp, n_s)
        last = layer == depth - 1
        x = _moe(x1, x1p, routing, cnt, wt, w, layer, moe_weights, alpha, tile, t_p + n_s, split=last)
        if not last:
            x_head, x_tail, tail_tile = x, x, n_head_tiles
        outs["gla_p"].append(s_p)
        outs["rg_p"].append(hl_p.reshape(n_p, -1))
        outs["cv_p"].append(cv_p)
        outs["rg_s"].append(hn_s)
        outs["cv_s"].append(cv_s)
    y_head, y_tail = x
    return (y_head.reshape(n_p, seq, d), y_tail[:n_s].reshape(n_s, dec_seq, d),
            jnp.stack(outs["gla_p"]), jnp.stack(outs["rg_p"]), jnp.stack(outs["cv_p"]),
            gla_s, jnp.stack(outs["rg_s"]), jnp.stack(outs["cv_s"]))
```

```python
import functools

import jax
import jax.numpy as jnp
from jax import lax
from jax.experimental import pallas as pl
from jax.experimental.pallas import tpu as pltpu
from jax.experimental.pallas import tpu_sc as plsc

F32 = jnp.float32
BF16 = jnp.bfloat16

TOP_K = 4
GLA_GATE_TAU = 16.0
GLA_CHUNK = 64
RG_C = 8.0
SWIGLU_LIMIT = 7.0
SWIGLU_ALPHA = 1.702
LN_EPS = 1e-5

LANES = 128
BF16_SUBLANES = 16
VMEM_LIMIT = 56 * 1024 * 1024

PROJ_ROWS = 512
SEQ_ROWS = 256
DECODE_SEQS = 8
MOE_ROWS = 1024
FFN_COLS = 512
FFN_SUB_ROWS = 256


def _params(*sem):
    return pltpu.CompilerParams(dimension_semantics=sem, vmem_limit_bytes=VMEM_LIMIT)


def _const_spec(shape):
    nd = len(shape)
    return pl.BlockSpec(shape, lambda *_: (0,) * nd)


def _weight_spec(shape):
    nd = len(shape)
    return pl.BlockSpec(shape, lambda *_: (0,) * nd, pipeline_mode=pl.Buffered(1))


def _bdot(a, b):
    return jnp.dot(a, b, preferred_element_type=F32)


def _split3(x):
    hi = x.astype(BF16)
    r1 = x - hi.astype(F32)
    mid = r1.astype(BF16)
    lo = (r1 - mid.astype(F32)).astype(BF16)
    return hi, mid, lo


def _sigmoid(x):
    return 0.5 * jnp.tanh(0.5 * x) + 0.5


def _log_sigmoid(x):
    return jnp.minimum(x, 0.0) - jnp.log1p(jnp.exp(-jnp.abs(x)))


def _softplus(x):
    return jnp.maximum(x, 0.0) + jnp.log1p(jnp.exp(-jnp.abs(x)))


def _layernorm(x, g, b):
    mu = jnp.mean(x, axis=-1, keepdims=True)
    xc = x - mu
    var = jnp.mean(xc * xc, axis=-1, keepdims=True)
    return xc * lax.rsqrt(var + LN_EPS) * g + b


def _head_tail_specs(n_head_tiles, tail_tile, tm, d):
    return [pl.BlockSpec((tm, d), lambda i: (jnp.minimum(i, n_head_tiles - 1), 0)),
            pl.BlockSpec((tm, d), lambda i: (tail_tile, 0))]


def _head_or_tail(head_ref, tail_ref):
    is_tail = pl.program_id(0) == pl.num_programs(0) - 1
    return jnp.where(is_tail, tail_ref[...], head_ref[...])


def _token_inputs(x_head, x_tail, n_head_tiles, tail_tile, tm):
    d = x_head.shape[1]
    if x_tail is x_head:
        assert tail_tile == n_head_tiles
        return [x_head], [pl.BlockSpec((tm, d), lambda i: (i, 0))]
    return [x_head, x_tail], _head_tail_specs(n_head_tiles, tail_tile, tm, d)


def _token_tile(x_refs):
    return x_refs[0][...] if len(x_refs) == 1 else _head_or_tail(*x_refs)


def _proj_kernel(*refs, n_x):
    x_refs = refs[:n_x]
    wq_ref, wal_ref, wa2_ref, ba_ref, wxr_ref, wmg_ref, bmg_ref, qkvg_ref, loga_ref, xr_ref, gate_ref = refs[n_x:]
    xb = _token_tile(x_refs).astype(BF16)
    d = xb.shape[1]
    for c in range(0, qkvg_ref.shape[1], d):
        qkvg_ref[:, c:c + d] = _bdot(xb, wq_ref[:, c:c + d]).astype(BF16)
    a_low = _bdot(xb, wal_ref[...])
    z = _bdot(a_low.astype(BF16), wa2_ref[...]) + ba_ref[...]
    loga_ref[...] = _log_sigmoid(z) * (1.0 / GLA_GATE_TAU)
    xr_ref[...] = _bdot(xb, wxr_ref[...])
    for c in range(0, gate_ref.shape[1], d):
        gate_ref[:, c:c + d] = _sigmoid(_bdot(xb, wmg_ref[:, c:c + d]) + bmg_ref[:, c:c + d]).astype(BF16)


def _proj(x_head, x_tail, n_head_tiles, tail_tile, w, tm):
    d = x_head.shape[1]
    t = (n_head_tiles + 1) * tm
    kw = w["wq"].shape[1]
    nk = w["wa2"].shape[1]
    consts = [w["wq"], w["wal"], w["wa2"], w["ba"], w["wxr"], w["wmg"], w["bmg"]]
    xs, x_specs = _token_inputs(x_head, x_tail, n_head_tiles, tail_tile, tm)
    return pl.pallas_call(
        functools.partial(_proj_kernel, n_x=len(xs)),
        grid=(t // tm,),
        in_specs=x_specs + [_weight_spec(c.shape) for c in consts],
        out_specs=[pl.BlockSpec((tm, kw), lambda i: (i, 0)),
                   pl.BlockSpec((tm, nk), lambda i: (i, 0)),
                   pl.BlockSpec((tm, d), lambda i: (i, 0)),
                   pl.BlockSpec((tm, 2 * d), lambda i: (i, 0))],
        out_shape=[jax.ShapeDtypeStruct((t, kw), BF16),
                   jax.ShapeDtypeStruct((t, nk), F32),
                   jax.ShapeDtypeStruct((t, d), F32),
                   jax.ShapeDtypeStruct((t, 2 * d), BF16)],
        compiler_params=_params("arbitrary"),
        name="proj",
    )(*xs, *consts)


def _gla_kernel(*refs, chunk, n_chunks, heads, has_state, n_inputs, state_slices, state_slot):
    q_ref, k_ref, v_ref, la_ref, g_ref, ng_ref = refs[:6]
    s0_ref = refs[6] if has_state else None
    o_ref, so_ref, s_scr, o_scr, p_scr, kv_scr = refs[n_inputs:]
    t = pl.program_id(1)
    dk = q_ref.shape[1] // heads
    dv = v_ref.shape[1] // heads
    scale = dk ** -0.5
    chunks_per_seq = n_chunks // s_scr.shape[0]

    @pl.when(t == 0)
    def _():
        if has_state:
            s_scr[...] = s0_ref[0]
        else:
            s_scr[...] = jnp.zeros_like(s_scr)

    n_rows = n_chunks * chunk
    r = lax.broadcasted_iota(jnp.int32, (n_rows, n_rows), 0)
    c = lax.broadcasted_iota(jnp.int32, (n_rows, n_rows), 1)
    shift = chunk.bit_length() - 1
    assert 1 << shift == chunk, "chunk must be a power of two"
    tril = jnp.where(jnp.logical_and(r >> shift == c >> shift, r >= c), 1.0, 0.0).astype(BF16)
    la_hi, la_mid, la_lo = _split3(la_ref[...])
    b_all = _bdot(tril, la_hi) + _bdot(tril, la_mid) + _bdot(tril, la_lo)
    q_all = (q_ref[...].astype(F32) * scale * jnp.exp(b_all)).astype(BF16)
    k_all = (k_ref[...].astype(F32) * jnp.exp(-b_all)).astype(BF16)
    causal = (lax.broadcasted_iota(jnp.int32, (chunk, chunk), 0) >=
              lax.broadcasted_iota(jnp.int32, (chunk, chunk), 1))

    decays = []
    for ci in range(n_chunks):
        rows = slice(ci * chunk, (ci + 1) * chunk)
        b = b_all[rows, :]
        b_last = b[chunk - 1:chunk, :]
        q_e, k_e = q_all[rows, :], k_all[rows, :]
        k_d = (k_ref[rows, :].astype(F32) * jnp.exp(b_last - b)).astype(BF16)
        decays.append(jnp.exp(b_last))
        for h in range(heads):
            ks = slice(h * dk, (h + 1) * dk)
            v_h = v_ref[rows, h * dv:(h + 1) * dv]
            scores = lax.dot_general(q_e[:, ks], k_e[:, ks], (((1,), (1,)), ((), ())),
                                     preferred_element_type=F32)
            p_scr[rows, h * chunk:(h + 1) * chunk] = jnp.where(causal, scores, 0.0).astype(BF16)
            kv_scr[ci, h] = lax.dot_general(k_d[:, ks], v_h, (((0,), (0,)), ((), ())),
                                            preferred_element_type=F32)

    for ci in range(n_chunks):
        rows = slice(ci * chunk, (ci + 1) * chunk)
        seq = ci // chunks_per_seq
        for h in range(heads):
            ks = slice(h * dk, (h + 1) * dk)
            vs = slice(h * dv, (h + 1) * dv)
            s_old = s_scr[seq, h]
            o_scr[rows, vs] = (_bdot(p_scr[rows, h * chunk:(h + 1) * chunk], v_ref[rows, vs]) +
                               _bdot(q_all[rows, ks], s_old.astype(BF16)))
            s_scr[seq, h] = jnp.transpose(decays[ci][:, ks]) * s_old + kv_scr[ci, h]

    for h in range(heads):
        vs = slice(h * dv, (h + 1) * dv)
        o = o_scr[:, vs]
        mu = jnp.mean(o, axis=-1, keepdims=True)
        oc = o - mu
        var = jnp.mean(oc * oc, axis=-1, keepdims=True)
        on = oc * lax.rsqrt(var + LN_EPS) * ng_ref[:, vs]
        g = g_ref[:, vs].astype(F32)
        o_ref[:, vs] = (on * (g * _sigmoid(g))).astype(BF16)

    @pl.when(t == pl.num_programs(1) - 1)
    def _():
        if state_slices == 0:
            so_ref[...] = s_scr[...]
        for layer_slice in range(state_slices):
            so_ref[layer_slice] = s_scr[...] if layer_slice == state_slot else jnp.zeros_like(s_scr)


def _gla(qkvg, loga, norm_g, s0, layer, nb, s_len, rows, chunk, heads, dk, dv, stack=None):
    t = nb * s_len
    seqs = max(1, rows // s_len)
    assert s_len * seqs % rows == 0 and nb % seqs == 0 and s_len % chunk == 0
    nb, tps = nb // seqs, s_len * seqs // rows
    kw, vw = heads * dk, heads * dv
    has_state = s0 is not None
    state_shape, state_block, state_index = (nb * seqs, heads, dk, dv), (seqs, heads, dk, dv), lambda b, i: (b, 0, 0, 0)
    state_slices, state_slot, aliases = 0, 0, {}
    if stack is not None:
        depth, prev = stack
        state_shape = (depth,) + state_shape
        if prev is None:
            state_slices, state_slot = depth, layer
            state_block, state_index = (depth,) + state_block, lambda b, i: (0, b, 0, 0, 0)
        else:
            state_slices = 1
            state_block, state_index = (1,) + state_block, lambda b, i: (layer, b, 0, 0, 0)
    in_specs = [pl.BlockSpec((rows, kw), lambda b, i: (b * tps + i, 0)),
                pl.BlockSpec((rows, kw), lambda b, i: (b * tps + i, 1)),
                pl.BlockSpec((rows, vw), lambda b, i: (b * tps + i, (2 * kw) // vw)),
                pl.BlockSpec((rows, kw), lambda b, i: (b * tps + i, 0)),
                pl.BlockSpec((rows, vw), lambda b, i: (b * tps + i, (2 * kw) // vw + 1)),
                _const_spec(norm_g.shape)]
    args = [qkvg, qkvg, qkvg, loga, qkvg, norm_g]
    if has_state:
        in_specs.append(pl.BlockSpec((1, seqs, heads, dk, dv), lambda b, i: (layer, b, 0, 0, 0)))
        args.append(s0)
    if stack is not None and stack[1] is not None:
        in_specs.append(pl.BlockSpec(memory_space=pl.ANY))
        aliases[len(args)] = 1
        args.append(stack[1])
    return pl.pallas_call(
        functools.partial(_gla_kernel, chunk=chunk, n_chunks=rows // chunk, heads=heads, has_state=has_state,
                          n_inputs=len(args), state_slices=state_slices, state_slot=state_slot),
        grid=(nb, tps),
        in_specs=in_specs,
        out_specs=[pl.BlockSpec((rows, vw), lambda b, i: (b * tps + i, 0)),
                   pl.BlockSpec(state_block, state_index)],
        out_shape=[jax.ShapeDtypeStruct((t, vw), BF16), jax.ShapeDtypeStruct(state_shape, F32)],
        scratch_shapes=[pltpu.VMEM((seqs, heads, dk, dv), F32), pltpu.VMEM((rows, vw), F32),
                        pltpu.VMEM((rows, heads * chunk), BF16), pltpu.VMEM((rows // chunk, heads, dk, dv), F32)],
        input_output_aliases=aliases,
        compiler_params=_params("arbitrary", "arbitrary"),
        name="gla",
    )(*args)


def _rg_gates(xc, wax_ref, ba_ref, bx_ref, lam_ref):
    n_blocks, bw, _ = wax_ref.shape
    r_parts, i_parts = [], []
    for n in range(n_blocks):
        cs = slice(n * bw, (n + 1) * bw)
        pre = _bdot(xc[:, cs].astype(BF16), wax_ref[n])
        r_parts.append(_sigmoid(pre[:, :bw] + ba_ref[:, cs]))
        i_parts.append(_sigmoid(pre[:, bw:] + bx_ref[:, cs]))
    r = jnp.concatenate(r_parts, axis=1)
    i = jnp.concatenate(i_parts, axis=1)
    log_a = r * (-RG_C * _softplus(-lam_ref[...]))
    a = jnp.exp(log_a)
    bx = jnp.sqrt(jnp.tanh(-log_a) * (1.0 + a * a)) * (i * xc)
    return a, bx


def _rglru_seq_kernel(xr_ref, cw_ref, cb_ref, wax_ref, ba_ref, bx_ref, lam_ref,
                      h_ref, hl_ref, cv_ref, cbuf, a_scr, b_scr, h_scr, hc):
    t = pl.program_id(1)
    rows = xr_ref.shape[0]
    taps = cw_ref.shape[0]
    head = 8

    @pl.when(t == 0)
    def _():
        cbuf[0:head, :] = jnp.zeros((head, cbuf.shape[1]), F32)
        hc[...] = jnp.zeros_like(hc)

    cbuf[head:head + rows, :] = xr_ref[...]
    first = head - (taps - 1)
    xc = cb_ref[...] + sum(cbuf[first + j:first + j + rows, :] * cw_ref[j:j + 1, :] for j in range(taps))
    tail = cbuf[first + rows:head + rows, :]
    cbuf[first:head, :] = tail

    a, bx = _rg_gates(xc, wax_ref, ba_ref, bx_ref, lam_ref)
    a_scr[...] = a
    b_scr[...] = bx

    def step(i, h):
        h = a_scr[pl.ds(i, 1), :] * h + b_scr[pl.ds(i, 1), :]
        h_scr[pl.ds(i, 1), :] = h
        return h

    h_last = lax.fori_loop(0, rows, step, hc[...], unroll=16)
    hc[...] = h_last
    h_ref[...] = h_scr[...].astype(BF16)

    @pl.when(t == pl.num_programs(1) - 1)
    def _():
        hl_ref[0] = h_last
        cv_ref[0] = tail


def _rglru_seq(xr, w, nb, s_len, rows):
    width = xr.shape[1]
    t = nb * s_len
    tps = s_len // rows
    taps = w["cw"].shape[0]
    consts = [w["cw"], w["cb"], w["wax"], w["rba"], w["rbx"], w["lam"]]
    return pl.pallas_call(
        _rglru_seq_kernel,
        grid=(nb, tps),
        in_specs=[pl.BlockSpec((rows, width), lambda b, i: (b * tps + i, 0))] + [_const_spec(c.shape) for c in consts],
        out_specs=[pl.BlockSpec((rows, width), lambda b, i: (b * tps + i, 0)),
                   pl.BlockSpec((1, 1, width), lambda b, i: (b, 0, 0)),
                   pl.BlockSpec((1, taps - 1, width), lambda b, i: (b, 0, 0))],
        out_shape=[jax.ShapeDtypeStruct((t, width), BF16),
                   jax.ShapeDtypeStruct((nb, 1, width), F32),
                   jax.ShapeDtypeStruct((nb, taps - 1, width), F32)],
        scratch_shapes=[pltpu.VMEM((rows + 8, width), F32), pltpu.VMEM((rows, width), F32),
                        pltpu.VMEM((rows, width), F32), pltpu.VMEM((rows, width), F32),
                        pltpu.VMEM((1, width), F32)],
        compiler_params=_params("arbitrary", "arbitrary"),
        name="rglru_seq",
    )(xr, *consts)


def _rglru_step_kernel(xr_ref, sc_ref, h0_ref, cw_ref, cb_ref, wax_ref, ba_ref, bx_ref, lam_ref,
                       h_ref, hn_ref, cv_ref):
    taps = cw_ref.shape[0]
    xr = xr_ref[...]
    xc = (cb_ref[...] + sum(sc_ref[:, j, :] * cw_ref[j:j + 1, :] for j in range(taps - 1)) +
          xr * cw_ref[taps - 1:taps, :])
    a, bx = _rg_gates(xc, wax_ref, ba_ref, bx_ref, lam_ref)
    h = a * h0_ref[...] + bx
    h_ref[...] = h.astype(BF16)
    hn_ref[...] = h
    for j in range(taps - 2):
        cv_ref[:, j, :] = sc_ref[:, j + 1, :]
    cv_ref[:, taps - 2, :] = xr


def _rglru_step(xr, conv, h0, w):
    n, width = xr.shape
    consts = [w["cw"], w["cb"], w["wax"], w["rba"], w["rbx"], w["lam"]]
    args = [xr, conv, h0] + consts
    return pl.pallas_call(
        _rglru_step_kernel,
        grid=(1,),
        in_specs=[_const_spec(a.shape) for a in args],
        out_specs=[_const_spec((n, width)), _const_spec((n, width)), _const_spec(conv.shape)],
        out_shape=[jax.ShapeDtypeStruct((n, width), BF16),
                   jax.ShapeDtypeStruct((n, width), F32),
                   jax.ShapeDtypeStruct(conv.shape, F32)],
        compiler_params=_params("arbitrary"),
        name="rglru_step",
    )(*args)


def _pack_bf16_pairs(x):
    half = x.shape[1] // 2
    hi = pltpu.bitcast(x[:, :half].astype(BF16).astype(F32), jnp.uint32)
    lo = pltpu.bitcast(x[:, half:].astype(BF16).astype(F32), jnp.uint32)
    return hi | (lo >> 16)


def _unpack_bf16_pairs(p):
    hi = pltpu.bitcast(p & jnp.uint32(0xFFFF0000), F32)
    lo = pltpu.bitcast(p << 16, F32)
    return jnp.concatenate([hi, lo], axis=1)


def _post_kernel(*refs, n_x, alpha, n_experts, n_tail_real):
    x_refs = refs[:n_x]
    (o_ref, ot_ref, h_ref, ht_ref, gate_ref, wb0_ref, wb1_ref, wo_ref, g1_ref, b1_ref, rw_ref, rb_ref,
     x1_ref, x1p_ref, idx_ref, wt_ref, cnt_ref, cnt_scr) = refs[n_x:]
    tm, d = o_ref.shape

    @pl.when(pl.program_id(0) == 0)
    def _():
        cnt_scr[...] = jnp.zeros_like(cnt_scr)

    x = _token_tile(x_refs)
    o = _head_or_tail(o_ref, ot_ref)
    h = _head_or_tail(h_ref, ht_ref)
    gate = gate_ref[...].astype(F32)
    merged = gate[:, :d] * _bdot(o, wb0_ref[...]) + gate[:, d:] * _bdot(h, wb1_ref[...])
    mix = _bdot(merged.astype(BF16), wo_ref[...])
    x1 = _layernorm(alpha * x + mix, g1_ref[...], b1_ref[...])
    x1_ref[...] = x1
    x1p_ref[...] = _pack_bf16_pairs(x1)

    xh = x1.astype(BF16)
    xl = (x1 - xh.astype(F32)).astype(BF16)
    n_lanes = rw_ref.shape[1] // 2
    hi_terms = _bdot(xh, rw_ref[...])
    logits = (hi_terms[:, n_lanes:] + _bdot(xl, rw_ref[:, :n_lanes])) + hi_terms[:, :n_lanes]
    logits = logits + rb_ref[...]
    lane = lax.broadcasted_iota(jnp.int32, logits.shape, 1)
    lane_f = lane.astype(F32)
    neg_inf = jnp.float32(-jnp.inf)
    cur = jnp.where(lane < n_experts, logits, neg_inf)
    vals, idxs = [], []
    for _ in range(TOP_K):
        m = jnp.max(cur, axis=-1, keepdims=True)
        sel = jnp.min(jnp.where(cur == m, lane_f, float(LANES)), axis=-1, keepdims=True)
        vals.append(m)
        idxs.append(sel)
        cur = jnp.where(lane_f == sel, neg_inf, cur)
    exps = [jnp.exp(v - vals[0]) for v in vals]
    total = sum(exps)

    onehots = [jnp.where(lane_f == idxs[j], 1.0, 0.0) for j in range(TOP_K)]
    chosen = sum(onehots)
    if n_tail_real < tm:
        is_fill = jnp.logical_and(pl.program_id(0) == pl.num_programs(0) - 1,
                                  lax.broadcasted_iota(jnp.int32, chosen.shape, 0) >= n_tail_real)
        chosen = jnp.where(is_fill, 0.0, chosen)
    r = lax.broadcasted_iota(jnp.int32, (tm, tm), 0)
    c = lax.broadcasted_iota(jnp.int32, (tm, tm), 1)
    earlier = jnp.where(r > c, 1.0, 0.0).astype(BF16)
    before = _bdot(earlier, chosen.astype(BF16)) + cnt_scr[...]
    ranks = [jnp.sum(onehots[j] * before, axis=-1, keepdims=True) for j in range(TOP_K)]
    cnt_scr[...] += jnp.sum(chosen, axis=0, keepdims=True)
    cnt_ref[...] = cnt_scr[...]

    idx_out = jnp.zeros(logits.shape, F32)
    wt_out = jnp.zeros(logits.shape, F32)
    for j in range(TOP_K):
        idx_out = jnp.where(lane == j, idxs[j], idx_out)
        idx_out = jnp.where(lane == TOP_K + j, ranks[j], idx_out)
        wt_out = jnp.where(lane == j, exps[j] / total, wt_out)
    idx_ref[...] = idx_out.astype(jnp.int32)
    wt_ref[...] = wt_out


def _post(x_head, x_tail, n_head_tiles, tail_tile, o, h, o_tail, h_tail, gate, w, tm, alpha, n_experts,
          n_tail_real):
    d = x_head.shape[1]
    t = (n_head_tiles + 1) * tm
    assert o.shape[0] == t - tm and o_tail.shape[0] == tm and gate.shape[0] == t
    consts = [w["wb0"], w["wb1"], w["wo"], w["ln1g"], w["ln1b"], w["rw"], w["rb"]]
    row = lambda width: pl.BlockSpec((tm, width), lambda i: (i, 0))
    xs, x_specs = _token_inputs(x_head, x_tail, n_head_tiles, tail_tile, tm)
    return pl.pallas_call(
        functools.partial(_post_kernel, n_x=len(xs), alpha=alpha, n_experts=n_experts, n_tail_real=n_tail_real),
        grid=(t // tm,),
        in_specs=x_specs + _head_tail_specs(n_head_tiles, 0, tm, d) + _head_tail_specs(n_head_tiles, 0, tm, d) +
                 [row(2 * d)] + [_weight_spec(c.shape) for c in consts],
        out_specs=[row(d), row(d // 2), row(LANES), row(LANES), _const_spec((1, LANES))],
        out_shape=[jax.ShapeDtypeStruct((t, d), F32),
                   jax.ShapeDtypeStruct((t, d // 2), jnp.uint32),
                   jax.ShapeDtypeStruct((t, LANES), jnp.int32),
                   jax.ShapeDtypeStruct((t, LANES), F32),
                   jax.ShapeDtypeStruct((1, LANES), F32)],
        scratch_shapes=[pltpu.VMEM((1, LANES), F32)],
        compiler_params=_params("arbitrary"),
        name="post",
    )(*xs, o, o_tail, h, h_tail, gate, *consts)


SC_INDEX_MAX = 128
SC_ALIGN = 8
SC_BUFFER_BYTES = 208 * 1024


def _sc_plan(n_rows, row_bytes):
    info = plsc.get_sparse_core_info()
    n_workers = info.num_cores * info.num_subcores
    per_worker, rem = divmod(n_rows, n_workers)
    assert rem == 0 and per_worker % SC_ALIGN == 0, "rows must split into aligned equal shares per subcore"
    cap = min(SC_INDEX_MAX, SC_BUFFER_BYTES // row_bytes)
    chunk = max(c for c in range(SC_ALIGN, cap + 1, SC_ALIGN) if per_worker % c == 0)
    return info.num_cores, n_workers, per_worker, chunk


def _sc_gather(table, idx):
    n_rows = idx.shape[0]
    d = table.shape[1]
    n_cores, _, per_worker, chunk = _sc_plan(n_rows, d * table.dtype.itemsize)
    mesh = plsc.VectorSubcoreMesh(core_axis_name="c", subcore_axis_name="s")

    n_chunks = per_worker // chunk

    @functools.partial(
        pl.kernel, mesh=mesh, out_type=jax.ShapeDtypeStruct((n_rows, d), table.dtype),
        scratch_types=[pltpu.VMEM((chunk,), jnp.int32)] * 2 + [pltpu.VMEM((chunk, d), table.dtype)] * 2 +
                      [pltpu.SemaphoreType.DMA] * 4)
    def gather(table_hbm, idx_hbm, out_hbm, idx0, idx1, rows0, rows1, g0, g1, w0, w1):
        idx_v, rows_v, gsem, wsem = (idx0, idx1), (rows0, rows1), (g0, g1), (w0, w1)
        worker = lax.axis_index("s") * n_cores + lax.axis_index("c")
        base = worker * per_worker

        def rows_of(c):
            return pl.ds(pl.multiple_of(base + c * chunk, SC_ALIGN), chunk)

        def start_gather(c):
            s = c % 2
            pltpu.sync_copy(idx_hbm.at[rows_of(c)], idx_v[s])
            return pltpu.async_copy(table_hbm.at[idx_v[s]], rows_v[s], gsem[s])

        gathers = {c: start_gather(c) for c in range(min(2, n_chunks))}
        writes = {}
        for c in range(n_chunks):
            s = c % 2
            gathers[c].wait()
            writes[c] = pltpu.async_copy(rows_v[s], out_hbm.at[rows_of(c)], wsem[s])
            if c + 2 < n_chunks:
                writes.pop(c).wait()
                gathers[c + 2] = start_gather(c + 2)
        for write in writes.values():
            write.wait()

    return gather(table, idx)


def _sc_scatter(x, pos, n_out):
    n, d = x.shape
    assert pos.shape == (TOP_K * n,)
    n_cores, _, per_worker, chunk = _sc_plan(n, d * x.dtype.itemsize)
    mesh = plsc.VectorSubcoreMesh(core_axis_name="c", subcore_axis_name="s")

    n_chunks = per_worker // chunk

    @functools.partial(
        pl.kernel, mesh=mesh, out_type=jax.ShapeDtypeStruct((n_out, d), x.dtype),
        scratch_types=[pltpu.VMEM((chunk, d), x.dtype)] * 2 + [pltpu.VMEM((chunk,), jnp.int32)] * TOP_K +
                      [pltpu.SemaphoreType.DMA] * 3)
    def scatter(x_hbm, pos_hbm, out_hbm, rows0, rows1, *rest):
        rows_v, idx_v, (l0, l1, ssem) = (rows0, rows1), rest[:TOP_K], rest[TOP_K:]
        lsem = (l0, l1)
        worker = lax.axis_index("s") * n_cores + lax.axis_index("c")
        base = worker * per_worker

        def first_row(c):
            return pl.multiple_of(base + c * chunk, SC_ALIGN)

        def start_load(c):
            return pltpu.async_copy(x_hbm.at[pl.ds(first_row(c), chunk)], rows_v[c % 2], lsem[c % 2])

        load = start_load(0)
        for c in range(n_chunks):
            next_load = start_load(c + 1) if c + 1 < n_chunks else None
            for k in range(TOP_K):
                pltpu.sync_copy(pos_hbm.at[pl.ds(k * n + first_row(c), chunk)], idx_v[k])
            load.wait()
            scatters = [pltpu.async_copy(rows_v[c % 2], out_hbm.at[idx_v[k]], ssem)
                        for k in range(TOP_K)]
            for scatter_k in scatters:
                scatter_k.wait()
            load = next_load

    return scatter(x, pos)


def _ffn_kernel(be_ref, valid_ref, x_ref, wgu_ref, bgu_ref, wd_ref, bd_ref, o_ref, wgu_bf, wd_bf):
    i = pl.program_id(0)
    bm = x_ref.shape[0]
    f = wd_ref.shape[2]
    valid = valid_ref[i]
    new_expert = jnp.logical_or(i == 0, be_ref[i] != be_ref[jnp.maximum(i - 1, 0)])

    @pl.when(new_expert)
    def _():
        wgu_bf[...] = wgu_ref[0, 0].astype(BF16)
        wd_bf[...] = wd_ref[0, 0].astype(BF16)

    def expert_rows(rows):
        x = _unpack_bf16_pairs(x_ref[rows, :]).astype(BF16)
        acc = jnp.zeros((x.shape[0], wd_ref.shape[3]), F32)
        for c in range(0, f, FFN_COLS):
            gate = _bdot(x, wgu_bf[:, c:c + FFN_COLS]) + bgu_ref[0, 0, :, c:c + FFN_COLS]
            up = _bdot(x, wgu_bf[:, f + c:f + c + FFN_COLS]) + bgu_ref[0, 0, :, f + c:f + c + FFN_COLS]
            gate = jnp.minimum(gate, SWIGLU_LIMIT)
            up = jnp.clip(up, -SWIGLU_LIMIT, SWIGLU_LIMIT)
            act = (up + 1.0) * gate * _sigmoid(SWIGLU_ALPHA * gate)
            acc = acc + _bdot(act.astype(BF16), wd_bf[c:c + FFN_COLS, :])
        o_ref[rows, :] = _pack_bf16_pairs(acc + bd_ref[0, 0])

    @pl.when(valid == bm)
    def _():
        expert_rows(slice(0, bm))

    @pl.when(valid < bm)
    def _():
        for s in range(0, bm, FFN_SUB_ROWS):
            rows = slice(s, s + FFN_SUB_ROWS)

            @pl.when(s < valid)
            def _():
                expert_rows(rows)

            @pl.when(s >= valid)
            def _():
                o_ref[rows, :] = jnp.zeros((FFN_SUB_ROWS, o_ref.shape[1]), o_ref.dtype)


def _ffn(block_e, valid, xs, layer, w_gu, b_gu, w_down, b_down, bm):
    n_layers, n_exp, d, f2 = w_gu.shape
    f = f2 // 2
    n_blocks = block_e.shape[0]
    expert = lambda i, be, nu: (layer, be[i], 0, 0)
    grid_spec = pltpu.PrefetchScalarGridSpec(
        num_scalar_prefetch=2,
        grid=(n_blocks,),
        in_specs=[pl.BlockSpec((bm, d // 2), lambda i, be, nu: (i, 0)),
                  pl.BlockSpec((1, 1, d, f2), expert),
                  pl.BlockSpec((1, 1, 1, f2), expert),
                  pl.BlockSpec((1, 1, f, d), expert),
                  pl.BlockSpec((1, 1, 1, d), expert)],
        out_specs=pl.BlockSpec((bm, d // 2), lambda i, be, nu: (i, 0)),
        scratch_shapes=[pltpu.VMEM((d, f2), BF16), pltpu.VMEM((f, d), BF16)],
    )
    return pl.pallas_call(
        _ffn_kernel,
        grid_spec=grid_spec,
        out_shape=jax.ShapeDtypeStruct((n_blocks * bm, d // 2), jnp.uint32),
        compiler_params=_params("arbitrary"),
        name="ffn",
    )(block_e, valid, xs, w_gu, b_gu.reshape(n_layers, n_exp, 1, f2),
      w_down, b_down.reshape(n_layers, n_exp, 1, d))


def _combine_kernel(x_ref, *refs, alpha, split):
    y_refs, (wt_ref, g_ref, b_ref), out_refs = refs[:TOP_K], refs[TOP_K:TOP_K + 3], refs[TOP_K + 3:]
    wt = wt_ref[...]
    ffn = sum(wt[:, j:j + 1] * _unpack_bf16_pairs(y_refs[j][...]) for j in range(TOP_K))
    res = _layernorm(alpha * x_ref[...] + ffn, g_ref[...], b_ref[...])
    if not split:
        out_refs[0][...] = res
    else:
        is_tail = pl.program_id(0) == pl.num_programs(0) - 1

        @pl.when(jnp.logical_not(is_tail))
        def _():
            out_refs[0][...] = res

        @pl.when(is_tail)
        def _():
            out_refs[1][...] = res


def _combine(x1, y, wt, g, b, tm, alpha, split=False):
    t, d = x1.shape
    n_tiles = t // tm
    planes = [pl.BlockSpec((tm, d // 2), lambda i, j=j: (j * n_tiles + i, 0)) for j in range(TOP_K)]
    if split:
        out_specs = [pl.BlockSpec((tm, d), lambda i: (jnp.minimum(i, n_tiles - 2), 0)), _const_spec((tm, d))]
        out_shape = [jax.ShapeDtypeStruct((t - tm, d), F32), jax.ShapeDtypeStruct((tm, d), F32)]
    else:
        out_specs = pl.BlockSpec((tm, d), lambda i: (i, 0))
        out_shape = jax.ShapeDtypeStruct((t, d), F32)
    return pl.pallas_call(
        functools.partial(_combine_kernel, alpha=alpha, split=split),
        grid=(n_tiles,),
        in_specs=[pl.BlockSpec((tm, d), lambda i: (i, 0))] + planes +
                 [pl.BlockSpec((tm, LANES), lambda i: (i, 0)), _const_spec(g.shape), _const_spec(b.shape)],
        out_specs=out_specs,
        out_shape=out_shape,
        compiler_params=_params("arbitrary"),
        name="combine",
    )(x1, *([y] * TOP_K), wt, g, b)


def _route(experts, ranks, counts, bm):
    n_experts = counts.shape[0]
    i32 = jnp.int32
    padded = (counts + bm - 1) // bm * bm
    ends_pad = jnp.cumsum(padded)
    start_pad = ends_pad - padded
    n_blocks = -(-experts.size // bm) + n_experts
    first_row = jnp.arange(n_blocks, dtype=i32)[:, None] * bm
    block_e = jnp.minimum(jnp.sum(ends_pad[None, :] <= first_row, axis=1), n_experts - 1).astype(i32)
    valid = jnp.clip(counts[block_e] - (first_row[:, 0] - start_pad[block_e]), 0, bm).astype(i32)
    onehot = experts[..., None] == jnp.arange(n_experts, dtype=i32)
    pos = jnp.sum(jnp.where(onehot, start_pad, 0), axis=-1).astype(i32) + ranks
    return block_e, valid, pos


def _moe(x1, x1p, routing, counts, top_w_pad, w, layer, moe_weights, alpha, tm, n_real, split):
    n_tok = x1.shape[0]
    n_experts = moe_weights[0].shape[1]
    block_e, valid, pos = _route(routing[:n_real, :TOP_K], routing[:n_real, TOP_K:2 * TOP_K],
                                 counts[0, :n_experts].astype(jnp.int32), MOE_ROWS)
    n_rows = block_e.shape[0] * MOE_ROWS
    n_fill = n_tok - n_real
    spare = n_rows + jnp.arange(n_fill * TOP_K, dtype=jnp.int32).reshape(n_fill, TOP_K)
    xs = _sc_scatter(x1p, jnp.concatenate([pos, spare]).T.reshape(-1), n_rows + n_fill * TOP_K)
    out_rows = _ffn(block_e, valid, xs, layer, *moe_weights, MOE_ROWS)
    y = _sc_gather(out_rows, jnp.concatenate([pos, spare - n_rows]).T.reshape(-1))
    return _combine(x1, y, top_w_pad, w["ln2g"], w["ln2b"], tm, alpha, split)


def _hi_lo(w):
    hi = w.astype(BF16)
    return jnp.concatenate([hi, (w - hi.astype(F32)).astype(BF16)], axis=1)


def _layer_weights(layer, w_in, gla_w_a2, gla_b_a, gla_norm_g, rg_conv_w, rg_conv_b, rg_w_a, rg_b_a, rg_w_x,
                   rg_b_x, rg_lambda, b_merge, w_branch, w_o, ln1_g, ln1_b, ln2_g, ln2_b, router_w, router_b,
                   kw, vw):
    d = w_in.shape[1]
    rank = gla_w_a2.shape[1]
    width = rg_conv_w.shape[2]
    n_exp = router_w.shape[2]
    c0 = 2 * kw + 2 * vw
    wi = w_in[layer]
    row = lambda v: v.reshape(1, -1)
    return {
        "wq": wi[:, :c0].astype(BF16),
        "wal": jnp.pad(wi[:, c0:c0 + rank], ((0, 0), (0, LANES - rank))).astype(BF16),
        "wa2": jnp.pad(gla_w_a2[layer], ((0, LANES - rank), (0, 0))).astype(BF16),
        "ba": row(gla_b_a[layer]),
        "wxr": wi[:, c0 + rank:c0 + rank + width].astype(BF16),
        "wmg": wi[:, c0 + rank + width:].astype(BF16),
        "bmg": row(b_merge[layer]),
        "ng": row(gla_norm_g[layer]),
        "cw": rg_conv_w[layer], "cb": row(rg_conv_b[layer]),
        "wax": jnp.concatenate([rg_w_a[layer], rg_w_x[layer]], axis=-1).astype(BF16),
        "rba": row(rg_b_a[layer]), "rbx": row(rg_b_x[layer]), "lam": row(rg_lambda[layer]),
        "wb0": w_branch[layer, 0].astype(BF16), "wb1": w_branch[layer, 1].astype(BF16),
        "wo": w_o[layer].astype(BF16),
        "ln1g": row(ln1_g[layer]), "ln1b": row(ln1_b[layer]),
        "ln2g": row(ln2_g[layer]), "ln2b": row(ln2_b[layer]),
        "rw": _hi_lo(jnp.pad(router_w[layer], ((0, 0), (0, LANES - n_exp)))),
        "rb": jnp.pad(row(router_b[layer]), ((0, 0), (0, LANES - n_exp))),
    }


def kernel(x_prompt, x_sample, state_gla, state_rglru, state_conv, ln1_g, ln1_b, w_in, gla_w_a2, gla_b_a, gla_norm_g, rg_conv_w, rg_conv_b, rg_w_a, rg_b_a, rg_w_x, rg_b_x, rg_lambda, b_merge, w_branch, w_o, ln2_g, ln2_b, router_w, router_b, moe_w_gu, moe_b_gu, moe_w_down, moe_b_down):
    n_p, seq, d = x_prompt.shape
    n_s, dec_seq, _ = x_sample.shape
    assert dec_seq == 1, "the sample group carries one new token per sequence"
    depth, _, heads, dk, dv = state_gla.shape
    kw, vw = heads * dk, heads * dv
    n_exp = router_w.shape[2]
    alpha = (2.0 * depth) ** 0.25
    t_p = n_p * seq
    pad_rows = BF16_SUBLANES

    tile = PROJ_ROWS
    assert t_p % tile == 0 and n_s <= tile
    n_head_tiles = t_p // tile
    tail_pad = lambda a: jnp.pad(a, ((0, tile - n_s), (0, 0)))
    x_head, x_tail, tail_tile = x_prompt.reshape(t_p, d), tail_pad(x_sample.reshape(n_s, d)), 0
    moe_weights = (moe_w_gu, moe_b_gu, moe_w_down, moe_b_down)
    outs = {k: [] for k in ("gla_p", "rg_p", "cv_p", "rg_s", "cv_s")}
    gla_s = None
    for layer in range(depth):
        w = _layer_weights(layer, w_in, gla_w_a2, gla_b_a, gla_norm_g, rg_conv_w, rg_conv_b, rg_w_a, rg_b_a,
                           rg_w_x, rg_b_x, rg_lambda, b_merge, w_branch, w_o, ln1_g, ln1_b, ln2_g, ln2_b,
                           router_w, router_b, kw, vw)
        qkvg, loga, xr, gate = _proj(x_head, x_tail, n_head_tiles, tail_tile, w, tile)
        o_p, s_p = _gla(qkvg, loga, w["ng"], None, layer, n_p, seq, SEQ_ROWS, GLA_CHUNK, heads, dk, dv)
        h_p, hl_p, cv_p = _rglru_seq(xr, w, n_p, seq, SEQ_ROWS)
        pad = lambda a: jnp.pad(a[t_p:t_p + n_s, None, :],
                                ((0, 0), (0, pad_rows - 1), (0, 0))).reshape(n_s * pad_rows, -1)
        o_s, gla_s = _gla(pad(qkvg), pad(loga), w["ng"], state_gla, layer, n_s, pad_rows,
                          DECODE_SEQS * pad_rows, pad_rows, heads, dk, dv, stack=(depth, gla_s))
        o_s = o_s.reshape(n_s, pad_rows, vw)[:, 0]
        h_s, hn_s, cv_s = _rglru_step(xr[t_p:t_p + n_s], state_conv[layer], state_rglru[layer], w)
        x1, x1p, routing, wt, cnt = _post(x_head, x_tail, n_head_tiles, tail_tile, o_p, h_p, tail_pad(o_s),
                                          tail_pad(h_s), gate, w, tile, alpha, n_exp, n_s)
        last = layer == depth - 1
        x = _moe(x1, x1p, routing, cnt, wt, w, layer, moe_weights, alpha, tile, t_p + n_s, split=last)
        if not last:
            x_head, x_tail, tail_tile = x, x, n_head_tiles
        outs["gla_p"].append(s_p)
        outs["rg_p"].append(hl_p.reshape(n_p, -1))
        outs["cv_p"].append(cv_p)
        outs["rg_s"].append(hn_s)
        outs["cv_s"].append(cv_s)
    y_head, y_tail = x
    return (y_head.reshape(n_p, seq, d), y_tail[:n_s].reshape(n_s, dec_seq, d),
            jnp.stack(outs["gla_p"]), jnp.stack(outs["rg_p"]), jnp.stack(outs["cv_p"]),
            gla_s, jnp.stack(outs["rg_s"]), jnp.stack(outs["cv_s"]))
```

```python
import functools

import jax
import jax.numpy as jnp
from jax import lax
from jax.experimental import pallas as pl
from jax.experimental.pallas import tpu as pltpu
from jax.experimental.pallas import tpu_sc as plsc

F32 = jnp.float32
BF16 = jnp.bfloat16

TOP_K = 4
GLA_GATE_TAU = 16.0
GLA_CHUNK = 64
RG_C = 8.0
SWIGLU_LIMIT = 7.0
SWIGLU_ALPHA = 1.702
LN_EPS = 1e-5

LANES = 128
BF16_SUBLANES = 16
VMEM_LIMIT = 56 * 1024 * 1024

PROJ_ROWS = 512
SEQ_ROWS = 256
DECODE_SEQS = 8
MOE_ROWS = 1024
FFN_COLS = 512
FFN_SUB_ROWS = 256


def _params(*sem):
    return pltpu.CompilerParams(dimension_semantics=sem, vmem_limit_bytes=VMEM_LIMIT)


def _const_spec(shape):
    nd = len(shape)
    return pl.BlockSpec(shape, lambda *_: (0,) * nd)


def _weight_spec(shape):
    nd = len(shape)
    return pl.BlockSpec(shape, lambda *_: (0,) * nd, pipeline_mode=pl.Buffered(1))


def _bdot(a, b):
    return jnp.dot(a, b, preferred_element_type=F32)


def _split3(x):
    hi = x.astype(BF16)
    r1 = x - hi.astype(F32)
    mid = r1.astype(BF16)
    lo = (r1 - mid.astype(F32)).astype(BF16)
    return hi, mid, lo


def _sigmoid(x):
    return 0.5 * jnp.tanh(0.5 * x) + 0.5


def _log_sigmoid(x):
    return jnp.minimum(x, 0.0) - jnp.log1p(jnp.exp(-jnp.abs(x)))


def _softplus(x):
    return jnp.maximum(x, 0.0) + jnp.log1p(jnp.exp(-jnp.abs(x)))


def _layernorm(x, g, b):
    mu = jnp.mean(x, axis=-1, keepdims=True)
    xc = x - mu
    var = jnp.mean(xc * xc, axis=-1, keepdims=True)
    return xc * lax.rsqrt(var + LN_EPS) * g + b


def _head_tail_specs(n_head_tiles, tail_tile, tm, d):
    return [pl.BlockSpec((tm, d), lambda i: (jnp.minimum(i, n_head_tiles - 1), 0)),
            pl.BlockSpec((tm, d), lambda i: (tail_tile, 0))]


def _head_or_tail(head_ref, tail_ref):
    is_tail = pl.program_id(0) == pl.num_programs(0) - 1
    return jnp.where(is_tail, tail_ref[...], head_ref[...])


def _token_inputs(x_head, x_tail, n_head_tiles, tail_tile, tm):
    d = x_head.shape[1]
    if x_tail is x_head:
        assert tail_tile == n_head_tiles
        return [x_head], [pl.BlockSpec((tm, d), lambda i: (i, 0))]
    return [x_head, x_tail], _head_tail_specs(n_head_tiles, tail_tile, tm, d)


def _token_tile(x_refs):
    return x_refs[0][...] if len(x_refs) == 1 else _head_or_tail(*x_refs)


def _proj_kernel(*refs, n_x):
    x_refs = refs[:n_x]
    wq_ref, wal_ref, wa2_ref, ba_ref, wxr_ref, wmg_ref, bmg_ref, qkvg_ref, loga_ref, xr_ref, gate_ref = refs[n_x:]
    xb = _token_tile(x_refs).astype(BF16)
    d = xb.shape[1]
    for c in range(0, qkvg_ref.shape[1], d):
        qkvg_ref[:, c:c + d] = _bdot(xb, wq_ref[:, c:c + d]).astype(BF16)
    a_low = _bdot(xb, wal_ref[...])
    z = _bdot(a_low.astype(BF16), wa2_ref[...]) + ba_ref[...]
    loga_ref[...] = _log_sigmoid(z) * (1.0 / GLA_GATE_TAU)
    xr_ref[...] = _bdot(xb, wxr_ref[...])
    for c in range(0, gate_ref.shape[1], d):
        gate_ref[:, c:c + d] = _sigmoid(_bdot(xb, wmg_ref[:, c:c + d]) + bmg_ref[:, c:c + d]).astype(BF16)


def _proj(x_head, x_tail, n_head_tiles, tail_tile, w, tm):
    d = x_head.shape[1]
    t = (n_head_tiles + 1) * tm
    kw = w["wq"].shape[1]
    nk = w["wa2"].shape[1]
    consts = [w["wq"], w["wal"], w["wa2"], w["ba"], w["wxr"], w["wmg"], w["bmg"]]
    xs, x_specs = _token_inputs(x_head, x_tail, n_head_tiles, tail_tile, tm)
    return pl.pallas_call(
        functools.partial(_proj_kernel, n_x=len(xs)),
        grid=(t // tm,),
        in_specs=x_specs + [_weight_spec(c.shape) for c in consts],
        out_specs=[pl.BlockSpec((tm, kw), lambda i: (i, 0)),
                   pl.BlockSpec((tm, nk), lambda i: (i, 0)),
                   pl.BlockSpec((tm, d), lambda i: (i, 0)),
                   pl.BlockSpec((tm, 2 * d), lambda i: (i, 0))],
        out_shape=[jax.ShapeDtypeStruct((t, kw), BF16),
                   jax.ShapeDtypeStruct((t, nk), F32),
                   jax.ShapeDtypeStruct((t, d), F32),
                   jax.ShapeDtypeStruct((t, 2 * d), BF16)],
        compiler_params=_params("arbitrary"),
        name="proj",
    )(*xs, *consts)


def _gla_kernel(*refs, chunk, n_chunks, heads, has_state, n_inputs, state_slices, state_slot):
    q_ref, k_ref, v_ref, la_ref, g_ref, ng_ref = refs[:6]
    s0_ref = refs[6] if has_state else None
    o_ref, so_ref, s_scr, o_scr, p_scr, kv_scr = refs[n_inputs:]
    t = pl.program_id(1)
    dk = q_ref.shape[1] // heads
    dv = v_ref.shape[1] // heads
    scale = dk ** -0.5
    chunks_per_seq = n_chunks // s_scr.shape[0]

    @pl.when(t == 0)
    def _():
        if has_state:
            s_scr[...] = s0_ref[0]
        else:
            s_scr[...] = jnp.zeros_like(s_scr)

    n_rows = n_chunks * chunk
    r = lax.broadcasted_iota(jnp.int32, (n_rows, n_rows), 0)
    c = lax.broadcasted_iota(jnp.int32, (n_rows, n_rows), 1)
    shift = chunk.bit_length() - 1
    assert 1 << shift == chunk, "chunk must be a power of two"
    tril = jnp.where(jnp.logical_and(r >> shift == c >> shift, r >= c), 1.0, 0.0).astype(BF16)
    la_hi, la_mid, la_lo = _split3(la_ref[...])
    b_all = _bdot(tril, la_hi) + _bdot(tril, la_mid) + _bdot(tril, la_lo)
    q_all = (q_ref[...].astype(F32) * scale * jnp.exp(b_all)).astype(BF16)
    k_all = (k_ref[...].astype(F32) * jnp.exp(-b_all)).astype(BF16)
    causal = (lax.broadcasted_iota(jnp.int32, (chunk, chunk), 0) >=
              lax.broadcasted_iota(jnp.int32, (chunk, chunk), 1))

    decays = []
    for ci in range(n_chunks):
        rows = slice(ci * chunk, (ci + 1) * chunk)
        b = b_all[rows, :]
        b_last = b[chunk - 1:chunk, :]
        q_e, k_e = q_all[rows, :], k_all[rows, :]
        k_d = (k_ref[rows, :].astype(F32) * jnp.exp(b_last - b)).astype(BF16)
        decays.append(jnp.exp(b_last))
        for h in range(heads):
            ks = slice(h * dk, (h + 1) * dk)
            v_h = v_ref[rows, h * dv:(h + 1) * dv]
            scores = lax.dot_general(q_e[:, ks], k_e[:, ks], (((1,), (1,)), ((), ())),
                                     preferred_element_type=F32)
            p_scr[rows, h * chunk:(h + 1) * chunk] = jnp.where(causal, scores, 0.0).astype(BF16)
            kv_scr[ci, h] = lax.dot_general(k_d[:, ks], v_h, (((0,), (0,)), ((), ())),
                                            preferred_element_type=F32)

    for ci in range(n_chunks):
        rows = slice(ci * chunk, (ci + 1) * chunk)
        seq = ci // chunks_per_seq
        for h in range(heads):
            ks = slice(h * dk, (h + 1) * dk)
            vs = slice(h * dv, (h + 1) * dv)
            s_old = s_scr[seq, h]
            o_scr[rows, vs] = (_bdot(p_scr[rows, h * chunk:(h + 1) * chunk], v_ref[rows, vs]) +
                               _bdot(q_all[rows, ks], s_old.astype(BF16)))
            s_scr[seq, h] = jnp.transpose(decays[ci][:, ks]) * s_old + kv_scr[ci, h]

    for h in range(heads):
        vs = slice(h * dv, (h + 1) * dv)
        o = o_scr[:, vs]
        mu = jnp.mean(o, axis=-1, keepdims=True)
        oc = o - mu
        var = jnp.mean(oc * oc, axis=-1, keepdims=True)
        on = oc * lax.rsqrt(var + LN_EPS) * ng_ref[:, vs]
        g = g_ref[:, vs].astype(F32)
        o_ref[:, vs] = (on * (g * _sigmoid(g))).astype(BF16)

    @pl.when(t == pl.num_programs(1) - 1)
    def _():
        if state_slices == 0:
            so_ref[...] = s_scr[...]
        for layer_slice in range(state_slices):
            so_ref[layer_slice] = s_scr[...] if layer_slice == state_slot else jnp.zeros_like(s_scr)


def _gla(qkvg, loga, norm_g, s0, layer, nb, s_len, rows, chunk, heads, dk, dv, stack=None):
    t = nb * s_len
    seqs = max(1, rows // s_len)
    assert s_len * seqs % rows == 0 and nb % seqs == 0 and s_len % chunk == 0
    nb, tps = nb // seqs, s_len * seqs // rows
    kw, vw = heads * dk, heads * dv
    has_state = s0 is not None
    state_shape, state_block, state_index = (nb * seqs, heads, dk, dv), (seqs, heads, dk, dv), lambda b, i: (b, 0, 0, 0)
    state_slices, state_slot, aliases = 0, 0, {}
    if stack is not None:
        depth, prev = stack
        state_shape = (depth,) + state_shape
        if prev is None:
            state_slices, state_slot = depth, layer
            state_block, state_index = (depth,) + state_block, lambda b, i: (0, b, 0, 0, 0)
        else:
            state_slices = 1
            state_block, state_index = (1,) + state_block, lambda b, i: (layer, b, 0, 0, 0)
    in_specs = [pl.BlockSpec((rows, kw), lambda b, i: (b * tps + i, 0)),
                pl.BlockSpec((rows, kw), lambda b, i: (b * tps + i, 1)),
                pl.BlockSpec((rows, vw), lambda b, i: (b * tps + i, (2 * kw) // vw)),
                pl.BlockSpec((rows, kw), lambda b, i: (b * tps + i, 0)),
                pl.BlockSpec((rows, vw), lambda b, i: (b * tps + i, (2 * kw) // vw + 1)),
                _const_spec(norm_g.shape)]
    args = [qkvg, qkvg, qkvg, loga, qkvg, norm_g]
    if has_state:
        in_specs.append(pl.BlockSpec((1, seqs, heads, dk, dv), lambda b, i: (layer, b, 0, 0, 0)))
        args.append(s0)
    if stack is not None and stack[1] is not None:
        in_specs.append(pl.BlockSpec(memory_space=pl.ANY))
        aliases[len(args)] = 1
        args.append(stack[1])
    return pl.pallas_call(
        functools.partial(_gla_kernel, chunk=chunk, n_chunks=rows // chunk, heads=heads, has_state=has_state,
                          n_inputs=len(args), state_slices=state_slices, state_slot=state_slot),
        grid=(nb, tps),
        in_specs=in_specs,
        out_specs=[pl.BlockSpec((rows, vw), lambda b, i: (b * tps + i, 0)),
                   pl.BlockSpec(state_block, state_index)],
        out_shape=[jax.ShapeDtypeStruct((t, vw), BF16), jax.ShapeDtypeStruct(state_shape, F32)],
        scratch_shapes=[pltpu.VMEM((seqs, heads, dk, dv), F32), pltpu.VMEM((rows, vw), F32),
                        pltpu.VMEM((rows, heads * chunk), BF16), pltpu.VMEM((rows // chunk, heads, dk, dv), F32)],
        input_output_aliases=aliases,
        compiler_params=_params("arbitrary", "arbitrary"),
        name="gla",
    )(*args)


def _rg_gates(xc, wax_ref, ba_ref, bx_ref, lam_ref):
    n_blocks, bw, _ = wax_ref.shape
    r_parts, i_parts = [], []
    for n in range(n_blocks):
        cs = slice(n * bw, (n + 1) * bw)
        pre = _bdot(xc[:, cs].astype(BF16), wax_ref[n])
        r_parts.append(_sigmoid(pre[:, :bw] + ba_ref[:, cs]))
        i_parts.append(_sigmoid(pre[:, bw:] + bx_ref[:, cs]))
    r = jnp.concatenate(r_parts, axis=1)
    i = jnp.concatenate(i_parts, axis=1)
    log_a = r * (-RG_C * _softplus(-lam_ref[...]))
    a = jnp.exp(log_a)
    bx = jnp.sqrt(jnp.tanh(-log_a) * (1.0 + a * a)) * (i * xc)
    return a, bx


def _rglru_seq_kernel(xr_ref, cw_ref, cb_ref, wax_ref, ba_ref, bx_ref, lam_ref,
                      h_ref, hl_ref, cv_ref, cbuf, a_scr, b_scr, h_scr, hc):
    t = pl.program_id(1)
    rows = xr_ref.shape[0]
    taps = cw_ref.shape[0]
    head = cbuf.shape[0]
    assert taps - 1 <= head

    @pl.when(t == 0)
    def _():
        cbuf[...] = jnp.zeros_like(cbuf)
        hc[...] = jnp.zeros_like(hc)

    x = xr_ref[...]
    prev = cbuf[...]
    in_head = lax.broadcasted_iota(jnp.int32, prev.shape, 0)
    xc = cb_ref[...] + x * cw_ref[taps - 1:taps, :]
    for back in range(1, taps):
        rolled = pltpu.roll(x, back, 0)
        first_rows = jnp.where(in_head < back, pltpu.roll(prev, back, 0), rolled[:head])
        xc = xc + jnp.concatenate([first_rows, rolled[head:]], axis=0) * cw_ref[taps - 1 - back:taps - back, :]
    cbuf[...] = x[rows - head:, :]
    tail = x[rows - (taps - 1):, :]

    a, bx = _rg_gates(xc, wax_ref, ba_ref, bx_ref, lam_ref)
    a_scr[...] = a
    b_scr[...] = bx

    def step(i, h):
        h = a_scr[pl.ds(i, 1), :] * h + b_scr[pl.ds(i, 1), :]
        h_scr[pl.ds(i, 1), :] = h
        return h

    h_last = lax.fori_loop(0, rows, step, hc[...], unroll=16)
    hc[...] = h_last
    h_ref[...] = h_scr[...].astype(BF16)

    @pl.when(t == pl.num_programs(1) - 1)
    def _():
        hl_ref[0] = h_last
        cv_ref[0] = tail


def _rglru_seq(xr, w, nb, s_len, rows):
    width = xr.shape[1]
    t = nb * s_len
    tps = s_len // rows
    taps = w["cw"].shape[0]
    consts = [w["cw"], w["cb"], w["wax"], w["rba"], w["rbx"], w["lam"]]
    return pl.pallas_call(
        _rglru_seq_kernel,
        grid=(nb, tps),
        in_specs=[pl.BlockSpec((rows, width), lambda b, i: (b * tps + i, 0))] + [_const_spec(c.shape) for c in consts],
        out_specs=[pl.BlockSpec((rows, width), lambda b, i: (b * tps + i, 0)),
                   pl.BlockSpec((1, 1, width), lambda b, i: (b, 0, 0)),
                   pl.BlockSpec((1, taps - 1, width), lambda b, i: (b, 0, 0))],
        out_shape=[jax.ShapeDtypeStruct((t, width), BF16),
                   jax.ShapeDtypeStruct((nb, 1, width), F32),
                   jax.ShapeDtypeStruct((nb, taps - 1, width), F32)],
        scratch_shapes=[pltpu.VMEM((8, width), F32), pltpu.VMEM((rows, width), F32),
                        pltpu.VMEM((rows, width), F32), pltpu.VMEM((rows, width), F32),
                        pltpu.VMEM((1, width), F32)],
        compiler_params=_params("arbitrary", "arbitrary"),
        name="rglru_seq",
    )(xr, *consts)


def _rglru_step_kernel(xr_ref, sc_ref, h0_ref, cw_ref, cb_ref, wax_ref, ba_ref, bx_ref, lam_ref,
                       h_ref, hn_ref, cv_ref):
    taps = cw_ref.shape[0]
    xr = xr_ref[...]
    xc = (cb_ref[...] + sum(sc_ref[:, j, :] * cw_ref[j:j + 1, :] for j in range(taps - 1)) +
          xr * cw_ref[taps - 1:taps, :])
    a, bx = _rg_gates(xc, wax_ref, ba_ref, bx_ref, lam_ref)
    h = a * h0_ref[...] + bx
    h_ref[...] = h.astype(BF16)
    hn_ref[...] = h
    for j in range(taps - 2):
        cv_ref[:, j, :] = sc_ref[:, j + 1, :]
    cv_ref[:, taps - 2, :] = xr


def _rglru_step(xr, conv, h0, w):
    n, width = xr.shape
    consts = [w["cw"], w["cb"], w["wax"], w["rba"], w["rbx"], w["lam"]]
    args = [xr, conv, h0] + consts
    return pl.pallas_call(
        _rglru_step_kernel,
        grid=(1,),
        in_specs=[_const_spec(a.shape) for a in args],
        out_specs=[_const_spec((n, width)), _const_spec((n, width)), _const_spec(conv.shape)],
        out_shape=[jax.ShapeDtypeStruct((n, width), BF16),
                   jax.ShapeDtypeStruct((n, width), F32),
                   jax.ShapeDtypeStruct(conv.shape, F32)],
        compiler_params=_params("arbitrary"),
        name="rglru_step",
    )(*args)


def _pack_bf16_pairs(x):
    half = x.shape[1] // 2
    hi = pltpu.bitcast(x[:, :half].astype(BF16).astype(F32), jnp.uint32)
    lo = pltpu.bitcast(x[:, half:].astype(BF16).astype(F32), jnp.uint32)
    return hi | (lo >> 16)


def _unpack_bf16_pairs(p):
    hi = pltpu.bitcast(p & jnp.uint32(0xFFFF0000), F32)
    lo = pltpu.bitcast(p << 16, F32)
    return jnp.concatenate([hi, lo], axis=1)


def _post_kernel(*refs, n_x, alpha, n_experts, n_tail_real):
    x_refs = refs[:n_x]
    (o_ref, ot_ref, h_ref, ht_ref, gate_ref, wb0_ref, wb1_ref, wo_ref, g1_ref, b1_ref, rw_ref, rb_ref,
     x1_ref, x1p_ref, idx_ref, wt_ref, cnt_ref, cnt_scr) = refs[n_x:]
    tm, d = o_ref.shape

    @pl.when(pl.program_id(0) == 0)
    def _():
        cnt_scr[...] = jnp.zeros_like(cnt_scr)

    x = _token_tile(x_refs)
    o = _head_or_tail(o_ref, ot_ref)
    h = _head_or_tail(h_ref, ht_ref)
    gate = gate_ref[...].astype(F32)
    merged = gate[:, :d] * _bdot(o, wb0_ref[...]) + gate[:, d:] * _bdot(h, wb1_ref[...])
    mix = _bdot(merged.astype(BF16), wo_ref[...])
    x1 = _layernorm(alpha * x + mix, g1_ref[...], b1_ref[...])
    x1_ref[...] = x1
    x1p_ref[...] = _pack_bf16_pairs(x1)

    xh = x1.astype(BF16)
    xl = (x1 - xh.astype(F32)).astype(BF16)
    n_lanes = rw_ref.shape[1] // 2
    hi_terms = _bdot(xh, rw_ref[...])
    logits = (hi_terms[:, n_lanes:] + _bdot(xl, rw_ref[:, :n_lanes])) + hi_terms[:, :n_lanes]
    logits = logits + rb_ref[...]
    lane = lax.broadcasted_iota(jnp.int32, logits.shape, 1)
    lane_f = lane.astype(F32)
    neg_inf = jnp.float32(-jnp.inf)
    cur = jnp.where(lane < n_experts, logits, neg_inf)
    vals, idxs = [], []
    for _ in range(TOP_K):
        m = jnp.max(cur, axis=-1, keepdims=True)
        sel = jnp.min(jnp.where(cur == m, lane_f, float(LANES)), axis=-1, keepdims=True)
        vals.append(m)
        idxs.append(sel)
        cur = jnp.where(lane_f == sel, neg_inf, cur)
    exps = [jnp.exp(v - vals[0]) for v in vals]
    total = sum(exps)

    onehots = [jnp.where(lane_f == idxs[j], 1.0, 0.0) for j in range(TOP_K)]
    chosen = sum(onehots)
    if n_tail_real < tm:
        is_fill = jnp.logical_and(pl.program_id(0) == pl.num_programs(0) - 1,
                                  lax.broadcasted_iota(jnp.int32, chosen.shape, 0) >= n_tail_real)
        chosen = jnp.where(is_fill, 0.0, chosen)
    r = lax.broadcasted_iota(jnp.int32, (tm, tm), 0)
    c = lax.broadcasted_iota(jnp.int32, (tm, tm), 1)
    earlier = jnp.where(r > c, 1.0, 0.0).astype(BF16)
    before = _bdot(earlier, chosen.astype(BF16)) + cnt_scr[...]
    ranks = [jnp.sum(onehots[j] * before, axis=-1, keepdims=True) for j in range(TOP_K)]
    cnt_scr[...] += jnp.sum(chosen, axis=0, keepdims=True)
    cnt_ref[...] = cnt_scr[...]

    idx_out = jnp.zeros(logits.shape, F32)
    wt_out = jnp.zeros(logits.shape, F32)
    for j in range(TOP_K):
        idx_out = jnp.where(lane == j, idxs[j], idx_out)
        idx_out = jnp.where(lane == TOP_K + j, ranks[j], idx_out)
        wt_out = jnp.where(lane == j, exps[j] / total, wt_out)
    idx_ref[...] = idx_out.astype(jnp.int32)
    wt_ref[...] = wt_out


def _post(x_head, x_tail, n_head_tiles, tail_tile, o, h, o_tail, h_tail, gate, w, tm, alpha, n_experts,
          n_tail_real):
    d = x_head.shape[1]
    t = (n_head_tiles + 1) * tm
    assert o.shape[0] == t - tm and o_tail.shape[0] == tm and gate.shape[0] == t
    consts = [w["wb0"], w["wb1"], w["wo"], w["ln1g"], w["ln1b"], w["rw"], w["rb"]]
    row = lambda width: pl.BlockSpec((tm, width), lambda i: (i, 0))
    xs, x_specs = _token_inputs(x_head, x_tail, n_head_tiles, tail_tile, tm)
    return pl.pallas_call(
        functools.partial(_post_kernel, n_x=len(xs), alpha=alpha, n_experts=n_experts, n_tail_real=n_tail_real),
        grid=(t // tm,),
        in_specs=x_specs + _head_tail_specs(n_head_tiles, 0, tm, d) + _head_tail_specs(n_head_tiles, 0, tm, d) +
                 [row(2 * d)] + [_weight_spec(c.shape) for c in consts],
        out_specs=[row(d), row(d // 2), row(LANES), row(LANES), _const_spec((1, LANES))],
        out_shape=[jax.ShapeDtypeStruct((t, d), F32),
                   jax.ShapeDtypeStruct((t, d // 2), jnp.uint32),
                   jax.ShapeDtypeStruct((t, LANES), jnp.int32),
                   jax.ShapeDtypeStruct((t, LANES), F32),
                   jax.ShapeDtypeStruct((1, LANES), F32)],
        scratch_shapes=[pltpu.VMEM((1, LANES), F32)],
        compiler_params=_params("arbitrary"),
        name="post",
    )(*xs, o, o_tail, h, h_tail, gate, *consts)


SC_INDEX_MAX = 128
SC_ALIGN = 8
SC_BUFFER_BYTES = 208 * 1024


def _sc_plan(n_rows, row_bytes):
    info = plsc.get_sparse_core_info()
    n_workers = info.num_cores * info.num_subcores
    per_worker, rem = divmod(n_rows, n_workers)
    assert rem == 0 and per_worker % SC_ALIGN == 0, "rows must split into aligned equal shares per subcore"
    cap = min(SC_INDEX_MAX, SC_BUFFER_BYTES // row_bytes)
    chunk = max(c for c in range(SC_ALIGN, cap + 1, SC_ALIGN) if per_worker % c == 0)
    return info.num_cores, n_workers, per_worker, chunk


def _sc_gather(table, idx):
    n_rows = idx.shape[0]
    d = table.shape[1]
    n_cores, _, per_worker, chunk = _sc_plan(n_rows, d * table.dtype.itemsize)
    mesh = plsc.VectorSubcoreMesh(core_axis_name="c", subcore_axis_name="s")

    n_chunks = per_worker // chunk

    @functools.partial(
        pl.kernel, mesh=mesh, out_type=jax.ShapeDtypeStruct((n_rows, d), table.dtype),
        scratch_types=[pltpu.VMEM((chunk,), jnp.int32)] * 2 + [pltpu.VMEM((chunk, d), table.dtype)] * 2 +
                      [pltpu.SemaphoreType.DMA] * 4)
    def gather(table_hbm, idx_hbm, out_hbm, idx0, idx1, rows0, rows1, g0, g1, w0, w1):
        idx_v, rows_v, gsem, wsem = (idx0, idx1), (rows0, rows1), (g0, g1), (w0, w1)
        worker = lax.axis_index("s") * n_cores + lax.axis_index("c")
        base = worker * per_worker

        def rows_of(c):
            return pl.ds(pl.multiple_of(base + c * chunk, SC_ALIGN), chunk)

        def start_gather(c):
            s = c % 2
            pltpu.sync_copy(idx_hbm.at[rows_of(c)], idx_v[s])
            return pltpu.async_copy(table_hbm.at[idx_v[s]], rows_v[s], gsem[s])

        gathers = {c: start_gather(c) for c in range(min(2, n_chunks))}
        writes = {}
        for c in range(n_chunks):
            s = c % 2
            gathers[c].wait()
            writes[c] = pltpu.async_copy(rows_v[s], out_hbm.at[rows_of(c)], wsem[s])
            if c + 2 < n_chunks:
                writes.pop(c).wait()
                gathers[c + 2] = start_gather(c + 2)
        for write in writes.values():
            write.wait()

    return gather(table, idx)


def _sc_scatter(x, pos, n_out):
    n, d = x.shape
    assert pos.shape == (TOP_K * n,)
    n_cores, _, per_worker, chunk = _sc_plan(n, d * x.dtype.itemsize)
    mesh = plsc.VectorSubcoreMesh(core_axis_name="c", subcore_axis_name="s")

    n_chunks = per_worker // chunk

    @functools.partial(
        pl.kernel, mesh=mesh, out_type=jax.ShapeDtypeStruct((n_out, d), x.dtype),
        scratch_types=[pltpu.VMEM((chunk, d), x.dtype)] * 2 + [pltpu.VMEM((chunk,), jnp.int32)] * TOP_K +
                      [pltpu.SemaphoreType.DMA] * 3)
    def scatter(x_hbm, pos_hbm, out_hbm, rows0, rows1, *rest):
        rows_v, idx_v, (l0, l1, ssem) = (rows0, rows1), rest[:TOP_K], rest[TOP_K:]
        lsem = (l0, l1)
        worker = lax.axis_index("s") * n_cores + lax.axis_index("c")
        base = worker * per_worker

        def first_row(c):
            return pl.multiple_of(base + c * chunk, SC_ALIGN)

        def start_load(c):
            return pltpu.async_copy(x_hbm.at[pl.ds(first_row(c), chunk)], rows_v[c % 2], lsem[c % 2])

        load = start_load(0)
        for c in range(n_chunks):
            next_load = start_load(c + 1) if c + 1 < n_chunks else None
            for k in range(TOP_K):
                pltpu.sync_copy(pos_hbm.at[pl.ds(k * n + first_row(c), chunk)], idx_v[k])
            load.wait()
            scatters = [pltpu.async_copy(rows_v[c % 2], out_hbm.at[idx_v[k]], ssem)
                        for k in range(TOP_K)]
            for scatter_k in scatters:
                scatter_k.wait()
            load = next_load

    return scatter(x, pos)


def _ffn_kernel(be_ref, valid_ref, x_ref, wgu_ref, bgu_ref, wd_ref, bd_ref, o_ref, wgu_bf, wd_bf):
    i = pl.program_id(0)
    bm = x_ref.shape[0]
    f = wd_ref.shape[2]
    valid = valid_ref[i]
    new_expert = jnp.logical_or(i == 0, be_ref[i] != be_ref[jnp.maximum(i - 1, 0)])

    @pl.when(new_expert)
    def _():
        wgu_bf[...] = wgu_ref[0, 0].astype(BF16)
        wd_bf[...] = wd_ref[0, 0].astype(BF16)

    def expert_rows(rows):
        x = _unpack_bf16_pairs(x_ref[rows, :]).astype(BF16)
        acc = jnp.zeros((x.shape[0], wd_ref.shape[3]), F32)
        for c in range(0, f, FFN_COLS):
            gate = _bdot(x, wgu_bf[:, c:c + FFN_COLS]) + bgu_ref[0, 0, :, c:c + FFN_COLS]
            up = _bdot(x, wgu_bf[:, f + c:f + c + FFN_COLS]) + bgu_ref[0, 0, :, f + c:f + c + FFN_COLS]
            gate = jnp.minimum(gate, SWIGLU_LIMIT)
            up = jnp.clip(up, -SWIGLU_LIMIT, SWIGLU_LIMIT)
            act = (up + 1.0) * gate * _sigmoid(SWIGLU_ALPHA * gate)
            acc = acc + _bdot(act.astype(BF16), wd_bf[c:c + FFN_COLS, :])
        o_ref[rows, :] = _pack_bf16_pairs(acc + bd_ref[0, 0])

    @pl.when(valid == bm)
    def _():
        expert_rows(slice(0, bm))

    @pl.when(valid < bm)
    def _():
        for s in range(0, bm, FFN_SUB_ROWS):
            rows = slice(s, s + FFN_SUB_ROWS)

            @pl.when(s < valid)
            def _():
                expert_rows(rows)

            @pl.when(s >= valid)
            def _():
                o_ref[rows, :] = jnp.zeros((FFN_SUB_ROWS, o_ref.shape[1]), o_ref.dtype)


def _ffn(block_e, valid, xs, layer, w_gu, b_gu, w_down, b_down, bm):
    n_layers, n_exp, d, f2 = w_gu.shape
    f = f2 // 2
    n_blocks = block_e.shape[0]
    expert = lambda i, be, nu: (layer, be[i], 0, 0)
    grid_spec = pltpu.PrefetchScalarGridSpec(
        num_scalar_prefetch=2,
        grid=(n_blocks,),
        in_specs=[pl.BlockSpec((bm, d // 2), lambda i, be, nu: (i, 0)),
                  pl.BlockSpec((1, 1, d, f2), expert),
                  pl.BlockSpec((1, 1, 1, f2), expert),
                  pl.BlockSpec((1, 1, f, d), expert),
                  pl.BlockSpec((1, 1, 1, d), expert)],
        out_specs=pl.BlockSpec((bm, d // 2), lambda i, be, nu: (i, 0)),
        scratch_shapes=[pltpu.VMEM((d, f2), BF16), pltpu.VMEM((f, d), BF16)],
    )
    return pl.pallas_call(
        _ffn_kernel,
        grid_spec=grid_spec,
        out_shape=jax.ShapeDtypeStruct((n_blocks * bm, d // 2), jnp.uint32),
        compiler_params=_params("arbitrary"),
        name="ffn",
    )(block_e, valid, xs, w_gu, b_gu.reshape(n_layers, n_exp, 1, f2),
      w_down, b_down.reshape(n_layers, n_exp, 1, d))


def _combine_kernel(x_ref, *refs, alpha, split):
    y_refs, (wt_ref, g_ref, b_ref), out_refs = refs[:TOP_K], refs[TOP_K:TOP_K + 3], refs[TOP_K + 3:]
    wt = wt_ref[...]
    ffn = sum(wt[:, j:j + 1] * _unpack_bf16_pairs(y_refs[j][...]) for j in range(TOP_K))
    res = _layernorm(alpha * x_ref[...] + ffn, g_ref[...], b_ref[...])
    if not split:
        out_refs[0][...] = res
    else:
        is_tail = pl.program_id(0) == pl.num_programs(0) - 1

        @pl.when(jnp.logical_not(is_tail))
        def _():
            out_refs[0][...] = res

        @pl.when(is_tail)
        def _():
            out_refs[1][...] = res


def _combine(x1, y, wt, g, b, tm, alpha, split=False):
    t, d = x1.shape
    n_tiles = t // tm
    planes = [pl.BlockSpec((tm, d // 2), lambda i, j=j: (j * n_tiles + i, 0)) for j in range(TOP_K)]
    if split:
        out_specs = [pl.BlockSpec((tm, d), lambda i: (jnp.minimum(i, n_tiles - 2), 0)), _const_spec((tm, d))]
        out_shape = [jax.ShapeDtypeStruct((t - tm, d), F32), jax.ShapeDtypeStruct((tm, d), F32)]
    else:
        out_specs = pl.BlockSpec((tm, d), lambda i: (i, 0))
        out_shape = jax.ShapeDtypeStruct((t, d), F32)
    return pl.pallas_call(
        functools.partial(_combine_kernel, alpha=alpha, split=split),
        grid=(n_tiles,),
        in_specs=[pl.BlockSpec((tm, d), lambda i: (i, 0))] + planes +
                 [pl.BlockSpec((tm, LANES), lambda i: (i, 0)), _const_spec(g.shape), _const_spec(b.shape)],
        out_specs=out_specs,
        out_shape=out_shape,
        compiler_params=_params("arbitrary"),
        name="combine",
    )(x1, *([y] * TOP_K), wt, g, b)


def _route(experts, ranks, counts, bm):
    n_experts = counts.shape[0]
    i32 = jnp.int32
    padded = (counts + bm - 1) // bm * bm
    ends_pad = jnp.cumsum(padded)
    start_pad = ends_pad - padded
    n_blocks = -(-experts.size // bm) + n_experts
    first_row = jnp.arange(n_blocks, dtype=i32)[:, None] * bm
    block_e = jnp.minimum(jnp.sum(ends_pad[None, :] <= first_row, axis=1), n_experts - 1).astype(i32)
    valid = jnp.clip(counts[block_e] - (first_row[:, 0] - start_pad[block_e]), 0, bm).astype(i32)
    onehot = experts[..., None] == jnp.arange(n_experts, dtype=i32)
    pos = jnp.sum(jnp.where(onehot, start_pad, 0), axis=-1).astype(i32) + ranks
    return block_e, valid, pos


def _moe(x1, x1p, routing, counts, top_w_pad, w, layer, moe_weights, alpha, tm, n_real, split):
    n_tok = x1.shape[0]
    n_experts = moe_weights[0].shape[1]
    block_e, valid, pos = _route(routing[:n_real, :TOP_K], routing[:n_real, TOP_K:2 * TOP_K],
                                 counts[0, :n_experts].astype(jnp.int32), MOE_ROWS)
    n_rows = block_e.shape[0] * MOE_ROWS
    n_fill = n_tok - n_real
    spare = n_rows + jnp.arange(n_fill * TOP_K, dtype=jnp.int32).reshape(n_fill, TOP_K)
    xs = _sc_scatter(x1p, jnp.concatenate([pos, spare]).T.reshape(-1), n_rows + n_fill * TOP_K)
    out_rows = _ffn(block_e, valid, xs, layer, *moe_weights, MOE_ROWS)
    y = _sc_gather(out_rows, jnp.concatenate([pos, spare - n_rows]).T.reshape(-1))
    return _combine(x1, y, top_w_pad, w["ln2g"], w["ln2b"], tm, alpha, split)


def _hi_lo(w):
    hi = w.astype(BF16)
    return jnp.concatenate([hi, (w - hi.astype(F32)).astype(BF16)], axis=1)


def _layer_weights(layer, w_in, gla_w_a2, gla_b_a, gla_norm_g, rg_conv_w, rg_conv_b, rg_w_a, rg_b_a, rg_w_x,
                   rg_b_x, rg_lambda, b_merge, w_branch, w_o, ln1_g, ln1_b, ln2_g, ln2_b, router_w, router_b,
                   kw, vw):
    d = w_in.shape[1]
    rank = gla_w_a2.shape[1]
    width = rg_conv_w.shape[2]
    n_exp = router_w.shape[2]
    c0 = 2 * kw + 2 * vw
    wi = w_in[layer]
    row = lambda v: v.reshape(1, -1)
    return {
        "wq": wi[:, :c0].astype(BF16),
        "wal": jnp.pad(wi[:, c0:c0 + rank], ((0, 0), (0, LANES - rank))).astype(BF16),
        "wa2": jnp.pad(gla_w_a2[layer], ((0, LANES - rank), (0, 0))).astype(BF16),
        "ba": row(gla_b_a[layer]),
        "wxr": wi[:, c0 + rank:c0 + rank + width].astype(BF16),
        "wmg": wi[:, c0 + rank + width:].astype(BF16),
        "bmg": row(b_merge[layer]),
        "ng": row(gla_norm_g[layer]),
        "cw": rg_conv_w[layer], "cb": row(rg_conv_b[layer]),
        "wax": jnp.concatenate([rg_w_a[layer], rg_w_x[layer]], axis=-1).astype(BF16),
        "rba": row(rg_b_a[layer]), "rbx": row(rg_b_x[layer]), "lam": row(rg_lambda[layer]),
        "wb0": w_branch[layer, 0].astype(BF16), "wb1": w_branch[layer, 1].astype(BF16),
        "wo": w_o[layer].astype(BF16),
        "ln1g": row(ln1_g[layer]), "ln1b": row(ln1_b[layer]),
        "ln2g": row(ln2_g[layer]), "ln2b": row(ln2_b[layer]),
        "rw": _hi_lo(jnp.pad(router_w[layer], ((0, 0), (0, LANES - n_exp)))),
        "rb": jnp.pad(row(router_b[layer]), ((0, 0), (0, LANES - n_exp))),
    }


def kernel(x_prompt, x_sample, state_gla, state_rglru, state_conv, ln1_g, ln1_b, w_in, gla_w_a2, gla_b_a, gla_norm_g, rg_conv_w, rg_conv_b, rg_w_a, rg_b_a, rg_w_x, rg_b_x, rg_lambda, b_merge, w_branch, w_o, ln2_g, ln2_b, router_w, router_b, moe_w_gu, moe_b_gu, moe_w_down, moe_b_down):
    n_p, seq, d = x_prompt.shape
    n_s, dec_seq, _ = x_sample.shape
    assert dec_seq == 1, "the sample group carries one new token per sequence"
    depth, _, heads, dk, dv = state_gla.shape
    kw, vw = heads * dk, heads * dv
    n_exp = router_w.shape[2]
    alpha = (2.0 * depth) ** 0.25
    t_p = n_p * seq
    pad_rows = BF16_SUBLANES

    tile = PROJ_ROWS
    assert t_p % tile == 0 and n_s <= tile
    n_head_tiles = t_p // tile
    tail_pad = lambda a: jnp.pad(a, ((0, tile - n_s), (0, 0)))
    x_head, x_tail, tail_tile = x_prompt.reshape(t_p, d), tail_pad(x_sample.reshape(n_s, d)), 0
    moe_weights = (moe_w_gu, moe_b_gu, moe_w_down, moe_b_down)
    outs = {k: [] for k in ("gla_p", "rg_p", "cv_p", "rg_s", "cv_s")}
    gla_s = None
    for layer in range(depth):
        w = _layer_weights(layer, w_in, gla_w_a2, gla_b_a, gla_norm_g, rg_conv_w, rg_conv_b, rg_w_a, rg_b_a,
                           rg_w_x, rg_b_x, rg_lambda, b_merge, w_branch, w_o, ln1_g, ln1_b, ln2_g, ln2_b,
                           router_w, router_b, kw, vw)
        qkvg, loga, xr, gate = _proj(x_head, x_tail, n_head_tiles, tail_tile, w, tile)
        o_p, s_p = _gla(qkvg, loga, w["ng"], None, layer, n_p, seq, SEQ_ROWS, GLA_CHUNK, heads, dk, dv)
        h_p, hl_p, cv_p = _rglru_seq(xr, w, n_p, seq, SEQ_ROWS)
        pad = lambda a: jnp.pad(a[t_p:t_p + n_s, None, :],
                                ((0, 0), (0, pad_rows - 1), (0, 0))).reshape(n_s * pad_rows, -1)
        o_s, gla_s = _gla(pad(qkvg), pad(loga), w["ng"], state_gla, layer, n_s, pad_rows,
                          DECODE_SEQS * pad_rows, pad_rows, heads, dk, dv, stack=(depth, gla_s))
        o_s = o_s.reshape(n_s, pad_rows, vw)[:, 0]
        h_s, hn_s, cv_s = _rglru_step(xr[t_p:t_p + n_s], state_conv[layer], state_rglru[layer], w)
        x1, x1p, routing, wt, cnt = _post(x_head, x_tail, n_head_tiles, tail_tile, o_p, h_p, tail_pad(o_s),
                                          tail_pad(h_s), gate, w, tile, alpha, n_exp, n_s)
        last = layer == depth - 1
        x = _moe(x1, x1p, routing, cnt, wt, w, layer, moe_weights, alpha, tile, t_p + n_s, split=last)
        if not last:
            x_head, x_tail, tail_tile = x, x, n_head_tiles
        outs["gla_p"].append(s_p)
        outs["rg_p"].append(hl_p.reshape(n_p, -1))
        outs["cv_p"].append(cv_p)
        outs["rg_s"].append(hn_s)
        outs["cv_s"].append(cv_s)
    y_head, y_tail = x
    return (y_head.reshape(n_p, seq, d), y_tail[:n_s].reshape(n_s, dec_seq, d),
            jnp.stack(outs["gla_p"]), jnp.stack(outs["rg_p"]), jnp.stack(outs["cv_p"]),
            gla_s, jnp.stack(outs["rg_s"]), jnp.stack(outs["cv_s"]))
```

```python
import functools

import jax
import jax.numpy as jnp
from jax import lax
from jax.experimental import pallas as pl
from jax.experimental.pallas import tpu as pltpu
from jax.experimental.pallas import tpu_sc as plsc

F32 = jnp.float32
BF16 = jnp.bfloat16

TOP_K = 4
GLA_GATE_TAU = 16.0
GLA_CHUNK = 64
RG_C = 8.0
SWIGLU_LIMIT = 7.0
SWIGLU_ALPHA = 1.702
LN_EPS = 1e-5

LANES = 128
BF16_SUBLANES = 16
VMEM_LIMIT = 56 * 1024 * 1024

PROJ_ROWS = 512
SEQ_ROWS = 256
DECODE_SEQS = 8
MOE_ROWS = 1024
FFN_COLS = 512
FFN_SUB_ROWS = 256


def _params(*sem):
    return pltpu.CompilerParams(dimension_semantics=sem, vmem_limit_bytes=VMEM_LIMIT)


def _const_spec(shape):
    nd = len(shape)
    return pl.BlockSpec(shape, lambda *_: (0,) * nd)


def _weight_spec(shape):
    nd = len(shape)
    return pl.BlockSpec(shape, lambda *_: (0,) * nd, pipeline_mode=pl.Buffered(1))


def _bdot(a, b):
    return jnp.dot(a, b, preferred_element_type=F32)


def _split3(x):
    hi = x.astype(BF16)
    r1 = x - hi.astype(F32)
    mid = r1.astype(BF16)
    lo = (r1 - mid.astype(F32)).astype(BF16)
    return hi, mid, lo


def _sigmoid(x):
    return 0.5 * jnp.tanh(0.5 * x) + 0.5


def _log_sigmoid(x):
    return jnp.minimum(x, 0.0) - jnp.log1p(jnp.exp(-jnp.abs(x)))


def _softplus(x):
    return jnp.maximum(x, 0.0) + jnp.log1p(jnp.exp(-jnp.abs(x)))


def _layernorm(x, g, b):
    mu = jnp.mean(x, axis=-1, keepdims=True)
    xc = x - mu
    var = jnp.mean(xc * xc, axis=-1, keepdims=True)
    return xc * lax.rsqrt(var + LN_EPS) * g + b


def _head_tail_specs(n_head_tiles, tail_tile, tm, d):
    return [pl.BlockSpec((tm, d), lambda i: (jnp.minimum(i, n_head_tiles - 1), 0)),
            pl.BlockSpec((tm, d), lambda i: (tail_tile, 0))]


def _head_or_tail(head_ref, tail_ref):
    is_tail = pl.program_id(0) == pl.num_programs(0) - 1
    return jnp.where(is_tail, tail_ref[...], head_ref[...])


def _token_inputs(x_head, x_tail, n_head_tiles, tail_tile, tm):
    d = x_head.shape[1]
    if x_tail is x_head:
        assert tail_tile == n_head_tiles
        return [x_head], [pl.BlockSpec((tm, d), lambda i: (i, 0))]
    return [x_head, x_tail], _head_tail_specs(n_head_tiles, tail_tile, tm, d)


def _token_tile(x_refs):
    return x_refs[0][...] if len(x_refs) == 1 else _head_or_tail(*x_refs)


def _proj_kernel(*refs, n_x):
    x_refs = refs[:n_x]
    wq_ref, wal_ref, wa2_ref, ba_ref, wxr_ref, wmg_ref, bmg_ref, qkvg_ref, loga_ref, xr_ref, gate_ref = refs[n_x:]
    xb = _token_tile(x_refs).astype(BF16)
    d = xb.shape[1]
    for c in range(0, qkvg_ref.shape[1], d):
        qkvg_ref[:, c:c + d] = _bdot(xb, wq_ref[:, c:c + d]).astype(BF16)
    a_low = _bdot(xb, wal_ref[...])
    z = _bdot(a_low.astype(BF16), wa2_ref[...]) + ba_ref[...]
    loga_ref[...] = _log_sigmoid(z) * (1.0 / GLA_GATE_TAU)
    xr_ref[...] = _bdot(xb, wxr_ref[...])
    for c in range(0, gate_ref.shape[1], d):
        gate_ref[:, c:c + d] = _sigmoid(_bdot(xb, wmg_ref[:, c:c + d]) + bmg_ref[:, c:c + d]).astype(BF16)


def _proj(x_head, x_tail, n_head_tiles, tail_tile, w, tm):
    d = x_head.shape[1]
    t = (n_head_tiles + 1) * tm
    kw = w["wq"].shape[1]
    nk = w["wa2"].shape[1]
    consts = [w["wq"], w["wal"], w["wa2"], w["ba"], w["wxr"], w["wmg"], w["bmg"]]
    xs, x_specs = _token_inputs(x_head, x_tail, n_head_tiles, tail_tile, tm)
    return pl.pallas_call(
        functools.partial(_proj_kernel, n_x=len(xs)),
        grid=(t // tm,),
        in_specs=x_specs + [_weight_spec(c.shape) for c in consts],
        out_specs=[pl.BlockSpec((tm, kw), lambda i: (i, 0)),
                   pl.BlockSpec((tm, nk), lambda i: (i, 0)),
                   pl.BlockSpec((tm, d), lambda i: (i, 0)),
                   pl.BlockSpec((tm, 2 * d), lambda i: (i, 0))],
        out_shape=[jax.ShapeDtypeStruct((t, kw), BF16),
                   jax.ShapeDtypeStruct((t, nk), F32),
                   jax.ShapeDtypeStruct((t, d), F32),
                   jax.ShapeDtypeStruct((t, 2 * d), BF16)],
        compiler_params=_params("arbitrary"),
        name="proj",
    )(*xs, *consts)


def _gla_kernel(*refs, chunk, n_chunks, heads, has_state, n_inputs, state_slices, state_slot):
    q_ref, k_ref, v_ref, la_ref, g_ref, ng_ref = refs[:6]
    s0_ref = refs[6] if has_state else None
    o_ref, so_ref, s_scr, o_scr, p_scr, kv_scr = refs[n_inputs:]
    t = pl.program_id(1)
    dk = q_ref.shape[1] // heads
    dv = v_ref.shape[1] // heads
    scale = dk ** -0.5
    chunks_per_seq = n_chunks // s_scr.shape[0]

    @pl.when(t == 0)
    def _():
        if has_state:
            s_scr[...] = s0_ref[0]
        else:
            s_scr[...] = jnp.zeros_like(s_scr)

    n_rows = n_chunks * chunk
    r = lax.broadcasted_iota(jnp.int32, (n_rows, n_rows), 0)
    c = lax.broadcasted_iota(jnp.int32, (n_rows, n_rows), 1)
    shift = chunk.bit_length() - 1
    assert 1 << shift == chunk, "chunk must be a power of two"
    tril = jnp.where(jnp.logical_and(r >> shift == c >> shift, r >= c), 1.0, 0.0).astype(BF16)
    la_hi, la_mid, la_lo = _split3(la_ref[...])
    b_all = _bdot(tril, la_hi) + _bdot(tril, la_mid) + _bdot(tril, la_lo)
    q_all = (q_ref[...].astype(F32) * scale * jnp.exp(b_all)).astype(BF16)
    k_all = (k_ref[...].astype(F32) * jnp.exp(-b_all)).astype(BF16)
    causal = (lax.broadcasted_iota(jnp.int32, (chunk, chunk), 0) >=
              lax.broadcasted_iota(jnp.int32, (chunk, chunk), 1))

    decays = []
    for ci in range(n_chunks):
        rows = slice(ci * chunk, (ci + 1) * chunk)
        b = b_all[rows, :]
        b_last = b[chunk - 1:chunk, :]
        q_e, k_e = q_all[rows, :], k_all[rows, :]
        k_d = (k_ref[rows, :].astype(F32) * jnp.exp(b_last - b)).astype(BF16)
        decays.append(jnp.exp(b_last))
        for h in range(heads):
            ks = slice(h * dk, (h + 1) * dk)
            v_h = v_ref[rows, h * dv:(h + 1) * dv]
            scores = lax.dot_general(q_e[:, ks], k_e[:, ks], (((1,), (1,)), ((), ())),
                                     preferred_element_type=F32)
            p_scr[rows, h * chunk:(h + 1) * chunk] = jnp.where(causal, scores, 0.0).astype(BF16)
            kv_scr[ci, h] = lax.dot_general(k_d[:, ks], v_h, (((0,), (0,)), ((), ())),
                                            preferred_element_type=F32)

    for ci in range(n_chunks):
        rows = slice(ci * chunk, (ci + 1) * chunk)
        seq = ci // chunks_per_seq
        for h in range(heads):
            ks = slice(h * dk, (h + 1) * dk)
            vs = slice(h * dv, (h + 1) * dv)
            s_old = s_scr[seq, h]
            o_scr[rows, vs] = (_bdot(p_scr[rows, h * chunk:(h + 1) * chunk], v_ref[rows, vs]) +
                               _bdot(q_all[rows, ks], s_old.astype(BF16)))
            s_scr[seq, h] = jnp.transpose(decays[ci][:, ks]) * s_old + kv_scr[ci, h]

    for h in range(heads):
        vs = slice(h * dv, (h + 1) * dv)
        o = o_scr[:, vs]
        mu = jnp.mean(o, axis=-1, keepdims=True)
        oc = o - mu
        var = jnp.mean(oc * oc, axis=-1, keepdims=True)
        on = oc * lax.rsqrt(var + LN_EPS) * ng_ref[:, vs]
        g = g_ref[:, vs].astype(F32)
        o_ref[:, vs] = (on * (g * _sigmoid(g))).astype(BF16)

    @pl.when(t == pl.num_programs(1) - 1)
    def _():
        if state_slices == 0:
            so_ref[...] = s_scr[...]
        for layer_slice in range(state_slices):
            so_ref[layer_slice] = s_scr[...] if layer_slice == state_slot else jnp.zeros_like(s_scr)


def _gla(qkvg, loga, norm_g, s0, layer, nb, s_len, rows, chunk, heads, dk, dv, stack=None):
    t = nb * s_len
    seqs = max(1, rows // s_len)
    assert s_len * seqs % rows == 0 and nb % seqs == 0 and s_len % chunk == 0
    nb, tps = nb // seqs, s_len * seqs // rows
    kw, vw = heads * dk, heads * dv
    has_state = s0 is not None
    state_shape, state_block, state_index = (nb * seqs, heads, dk, dv), (seqs, heads, dk, dv), lambda b, i: (b, 0, 0, 0)
    state_slices, state_slot, aliases = 0, 0, {}
    if stack is not None:
        depth, prev = stack
        state_shape = (depth,) + state_shape
        if prev is None:
            state_slices, state_slot = depth, layer
            state_block, state_index = (depth,) + state_block, lambda b, i: (0, b, 0, 0, 0)
        else:
            state_slices = 1
            state_block, state_index = (1,) + state_block, lambda b, i: (layer, b, 0, 0, 0)
    in_specs = [pl.BlockSpec((rows, kw), lambda b, i: (b * tps + i, 0)),
                pl.BlockSpec((rows, kw), lambda b, i: (b * tps + i, 1)),
                pl.BlockSpec((rows, vw), lambda b, i: (b * tps + i, (2 * kw) // vw)),
                pl.BlockSpec((rows, kw), lambda b, i: (b * tps + i, 0)),
                pl.BlockSpec((rows, vw), lambda b, i: (b * tps + i, (2 * kw) // vw + 1)),
                _const_spec(norm_g.shape)]
    args = [qkvg, qkvg, qkvg, loga, qkvg, norm_g]
    if has_state:
        in_specs.append(pl.BlockSpec((1, seqs, heads, dk, dv), lambda b, i: (layer, b, 0, 0, 0)))
        args.append(s0)
    if stack is not None and stack[1] is not None:
        in_specs.append(pl.BlockSpec(memory_space=pl.ANY))
        aliases[len(args)] = 1
        args.append(stack[1])
    return pl.pallas_call(
        functools.partial(_gla_kernel, chunk=chunk, n_chunks=rows // chunk, heads=heads, has_state=has_state,
                          n_inputs=len(args), state_slices=state_slices, state_slot=state_slot),
        grid=(nb, tps),
        in_specs=in_specs,
        out_specs=[pl.BlockSpec((rows, vw), lambda b, i: (b * tps + i, 0)),
                   pl.BlockSpec(state_block, state_index)],
        out_shape=[jax.ShapeDtypeStruct((t, vw), BF16), jax.ShapeDtypeStruct(state_shape, F32)],
        scratch_shapes=[pltpu.VMEM((seqs, heads, dk, dv), F32), pltpu.VMEM((rows, vw), F32),
                        pltpu.VMEM((rows, heads * chunk), BF16), pltpu.VMEM((rows // chunk, heads, dk, dv), F32)],
        input_output_aliases=aliases,
        compiler_params=_params("arbitrary", "arbitrary"),
        name="gla",
    )(*args)


def _rg_gates(xc, wax_ref, ba_ref, bx_ref, lam_ref):
    n_blocks, bw, _ = wax_ref.shape
    r_parts, i_parts = [], []
    for n in range(n_blocks):
        cs = slice(n * bw, (n + 1) * bw)
        pre = _bdot(xc[:, cs].astype(BF16), wax_ref[n])
        r_parts.append(jnp.tanh(pre[:, :bw] + ba_ref[:, cs]))
        i_parts.append(jnp.tanh(pre[:, bw:] + bx_ref[:, cs]))
    tanh_r = jnp.concatenate(r_parts, axis=1)
    tanh_i = jnp.concatenate(i_parts, axis=1)
    half_rate = (-0.5 * RG_C) * _softplus(-lam_ref[...])
    log_a = tanh_r * half_rate + half_rate
    a = jnp.exp(log_a)
    u = jnp.tanh(-log_a) * (1.0 + a * a)
    root = jnp.where(u > 0.0, u * lax.rsqrt(u), 0.0)
    bx = root * ((0.5 * tanh_i + 0.5) * xc)
    return a, bx


def _rglru_seq_kernel(xr_ref, cw_ref, cb_ref, wax_ref, ba_ref, bx_ref, lam_ref,
                      h_ref, hl_ref, cv_ref, cbuf, a_scr, b_scr, h_scr, hc):
    t = pl.program_id(1)
    rows = xr_ref.shape[0]
    taps = cw_ref.shape[0]
    head = cbuf.shape[0]
    assert taps - 1 <= head

    @pl.when(t == 0)
    def _():
        cbuf[...] = jnp.zeros_like(cbuf)
        hc[...] = jnp.zeros_like(hc)

    x = xr_ref[...]
    prev = cbuf[...]
    in_head = lax.broadcasted_iota(jnp.int32, prev.shape, 0)
    xc = cb_ref[...] + x * cw_ref[taps - 1:taps, :]
    for back in range(1, taps):
        rolled = pltpu.roll(x, back, 0)
        first_rows = jnp.where(in_head < back, pltpu.roll(prev, back, 0), rolled[:head])
        xc = xc + jnp.concatenate([first_rows, rolled[head:]], axis=0) * cw_ref[taps - 1 - back:taps - back, :]
    cbuf[...] = x[rows - head:, :]
    tail = x[rows - (taps - 1):, :]

    a, bx = _rg_gates(xc, wax_ref, ba_ref, bx_ref, lam_ref)
    a_scr[...] = a
    b_scr[...] = bx

    def step(i, h):
        h = a_scr[pl.ds(i, 1), :] * h + b_scr[pl.ds(i, 1), :]
        h_scr[pl.ds(i, 1), :] = h
        return h

    h_last = lax.fori_loop(0, rows, step, hc[...], unroll=16)
    hc[...] = h_last
    h_ref[...] = h_scr[...].astype(BF16)

    @pl.when(t == pl.num_programs(1) - 1)
    def _():
        hl_ref[0] = h_last
        cv_ref[0] = tail


def _rglru_seq(xr, w, nb, s_len, rows):
    width = xr.shape[1]
    t = nb * s_len
    tps = s_len // rows
    taps = w["cw"].shape[0]
    consts = [w["cw"], w["cb"], w["wax"], w["rba"], w["rbx"], w["lam"]]
    return pl.pallas_call(
        _rglru_seq_kernel,
        grid=(nb, tps),
        in_specs=[pl.BlockSpec((rows, width), lambda b, i: (b * tps + i, 0))] + [_const_spec(c.shape) for c in consts],
        out_specs=[pl.BlockSpec((rows, width), lambda b, i: (b * tps + i, 0)),
                   pl.BlockSpec((1, 1, width), lambda b, i: (b, 0, 0)),
                   pl.BlockSpec((1, taps - 1, width), lambda b, i: (b, 0, 0))],
        out_shape=[jax.ShapeDtypeStruct((t, width), BF16),
                   jax.ShapeDtypeStruct((nb, 1, width), F32),
                   jax.ShapeDtypeStruct((nb, taps - 1, width), F32)],
        scratch_shapes=[pltpu.VMEM((8, width), F32), pltpu.VMEM((rows, width), F32),
                        pltpu.VMEM((rows, width), F32), pltpu.VMEM((rows, width), F32),
                        pltpu.VMEM((1, width), F32)],
        compiler_params=_params("arbitrary", "arbitrary"),
        name="rglru_seq",
    )(xr, *consts)


def _rglru_step_kernel(xr_ref, sc_ref, h0_ref, cw_ref, cb_ref, wax_ref, ba_ref, bx_ref, lam_ref,
                       h_ref, hn_ref, cv_ref):
    taps = cw_ref.shape[0]
    xr = xr_ref[...]
    xc = (cb_ref[...] + sum(sc_ref[:, j, :] * cw_ref[j:j + 1, :] for j in range(taps - 1)) +
          xr * cw_ref[taps - 1:taps, :])
    a, bx = _rg_gates(xc, wax_ref, ba_ref, bx_ref, lam_ref)
    h = a * h0_ref[...] + bx
    h_ref[...] = h.astype(BF16)
    hn_ref[...] = h
    for j in range(taps - 2):
        cv_ref[:, j, :] = sc_ref[:, j + 1, :]
    cv_ref[:, taps - 2, :] = xr


def _rglru_step(xr, conv, h0, w):
    n, width = xr.shape
    consts = [w["cw"], w["cb"], w["wax"], w["rba"], w["rbx"], w["lam"]]
    args = [xr, conv, h0] + consts
    return pl.pallas_call(
        _rglru_step_kernel,
        grid=(1,),
        in_specs=[_const_spec(a.shape) for a in args],
        out_specs=[_const_spec((n, width)), _const_spec((n, width)), _const_spec(conv.shape)],
        out_shape=[jax.ShapeDtypeStruct((n, width), BF16),
                   jax.ShapeDtypeStruct((n, width), F32),
                   jax.ShapeDtypeStruct(conv.shape, F32)],
        compiler_params=_params("arbitrary"),
        name="rglru_step",
    )(*args)


def _pack_bf16_pairs(x):
    half = x.shape[1] // 2
    hi = pltpu.bitcast(x[:, :half].astype(BF16).astype(F32), jnp.uint32)
    lo = pltpu.bitcast(x[:, half:].astype(BF16).astype(F32), jnp.uint32)
    return hi | (lo >> 16)


def _unpack_bf16_pairs(p):
    hi = pltpu.bitcast(p & jnp.uint32(0xFFFF0000), F32)
    lo = pltpu.bitcast(p << 16, F32)
    return jnp.concatenate([hi, lo], axis=1)


def _post_kernel(*refs, n_x, alpha, n_experts, n_tail_real):
    x_refs = refs[:n_x]
    (o_ref, ot_ref, h_ref, ht_ref, gate_ref, wb0_ref, wb1_ref, wo_ref, g1_ref, b1_ref, rw_ref, rb_ref,
     x1_ref, x1p_ref, idx_ref, wt_ref, cnt_ref, cnt_scr) = refs[n_x:]
    tm, d = o_ref.shape

    @pl.when(pl.program_id(0) == 0)
    def _():
        cnt_scr[...] = jnp.zeros_like(cnt_scr)

    x = _token_tile(x_refs)
    o = _head_or_tail(o_ref, ot_ref)
    h = _head_or_tail(h_ref, ht_ref)
    gate = gate_ref[...].astype(F32)
    merged = gate[:, :d] * _bdot(o, wb0_ref[...]) + gate[:, d:] * _bdot(h, wb1_ref[...])
    mix = _bdot(merged.astype(BF16), wo_ref[...])
    x1 = _layernorm(alpha * x + mix, g1_ref[...], b1_ref[...])
    x1_ref[...] = x1
    x1p_ref[...] = _pack_bf16_pairs(x1)

    xh = x1.astype(BF16)
    xl = (x1 - xh.astype(F32)).astype(BF16)
    n_lanes = rw_ref.shape[1] // 2
    hi_terms = _bdot(xh, rw_ref[...])
    logits = (hi_terms[:, n_lanes:] + _bdot(xl, rw_ref[:, :n_lanes])) + hi_terms[:, :n_lanes]
    logits = logits + rb_ref[...]
    lane = lax.broadcasted_iota(jnp.int32, logits.shape, 1)
    lane_f = lane.astype(F32)
    neg_inf = jnp.float32(-jnp.inf)
    cur = jnp.where(lane < n_experts, logits, neg_inf)
    vals, idxs = [], []
    for _ in range(TOP_K):
        m = jnp.max(cur, axis=-1, keepdims=True)
        sel = jnp.min(jnp.where(cur == m, lane_f, float(LANES)), axis=-1, keepdims=True)
        vals.append(m)
        idxs.append(sel)
        cur = jnp.where(lane_f == sel, neg_inf, cur)
    exps = [jnp.exp(v - vals[0]) for v in vals]
    total = sum(exps)

    onehots = [jnp.where(lane_f == idxs[j], 1.0, 0.0) for j in range(TOP_K)]
    chosen = sum(onehots)
    if n_tail_real < tm:
        is_fill = jnp.logical_and(pl.program_id(0) == pl.num_programs(0) - 1,
                                  lax.broadcasted_iota(jnp.int32, chosen.shape, 0) >= n_tail_real)
        chosen = jnp.where(is_fill, 0.0, chosen)
    r = lax.broadcasted_iota(jnp.int32, (tm, tm), 0)
    c = lax.broadcasted_iota(jnp.int32, (tm, tm), 1)
    earlier = jnp.where(r > c, 1.0, 0.0).astype(BF16)
    before = _bdot(earlier, chosen.astype(BF16)) + cnt_scr[...]
    ranks = [jnp.sum(onehots[j] * before, axis=-1, keepdims=True) for j in range(TOP_K)]
    cnt_scr[...] += jnp.sum(chosen, axis=0, keepdims=True)
    cnt_ref[...] = cnt_scr[...]

    idx_out = jnp.zeros(logits.shape, F32)
    wt_out = jnp.zeros(logits.shape, F32)
    for j in range(TOP_K):
        idx_out = jnp.where(lane == j, idxs[j], idx_out)
        idx_out = jnp.where(lane == TOP_K + j, ranks[j], idx_out)
        wt_out = jnp.where(lane == j, exps[j] / total, wt_out)
    idx_ref[...] = idx_out.astype(jnp.int32)
    wt_ref[...] = wt_out


def _post(x_head, x_tail, n_head_tiles, tail_tile, o, h, o_tail, h_tail, gate, w, tm, alpha, n_experts,
          n_tail_real):
    d = x_head.shape[1]
    t = (n_head_tiles + 1) * tm
    assert o.shape[0] == t - tm and o_tail.shape[0] == tm and gate.shape[0] == t
    consts = [w["wb0"], w["wb1"], w["wo"], w["ln1g"], w["ln1b"], w["rw"], w["rb"]]
    row = lambda width: pl.BlockSpec((tm, width), lambda i: (i, 0))
    xs, x_specs = _token_inputs(x_head, x_tail, n_head_tiles, tail_tile, tm)
    return pl.pallas_call(
        functools.partial(_post_kernel, n_x=len(xs), alpha=alpha, n_experts=n_experts, n_tail_real=n_tail_real),
        grid=(t // tm,),
        in_specs=x_specs + _head_tail_specs(n_head_tiles, 0, tm, d) + _head_tail_specs(n_head_tiles, 0, tm, d) +
                 [row(2 * d)] + [_weight_spec(c.shape) for c in consts],
        out_specs=[row(d), row(d // 2), row(LANES), row(LANES), _const_spec((1, LANES))],
        out_shape=[jax.ShapeDtypeStruct((t, d), F32),
                   jax.ShapeDtypeStruct((t, d // 2), jnp.uint32),
                   jax.ShapeDtypeStruct((t, LANES), jnp.int32),
                   jax.ShapeDtypeStruct((t, LANES), F32),
                   jax.ShapeDtypeStruct((1, LANES), F32)],
        scratch_shapes=[pltpu.VMEM((1, LANES), F32)],
        compiler_params=_params("arbitrary"),
        name="post",
    )(*xs, o, o_tail, h, h_tail, gate, *consts)


SC_INDEX_MAX = 128
SC_ALIGN = 8
SC_BUFFER_BYTES = 208 * 1024


def _sc_plan(n_rows, row_bytes):
    info = plsc.get_sparse_core_info()
    n_workers = info.num_cores * info.num_subcores
    per_worker, rem = divmod(n_rows, n_workers)
    assert rem == 0 and per_worker % SC_ALIGN == 0, "rows must split into aligned equal shares per subcore"
    cap = min(SC_INDEX_MAX, SC_BUFFER_BYTES // row_bytes)
    chunk = max(c for c in range(SC_ALIGN, cap + 1, SC_ALIGN) if per_worker % c == 0)
    return info.num_cores, n_workers, per_worker, chunk


def _sc_gather(table, idx):
    n_rows = idx.shape[0]
    d = table.shape[1]
    n_cores, _, per_worker, chunk = _sc_plan(n_rows, d * table.dtype.itemsize)
    mesh = plsc.VectorSubcoreMesh(core_axis_name="c", subcore_axis_name="s")

    n_chunks = per_worker // chunk

    @functools.partial(
        pl.kernel, mesh=mesh, out_type=jax.ShapeDtypeStruct((n_rows, d), table.dtype),
        scratch_types=[pltpu.VMEM((chunk,), jnp.int32)] * 2 + [pltpu.VMEM((chunk, d), table.dtype)] * 2 +
                      [pltpu.SemaphoreType.DMA] * 4)
    def gather(table_hbm, idx_hbm, out_hbm, idx0, idx1, rows0, rows1, g0, g1, w0, w1):
        idx_v, rows_v, gsem, wsem = (idx0, idx1), (rows0, rows1), (g0, g1), (w0, w1)
        worker = lax.axis_index("s") * n_cores + lax.axis_index("c")
        base = worker * per_worker

        def rows_of(c):
            return pl.ds(pl.multiple_of(base + c * chunk, SC_ALIGN), chunk)

        def start_gather(c):
            s = c % 2
            pltpu.sync_copy(idx_hbm.at[rows_of(c)], idx_v[s])
            return pltpu.async_copy(table_hbm.at[idx_v[s]], rows_v[s], gsem[s])

        gathers = {c: start_gather(c) for c in range(min(2, n_chunks))}
        writes = {}
        for c in range(n_chunks):
            s = c % 2
            gathers[c].wait()
            writes[c] = pltpu.async_copy(rows_v[s], out_hbm.at[rows_of(c)], wsem[s])
            if c + 2 < n_chunks:
                writes.pop(c).wait()
                gathers[c + 2] = start_gather(c + 2)
        for write in writes.values():
            write.wait()

    return gather(table, idx)


def _sc_scatter(x, pos, n_out):
    n, d = x.shape
    assert pos.shape == (TOP_K * n,)
    n_cores, _, per_worker, chunk = _sc_plan(n, d * x.dtype.itemsize)
    mesh = plsc.VectorSubcoreMesh(core_axis_name="c", subcore_axis_name="s")

    n_chunks = per_worker // chunk

    @functools.partial(
        pl.kernel, mesh=mesh, out_type=jax.ShapeDtypeStruct((n_out, d), x.dtype),
        scratch_types=[pltpu.VMEM((chunk, d), x.dtype)] * 2 + [pltpu.VMEM((chunk,), jnp.int32)] * TOP_K +
                      [pltpu.SemaphoreType.DMA] * 3)
    def scatter(x_hbm, pos_hbm, out_hbm, rows0, rows1, *rest):
        rows_v, idx_v, (l0, l1, ssem) = (rows0, rows1), rest[:TOP_K], rest[TOP_K:]
        lsem = (l0, l1)
        worker = lax.axis_index("s") * n_cores + lax.axis_index("c")
        base = worker * per_worker

        def first_row(c):
            return pl.multiple_of(base + c * chunk, SC_ALIGN)

        def start_load(c):
            return pltpu.async_copy(x_hbm.at[pl.ds(first_row(c), chunk)], rows_v[c % 2], lsem[c % 2])

        load = start_load(0)
        for c in range(n_chunks):
            next_load = start_load(c + 1) if c + 1 < n_chunks else None
            for k in range(TOP_K):
                pltpu.sync_copy(pos_hbm.at[pl.ds(k * n + first_row(c), chunk)], idx_v[k])
            load.wait()
            scatters = [pltpu.async_copy(rows_v[c % 2], out_hbm.at[idx_v[k]], ssem)
                        for k in range(TOP_K)]
            for scatter_k in scatters:
                scatter_k.wait()
            load = next_load

    return scatter(x, pos)


def _ffn_kernel(be_ref, valid_ref, x_ref, wgu_ref, bgu_ref, wd_ref, bd_ref, o_ref, wgu_bf, wd_bf):
    i = pl.program_id(0)
    bm = x_ref.shape[0]
    f = wd_ref.shape[2]
    valid = valid_ref[i]
    new_expert = jnp.logical_or(i == 0, be_ref[i] != be_ref[jnp.maximum(i - 1, 0)])

    @pl.when(new_expert)
    def _():
        wgu_bf[...] = wgu_ref[0, 0].astype(BF16)
        wd_bf[...] = wd_ref[0, 0].astype(BF16)

    def expert_rows(rows):
        x = _unpack_bf16_pairs(x_ref[rows, :]).astype(BF16)
        acc = jnp.zeros((x.shape[0], wd_ref.shape[3]), F32)
        for c in range(0, f, FFN_COLS):
            gate = _bdot(x, wgu_bf[:, c:c + FFN_COLS]) + bgu_ref[0, 0, :, c:c + FFN_COLS]
            up = _bdot(x, wgu_bf[:, f + c:f + c + FFN_COLS]) + bgu_ref[0, 0, :, f + c:f + c + FFN_COLS]
            gate = jnp.minimum(gate, SWIGLU_LIMIT)
            up = jnp.clip(up, -SWIGLU_LIMIT, SWIGLU_LIMIT)
            act = (up + 1.0) * gate * _sigmoid(SWIGLU_ALPHA * gate)
            acc = acc + _bdot(act.astype(BF16), wd_bf[c:c + FFN_COLS, :])
        o_ref[rows, :] = _pack_bf16_pairs(acc + bd_ref[0, 0])

    @pl.when(valid == bm)
    def _():
        expert_rows(slice(0, bm))

    @pl.when(valid < bm)
    def _():
        for s in range(0, bm, FFN_SUB_ROWS):
            rows = slice(s, s + FFN_SUB_ROWS)

            @pl.when(s < valid)
            def _():
                expert_rows(rows)

            @pl.when(s >= valid)
            def _():
                o_ref[rows, :] = jnp.zeros((FFN_SUB_ROWS, o_ref.shape[1]), o_ref.dtype)


def _ffn(block_e, valid, xs, layer, w_gu, b_gu, w_down, b_down, bm):
    n_layers, n_exp, d, f2 = w_gu.shape
    f = f2 // 2
    n_blocks = block_e.shape[0]
    expert = lambda i, be, nu: (layer, be[i], 0, 0)
    grid_spec = pltpu.PrefetchScalarGridSpec(
        num_scalar_prefetch=2,
        grid=(n_blocks,),
        in_specs=[pl.BlockSpec((bm, d // 2), lambda i, be, nu: (i, 0)),
                  pl.BlockSpec((1, 1, d, f2), expert),
                  pl.BlockSpec((1, 1, 1, f2), expert),
                  pl.BlockSpec((1, 1, f, d), expert),
                  pl.BlockSpec((1, 1, 1, d), expert)],
        out_specs=pl.BlockSpec((bm, d // 2), lambda i, be, nu: (i, 0)),
        scratch_shapes=[pltpu.VMEM((d, f2), BF16), pltpu.VMEM((f, d), BF16)],
    )
    return pl.pallas_call(
        _ffn_kernel,
        grid_spec=grid_spec,
        out_shape=jax.ShapeDtypeStruct((n_blocks * bm, d // 2), jnp.uint32),
        compiler_params=_params("arbitrary"),
        name="ffn",
    )(block_e, valid, xs, w_gu, b_gu.reshape(n_layers, n_exp, 1, f2),
      w_down, b_down.reshape(n_layers, n_exp, 1, d))


def _combine_kernel(x_ref, *refs, alpha, split):
    y_refs, (wt_ref, g_ref, b_ref), out_refs = refs[:TOP_K], refs[TOP_K:TOP_K + 3], refs[TOP_K + 3:]
    wt = wt_ref[...]
    ffn = sum(wt[:, j:j + 1] * _unpack_bf16_pairs(y_refs[j][...]) for j in range(TOP_K))
    res = _layernorm(alpha * x_ref[...] + ffn, g_ref[...], b_ref[...])
    if not split:
        out_refs[0][...] = res
    else:
        is_tail = pl.program_id(0) == pl.num_programs(0) - 1

        @pl.when(jnp.logical_not(is_tail))
        def _():
            out_refs[0][...] = res

        @pl.when(is_tail)
        def _():
            out_refs[1][...] = res


def _combine(x1, y, wt, g, b, tm, alpha, split=False):
    t, d = x1.shape
    n_tiles = t // tm
    planes = [pl.BlockSpec((tm, d // 2), lambda i, j=j: (j * n_tiles + i, 0)) for j in range(TOP_K)]
    if split:
        out_specs = [pl.BlockSpec((tm, d), lambda i: (jnp.minimum(i, n_tiles - 2), 0)), _const_spec((tm, d))]
        out_shape = [jax.ShapeDtypeStruct((t - tm, d), F32), jax.ShapeDtypeStruct((tm, d), F32)]
    else:
        out_specs = pl.BlockSpec((tm, d), lambda i: (i, 0))
        out_shape = jax.ShapeDtypeStruct((t, d), F32)
    return pl.pallas_call(
        functools.partial(_combine_kernel, alpha=alpha, split=split),
        grid=(n_tiles,),
        in_specs=[pl.BlockSpec((tm, d), lambda i: (i, 0))] + planes +
                 [pl.BlockSpec((tm, LANES), lambda i: (i, 0)), _const_spec(g.shape), _const_spec(b.shape)],
        out_specs=out_specs,
        out_shape=out_shape,
        compiler_params=_params("arbitrary"),
        name="combine",
    )(x1, *([y] * TOP_K), wt, g, b)


def _route(experts, ranks, counts, bm):
    n_experts = counts.shape[0]
    i32 = jnp.int32
    padded = (counts + bm - 1) // bm * bm
    ends_pad = jnp.cumsum(padded)
    start_pad = ends_pad - padded
    n_blocks = -(-experts.size // bm) + n_experts
    first_row = jnp.arange(n_blocks, dtype=i32)[:, None] * bm
    block_e = jnp.minimum(jnp.sum(ends_pad[None, :] <= first_row, axis=1), n_experts - 1).astype(i32)
    valid = jnp.clip(counts[block_e] - (first_row[:, 0] - start_pad[block_e]), 0, bm).astype(i32)
    onehot = experts[..., None] == jnp.arange(n_experts, dtype=i32)
    pos = jnp.sum(jnp.where(onehot, start_pad, 0), axis=-1).astype(i32) + ranks
    return block_e, valid, pos


def _moe(x1, x1p, routing, counts, top_w_pad, w, layer, moe_weights, alpha, tm, n_real, split):
    n_tok = x1.shape[0]
    n_experts = moe_weights[0].shape[1]
    block_e, valid, pos = _route(routing[:n_real, :TOP_K], routing[:n_real, TOP_K:2 * TOP_K],
                                 counts[0, :n_experts].astype(jnp.int32), MOE_ROWS)
    n_rows = block_e.shape[0] * MOE_ROWS
    n_fill = n_tok - n_real
    spare = n_rows + jnp.arange(n_fill * TOP_K, dtype=jnp.int32).reshape(n_fill, TOP_K)
    xs = _sc_scatter(x1p, jnp.concatenate([pos, spare]).T.reshape(-1), n_rows + n_fill * TOP_K)
    out_rows = _ffn(block_e, valid, xs, layer, *moe_weights, MOE_ROWS)
    y = _sc_gather(out_rows, jnp.concatenate([pos, spare - n_rows]).T.reshape(-1))
    return _combine(x1, y, top_w_pad, w["ln2g"], w["ln2b"], tm, alpha, split)


def _hi_lo(w):
    hi = w.astype(BF16)
    return jnp.concatenate([hi, (w - hi.astype(F32)).astype(BF16)], axis=1)


def _layer_weights(layer, w_in, gla_w_a2, gla_b_a, gla_norm_g, rg_conv_w, rg_conv_b, rg_w_a, rg_b_a, rg_w_x,
                   rg_b_x, rg_lambda, b_merge, w_branch, w_o, ln1_g, ln1_b, ln2_g, ln2_b, router_w, router_b,
                   kw, vw):
    d = w_in.shape[1]
    rank = gla_w_a2.shape[1]
    width = rg_conv_w.shape[2]
    n_exp = router_w.shape[2]
    c0 = 2 * kw + 2 * vw
    wi = w_in[layer]
    row = lambda v: v.reshape(1, -1)
    return {
        "wq": wi[:, :c0].astype(BF16),
        "wal": jnp.pad(wi[:, c0:c0 + rank], ((0, 0), (0, LANES - rank))).astype(BF16),
        "wa2": jnp.pad(gla_w_a2[layer], ((0, LANES - rank), (0, 0))).astype(BF16),
        "ba": row(gla_b_a[layer]),
        "wxr": wi[:, c0 + rank:c0 + rank + width].astype(BF16),
        "wmg": wi[:, c0 + rank + width:].astype(BF16),
        "bmg": row(b_merge[layer]),
        "ng": row(gla_norm_g[layer]),
        "cw": rg_conv_w[layer], "cb": row(rg_conv_b[layer]),
        "wax": (0.5 * jnp.concatenate([rg_w_a[layer], rg_w_x[layer]], axis=-1)).astype(BF16),
        "rba": row(0.5 * rg_b_a[layer]), "rbx": row(0.5 * rg_b_x[layer]), "lam": row(rg_lambda[layer]),
        "wb0": w_branch[layer, 0].astype(BF16), "wb1": w_branch[layer, 1].astype(BF16),
        "wo": w_o[layer].astype(BF16),
        "ln1g": row(ln1_g[layer]), "ln1b": row(ln1_b[layer]),
        "ln2g": row(ln2_g[layer]), "ln2b": row(ln2_b[layer]),
        "rw": _hi_lo(jnp.pad(router_w[layer], ((0, 0), (0, LANES - n_exp)))),
        "rb": jnp.pad(row(router_b[layer]), ((0, 0), (0, LANES - n_exp))),
    }


def kernel(x_prompt, x_sample, state_gla, state_rglru, state_conv, ln1_g, ln1_b, w_in, gla_w_a2, gla_b_a, gla_norm_g, rg_conv_w, rg_conv_b, rg_w_a, rg_b_a, rg_w_x, rg_b_x, rg_lambda, b_merge, w_branch, w_o, ln2_g, ln2_b, router_w, router_b, moe_w_gu, moe_b_gu, moe_w_down, moe_b_down):
    n_p, seq, d = x_prompt.shape
    n_s, dec_seq, _ = x_sample.shape
    assert dec_seq == 1, "the sample group carries one new token per sequence"
    depth, _, heads, dk, dv = state_gla.shape
    kw, vw = heads * dk, heads * dv
    n_exp = router_w.shape[2]
    alpha = (2.0 * depth) ** 0.25
    t_p = n_p * seq
    pad_rows = BF16_SUBLANES

    tile = PROJ_ROWS
    assert t_p % tile == 0 and n_s <= tile
    n_head_tiles = t_p // tile
    tail_pad = lambda a: jnp.pad(a, ((0, tile - n_s), (0, 0)))
    x_head, x_tail, tail_tile = x_prompt.reshape(t_p, d), tail_pad(x_sample.reshape(n_s, d)), 0
    moe_weights = (moe_w_gu, moe_b_gu, moe_w_down, moe_b_down)
    outs = {k: [] for k in ("gla_p", "rg_p", "cv_p", "rg_s", "cv_s")}
    gla_s = None
    for layer in range(depth):
        w = _layer_weights(layer, w_in, gla_w_a2, gla_b_a, gla_norm_g, rg_conv_w, rg_conv_b, rg_w_a, rg_b_a,
                           rg_w_x, rg_b_x, rg_lambda, b_merge, w_branch, w_o, ln1_g, ln1_b, ln2_g, ln2_b,
                           router_w, router_b, kw, vw)
        qkvg, loga, xr, gate = _proj(x_head, x_tail, n_head_tiles, tail_tile, w, tile)
        o_p, s_p = _gla(qkvg, loga, w["ng"], None, layer, n_p, seq, SEQ_ROWS, GLA_CHUNK, heads, dk, dv)
        h_p, hl_p, cv_p = _rglru_seq(xr, w, n_p, seq, SEQ_ROWS)
        pad = lambda a: jnp.pad(a[t_p:t_p + n_s, None, :],
                                ((0, 0), (0, pad_rows - 1), (0, 0))).reshape(n_s * pad_rows, -1)
        o_s, gla_s = _gla(pad(qkvg), pad(loga), w["ng"], state_gla, layer, n_s, pad_rows,
                          DECODE_SEQS * pad_rows, pad_rows, heads, dk, dv, stack=(depth, gla_s))
        o_s = o_s.reshape(n_s, pad_rows, vw)[:, 0]
        h_s, hn_s, cv_s = _rglru_step(xr[t_p:t_p + n_s], state_conv[layer], state_rglru[layer], w)
        x1, x1p, routing, wt, cnt = _post(x_head, x_tail, n_head_tiles, tail_tile, o_p, h_p, tail_pad(o_s),
                                          tail_pad(h_s), gate, w, tile, alpha, n_exp, n_s)
        last = layer == depth - 1
        x = _moe(x1, x1p, routing, cnt, wt, w, layer, moe_weights, alpha, tile, t_p + n_s, split=last)
        if not last:
            x_head, x_tail, tail_tile = x, x, n_head_tiles
        outs["gla_p"].append(s_p)
        outs["rg_p"].append(hl_p.reshape(n_p, -1))
        outs["cv_p"].append(cv_p)
        outs["rg_s"].append(hn_s)
        outs["cv_s"].append(cv_s)
    y_head, y_tail = x
    return (y_head.reshape(n_p, seq, d), y_tail[:n_s].reshape(n_s, dec_seq, d),
            jnp.stack(outs["gla_p"]), jnp.stack(outs["rg_p"]), jnp.stack(outs["cv_p"]),
            gla_s, jnp.stack(outs["rg_s"]), jnp.stack(outs["cv_s"]))
```

```python
import functools

import jax
import jax.numpy as jnp
from jax import lax
from jax.experimental import pallas as pl
from jax.experimental.pallas import tpu as pltpu
from jax.experimental.pallas import tpu_sc as plsc

F32 = jnp.float32
BF16 = jnp.bfloat16

TOP_K = 4
GLA_GATE_TAU = 16.0
GLA_CHUNK = 64
RG_C = 8.0
SWIGLU_LIMIT = 7.0
SWIGLU_ALPHA = 1.702
LN_EPS = 1e-5

LANES = 128
BF16_SUBLANES = 16
VMEM_LIMIT = 56 * 1024 * 1024

PROJ_ROWS = 512
SEQ_ROWS = 256
DECODE_SEQS = 8
MOE_ROWS = 1024
FFN_COLS = 512
FFN_SUB_ROWS = 256


def _params(*sem):
    return pltpu.CompilerParams(dimension_semantics=sem, vmem_limit_bytes=VMEM_LIMIT)


def _const_spec(shape):
    nd = len(shape)
    return pl.BlockSpec(shape, lambda *_: (0,) * nd)


def _weight_spec(shape):
    nd = len(shape)
    return pl.BlockSpec(shape, lambda *_: (0,) * nd, pipeline_mode=pl.Buffered(1))


def _bdot(a, b):
    return jnp.dot(a, b, preferred_element_type=F32)


def _split3(x):
    hi = x.astype(BF16)
    r1 = x - hi.astype(F32)
    mid = r1.astype(BF16)
    lo = (r1 - mid.astype(F32)).astype(BF16)
    return hi, mid, lo


def _sigmoid(x):
    return 0.5 * jnp.tanh(0.5 * x) + 0.5


def _log_sigmoid(x):
    return jnp.minimum(x, 0.0) - jnp.log1p(jnp.exp(-jnp.abs(x)))


def _softplus(x):
    return jnp.maximum(x, 0.0) + jnp.log1p(jnp.exp(-jnp.abs(x)))


def _layernorm(x, g, b):
    mu = jnp.mean(x, axis=-1, keepdims=True)
    xc = x - mu
    var = jnp.mean(xc * xc, axis=-1, keepdims=True)
    return xc * lax.rsqrt(var + LN_EPS) * g + b


def _head_tail_specs(n_head_tiles, tail_tile, tm, d):
    return [pl.BlockSpec((tm, d), lambda i: (jnp.minimum(i, n_head_tiles - 1), 0)),
            pl.BlockSpec((tm, d), lambda i: (tail_tile, 0))]


def _head_or_tail(head_ref, tail_ref):
    is_tail = pl.program_id(0) == pl.num_programs(0) - 1
    return jnp.where(is_tail, tail_ref[...], head_ref[...])


def _token_inputs(x_head, x_tail, n_head_tiles, tail_tile, tm):
    d = x_head.shape[1]
    if x_tail is x_head:
        assert tail_tile == n_head_tiles
        return [x_head], [pl.BlockSpec((tm, d), lambda i: (i, 0))]
    return [x_head, x_tail], _head_tail_specs(n_head_tiles, tail_tile, tm, d)


def _token_tile(x_refs):
    return x_refs[0][...] if len(x_refs) == 1 else _head_or_tail(*x_refs)


def _proj_kernel(*refs, n_x):
    _proj_tile(_token_tile(refs[:n_x]).astype(BF16), *refs[n_x:])


def _proj_tile(xb, wq_ref, wal_ref, wa2_ref, ba_ref, wxr_ref, wmg_ref, bmg_ref, qkvg_ref, loga_ref, xr_ref, gate_ref):
    d = xb.shape[1]
    for c in range(0, qkvg_ref.shape[1], d):
        qkvg_ref[:, c:c + d] = _bdot(xb, wq_ref[:, c:c + d]).astype(BF16)
    a_low = _bdot(xb, wal_ref[...])
    z = _bdot(a_low.astype(BF16), wa2_ref[...]) + ba_ref[...]
    loga_ref[...] = _log_sigmoid(z) * (1.0 / GLA_GATE_TAU)
    xr_ref[...] = _bdot(xb, wxr_ref[...])
    for c in range(0, gate_ref.shape[1], d):
        gate_ref[:, c:c + d] = _sigmoid(_bdot(xb, wmg_ref[:, c:c + d]) + bmg_ref[:, c:c + d]).astype(BF16)


def _proj_operands(w, t, d, tm):
    kw = w["wq"].shape[1]
    nk = w["wa2"].shape[1]
    consts = [w["wq"], w["wal"], w["wa2"], w["ba"], w["wxr"], w["wmg"], w["bmg"]]
    widths = [(kw, BF16), (nk, F32), (d, F32), (2 * d, BF16)]
    return (consts, [_weight_spec(c.shape) for c in consts],
            [pl.BlockSpec((tm, n), lambda i: (i, 0)) for n, _ in widths],
            [jax.ShapeDtypeStruct((t, n), dt) for n, dt in widths])


def _proj(x_head, x_tail, n_head_tiles, tail_tile, w, tm):
    d = x_head.shape[1]
    t = (n_head_tiles + 1) * tm
    consts, w_specs, out_specs, out_shape = _proj_operands(w, t, d, tm)
    xs, x_specs = _token_inputs(x_head, x_tail, n_head_tiles, tail_tile, tm)
    return pl.pallas_call(
        functools.partial(_proj_kernel, n_x=len(xs)),
        grid=(t // tm,),
        in_specs=x_specs + w_specs,
        out_specs=out_specs,
        out_shape=out_shape,
        compiler_params=_params("arbitrary"),
        name="proj",
    )(*xs, *consts)


def _gla_kernel(*refs, chunk, n_chunks, heads, has_state, n_inputs, state_slices, state_slot):
    q_ref, k_ref, v_ref, la_ref, g_ref, ng_ref = refs[:6]
    s0_ref = refs[6] if has_state else None
    o_ref, so_ref, s_scr, o_scr, p_scr, kv_scr = refs[n_inputs:]
    t = pl.program_id(1)
    dk = q_ref.shape[1] // heads
    dv = v_ref.shape[1] // heads
    scale = dk ** -0.5
    chunks_per_seq = n_chunks // s_scr.shape[0]

    @pl.when(t == 0)
    def _():
        if has_state:
            s_scr[...] = s0_ref[0]
        else:
            s_scr[...] = jnp.zeros_like(s_scr)

    n_rows = n_chunks * chunk
    r = lax.broadcasted_iota(jnp.int32, (n_rows, n_rows), 0)
    c = lax.broadcasted_iota(jnp.int32, (n_rows, n_rows), 1)
    shift = chunk.bit_length() - 1
    assert 1 << shift == chunk, "chunk must be a power of two"
    tril = jnp.where(jnp.logical_and(r >> shift == c >> shift, r >= c), 1.0, 0.0).astype(BF16)
    la_hi, la_mid, la_lo = _split3(la_ref[...])
    b_all = _bdot(tril, la_hi) + _bdot(tril, la_mid) + _bdot(tril, la_lo)
    q_all = (q_ref[...].astype(F32) * scale * jnp.exp(b_all)).astype(BF16)
    k_all = (k_ref[...].astype(F32) * jnp.exp(-b_all)).astype(BF16)
    causal = (lax.broadcasted_iota(jnp.int32, (chunk, chunk), 0) >=
              lax.broadcasted_iota(jnp.int32, (chunk, chunk), 1))

    decays = []
    for ci in range(n_chunks):
        rows = slice(ci * chunk, (ci + 1) * chunk)
        b = b_all[rows, :]
        b_last = b[chunk - 1:chunk, :]
        q_e, k_e = q_all[rows, :], k_all[rows, :]
        k_d = (k_ref[rows, :].astype(F32) * jnp.exp(b_last - b)).astype(BF16)
        decays.append(jnp.exp(b_last))
        for h in range(heads):
            ks = slice(h * dk, (h + 1) * dk)
            v_h = v_ref[rows, h * dv:(h + 1) * dv]
            scores = lax.dot_general(q_e[:, ks], k_e[:, ks], (((1,), (1,)), ((), ())),
                                     preferred_element_type=F32)
            p_scr[rows, h * chunk:(h + 1) * chunk] = jnp.where(causal, scores, 0.0).astype(BF16)
            kv_scr[ci, h] = lax.dot_general(k_d[:, ks], v_h, (((0,), (0,)), ((), ())),
                                            preferred_element_type=F32)

    for ci in range(n_chunks):
        rows = slice(ci * chunk, (ci + 1) * chunk)
        seq = ci // chunks_per_seq
        for h in range(heads):
            ks = slice(h * dk, (h + 1) * dk)
            vs = slice(h * dv, (h + 1) * dv)
            s_old = s_scr[seq, h]
            o_scr[rows, vs] = (_bdot(p_scr[rows, h * chunk:(h + 1) * chunk], v_ref[rows, vs]) +
                               _bdot(q_all[rows, ks], s_old.astype(BF16)))
            s_scr[seq, h] = jnp.transpose(decays[ci][:, ks]) * s_old + kv_scr[ci, h]

    for h in range(heads):
        vs = slice(h * dv, (h + 1) * dv)
        o = o_scr[:, vs]
        mu = jnp.mean(o, axis=-1, keepdims=True)
        oc = o - mu
        var = jnp.mean(oc * oc, axis=-1, keepdims=True)
        on = oc * lax.rsqrt(var + LN_EPS) * ng_ref[:, vs]
        g = g_ref[:, vs].astype(F32)
        o_ref[:, vs] = (on * (g * _sigmoid(g))).astype(BF16)

    @pl.when(t == pl.num_programs(1) - 1)
    def _():
        if state_slices == 0:
            so_ref[...] = s_scr[...]
        for layer_slice in range(state_slices):
            so_ref[layer_slice] = s_scr[...] if layer_slice == state_slot else jnp.zeros_like(s_scr)


def _gla(qkvg, loga, norm_g, s0, layer, nb, s_len, rows, chunk, heads, dk, dv, stack=None):
    t = nb * s_len
    seqs = max(1, rows // s_len)
    assert s_len * seqs % rows == 0 and nb % seqs == 0 and s_len % chunk == 0
    nb, tps = nb // seqs, s_len * seqs // rows
    kw, vw = heads * dk, heads * dv
    has_state = s0 is not None
    state_shape, state_block, state_index = (nb * seqs, heads, dk, dv), (seqs, heads, dk, dv), lambda b, i: (b, 0, 0, 0)
    state_slices, state_slot, aliases = 0, 0, {}
    if stack is not None:
        depth, prev = stack
        state_shape = (depth,) + state_shape
        if prev is None:
            state_slices, state_slot = depth, layer
            state_block, state_index = (depth,) + state_block, lambda b, i: (0, b, 0, 0, 0)
        else:
            state_slices = 1
            state_block, state_index = (1,) + state_block, lambda b, i: (layer, b, 0, 0, 0)
    in_specs = [pl.BlockSpec((rows, kw), lambda b, i: (b * tps + i, 0)),
                pl.BlockSpec((rows, kw), lambda b, i: (b * tps + i, 1)),
                pl.BlockSpec((rows, vw), lambda b, i: (b * tps + i, (2 * kw) // vw)),
                pl.BlockSpec((rows, kw), lambda b, i: (b * tps + i, 0)),
                pl.BlockSpec((rows, vw), lambda b, i: (b * tps + i, (2 * kw) // vw + 1)),
                _const_spec(norm_g.shape)]
    args = [qkvg, qkvg, qkvg, loga, qkvg, norm_g]
    if has_state:
        in_specs.append(pl.BlockSpec((1, seqs, heads, dk, dv), lambda b, i: (layer, b, 0, 0, 0)))
        args.append(s0)
    if stack is not None and stack[1] is not None:
        in_specs.append(pl.BlockSpec(memory_space=pl.ANY))
        aliases[len(args)] = 1
        args.append(stack[1])
    return pl.pallas_call(
        functools.partial(_gla_kernel, chunk=chunk, n_chunks=rows // chunk, heads=heads, has_state=has_state,
                          n_inputs=len(args), state_slices=state_slices, state_slot=state_slot),
        grid=(nb, tps),
        in_specs=in_specs,
        out_specs=[pl.BlockSpec((rows, vw), lambda b, i: (b * tps + i, 0)),
                   pl.BlockSpec(state_block, state_index)],
        out_shape=[jax.ShapeDtypeStruct((t, vw), BF16), jax.ShapeDtypeStruct(state_shape, F32)],
        scratch_shapes=[pltpu.VMEM((seqs, heads, dk, dv), F32), pltpu.VMEM((rows, vw), F32),
                        pltpu.VMEM((rows, heads * chunk), BF16), pltpu.VMEM((rows // chunk, heads, dk, dv), F32)],
        input_output_aliases=aliases,
        compiler_params=_params("arbitrary", "arbitrary"),
        name="gla",
    )(*args)


def _rg_gates(xc, wax_ref, ba_ref, bx_ref, lam_ref):
    n_blocks, bw, _ = wax_ref.shape
    r_parts, i_parts = [], []
    for n in range(n_blocks):
        cs = slice(n * bw, (n + 1) * bw)
        pre = _bdot(xc[:, cs].astype(BF16), wax_ref[n])
        r_parts.append(jnp.tanh(pre[:, :bw] + ba_ref[:, cs]))
        i_parts.append(jnp.tanh(pre[:, bw:] + bx_ref[:, cs]))
    tanh_r = jnp.concatenate(r_parts, axis=1)
    tanh_i = jnp.concatenate(i_parts, axis=1)
    half_rate = (-0.5 * RG_C) * _softplus(-lam_ref[...])
    log_a = tanh_r * half_rate + half_rate
    a = jnp.exp(log_a)
    u = jnp.tanh(-log_a) * (1.0 + a * a)
    root = jnp.where(u > 0.0, u * lax.rsqrt(u), 0.0)
    bx = root * ((0.5 * tanh_i + 0.5) * xc)
    return a, bx


def _rglru_seq_kernel(xr_ref, cw_ref, cb_ref, wax_ref, ba_ref, bx_ref, lam_ref,
                      h_ref, hl_ref, cv_ref, cbuf, a_scr, b_scr, h_scr, hc):
    t = pl.program_id(1)
    rows = xr_ref.shape[0]
    taps = cw_ref.shape[0]
    head = cbuf.shape[0]
    assert taps - 1 <= head

    @pl.when(t == 0)
    def _():
        cbuf[...] = jnp.zeros_like(cbuf)
        hc[...] = jnp.zeros_like(hc)

    x = xr_ref[...]
    prev = cbuf[...]
    in_head = lax.broadcasted_iota(jnp.int32, prev.shape, 0)
    xc = cb_ref[...] + x * cw_ref[taps - 1:taps, :]
    for back in range(1, taps):
        rolled = pltpu.roll(x, back, 0)
        first_rows = jnp.where(in_head < back, pltpu.roll(prev, back, 0), rolled[:head])
        xc = xc + jnp.concatenate([first_rows, rolled[head:]], axis=0) * cw_ref[taps - 1 - back:taps - back, :]
    cbuf[...] = x[rows - head:, :]
    tail = x[rows - (taps - 1):, :]

    a, bx = _rg_gates(xc, wax_ref, ba_ref, bx_ref, lam_ref)
    a_scr[...] = a
    b_scr[...] = bx

    def step(i, h):
        h = a_scr[pl.ds(i, 1), :] * h + b_scr[pl.ds(i, 1), :]
        h_scr[pl.ds(i, 1), :] = h
        return h

    h_last = lax.fori_loop(0, rows, step, hc[...], unroll=16)
    hc[...] = h_last
    h_ref[...] = h_scr[...].astype(BF16)

    @pl.when(t == pl.num_programs(1) - 1)
    def _():
        hl_ref[0] = h_last
        cv_ref[0] = tail


def _rglru_seq(xr, w, nb, s_len, rows):
    width = xr.shape[1]
    t = nb * s_len
    tps = s_len // rows
    taps = w["cw"].shape[0]
    consts = [w["cw"], w["cb"], w["wax"], w["rba"], w["rbx"], w["lam"]]
    return pl.pallas_call(
        _rglru_seq_kernel,
        grid=(nb, tps),
        in_specs=[pl.BlockSpec((rows, width), lambda b, i: (b * tps + i, 0))] + [_const_spec(c.shape) for c in consts],
        out_specs=[pl.BlockSpec((rows, width), lambda b, i: (b * tps + i, 0)),
                   pl.BlockSpec((1, 1, width), lambda b, i: (b, 0, 0)),
                   pl.BlockSpec((1, taps - 1, width), lambda b, i: (b, 0, 0))],
        out_shape=[jax.ShapeDtypeStruct((t, width), BF16),
                   jax.ShapeDtypeStruct((nb, 1, width), F32),
                   jax.ShapeDtypeStruct((nb, taps - 1, width), F32)],
        scratch_shapes=[pltpu.VMEM((8, width), F32), pltpu.VMEM((rows, width), F32),
                        pltpu.VMEM((rows, width), F32), pltpu.VMEM((rows, width), F32),
                        pltpu.VMEM((1, width), F32)],
        compiler_params=_params("arbitrary", "arbitrary"),
        name="rglru_seq",
    )(xr, *consts)


def _rglru_step_kernel(xr_ref, sc_ref, h0_ref, cw_ref, cb_ref, wax_ref, ba_ref, bx_ref, lam_ref,
                       h_ref, hn_ref, cv_ref):
    taps = cw_ref.shape[0]
    xr = xr_ref[...]
    xc = (cb_ref[...] + sum(sc_ref[:, j, :] * cw_ref[j:j + 1, :] for j in range(taps - 1)) +
          xr * cw_ref[taps - 1:taps, :])
    a, bx = _rg_gates(xc, wax_ref, ba_ref, bx_ref, lam_ref)
    h = a * h0_ref[...] + bx
    h_ref[...] = h.astype(BF16)
    hn_ref[...] = h
    for j in range(taps - 2):
        cv_ref[:, j, :] = sc_ref[:, j + 1, :]
    cv_ref[:, taps - 2, :] = xr


def _rglru_step(xr, conv, h0, w):
    n, width = xr.shape
    consts = [w["cw"], w["cb"], w["wax"], w["rba"], w["rbx"], w["lam"]]
    args = [xr, conv, h0] + consts
    return pl.pallas_call(
        _rglru_step_kernel,
        grid=(1,),
        in_specs=[_const_spec(a.shape) for a in args],
        out_specs=[_const_spec((n, width)), _const_spec((n, width)), _const_spec(conv.shape)],
        out_shape=[jax.ShapeDtypeStruct((n, width), BF16),
                   jax.ShapeDtypeStruct((n, width), F32),
                   jax.ShapeDtypeStruct(conv.shape, F32)],
        compiler_params=_params("arbitrary"),
        name="rglru_step",
    )(*args)


def _pack_bf16_pairs(x):
    half = x.shape[1] // 2
    hi = pltpu.bitcast(x[:, :half].astype(BF16).astype(F32), jnp.uint32)
    lo = pltpu.bitcast(x[:, half:].astype(BF16).astype(F32), jnp.uint32)
    return hi | (lo >> 16)


def _unpack_bf16_pairs(p):
    hi = pltpu.bitcast(p & jnp.uint32(0xFFFF0000), F32)
    lo = pltpu.bitcast(p << 16, F32)
    return jnp.concatenate([hi, lo], axis=1)


def _post_kernel(*refs, n_x, alpha, n_experts, n_tail_real):
    x_refs = refs[:n_x]
    (o_ref, ot_ref, h_ref, ht_ref, gate_ref, wb0_ref, wb1_ref, wo_ref, g1_ref, b1_ref, rw_ref, rb_ref,
     x1_ref, x1p_ref, idx_ref, wt_ref, cnt_ref, cnt_scr) = refs[n_x:]
    tm, d = o_ref.shape

    @pl.when(pl.program_id(0) == 0)
    def _():
        cnt_scr[...] = jnp.zeros_like(cnt_scr)

    x = _token_tile(x_refs)
    o = _head_or_tail(o_ref, ot_ref)
    h = _head_or_tail(h_ref, ht_ref)
    gate = gate_ref[...].astype(F32)
    merged = gate[:, :d] * _bdot(o, wb0_ref[...]) + gate[:, d:] * _bdot(h, wb1_ref[...])
    mix = _bdot(merged.astype(BF16), wo_ref[...])
    x1 = _layernorm(alpha * x + mix, g1_ref[...], b1_ref[...])
    x1_ref[...] = x1
    x1p_ref[...] = _pack_bf16_pairs(x1)

    xh = x1.astype(BF16)
    xl = (x1 - xh.astype(F32)).astype(BF16)
    n_lanes = rw_ref.shape[1] // 2
    hi_terms = _bdot(xh, rw_ref[...])
    logits = (hi_terms[:, n_lanes:] + _bdot(xl, rw_ref[:, :n_lanes])) + hi_terms[:, :n_lanes]
    logits = logits + rb_ref[...]
    lane = lax.broadcasted_iota(jnp.int32, logits.shape, 1)
    lane_f = lane.astype(F32)
    neg_inf = jnp.float32(-jnp.inf)
    cur = jnp.where(lane < n_experts, logits, neg_inf)
    vals, idxs = [], []
    for _ in range(TOP_K):
        m = jnp.max(cur, axis=-1, keepdims=True)
        sel = jnp.min(jnp.where(cur == m, lane_f, float(LANES)), axis=-1, keepdims=True)
        vals.append(m)
        idxs.append(sel)
        cur = jnp.where(lane_f == sel, neg_inf, cur)
    exps = [jnp.exp(v - vals[0]) for v in vals]
    total = sum(exps)

    onehots = [jnp.where(lane_f == idxs[j], 1.0, 0.0) for j in range(TOP_K)]
    chosen = sum(onehots)
    if n_tail_real < tm:
        is_fill = jnp.logical_and(pl.program_id(0) == pl.num_programs(0) - 1,
                                  lax.broadcasted_iota(jnp.int32, chosen.shape, 0) >= n_tail_real)
        chosen = jnp.where(is_fill, 0.0, chosen)
    r = lax.broadcasted_iota(jnp.int32, (tm, tm), 0)
    c = lax.broadcasted_iota(jnp.int32, (tm, tm), 1)
    earlier = jnp.where(r > c, 1.0, 0.0).astype(BF16)
    before = _bdot(earlier, chosen.astype(BF16)) + cnt_scr[...]
    ranks = [jnp.sum(onehots[j] * before, axis=-1, keepdims=True) for j in range(TOP_K)]
    cnt_scr[...] += jnp.sum(chosen, axis=0, keepdims=True)
    cnt_ref[...] = cnt_scr[...]

    idx_out = jnp.zeros(logits.shape, F32)
    wt_out = jnp.zeros(logits.shape, F32)
    for j in range(TOP_K):
        idx_out = jnp.where(lane == j, idxs[j], idx_out)
        idx_out = jnp.where(lane == TOP_K + j, ranks[j], idx_out)
        wt_out = jnp.where(lane == j, exps[j] / total, wt_out)
    idx_ref[...] = idx_out.astype(jnp.int32)
    wt_ref[...] = wt_out


def _post(x_head, x_tail, n_head_tiles, tail_tile, o, h, o_tail, h_tail, gate, w, tm, alpha, n_experts,
          n_tail_real):
    d = x_head.shape[1]
    t = (n_head_tiles + 1) * tm
    assert o.shape[0] == t - tm and o_tail.shape[0] == tm and gate.shape[0] == t
    consts = [w["wb0"], w["wb1"], w["wo"], w["ln1g"], w["ln1b"], w["rw"], w["rb"]]
    row = lambda width: pl.BlockSpec((tm, width), lambda i: (i, 0))
    xs, x_specs = _token_inputs(x_head, x_tail, n_head_tiles, tail_tile, tm)
    return pl.pallas_call(
        functools.partial(_post_kernel, n_x=len(xs), alpha=alpha, n_experts=n_experts, n_tail_real=n_tail_real),
        grid=(t // tm,),
        in_specs=x_specs + _head_tail_specs(n_head_tiles, 0, tm, d) + _head_tail_specs(n_head_tiles, 0, tm, d) +
                 [row(2 * d)] + [_weight_spec(c.shape) for c in consts],
        out_specs=[row(d), row(d // 2), row(LANES), row(LANES), _const_spec((1, LANES))],
        out_shape=[jax.ShapeDtypeStruct((t, d), F32),
                   jax.ShapeDtypeStruct((t, d // 2), jnp.uint32),
                   jax.ShapeDtypeStruct((t, LANES), jnp.int32),
                   jax.ShapeDtypeStruct((t, LANES), F32),
                   jax.ShapeDtypeStruct((1, LANES), F32)],
        scratch_shapes=[pltpu.VMEM((1, LANES), F32)],
        compiler_params=_params("arbitrary"),
        name="post",
    )(*xs, o, o_tail, h, h_tail, gate, *consts)


SC_INDEX_MAX = 128
SC_ALIGN = 8
SC_BUFFER_BYTES = 208 * 1024


def _sc_plan(n_rows, row_bytes):
    info = plsc.get_sparse_core_info()
    n_workers = info.num_cores * info.num_subcores
    per_worker, rem = divmod(n_rows, n_workers)
    assert rem == 0 and per_worker % SC_ALIGN == 0, "rows must split into aligned equal shares per subcore"
    cap = min(SC_INDEX_MAX, SC_BUFFER_BYTES // row_bytes)
    chunk = max(c for c in range(SC_ALIGN, cap + 1, SC_ALIGN) if per_worker % c == 0)
    return info.num_cores, n_workers, per_worker, chunk


def _sc_gather(table, idx):
    n_rows = idx.shape[0]
    d = table.shape[1]
    n_cores, _, per_worker, chunk = _sc_plan(n_rows, d * table.dtype.itemsize)
    mesh = plsc.VectorSubcoreMesh(core_axis_name="c", subcore_axis_name="s")

    n_chunks = per_worker // chunk

    @functools.partial(
        pl.kernel, mesh=mesh, out_type=jax.ShapeDtypeStruct((n_rows, d), table.dtype),
        scratch_types=[pltpu.VMEM((chunk,), jnp.int32)] * 2 + [pltpu.VMEM((chunk, d), table.dtype)] * 2 +
                      [pltpu.SemaphoreType.DMA] * 4)
    def gather(table_hbm, idx_hbm, out_hbm, idx0, idx1, rows0, rows1, g0, g1, w0, w1):
        idx_v, rows_v, gsem, wsem = (idx0, idx1), (rows0, rows1), (g0, g1), (w0, w1)
        worker = lax.axis_index("s") * n_cores + lax.axis_index("c")
        base = worker * per_worker

        def rows_of(c):
            return pl.ds(pl.multiple_of(base + c * chunk, SC_ALIGN), chunk)

        def start_gather(c):
            s = c % 2
            pltpu.sync_copy(idx_hbm.at[rows_of(c)], idx_v[s])
            return pltpu.async_copy(table_hbm.at[idx_v[s]], rows_v[s], gsem[s])

        gathers = {c: start_gather(c) for c in range(min(2, n_chunks))}
        writes = {}
        for c in range(n_chunks):
            s = c % 2
            gathers[c].wait()
            writes[c] = pltpu.async_copy(rows_v[s], out_hbm.at[rows_of(c)], wsem[s])
            if c + 2 < n_chunks:
                writes.pop(c).wait()
                gathers[c + 2] = start_gather(c + 2)
        for write in writes.values():
            write.wait()

    return gather(table, idx)


def _sc_scatter(x, pos, n_out):
    n, d = x.shape
    assert pos.shape == (TOP_K * n,)
    n_cores, _, per_worker, chunk = _sc_plan(n, d * x.dtype.itemsize)
    mesh = plsc.VectorSubcoreMesh(core_axis_name="c", subcore_axis_name="s")

    n_chunks = per_worker // chunk

    @functools.partial(
        pl.kernel, mesh=mesh, out_type=jax.ShapeDtypeStruct((n_out, d), x.dtype),
        scratch_types=[pltpu.VMEM((chunk, d), x.dtype)] * 2 + [pltpu.VMEM((chunk,), jnp.int32)] * TOP_K +
                      [pltpu.SemaphoreType.DMA] * 3)
    def scatter(x_hbm, pos_hbm, out_hbm, rows0, rows1, *rest):
        rows_v, idx_v, (l0, l1, ssem) = (rows0, rows1), rest[:TOP_K], rest[TOP_K:]
        lsem = (l0, l1)
        worker = lax.axis_index("s") * n_cores + lax.axis_index("c")
        base = worker * per_worker

        def first_row(c):
            return pl.multiple_of(base + c * chunk, SC_ALIGN)

        def start_load(c):
            return pltpu.async_copy(x_hbm.at[pl.ds(first_row(c), chunk)], rows_v[c % 2], lsem[c % 2])

        load = start_load(0)
        for c in range(n_chunks):
            next_load = start_load(c + 1) if c + 1 < n_chunks else None
            for k in range(TOP_K):
                pltpu.sync_copy(pos_hbm.at[pl.ds(k * n + first_row(c), chunk)], idx_v[k])
            load.wait()
            scatters = [pltpu.async_copy(rows_v[c % 2], out_hbm.at[idx_v[k]], ssem)
                        for k in range(TOP_K)]
            for scatter_k in scatters:
                scatter_k.wait()
            load = next_load

    return scatter(x, pos)


def _ffn_kernel(be_ref, valid_ref, x_ref, wgu_ref, bgu_ref, wd_ref, bd_ref, o_ref, wgu_bf, wd_bf):
    i = pl.program_id(0)
    bm = x_ref.shape[0]
    f = wd_ref.shape[2]
    valid = valid_ref[i]
    new_expert = jnp.logical_or(i == 0, be_ref[i] != be_ref[jnp.maximum(i - 1, 0)])

    @pl.when(new_expert)
    def _():
        wgu_bf[...] = wgu_ref[0, 0].astype(BF16)
        wd_bf[...] = wd_ref[0, 0].astype(BF16)

    def expert_rows(rows):
        x = _unpack_bf16_pairs(x_ref[rows, :]).astype(BF16)
        acc = jnp.zeros((x.shape[0], wd_ref.shape[3]), F32)
        for c in range(0, f, FFN_COLS):
            gate = _bdot(x, wgu_bf[:, c:c + FFN_COLS]) + bgu_ref[0, 0, :, c:c + FFN_COLS]
            up = _bdot(x, wgu_bf[:, f + c:f + c + FFN_COLS]) + bgu_ref[0, 0, :, f + c:f + c + FFN_COLS]
            gate = jnp.minimum(gate, SWIGLU_LIMIT)
            up = jnp.clip(up, -SWIGLU_LIMIT, SWIGLU_LIMIT)
            act = (up + 1.0) * gate * _sigmoid(SWIGLU_ALPHA * gate)
            acc = acc + _bdot(act.astype(BF16), wd_bf[c:c + FFN_COLS, :])
        o_ref[rows, :] = _pack_bf16_pairs(acc + bd_ref[0, 0])

    @pl.when(valid == bm)
    def _():
        expert_rows(slice(0, bm))

    @pl.when(valid < bm)
    def _():
        for s in range(0, bm, FFN_SUB_ROWS):
            rows = slice(s, s + FFN_SUB_ROWS)

            @pl.when(s < valid)
            def _():
                expert_rows(rows)

            @pl.when(s >= valid)
            def _():
                o_ref[rows, :] = jnp.zeros((FFN_SUB_ROWS, o_ref.shape[1]), o_ref.dtype)


def _ffn(block_e, valid, xs, layer, w_gu, b_gu, w_down, b_down, bm):
    n_layers, n_exp, d, f2 = w_gu.shape
    f = f2 // 2
    n_blocks = block_e.shape[0]
    expert = lambda i, be, nu: (layer, be[i], 0, 0)
    grid_spec = pltpu.PrefetchScalarGridSpec(
        num_scalar_prefetch=2,
        grid=(n_blocks,),
        in_specs=[pl.BlockSpec((bm, d // 2), lambda i, be, nu: (i, 0)),
                  pl.BlockSpec((1, 1, d, f2), expert),
                  pl.BlockSpec((1, 1, 1, f2), expert),
                  pl.BlockSpec((1, 1, f, d), expert),
                  pl.BlockSpec((1, 1, 1, d), expert)],
        out_specs=pl.BlockSpec((bm, d // 2), lambda i, be, nu: (i, 0)),
        scratch_shapes=[pltpu.VMEM((d, f2), BF16), pltpu.VMEM((f, d), BF16)],
    )
    return pl.pallas_call(
        _ffn_kernel,
        grid_spec=grid_spec,
        out_shape=jax.ShapeDtypeStruct((n_blocks * bm, d // 2), jnp.uint32),
        compiler_params=_params("arbitrary"),
        name="ffn",
    )(block_e, valid, xs, w_gu, b_gu.reshape(n_layers, n_exp, 1, f2),
      w_down, b_down.reshape(n_layers, n_exp, 1, d))


def _combine_kernel(x_ref, *refs, alpha, split):
    y_refs, (wt_ref, g_ref, b_ref), out_refs = refs[:TOP_K], refs[TOP_K:TOP_K + 3], refs[TOP_K + 3:]
    wt = wt_ref[...]
    ffn = sum(wt[:, j:j + 1] * _unpack_bf16_pairs(y_refs[j][...]) for j in range(TOP_K))
    res = _layernorm(alpha * x_ref[...] + ffn, g_ref[...], b_ref[...])
    if not split:
        out_refs[0][...] = res
    else:
        is_tail = pl.program_id(0) == pl.num_programs(0) - 1

        @pl.when(jnp.logical_not(is_tail))
        def _():
            out_refs[0][...] = res

        @pl.when(is_tail)
        def _():
            out_refs[1][...] = res


def _combine(x1, y, wt, g, b, tm, alpha, split=False):
    t, d = x1.shape
    n_tiles = t // tm
    planes = [pl.BlockSpec((tm, d // 2), lambda i, j=j: (j * n_tiles + i, 0)) for j in range(TOP_K)]
    if split:
        out_specs = [pl.BlockSpec((tm, d), lambda i: (jnp.minimum(i, n_tiles - 2), 0)), _const_spec((tm, d))]
        out_shape = [jax.ShapeDtypeStruct((t - tm, d), F32), jax.ShapeDtypeStruct((tm, d), F32)]
    else:
        out_specs = pl.BlockSpec((tm, d), lambda i: (i, 0))
        out_shape = jax.ShapeDtypeStruct((t, d), F32)
    return pl.pallas_call(
        functools.partial(_combine_kernel, alpha=alpha, split=split),
        grid=(n_tiles,),
        in_specs=[pl.BlockSpec((tm, d), lambda i: (i, 0))] + planes +
                 [pl.BlockSpec((tm, LANES), lambda i: (i, 0)), _const_spec(g.shape), _const_spec(b.shape)],
        out_specs=out_specs,
        out_shape=out_shape,
        compiler_params=_params("arbitrary"),
        name="combine",
    )(x1, *([y] * TOP_K), wt, g, b)


def _combine_proj_kernel(x_ref, *refs, alpha):
    y_refs, (wt_ref, g_ref, b_ref) = refs[:TOP_K], refs[TOP_K:TOP_K + 3]
    proj_weights, (x2_ref, *proj_outs) = refs[TOP_K + 3:TOP_K + 10], refs[TOP_K + 10:]
    wt = wt_ref[...]
    ffn = sum(wt[:, j:j + 1] * _unpack_bf16_pairs(y_refs[j][...]) for j in range(TOP_K))
    x2 = _layernorm(alpha * x_ref[...] + ffn, g_ref[...], b_ref[...])
    x2_ref[...] = x2
    _proj_tile(x2.astype(BF16), *proj_weights, *proj_outs)


def _combine_proj(x1, y, wt, g, b, w_next, tm, alpha):
    t, d = x1.shape
    n_tiles = t // tm
    row = lambda width: pl.BlockSpec((tm, width), lambda i: (i, 0))
    planes = [pl.BlockSpec((tm, d // 2), lambda i, j=j: (j * n_tiles + i, 0)) for j in range(TOP_K)]
    consts, w_specs, proj_out_specs, proj_out_shape = _proj_operands(w_next, t, d, tm)
    return pl.pallas_call(
        functools.partial(_combine_proj_kernel, alpha=alpha),
        grid=(n_tiles,),
        in_specs=[row(d)] + planes + [row(LANES), _const_spec(g.shape), _const_spec(b.shape)] + w_specs,
        out_specs=[row(d)] + proj_out_specs,
        out_shape=[jax.ShapeDtypeStruct((t, d), F32)] + proj_out_shape,
        compiler_params=_params("arbitrary"),
        name="combine_proj",
    )(x1, *([y] * TOP_K), wt, g, b, *consts)


def _route(experts, ranks, counts, bm):
    n_experts = counts.shape[0]
    i32 = jnp.int32
    padded = (counts + bm - 1) // bm * bm
    ends_pad = jnp.cumsum(padded)
    start_pad = ends_pad - padded
    n_blocks = -(-experts.size // bm) + n_experts
    first_row = jnp.arange(n_blocks, dtype=i32)[:, None] * bm
    block_e = jnp.minimum(jnp.sum(ends_pad[None, :] <= first_row, axis=1), n_experts - 1).astype(i32)
    valid = jnp.clip(counts[block_e] - (first_row[:, 0] - start_pad[block_e]), 0, bm).astype(i32)
    onehot = experts[..., None] == jnp.arange(n_experts, dtype=i32)
    pos = jnp.sum(jnp.where(onehot, start_pad, 0), axis=-1).astype(i32) + ranks
    return block_e, valid, pos


def _moe(x1, x1p, routing, counts, top_w_pad, w, layer, moe_weights, alpha, tm, n_real, w_next):
    n_tok = x1.shape[0]
    n_experts = moe_weights[0].shape[1]
    block_e, valid, pos = _route(routing[:n_real, :TOP_K], routing[:n_real, TOP_K:2 * TOP_K],
                                 counts[0, :n_experts].astype(jnp.int32), MOE_ROWS)
    n_rows = block_e.shape[0] * MOE_ROWS
    n_fill = n_tok - n_real
    spare = n_rows + jnp.arange(n_fill * TOP_K, dtype=jnp.int32).reshape(n_fill, TOP_K)
    xs = _sc_scatter(x1p, jnp.concatenate([pos, spare]).T.reshape(-1), n_rows + n_fill * TOP_K)
    out_rows = _ffn(block_e, valid, xs, layer, *moe_weights, MOE_ROWS)
    y = _sc_gather(out_rows, jnp.concatenate([pos, spare - n_rows]).T.reshape(-1))
    if w_next is None:
        return _combine(x1, y, top_w_pad, w["ln2g"], w["ln2b"], tm, alpha, split=True)
    x2, *proj_outs = _combine_proj(x1, y, top_w_pad, w["ln2g"], w["ln2b"], w_next, tm, alpha)
    return x2, proj_outs


def _hi_lo(w):
    hi = w.astype(BF16)
    return jnp.concatenate([hi, (w - hi.astype(F32)).astype(BF16)], axis=1)


def _layer_weights(layer, w_in, gla_w_a2, gla_b_a, gla_norm_g, rg_conv_w, rg_conv_b, rg_w_a, rg_b_a, rg_w_x,
                   rg_b_x, rg_lambda, b_merge, w_branch, w_o, ln1_g, ln1_b, ln2_g, ln2_b, router_w, router_b,
                   kw, vw):
    d = w_in.shape[1]
    rank = gla_w_a2.shape[1]
    width = rg_conv_w.shape[2]
    n_exp = router_w.shape[2]
    c0 = 2 * kw + 2 * vw
    wi = w_in[layer]
    row = lambda v: v.reshape(1, -1)
    return {
        "wq": wi[:, :c0].astype(BF16),
        "wal": jnp.pad(wi[:, c0:c0 + rank], ((0, 0), (0, LANES - rank))).astype(BF16),
        "wa2": jnp.pad(gla_w_a2[layer], ((0, LANES - rank), (0, 0))).astype(BF16),
        "ba": row(gla_b_a[layer]),
        "wxr": wi[:, c0 + rank:c0 + rank + width].astype(BF16),
        "wmg": wi[:, c0 + rank + width:].astype(BF16),
        "bmg": row(b_merge[layer]),
        "ng": row(gla_norm_g[layer]),
        "cw": rg_conv_w[layer], "cb": row(rg_conv_b[layer]),
        "wax": (0.5 * jnp.concatenate([rg_w_a[layer], rg_w_x[layer]], axis=-1)).astype(BF16),
        "rba": row(0.5 * rg_b_a[layer]), "rbx": row(0.5 * rg_b_x[layer]), "lam": row(rg_lambda[layer]),
        "wb0": w_branch[layer, 0].astype(BF16), "wb1": w_branch[layer, 1].astype(BF16),
        "wo": w_o[layer].astype(BF16),
        "ln1g": row(ln1_g[layer]), "ln1b": row(ln1_b[layer]),
        "ln2g": row(ln2_g[layer]), "ln2b": row(ln2_b[layer]),
        "rw": _hi_lo(jnp.pad(router_w[layer], ((0, 0), (0, LANES - n_exp)))),
        "rb": jnp.pad(row(router_b[layer]), ((0, 0), (0, LANES - n_exp))),
    }


def kernel(x_prompt, x_sample, state_gla, state_rglru, state_conv, ln1_g, ln1_b, w_in, gla_w_a2, gla_b_a, gla_norm_g, rg_conv_w, rg_conv_b, rg_w_a, rg_b_a, rg_w_x, rg_b_x, rg_lambda, b_merge, w_branch, w_o, ln2_g, ln2_b, router_w, router_b, moe_w_gu, moe_b_gu, moe_w_down, moe_b_down):
    n_p, seq, d = x_prompt.shape
    n_s, dec_seq, _ = x_sample.shape
    assert dec_seq == 1, "the sample group carries one new token per sequence"
    depth, _, heads, dk, dv = state_gla.shape
    kw, vw = heads * dk, heads * dv
    n_exp = router_w.shape[2]
    alpha = (2.0 * depth) ** 0.25
    t_p = n_p * seq
    pad_rows = BF16_SUBLANES

    tile = PROJ_ROWS
    assert t_p % tile == 0 and n_s <= tile
    n_head_tiles = t_p // tile
    tail_pad = lambda a: jnp.pad(a, ((0, tile - n_s), (0, 0)))
    x_head, x_tail, tail_tile = x_prompt.reshape(t_p, d), tail_pad(x_sample.reshape(n_s, d)), 0
    moe_weights = (moe_w_gu, moe_b_gu, moe_w_down, moe_b_down)
    outs = {k: [] for k in ("gla_p", "rg_p", "cv_p", "rg_s", "cv_s")}
    gla_s = None
    weights = [_layer_weights(layer, w_in, gla_w_a2, gla_b_a, gla_norm_g, rg_conv_w, rg_conv_b, rg_w_a, rg_b_a,
                              rg_w_x, rg_b_x, rg_lambda, b_merge, w_branch, w_o, ln1_g, ln1_b, ln2_g, ln2_b,
                              router_w, router_b, kw, vw) for layer in range(depth)]
    projections = _proj(x_head, x_tail, n_head_tiles, tail_tile, weights[0], tile)
    for layer in range(depth):
        w = weights[layer]
        qkvg, loga, xr, gate = projections
        o_p, s_p = _gla(qkvg, loga, w["ng"], None, layer, n_p, seq, SEQ_ROWS, GLA_CHUNK, heads, dk, dv)
        h_p, hl_p, cv_p = _rglru_seq(xr, w, n_p, seq, SEQ_ROWS)
        pad = lambda a: jnp.pad(a[t_p:t_p + n_s, None, :],
                                ((0, 0), (0, pad_rows - 1), (0, 0))).reshape(n_s * pad_rows, -1)
        o_s, gla_s = _gla(pad(qkvg), pad(loga), w["ng"], state_gla, layer, n_s, pad_rows,
                          DECODE_SEQS * pad_rows, pad_rows, heads, dk, dv, stack=(depth, gla_s))
        o_s = o_s.reshape(n_s, pad_rows, vw)[:, 0]
        h_s, hn_s, cv_s = _rglru_step(xr[t_p:t_p + n_s], state_conv[layer], state_rglru[layer], w)
        x1, x1p, routing, wt, cnt = _post(x_head, x_tail, n_head_tiles, tail_tile, o_p, h_p, tail_pad(o_s),
                                          tail_pad(h_s), gate, w, tile, alpha, n_exp, n_s)
        w_next = weights[layer + 1] if layer + 1 < depth else None
        result = _moe(x1, x1p, routing, cnt, wt, w, layer, moe_weights, alpha, tile, t_p + n_s, w_next)
        if w_next is not None:
            x, projections = result
            x_head, x_tail, tail_tile = x, x, n_head_tiles
        outs["gla_p"].append(s_p)
        outs["rg_p"].append(hl_p.reshape(n_p, -1))
        outs["cv_p"].append(cv_p)
        outs["rg_s"].append(hn_s)
        outs["cv_s"].append(cv_s)
    y_head, y_tail = result
    return (y_head.reshape(n_p, seq, d), y_tail[:n_s].reshape(n_s, dec_seq, d),
            jnp.stack(outs["gla_p"]), jnp.stack(outs["rg_p"]), jnp.stack(outs["cv_p"]),
            gla_s, jnp.stack(outs["rg_s"]), jnp.stack(outs["cv_s"]))
```

```python
import functools

import jax
import jax.numpy as jnp
from jax import lax
from jax.experimental import pallas as pl
from jax.experimental.pallas import tpu as pltpu
from jax.experimental.pallas import tpu_sc as plsc

F32 = jnp.float32
BF16 = jnp.bfloat16

TOP_K = 4
GLA_GATE_TAU = 16.0
GLA_CHUNK = 64
RG_C = 8.0
SWIGLU_LIMIT = 7.0
SWIGLU_ALPHA = 1.702
LN_EPS = 1e-5

LANES = 128
BF16_SUBLANES = 16
VMEM_LIMIT = 56 * 1024 * 1024

PROJ_ROWS = 512
SEQ_ROWS = 256
DECODE_SEQS = 8
MOE_ROWS = 1024
FFN_COLS = 512
FFN_SUB_ROWS = 256


def _params(*sem):
    return pltpu.CompilerParams(dimension_semantics=sem, vmem_limit_bytes=VMEM_LIMIT)


def _const_spec(shape):
    nd = len(shape)
    return pl.BlockSpec(shape, lambda *_: (0,) * nd)


def _weight_spec(shape):
    nd = len(shape)
    return pl.BlockSpec(shape, lambda *_: (0,) * nd, pipeline_mode=pl.Buffered(1))


def _bdot(a, b):
    return jnp.dot(a, b, preferred_element_type=F32)


def _split3(x):
    hi = x.astype(BF16)
    r1 = x - hi.astype(F32)
    mid = r1.astype(BF16)
    lo = (r1 - mid.astype(F32)).astype(BF16)
    return hi, mid, lo


def _sigmoid(x):
    return 0.5 * jnp.tanh(0.5 * x) + 0.5


def _log_sigmoid(x):
    return jnp.minimum(x, 0.0) - jnp.log1p(jnp.exp(-jnp.abs(x)))


def _softplus(x):
    return jnp.maximum(x, 0.0) + jnp.log1p(jnp.exp(-jnp.abs(x)))


def _layernorm(x, g, b):
    mu = jnp.mean(x, axis=-1, keepdims=True)
    xc = x - mu
    var = jnp.mean(xc * xc, axis=-1, keepdims=True)
    return xc * lax.rsqrt(var + LN_EPS) * g + b


def _head_tail_specs(n_head_tiles, tail_tile, tm, d):
    return [pl.BlockSpec((tm, d), lambda i: (jnp.minimum(i, n_head_tiles - 1), 0)),
            pl.BlockSpec((tm, d), lambda i: (tail_tile, 0))]


def _head_or_tail(head_ref, tail_ref):
    is_tail = pl.program_id(0) == pl.num_programs(0) - 1
    return jnp.where(is_tail, tail_ref[...], head_ref[...])


def _token_inputs(x_head, x_tail, n_head_tiles, tail_tile, tm):
    d = x_head.shape[1]
    if x_tail is x_head:
        assert tail_tile == n_head_tiles
        return [x_head], [pl.BlockSpec((tm, d), lambda i: (i, 0))]
    return [x_head, x_tail], _head_tail_specs(n_head_tiles, tail_tile, tm, d)


def _token_tile(x_refs):
    return x_refs[0][...] if len(x_refs) == 1 else _head_or_tail(*x_refs)


def _proj_kernel(*refs, n_x):
    _proj_tile(_token_tile(refs[:n_x]).astype(BF16), *refs[n_x:])


def _proj_tile(xb, wq_ref, wal_ref, wa2_ref, ba_ref, wxr_ref, wmg_ref, bmg_ref, qkvg_ref, loga_ref, xr_ref, gate_ref):
    d = xb.shape[1]
    for c in range(0, qkvg_ref.shape[1], d):
        qkvg_ref[:, c:c + d] = _bdot(xb, wq_ref[:, c:c + d]).astype(BF16)
    a_low = _bdot(xb, wal_ref[...])
    z = _bdot(a_low.astype(BF16), wa2_ref[...]) + ba_ref[...]
    loga_ref[...] = _log_sigmoid(z) * (1.0 / GLA_GATE_TAU)
    xr_ref[...] = _bdot(xb, wxr_ref[...])
    for c in range(0, gate_ref.shape[1], d):
        gate_ref[:, c:c + d] = _sigmoid(_bdot(xb, wmg_ref[:, c:c + d]) + bmg_ref[:, c:c + d]).astype(BF16)


def _proj_operands(w, t, d, tm):
    kw = w["wq"].shape[1]
    nk = w["wa2"].shape[1]
    consts = [w["wq"], w["wal"], w["wa2"], w["ba"], w["wxr"], w["wmg"], w["bmg"]]
    widths = [(kw, BF16), (nk, F32), (d, F32), (2 * d, BF16)]
    return (consts, [_weight_spec(c.shape) for c in consts],
            [pl.BlockSpec((tm, n), lambda i: (i, 0)) for n, _ in widths],
            [jax.ShapeDtypeStruct((t, n), dt) for n, dt in widths])


def _proj(x_head, x_tail, n_head_tiles, tail_tile, w, tm):
    d = x_head.shape[1]
    t = (n_head_tiles + 1) * tm
    consts, w_specs, out_specs, out_shape = _proj_operands(w, t, d, tm)
    xs, x_specs = _token_inputs(x_head, x_tail, n_head_tiles, tail_tile, tm)
    return pl.pallas_call(
        functools.partial(_proj_kernel, n_x=len(xs)),
        grid=(t // tm,),
        in_specs=x_specs + w_specs,
        out_specs=out_specs,
        out_shape=out_shape,
        compiler_params=_params("arbitrary"),
        name="proj",
    )(*xs, *consts)


def _gla_kernel(*refs, chunk, n_chunks, heads, has_state, n_inputs, state_slices, state_slot):
    q_ref, k_ref, v_ref, la_ref, g_ref, ng_ref = refs[:6]
    s0_ref = refs[6] if has_state else None
    o_ref, so_ref, s_scr, o_scr, p_scr, kv_scr = refs[n_inputs:]
    t = pl.program_id(1)
    dk = q_ref.shape[1] // heads
    dv = v_ref.shape[1] // heads
    scale = dk ** -0.5
    chunks_per_seq = n_chunks // s_scr.shape[0]

    @pl.when(t == 0)
    def _():
        if has_state:
            s_scr[...] = s0_ref[0]
        else:
            s_scr[...] = jnp.zeros_like(s_scr)

    n_rows = n_chunks * chunk
    r = lax.broadcasted_iota(jnp.int32, (n_rows, n_rows), 0)
    c = lax.broadcasted_iota(jnp.int32, (n_rows, n_rows), 1)
    shift = chunk.bit_length() - 1
    assert 1 << shift == chunk, "chunk must be a power of two"
    tril = jnp.where(jnp.logical_and(r >> shift == c >> shift, r >= c), 1.0, 0.0).astype(BF16)
    la_hi, la_mid, la_lo = _split3(la_ref[...])
    b_all = _bdot(tril, la_hi) + _bdot(tril, la_mid) + _bdot(tril, la_lo)
    q_all = (q_ref[...].astype(F32) * scale * jnp.exp(b_all)).astype(BF16)
    k_all = (k_ref[...].astype(F32) * jnp.exp(-b_all)).astype(BF16)
    causal = (lax.broadcasted_iota(jnp.int32, (chunk, chunk), 0) >=
              lax.broadcasted_iota(jnp.int32, (chunk, chunk), 1))

    decays = []
    for ci in range(n_chunks):
        rows = slice(ci * chunk, (ci + 1) * chunk)
        b = b_all[rows, :]
        b_last = b[chunk - 1:chunk, :]
        q_e, k_e = q_all[rows, :], k_all[rows, :]
        k_d = (k_ref[rows, :].astype(F32) * jnp.exp(b_last - b)).astype(BF16)
        decays.append(jnp.exp(b_last))
        for h in range(heads):
            ks = slice(h * dk, (h + 1) * dk)
            v_h = v_ref[rows, h * dv:(h + 1) * dv]
            scores = lax.dot_general(q_e[:, ks], k_e[:, ks], (((1,), (1,)), ((), ())),
                                     preferred_element_type=F32)
            p_scr[rows, h * chunk:(h + 1) * chunk] = jnp.where(causal, scores, 0.0).astype(BF16)
            kv_scr[ci, h] = lax.dot_general(k_d[:, ks], v_h, (((0,), (0,)), ((), ())),
                                            preferred_element_type=F32)

    for ci in range(n_chunks):
        rows = slice(ci * chunk, (ci + 1) * chunk)
        seq = ci // chunks_per_seq
        for h in range(heads):
            ks = slice(h * dk, (h + 1) * dk)
            vs = slice(h * dv, (h + 1) * dv)
            s_old = s_scr[seq, h]
            o_scr[rows, vs] = (_bdot(p_scr[rows, h * chunk:(h + 1) * chunk], v_ref[rows, vs]) +
                               _bdot(q_all[rows, ks], s_old.astype(BF16)))
            s_scr[seq, h] = jnp.transpose(decays[ci][:, ks]) * s_old + kv_scr[ci, h]

    for h in range(heads):
        vs = slice(h * dv, (h + 1) * dv)
        o = o_scr[:, vs]
        mu = jnp.mean(o, axis=-1, keepdims=True)
        oc = o - mu
        var = jnp.mean(oc * oc, axis=-1, keepdims=True)
        on = oc * lax.rsqrt(var + LN_EPS) * ng_ref[:, vs]
        g = g_ref[:, vs].astype(F32)
        o_ref[:, vs] = (on * (g * _sigmoid(g))).astype(BF16)

    @pl.when(t == pl.num_programs(1) - 1)
    def _():
        if state_slices == 0:
            so_ref[...] = s_scr[...]
        for layer_slice in range(state_slices):
            so_ref[layer_slice] = s_scr[...] if layer_slice == state_slot else jnp.zeros_like(s_scr)


def _gla(qkvg, loga, norm_g, s0, layer, nb, s_len, rows, chunk, heads, dk, dv, stack=None):
    t = nb * s_len
    seqs = max(1, rows // s_len)
    assert s_len * seqs % rows == 0 and nb % seqs == 0 and s_len % chunk == 0
    nb, tps = nb // seqs, s_len * seqs // rows
    kw, vw = heads * dk, heads * dv
    has_state = s0 is not None
    state_shape, state_block, state_index = (nb * seqs, heads, dk, dv), (seqs, heads, dk, dv), lambda b, i: (b, 0, 0, 0)
    state_slices, state_slot, aliases = 0, 0, {}
    if stack is not None:
        depth, prev = stack
        state_shape = (depth,) + state_shape
        if prev is None:
            state_slices, state_slot = depth, layer
            state_block, state_index = (depth,) + state_block, lambda b, i: (0, b, 0, 0, 0)
        else:
            state_slices = 1
            state_block, state_index = (1,) + state_block, lambda b, i: (layer, b, 0, 0, 0)
    in_specs = [pl.BlockSpec((rows, kw), lambda b, i: (b * tps + i, 0)),
                pl.BlockSpec((rows, kw), lambda b, i: (b * tps + i, 1)),
                pl.BlockSpec((rows, vw), lambda b, i: (b * tps + i, (2 * kw) // vw)),
                pl.BlockSpec((rows, kw), lambda b, i: (b * tps + i, 0)),
                pl.BlockSpec((rows, vw), lambda b, i: (b * tps + i, (2 * kw) // vw + 1)),
                _const_spec(norm_g.shape)]
    args = [qkvg, qkvg, qkvg, loga, qkvg, norm_g]
    if has_state:
        in_specs.append(pl.BlockSpec((1, seqs, heads, dk, dv), lambda b, i: (layer, b, 0, 0, 0)))
        args.append(s0)
    if stack is not None and stack[1] is not None:
        in_specs.append(pl.BlockSpec(memory_space=pl.ANY))
        aliases[len(args)] = 1
        args.append(stack[1])
    return pl.pallas_call(
        functools.partial(_gla_kernel, chunk=chunk, n_chunks=rows // chunk, heads=heads, has_state=has_state,
                          n_inputs=len(args), state_slices=state_slices, state_slot=state_slot),
        grid=(nb, tps),
        in_specs=in_specs,
        out_specs=[pl.BlockSpec((rows, vw), lambda b, i: (b * tps + i, 0)),
                   pl.BlockSpec(state_block, state_index)],
        out_shape=[jax.ShapeDtypeStruct((t, vw), BF16), jax.ShapeDtypeStruct(state_shape, F32)],
        scratch_shapes=[pltpu.VMEM((seqs, heads, dk, dv), F32), pltpu.VMEM((rows, vw), F32),
                        pltpu.VMEM((rows, heads * chunk), BF16), pltpu.VMEM((rows // chunk, heads, dk, dv), F32)],
        input_output_aliases=aliases,
        compiler_params=_params("arbitrary", "arbitrary"),
        name="gla",
    )(*args)


def _rg_gates(xc, wax_ref, ba_ref, bx_ref, lam_ref):
    n_blocks, bw, _ = wax_ref.shape
    r_parts, i_parts = [], []
    for n in range(n_blocks):
        cs = slice(n * bw, (n + 1) * bw)
        pre = _bdot(xc[:, cs].astype(BF16), wax_ref[n])
        r_parts.append(jnp.tanh(pre[:, :bw] + ba_ref[:, cs]))
        i_parts.append(jnp.tanh(pre[:, bw:] + bx_ref[:, cs]))
    tanh_r = jnp.concatenate(r_parts, axis=1)
    tanh_i = jnp.concatenate(i_parts, axis=1)
    half_rate = (-0.5 * RG_C) * _softplus(-lam_ref[...])
    log_a = tanh_r * half_rate + half_rate
    a = jnp.exp(log_a)
    u = jnp.tanh(-log_a) * (1.0 + a * a)
    root = jnp.where(u > 0.0, u * lax.rsqrt(u), 0.0)
    bx = root * ((0.5 * tanh_i + 0.5) * xc)
    return a, bx


def _rglru_seq_kernel(xr_ref, cw_ref, cb_ref, wax_ref, ba_ref, bx_ref, lam_ref,
                      h_ref, hl_ref, cv_ref, cbuf, a_scr, b_scr, h_scr, hc):
    t = pl.program_id(1)
    rows = xr_ref.shape[0]
    taps = cw_ref.shape[0]
    head = cbuf.shape[0]
    assert taps - 1 <= head

    @pl.when(t == 0)
    def _():
        cbuf[...] = jnp.zeros_like(cbuf)
        hc[...] = jnp.zeros_like(hc)

    x = xr_ref[...]
    prev = cbuf[...]
    in_head = lax.broadcasted_iota(jnp.int32, prev.shape, 0)
    xc = cb_ref[...] + x * cw_ref[taps - 1:taps, :]
    for back in range(1, taps):
        rolled = pltpu.roll(x, back, 0)
        first_rows = jnp.where(in_head < back, pltpu.roll(prev, back, 0), rolled[:head])
        xc = xc + jnp.concatenate([first_rows, rolled[head:]], axis=0) * cw_ref[taps - 1 - back:taps - back, :]
    cbuf[...] = x[rows - head:, :]
    tail = x[rows - (taps - 1):, :]

    a, bx = _rg_gates(xc, wax_ref, ba_ref, bx_ref, lam_ref)
    a_scr[...] = a
    b_scr[...] = bx

    def step(i, h):
        h = a_scr[pl.ds(i, 1), :] * h + b_scr[pl.ds(i, 1), :]
        h_scr[pl.ds(i, 1), :] = h
        return h

    h_last = lax.fori_loop(0, rows, step, hc[...], unroll=16)
    hc[...] = h_last
    h_ref[...] = h_scr[...].astype(BF16)

    @pl.when(t == pl.num_programs(1) - 1)
    def _():
        hl_ref[0] = h_last
        cv_ref[0] = tail


def _rglru_seq(xr, w, nb, s_len, rows):
    width = xr.shape[1]
    t = nb * s_len
    tps = s_len // rows
    taps = w["cw"].shape[0]
    consts = [w["cw"], w["cb"], w["wax"], w["rba"], w["rbx"], w["lam"]]
    return pl.pallas_call(
        _rglru_seq_kernel,
        grid=(nb, tps),
        in_specs=[pl.BlockSpec((rows, width), lambda b, i: (b * tps + i, 0))] + [_const_spec(c.shape) for c in consts],
        out_specs=[pl.BlockSpec((rows, width), lambda b, i: (b * tps + i, 0)),
                   pl.BlockSpec((1, 1, width), lambda b, i: (b, 0, 0)),
                   pl.BlockSpec((1, taps - 1, width), lambda b, i: (b, 0, 0))],
        out_shape=[jax.ShapeDtypeStruct((t, width), BF16),
                   jax.ShapeDtypeStruct((nb, 1, width), F32),
                   jax.ShapeDtypeStruct((nb, taps - 1, width), F32)],
        scratch_shapes=[pltpu.VMEM((8, width), F32), pltpu.VMEM((rows, width), F32),
                        pltpu.VMEM((rows, width), F32), pltpu.VMEM((rows, width), F32),
                        pltpu.VMEM((1, width), F32)],
        compiler_params=_params("arbitrary", "arbitrary"),
        name="rglru_seq",
    )(xr, *consts)


GLA_INPUTS, RG_INPUTS = 6, 7


def _mixers_kernel(*refs, chunk, n_chunks, heads):
    ins, rest = refs[:GLA_INPUTS + RG_INPUTS], refs[GLA_INPUTS + RG_INPUTS:]
    gla_outs, rg_outs, gla_scr, rg_scr = rest[:2], rest[2:5], rest[5:9], rest[9:]
    _gla_kernel(*ins[:GLA_INPUTS], *gla_outs, *gla_scr, chunk=chunk, n_chunks=n_chunks, heads=heads,
                has_state=False, n_inputs=GLA_INPUTS, state_slices=0, state_slot=0)
    _rglru_seq_kernel(*ins[GLA_INPUTS:], *rg_outs, *rg_scr)


def _mixers_seq(qkvg, loga, norm_g, xr, w, nb, s_len, rows, chunk, heads, dk, dv):
    t = nb * s_len
    tps = s_len // rows
    kw, vw = heads * dk, heads * dv
    width = xr.shape[1]
    taps = w["cw"].shape[0]
    tile = lambda cols, col: pl.BlockSpec((rows, cols), lambda b, i: (b * tps + i, col))
    rg_consts = [w["cw"], w["cb"], w["wax"], w["rba"], w["rbx"], w["lam"]]
    in_specs = ([tile(kw, 0), tile(kw, 1), tile(vw, (2 * kw) // vw), tile(kw, 0), tile(vw, (2 * kw) // vw + 1),
                 _const_spec(norm_g.shape), tile(width, 0)] + [_const_spec(c.shape) for c in rg_consts])
    return pl.pallas_call(
        functools.partial(_mixers_kernel, chunk=chunk, n_chunks=rows // chunk, heads=heads),
        grid=(nb, tps),
        in_specs=in_specs,
        out_specs=[tile(vw, 0), pl.BlockSpec((1, heads, dk, dv), lambda b, i: (b, 0, 0, 0)),
                   tile(width, 0), pl.BlockSpec((1, 1, width), lambda b, i: (b, 0, 0)),
                   pl.BlockSpec((1, taps - 1, width), lambda b, i: (b, 0, 0))],
        out_shape=[jax.ShapeDtypeStruct((t, vw), BF16), jax.ShapeDtypeStruct((nb, heads, dk, dv), F32),
                   jax.ShapeDtypeStruct((t, width), BF16), jax.ShapeDtypeStruct((nb, 1, width), F32),
                   jax.ShapeDtypeStruct((nb, taps - 1, width), F32)],
        scratch_shapes=[pltpu.VMEM((1, heads, dk, dv), F32), pltpu.VMEM((rows, vw), F32),
                        pltpu.VMEM((rows, heads * chunk), BF16), pltpu.VMEM((rows // chunk, heads, dk, dv), F32),
                        pltpu.VMEM((8, width), F32), pltpu.VMEM((rows, width), F32),
                        pltpu.VMEM((rows, width), F32), pltpu.VMEM((rows, width), F32),
                        pltpu.VMEM((1, width), F32)],
        compiler_params=_params("arbitrary", "arbitrary"),
        name="mixers",
    )(qkvg, qkvg, qkvg, loga, qkvg, norm_g, xr, *rg_consts)


def _rglru_step_kernel(xr_ref, sc_ref, h0_ref, cw_ref, cb_ref, wax_ref, ba_ref, bx_ref, lam_ref,
                       h_ref, hn_ref, cv_ref):
    taps = cw_ref.shape[0]
    xr = xr_ref[...]
    xc = (cb_ref[...] + sum(sc_ref[:, j, :] * cw_ref[j:j + 1, :] for j in range(taps - 1)) +
          xr * cw_ref[taps - 1:taps, :])
    a, bx = _rg_gates(xc, wax_ref, ba_ref, bx_ref, lam_ref)
    h = a * h0_ref[...] + bx
    h_ref[...] = h.astype(BF16)
    hn_ref[...] = h
    for j in range(taps - 2):
        cv_ref[:, j, :] = sc_ref[:, j + 1, :]
    cv_ref[:, taps - 2, :] = xr


def _rglru_step(xr, conv, h0, w):
    n, width = xr.shape
    consts = [w["cw"], w["cb"], w["wax"], w["rba"], w["rbx"], w["lam"]]
    args = [xr, conv, h0] + consts
    return pl.pallas_call(
        _rglru_step_kernel,
        grid=(1,),
        in_specs=[_const_spec(a.shape) for a in args],
        out_specs=[_const_spec((n, width)), _const_spec((n, width)), _const_spec(conv.shape)],
        out_shape=[jax.ShapeDtypeStruct((n, width), BF16),
                   jax.ShapeDtypeStruct((n, width), F32),
                   jax.ShapeDtypeStruct(conv.shape, F32)],
        compiler_params=_params("arbitrary"),
        name="rglru_step",
    )(*args)


def _pack_bf16_pairs(x):
    half = x.shape[1] // 2
    hi = pltpu.bitcast(x[:, :half].astype(BF16).astype(F32), jnp.uint32)
    lo = pltpu.bitcast(x[:, half:].astype(BF16).astype(F32), jnp.uint32)
    return hi | (lo >> 16)


def _unpack_bf16_pairs(p):
    hi = pltpu.bitcast(p & jnp.uint32(0xFFFF0000), F32)
    lo = pltpu.bitcast(p << 16, F32)
    return jnp.concatenate([hi, lo], axis=1)


def _post_kernel(*refs, n_x, alpha, n_experts, n_tail_real):
    x_refs = refs[:n_x]
    (o_ref, ot_ref, h_ref, ht_ref, gate_ref, wb0_ref, wb1_ref, wo_ref, g1_ref, b1_ref, rw_ref, rb_ref,
     x1_ref, x1p_ref, idx_ref, wt_ref, cnt_ref, cnt_scr) = refs[n_x:]
    tm, d = o_ref.shape

    @pl.when(pl.program_id(0) == 0)
    def _():
        cnt_scr[...] = jnp.zeros_like(cnt_scr)

    x = _token_tile(x_refs)
    o = _head_or_tail(o_ref, ot_ref)
    h = _head_or_tail(h_ref, ht_ref)
    gate = gate_ref[...].astype(F32)
    merged = gate[:, :d] * _bdot(o, wb0_ref[...]) + gate[:, d:] * _bdot(h, wb1_ref[...])
    mix = _bdot(merged.astype(BF16), wo_ref[...])
    x1 = _layernorm(alpha * x + mix, g1_ref[...], b1_ref[...])
    x1_ref[...] = x1
    x1p_ref[...] = _pack_bf16_pairs(x1)

    xh = x1.astype(BF16)
    xl = (x1 - xh.astype(F32)).astype(BF16)
    n_lanes = rw_ref.shape[1] // 2
    hi_terms = _bdot(xh, rw_ref[...])
    logits = (hi_terms[:, n_lanes:] + _bdot(xl, rw_ref[:, :n_lanes])) + hi_terms[:, :n_lanes]
    logits = logits + rb_ref[...]
    lane = lax.broadcasted_iota(jnp.int32, logits.shape, 1)
    lane_f = lane.astype(F32)
    neg_inf = jnp.float32(-jnp.inf)
    cur = jnp.where(lane < n_experts, logits, neg_inf)
    vals, idxs = [], []
    for _ in range(TOP_K):
        m = jnp.max(cur, axis=-1, keepdims=True)
        sel = jnp.min(jnp.where(cur == m, lane_f, float(LANES)), axis=-1, keepdims=True)
        vals.append(m)
        idxs.append(sel)
        cur = jnp.where(lane_f == sel, neg_inf, cur)
    exps = [jnp.exp(v - vals[0]) for v in vals]
    total = sum(exps)

    onehots = [jnp.where(lane_f == idxs[j], 1.0, 0.0) for j in range(TOP_K)]
    chosen = sum(onehots)
    if n_tail_real < tm:
        is_fill = jnp.logical_and(pl.program_id(0) == pl.num_programs(0) - 1,
                                  lax.broadcasted_iota(jnp.int32, chosen.shape, 0) >= n_tail_real)
        chosen = jnp.where(is_fill, 0.0, chosen)
    r = lax.broadcasted_iota(jnp.int32, (tm, tm), 0)
    c = lax.broadcasted_iota(jnp.int32, (tm, tm), 1)
    earlier = jnp.where(r > c, 1.0, 0.0).astype(BF16)
    before = _bdot(earlier, chosen.astype(BF16)) + cnt_scr[...]
    ranks = [jnp.sum(onehots[j] * before, axis=-1, keepdims=True) for j in range(TOP_K)]
    cnt_scr[...] += jnp.sum(chosen, axis=0, keepdims=True)
    cnt_ref[...] = cnt_scr[...]

    idx_out = jnp.zeros(logits.shape, F32)
    wt_out = jnp.zeros(logits.shape, F32)
    for j in range(TOP_K):
        idx_out = jnp.where(lane == j, idxs[j], idx_out)
        idx_out = jnp.where(lane == TOP_K + j, ranks[j], idx_out)
        wt_out = jnp.where(lane == j, exps[j] / total, wt_out)
    idx_ref[...] = idx_out.astype(jnp.int32)
    wt_ref[...] = wt_out


def _post(x_head, x_tail, n_head_tiles, tail_tile, o, h, o_tail, h_tail, gate, w, tm, alpha, n_experts,
          n_tail_real):
    d = x_head.shape[1]
    t = (n_head_tiles + 1) * tm
    assert o.shape[0] == t - tm and o_tail.shape[0] == tm and gate.shape[0] == t
    consts = [w["wb0"], w["wb1"], w["wo"], w["ln1g"], w["ln1b"], w["rw"], w["rb"]]
    row = lambda width: pl.BlockSpec((tm, width), lambda i: (i, 0))
    xs, x_specs = _token_inputs(x_head, x_tail, n_head_tiles, tail_tile, tm)
    return pl.pallas_call(
        functools.partial(_post_kernel, n_x=len(xs), alpha=alpha, n_experts=n_experts, n_tail_real=n_tail_real),
        grid=(t // tm,),
        in_specs=x_specs + _head_tail_specs(n_head_tiles, 0, tm, d) + _head_tail_specs(n_head_tiles, 0, tm, d) +
                 [row(2 * d)] + [_weight_spec(c.shape) for c in consts],
        out_specs=[row(d), row(d // 2), row(LANES), row(LANES), _const_spec((1, LANES))],
        out_shape=[jax.ShapeDtypeStruct((t, d), F32),
                   jax.ShapeDtypeStruct((t, d // 2), jnp.uint32),
                   jax.ShapeDtypeStruct((t, LANES), jnp.int32),
                   jax.ShapeDtypeStruct((t, LANES), F32),
                   jax.ShapeDtypeStruct((1, LANES), F32)],
        scratch_shapes=[pltpu.VMEM((1, LANES), F32)],
        compiler_params=_params("arbitrary"),
        name="post",
    )(*xs, o, o_tail, h, h_tail, gate, *consts)


SC_INDEX_MAX = 128
SC_ALIGN = 8
SC_BUFFER_BYTES = 208 * 1024


def _sc_plan(n_rows, row_bytes):
    info = plsc.get_sparse_core_info()
    n_workers = info.num_cores * info.num_subcores
    per_worker, rem = divmod(n_rows, n_workers)
    assert rem == 0 and per_worker % SC_ALIGN == 0, "rows must split into aligned equal shares per subcore"
    cap = min(SC_INDEX_MAX, SC_BUFFER_BYTES // row_bytes)
    chunk = max(c for c in range(SC_ALIGN, cap + 1, SC_ALIGN) if per_worker % c == 0)
    return info.num_cores, n_workers, per_worker, chunk


def _sc_gather(table, idx):
    n_rows = idx.shape[0]
    d = table.shape[1]
    n_cores, _, per_worker, chunk = _sc_plan(n_rows, d * table.dtype.itemsize)
    mesh = plsc.VectorSubcoreMesh(core_axis_name="c", subcore_axis_name="s")

    n_chunks = per_worker // chunk

    @functools.partial(
        pl.kernel, mesh=mesh, out_type=jax.ShapeDtypeStruct((n_rows, d), table.dtype),
        scratch_types=[pltpu.VMEM((chunk,), jnp.int32)] * 2 + [pltpu.VMEM((chunk, d), table.dtype)] * 2 +
                      [pltpu.SemaphoreType.DMA] * 4)
    def gather(table_hbm, idx_hbm, out_hbm, idx0, idx1, rows0, rows1, g0, g1, w0, w1):
        idx_v, rows_v, gsem, wsem = (idx0, idx1), (rows0, rows1), (g0, g1), (w0, w1)
        worker = lax.axis_index("s") * n_cores + lax.axis_index("c")
        base = worker * per_worker

        def rows_of(c):
            return pl.ds(pl.multiple_of(base + c * chunk, SC_ALIGN), chunk)

        def start_gather(c):
            s = c % 2
            pltpu.sync_copy(idx_hbm.at[rows_of(c)], idx_v[s])
            return pltpu.async_copy(table_hbm.at[idx_v[s]], rows_v[s], gsem[s])

        gathers = {c: start_gather(c) for c in range(min(2, n_chunks))}
        writes = {}
        for c in range(n_chunks):
            s = c % 2
            gathers[c].wait()
            writes[c] = pltpu.async_copy(rows_v[s], out_hbm.at[rows_of(c)], wsem[s])
            if c + 2 < n_chunks:
                writes.pop(c).wait()
                gathers[c + 2] = start_gather(c + 2)
        for write in writes.values():
            write.wait()

    return gather(table, idx)


def _sc_scatter(x, pos, n_out):
    n, d = x.shape
    assert pos.shape == (TOP_K * n,)
    n_cores, _, per_worker, chunk = _sc_plan(n, d * x.dtype.itemsize)
    mesh = plsc.VectorSubcoreMesh(core_axis_name="c", subcore_axis_name="s")

    n_chunks = per_worker // chunk

    @functools.partial(
        pl.kernel, mesh=mesh, out_type=jax.ShapeDtypeStruct((n_out, d), x.dtype),
        scratch_types=[pltpu.VMEM((chunk, d), x.dtype)] * 2 + [pltpu.VMEM((chunk,), jnp.int32)] * TOP_K +
                      [pltpu.SemaphoreType.DMA] * 3)
    def scatter(x_hbm, pos_hbm, out_hbm, rows0, rows1, *rest):
        rows_v, idx_v, (l0, l1, ssem) = (rows0, rows1), rest[:TOP_K], rest[TOP_K:]
        lsem = (l0, l1)
        worker = lax.axis_index("s") * n_cores + lax.axis_index("c")
        base = worker * per_worker

        def first_row(c):
            return pl.multiple_of(base + c * chunk, SC_ALIGN)

        def start_load(c):
            return pltpu.async_copy(x_hbm.at[pl.ds(first_row(c), chunk)], rows_v[c % 2], lsem[c % 2])

        load = start_load(0)
        for c in range(n_chunks):
            next_load = start_load(c + 1) if c + 1 < n_chunks else None
            for k in range(TOP_K):
                pltpu.sync_copy(pos_hbm.at[pl.ds(k * n + first_row(c), chunk)], idx_v[k])
            load.wait()
            scatters = [pltpu.async_copy(rows_v[c % 2], out_hbm.at[idx_v[k]], ssem)
                        for k in range(TOP_K)]
            for scatter_k in scatters:
                scatter_k.wait()
            load = next_load

    return scatter(x, pos)


def _ffn_kernel(be_ref, valid_ref, x_ref, wgu_ref, bgu_ref, wd_ref, bd_ref, o_ref, wgu_bf, wd_bf):
    i = pl.program_id(0)
    bm = x_ref.shape[0]
    f = wd_ref.shape[2]
    valid = valid_ref[i]
    new_expert = jnp.logical_or(i == 0, be_ref[i] != be_ref[jnp.maximum(i - 1, 0)])

    @pl.when(new_expert)
    def _():
        wgu_bf[...] = wgu_ref[0, 0].astype(BF16)
        wd_bf[...] = wd_ref[0, 0].astype(BF16)

    def expert_rows(rows):
        x = _unpack_bf16_pairs(x_ref[rows, :]).astype(BF16)
        acc = jnp.zeros((x.shape[0], wd_ref.shape[3]), F32)
        for c in range(0, f, FFN_COLS):
            gate = _bdot(x, wgu_bf[:, c:c + FFN_COLS]) + bgu_ref[0, 0, :, c:c + FFN_COLS]
            up = _bdot(x, wgu_bf[:, f + c:f + c + FFN_COLS]) + bgu_ref[0, 0, :, f + c:f + c + FFN_COLS]
            gate = jnp.minimum(gate, SWIGLU_LIMIT)
            up = jnp.clip(up, -SWIGLU_LIMIT, SWIGLU_LIMIT)
            act = (up + 1.0) * gate * _sigmoid(SWIGLU_ALPHA * gate)
            acc = acc + _bdot(act.astype(BF16), wd_bf[c:c + FFN_COLS, :])
        o_ref[rows, :] = _pack_bf16_pairs(acc + bd_ref[0, 0])

    @pl.when(valid == bm)
    def _():
        expert_rows(slice(0, bm))

    @pl.when(valid < bm)
    def _():
        for s in range(0, bm, FFN_SUB_ROWS):
            rows = slice(s, s + FFN_SUB_ROWS)

            @pl.when(s < valid)
            def _():
                expert_rows(rows)

            @pl.when(s >= valid)
            def _():
                o_ref[rows, :] = jnp.zeros((FFN_SUB_ROWS, o_ref.shape[1]), o_ref.dtype)


def _ffn(block_e, valid, xs, layer, w_gu, b_gu, w_down, b_down, bm):
    n_layers, n_exp, d, f2 = w_gu.shape
    f = f2 // 2
    n_blocks = block_e.shape[0]
    expert = lambda i, be, nu: (layer, be[i], 0, 0)
    grid_spec = pltpu.PrefetchScalarGridSpec(
        num_scalar_prefetch=2,
        grid=(n_blocks,),
        in_specs=[pl.BlockSpec((bm, d // 2), lambda i, be, nu: (i, 0)),
                  pl.BlockSpec((1, 1, d, f2), expert),
                  pl.BlockSpec((1, 1, 1, f2), expert),
                  pl.BlockSpec((1, 1, f, d), expert),
                  pl.BlockSpec((1, 1, 1, d), expert)],
        out_specs=pl.BlockSpec((bm, d // 2), lambda i, be, nu: (i, 0)),
        scratch_shapes=[pltpu.VMEM((d, f2), BF16), pltpu.VMEM((f, d), BF16)],
    )
    return pl.pallas_call(
        _ffn_kernel,
        grid_spec=grid_spec,
        out_shape=jax.ShapeDtypeStruct((n_blocks * bm, d // 2), jnp.uint32),
        compiler_params=_params("arbitrary"),
        name="ffn",
    )(block_e, valid, xs, w_gu, b_gu.reshape(n_layers, n_exp, 1, f2),
      w_down, b_down.reshape(n_layers, n_exp, 1, d))


def _combine_kernel(x_ref, *refs, alpha, split):
    y_refs, (wt_ref, g_ref, b_ref), out_refs = refs[:TOP_K], refs[TOP_K:TOP_K + 3], refs[TOP_K + 3:]
    wt = wt_ref[...]
    ffn = sum(wt[:, j:j + 1] * _unpack_bf16_pairs(y_refs[j][...]) for j in range(TOP_K))
    res = _layernorm(alpha * x_ref[...] + ffn, g_ref[...], b_ref[...])
    if not split:
        out_refs[0][...] = res
    else:
        is_tail = pl.program_id(0) == pl.num_programs(0) - 1

        @pl.when(jnp.logical_not(is_tail))
        def _():
            out_refs[0][...] = res

        @pl.when(is_tail)
        def _():
            out_refs[1][...] = res


def _combine(x1, y, wt, g, b, tm, alpha, split=False):
    t, d = x1.shape
    n_tiles = t // tm
    planes = [pl.BlockSpec((tm, d // 2), lambda i, j=j: (j * n_tiles + i, 0)) for j in range(TOP_K)]
    if split:
        out_specs = [pl.BlockSpec((tm, d), lambda i: (jnp.minimum(i, n_tiles - 2), 0)), _const_spec((tm, d))]
        out_shape = [jax.ShapeDtypeStruct((t - tm, d), F32), jax.ShapeDtypeStruct((tm, d), F32)]
    else:
        out_specs = pl.BlockSpec((tm, d), lambda i: (i, 0))
        out_shape = jax.ShapeDtypeStruct((t, d), F32)
    return pl.pallas_call(
        functools.partial(_combine_kernel, alpha=alpha, split=split),
        grid=(n_tiles,),
        in_specs=[pl.BlockSpec((tm, d), lambda i: (i, 0))] + planes +
                 [pl.BlockSpec((tm, LANES), lambda i: (i, 0)), _const_spec(g.shape), _const_spec(b.shape)],
        out_specs=out_specs,
        out_shape=out_shape,
        compiler_params=_params("arbitrary"),
        name="combine",
    )(x1, *([y] * TOP_K), wt, g, b)


def _combine_proj_kernel(x_ref, *refs, alpha):
    y_refs, (wt_ref, g_ref, b_ref) = refs[:TOP_K], refs[TOP_K:TOP_K + 3]
    proj_weights, (x2_ref, *proj_outs) = refs[TOP_K + 3:TOP_K + 10], refs[TOP_K + 10:]
    wt = wt_ref[...]
    ffn = sum(wt[:, j:j + 1] * _unpack_bf16_pairs(y_refs[j][...]) for j in range(TOP_K))
    x2 = _layernorm(alpha * x_ref[...] + ffn, g_ref[...], b_ref[...])
    x2_ref[...] = x2
    _proj_tile(x2.astype(BF16), *proj_weights, *proj_outs)


def _combine_proj(x1, y, wt, g, b, w_next, tm, alpha):
    t, d = x1.shape
    n_tiles = t // tm
    row = lambda width: pl.BlockSpec((tm, width), lambda i: (i, 0))
    planes = [pl.BlockSpec((tm, d // 2), lambda i, j=j: (j * n_tiles + i, 0)) for j in range(TOP_K)]
    consts, w_specs, proj_out_specs, proj_out_shape = _proj_operands(w_next, t, d, tm)
    return pl.pallas_call(
        functools.partial(_combine_proj_kernel, alpha=alpha),
        grid=(n_tiles,),
        in_specs=[row(d)] + planes + [row(LANES), _const_spec(g.shape), _const_spec(b.shape)] + w_specs,
        out_specs=[row(d)] + proj_out_specs,
        out_shape=[jax.ShapeDtypeStruct((t, d), F32)] + proj_out_shape,
        compiler_params=_params("arbitrary"),
        name="combine_proj",
    )(x1, *([y] * TOP_K), wt, g, b, *consts)


def _route(experts, ranks, counts, bm):
    n_experts = counts.shape[0]
    i32 = jnp.int32
    padded = (counts + bm - 1) // bm * bm
    ends_pad = jnp.cumsum(padded)
    start_pad = ends_pad - padded
    n_blocks = -(-experts.size // bm) + n_experts
    first_row = jnp.arange(n_blocks, dtype=i32)[:, None] * bm
    block_e = jnp.minimum(jnp.sum(ends_pad[None, :] <= first_row, axis=1), n_experts - 1).astype(i32)
    valid = jnp.clip(counts[block_e] - (first_row[:, 0] - start_pad[block_e]), 0, bm).astype(i32)
    onehot = experts[..., None] == jnp.arange(n_experts, dtype=i32)
    pos = jnp.sum(jnp.where(onehot, start_pad, 0), axis=-1).astype(i32) + ranks
    return block_e, valid, pos


def _moe(x1, x1p, routing, counts, top_w_pad, w, layer, moe_weights, alpha, tm, n_real, w_next):
    n_tok = x1.shape[0]
    n_experts = moe_weights[0].shape[1]
    block_e, valid, pos = _route(routing[:n_real, :TOP_K], routing[:n_real, TOP_K:2 * TOP_K],
                                 counts[0, :n_experts].astype(jnp.int32), MOE_ROWS)
    n_rows = block_e.shape[0] * MOE_ROWS
    n_fill = n_tok - n_real
    spare = n_rows + jnp.arange(n_fill * TOP_K, dtype=jnp.int32).reshape(n_fill, TOP_K)
    xs = _sc_scatter(x1p, jnp.concatenate([pos, spare]).T.reshape(-1), n_rows + n_fill * TOP_K)
    out_rows = _ffn(block_e, valid, xs, layer, *moe_weights, MOE_ROWS)
    y = _sc_gather(out_rows, jnp.concatenate([pos, spare - n_rows]).T.reshape(-1))
    if w_next is None:
        return _combine(x1, y, top_w_pad, w["ln2g"], w["ln2b"], tm, alpha, split=True)
    x2, *proj_outs = _combine_proj(x1, y, top_w_pad, w["ln2g"], w["ln2b"], w_next, tm, alpha)
    return x2, proj_outs


def _hi_lo(w):
    hi = w.astype(BF16)
    return jnp.concatenate([hi, (w - hi.astype(F32)).astype(BF16)], axis=1)


def _layer_weights(layer, w_in, gla_w_a2, gla_b_a, gla_norm_g, rg_conv_w, rg_conv_b, rg_w_a, rg_b_a, rg_w_x,
                   rg_b_x, rg_lambda, b_merge, w_branch, w_o, ln1_g, ln1_b, ln2_g, ln2_b, router_w, router_b,
                   kw, vw):
    d = w_in.shape[1]
    rank = gla_w_a2.shape[1]
    width = rg_conv_w.shape[2]
    n_exp = router_w.shape[2]
    c0 = 2 * kw + 2 * vw
    wi = w_in[layer]
    row = lambda v: v.reshape(1, -1)
    return {
        "wq": wi[:, :c0].astype(BF16),
        "wal": jnp.pad(wi[:, c0:c0 + rank], ((0, 0), (0, LANES - rank))).astype(BF16),
        "wa2": jnp.pad(gla_w_a2[layer], ((0, LANES - rank), (0, 0))).astype(BF16),
        "ba": row(gla_b_a[layer]),
        "wxr": wi[:, c0 + rank:c0 + rank + width].astype(BF16),
        "wmg": wi[:, c0 + rank + width:].astype(BF16),
        "bmg": row(b_merge[layer]),
        "ng": row(gla_norm_g[layer]),
        "cw": rg_conv_w[layer], "cb": row(rg_conv_b[layer]),
        "wax": (0.5 * jnp.concatenate([rg_w_a[layer], rg_w_x[layer]], axis=-1)).astype(BF16),
        "rba": row(0.5 * rg_b_a[layer]), "rbx": row(0.5 * rg_b_x[layer]), "lam": row(rg_lambda[layer]),
        "wb0": w_branch[layer, 0].astype(BF16), "wb1": w_branch[layer, 1].astype(BF16),
        "wo": w_o[layer].astype(BF16),
        "ln1g": row(ln1_g[layer]), "ln1b": row(ln1_b[layer]),
        "ln2g": row(ln2_g[layer]), "ln2b": row(ln2_b[layer]),
        "rw": _hi_lo(jnp.pad(router_w[layer], ((0, 0), (0, LANES - n_exp)))),
        "rb": jnp.pad(row(router_b[layer]), ((0, 0), (0, LANES - n_exp))),
    }


def kernel(x_prompt, x_sample, state_gla, state_rglru, state_conv, ln1_g, ln1_b, w_in, gla_w_a2, gla_b_a, gla_norm_g, rg_conv_w, rg_conv_b, rg_w_a, rg_b_a, rg_w_x, rg_b_x, rg_lambda, b_merge, w_branch, w_o, ln2_g, ln2_b, router_w, router_b, moe_w_gu, moe_b_gu, moe_w_down, moe_b_down):
    n_p, seq, d = x_prompt.shape
    n_s, dec_seq, _ = x_sample.shape
    assert dec_seq == 1, "the sample group carries one new token per sequence"
    depth, _, heads, dk, dv = state_gla.shape
    kw, vw = heads * dk, heads * dv
    n_exp = router_w.shape[2]
    alpha = (2.0 * depth) ** 0.25
    t_p = n_p * seq
    pad_rows = BF16_SUBLANES

    tile = PROJ_ROWS
    assert t_p % tile == 0 and n_s <= tile
    n_head_tiles = t_p // tile
    tail_pad = lambda a: jnp.pad(a, ((0, tile - n_s), (0, 0)))
    x_head, x_tail, tail_tile = x_prompt.reshape(t_p, d), tail_pad(x_sample.reshape(n_s, d)), 0
    moe_weights = (moe_w_gu, moe_b_gu, moe_w_down, moe_b_down)
    outs = {k: [] for k in ("gla_p", "rg_p", "cv_p", "rg_s", "cv_s")}
    gla_s = None
    weights = [_layer_weights(layer, w_in, gla_w_a2, gla_b_a, gla_norm_g, rg_conv_w, rg_conv_b, rg_w_a, rg_b_a,
                              rg_w_x, rg_b_x, rg_lambda, b_merge, w_branch, w_o, ln1_g, ln1_b, ln2_g, ln2_b,
                              router_w, router_b, kw, vw) for layer in range(depth)]
    projections = _proj(x_head, x_tail, n_head_tiles, tail_tile, weights[0], tile)
    for layer in range(depth):
        w = weights[layer]
        qkvg, loga, xr, gate = projections
        o_p, s_p, h_p, hl_p, cv_p = _mixers_seq(qkvg, loga, w["ng"], xr, w, n_p, seq, SEQ_ROWS, GLA_CHUNK,
                                                heads, dk, dv)
        pad = lambda a: jnp.pad(a[t_p:t_p + n_s, None, :],
                                ((0, 0), (0, pad_rows - 1), (0, 0))).reshape(n_s * pad_rows, -1)
        o_s, gla_s = _gla(pad(qkvg), pad(loga), w["ng"], state_gla, layer, n_s, pad_rows,
                          DECODE_SEQS * pad_rows, pad_rows, heads, dk, dv, stack=(depth, gla_s))
        o_s = o_s.reshape(n_s, pad_rows, vw)[:, 0]
        h_s, hn_s, cv_s = _rglru_step(xr[t_p:t_p + n_s], state_conv[layer], state_rglru[layer], w)
        x1, x1p, routing, wt, cnt = _post(x_head, x_tail, n_head_tiles, tail_tile, o_p, h_p, tail_pad(o_s),
                                          tail_pad(h_s), gate, w, tile, alpha, n_exp, n_s)
        w_next = weights[layer + 1] if layer + 1 < depth else None
        result = _moe(x1, x1p, routing, cnt, wt, w, layer, moe_weights, alpha, tile, t_p + n_s, w_next)
        if w_next is not None:
            x, projections = result
            x_head, x_tail, tail_tile = x, x, n_head_tiles
        outs["gla_p"].append(s_p)
        outs["rg_p"].append(hl_p.reshape(n_p, -1))
        outs["cv_p"].append(cv_p)
        outs["rg_s"].append(hn_s)
        outs["cv_s"].append(cv_s)
    y_head, y_tail = result
    return (y_head.reshape(n_p, seq, d), y_tail[:n_s].reshape(n_s, dec_seq, d),
            jnp.stack(outs["gla_p"]), jnp.stack(outs["rg_p"]), jnp.stack(outs["cv_p"]),
            gla_s, jnp.stack(outs["rg_s"]), jnp.stack(outs["cv_s"]))
```

```python
import functools

import jax
import jax.numpy as jnp
from jax import lax
from jax.experimental import pallas as pl
from jax.experimental.pallas import tpu as pltpu
from jax.experimental.pallas import tpu_sc as plsc

F32 = jnp.float32
BF16 = jnp.bfloat16

TOP_K = 4
GLA_GATE_TAU = 16.0
GLA_CHUNK = 64
RG_C = 8.0
SWIGLU_LIMIT = 7.0
SWIGLU_ALPHA = 1.702
LN_EPS = 1e-5

LANES = 128
BF16_SUBLANES = 16
VMEM_LIMIT = 56 * 1024 * 1024

PROJ_ROWS = 512
SEQ_ROWS = 256
DECODE_SEQS = 8
MOE_ROWS = 1024
FFN_COLS = 512
FFN_SUB_ROWS = 256


def _params(*sem):
    return pltpu.CompilerParams(dimension_semantics=sem, vmem_limit_bytes=VMEM_LIMIT)


def _const_spec(shape):
    nd = len(shape)
    return pl.BlockSpec(shape, lambda *_: (0,) * nd)


def _weight_spec(shape):
    nd = len(shape)
    return pl.BlockSpec(shape, lambda *_: (0,) * nd, pipeline_mode=pl.Buffered(1))


def _bdot(a, b):
    return jnp.dot(a, b, preferred_element_type=F32)


def _split3(x):
    hi = x.astype(BF16)
    r1 = x - hi.astype(F32)
    mid = r1.astype(BF16)
    lo = (r1 - mid.astype(F32)).astype(BF16)
    return hi, mid, lo


def _sigmoid(x):
    return 0.5 * jnp.tanh(0.5 * x) + 0.5


def _log_sigmoid(x):
    return jnp.minimum(x, 0.0) - jnp.log1p(jnp.exp(-jnp.abs(x)))


def _softplus(x):
    return jnp.maximum(x, 0.0) + jnp.log1p(jnp.exp(-jnp.abs(x)))


def _layernorm(x, g, b):
    mu = jnp.mean(x, axis=-1, keepdims=True)
    xc = x - mu
    var = jnp.mean(xc * xc, axis=-1, keepdims=True)
    return xc * lax.rsqrt(var + LN_EPS) * g + b


def _head_tail_specs(n_head_tiles, tail_tile, tm, d):
    return [pl.BlockSpec((tm, d), lambda i: (jnp.minimum(i, n_head_tiles - 1), 0)),
            pl.BlockSpec((tm, d), lambda i: (tail_tile, 0))]


def _head_or_tail(head_ref, tail_ref):
    is_tail = pl.program_id(0) == pl.num_programs(0) - 1
    return jnp.where(is_tail, tail_ref[...], head_ref[...])


def _token_inputs(x_head, x_tail, n_head_tiles, tail_tile, tm):
    d = x_head.shape[1]
    if x_tail is x_head:
        assert tail_tile == n_head_tiles
        return [x_head], [pl.BlockSpec((tm, d), lambda i: (i, 0))]
    return [x_head, x_tail], _head_tail_specs(n_head_tiles, tail_tile, tm, d)


def _token_tile(x_refs):
    return x_refs[0][...] if len(x_refs) == 1 else _head_or_tail(*x_refs)


def _proj_kernel(*refs, n_x):
    _proj_tile(_token_tile(refs[:n_x]).astype(BF16), *refs[n_x:])


def _proj_tile(xb, wq_ref, wal_ref, wa2_ref, ba_ref, wxr_ref, wmg_ref, bmg_ref, qkvg_ref, loga_ref, xr_ref, gate_ref):
    d = xb.shape[1]
    for c in range(0, qkvg_ref.shape[1], d):
        qkvg_ref[:, c:c + d] = _bdot(xb, wq_ref[:, c:c + d]).astype(BF16)
    a_low = _bdot(xb, wal_ref[...])
    z = _bdot(a_low.astype(BF16), wa2_ref[...]) + ba_ref[...]
    loga_ref[...] = _log_sigmoid(z) * (1.0 / GLA_GATE_TAU)
    xr_ref[...] = _bdot(xb, wxr_ref[...])
    for c in range(0, gate_ref.shape[1], d):
        gate_ref[:, c:c + d] = _sigmoid(_bdot(xb, wmg_ref[:, c:c + d]) + bmg_ref[:, c:c + d]).astype(BF16)


def _proj_operands(w, t, d, tm):
    kw = w["wq"].shape[1]
    nk = w["wa2"].shape[1]
    consts = [w["wq"], w["wal"], w["wa2"], w["ba"], w["wxr"], w["wmg"], w["bmg"]]
    widths = [(kw, BF16), (nk, F32), (d, F32), (2 * d, BF16)]
    return (consts, [_weight_spec(c.shape) for c in consts],
            [pl.BlockSpec((tm, n), lambda i: (i, 0)) for n, _ in widths],
            [jax.ShapeDtypeStruct((t, n), dt) for n, dt in widths])


def _proj(x_head, x_tail, n_head_tiles, tail_tile, w, tm):
    d = x_head.shape[1]
    t = (n_head_tiles + 1) * tm
    consts, w_specs, out_specs, out_shape = _proj_operands(w, t, d, tm)
    xs, x_specs = _token_inputs(x_head, x_tail, n_head_tiles, tail_tile, tm)
    return pl.pallas_call(
        functools.partial(_proj_kernel, n_x=len(xs)),
        grid=(t // tm,),
        in_specs=x_specs + w_specs,
        out_specs=out_specs,
        out_shape=out_shape,
        compiler_params=_params("arbitrary"),
        name="proj",
    )(*xs, *consts)


def _gla_kernel(*refs, chunk, n_chunks, heads, has_state, n_inputs, state_slices, state_slot):
    q_ref, k_ref, v_ref, la_ref, g_ref, ng_ref = refs[:6]
    s0_ref = refs[6] if has_state else None
    o_ref, so_ref, s_scr, o_scr, p_scr, kv_scr = refs[n_inputs:]
    t = pl.program_id(1)
    dk = q_ref.shape[1] // heads
    dv = v_ref.shape[1] // heads
    scale = dk ** -0.5
    chunks_per_seq = n_chunks // s_scr.shape[0]

    @pl.when(t == 0)
    def _():
        if has_state:
            s_scr[...] = s0_ref[0]
        else:
            s_scr[...] = jnp.zeros_like(s_scr)

    n_rows = n_chunks * chunk
    r = lax.broadcasted_iota(jnp.int32, (n_rows, n_rows), 0)
    c = lax.broadcasted_iota(jnp.int32, (n_rows, n_rows), 1)
    shift = chunk.bit_length() - 1
    assert 1 << shift == chunk, "chunk must be a power of two"
    tril = jnp.where(jnp.logical_and(r >> shift == c >> shift, r >= c), 1.0, 0.0).astype(BF16)
    la_hi, la_mid, la_lo = _split3(la_ref[...])
    b_all = _bdot(tril, la_hi) + _bdot(tril, la_mid) + _bdot(tril, la_lo)
    q_all = (q_ref[...].astype(F32) * scale * jnp.exp(b_all)).astype(BF16)
    k_all = (k_ref[...].astype(F32) * jnp.exp(-b_all)).astype(BF16)
    causal = (lax.broadcasted_iota(jnp.int32, (chunk, chunk), 0) >=
              lax.broadcasted_iota(jnp.int32, (chunk, chunk), 1))

    decays = []
    for ci in range(n_chunks):
        rows = slice(ci * chunk, (ci + 1) * chunk)
        b = b_all[rows, :]
        b_last = b[chunk - 1:chunk, :]
        q_e, k_e = q_all[rows, :], k_all[rows, :]
        k_d = (k_ref[rows, :].astype(F32) * jnp.exp(b_last - b)).astype(BF16)
        decays.append(jnp.exp(b_last))
        for h in range(heads):
            ks = slice(h * dk, (h + 1) * dk)
            v_h = v_ref[rows, h * dv:(h + 1) * dv]
            scores = lax.dot_general(q_e[:, ks], k_e[:, ks], (((1,), (1,)), ((), ())),
                                     preferred_element_type=F32)
            p_scr[rows, h * chunk:(h + 1) * chunk] = jnp.where(causal, scores, 0.0).astype(BF16)
            kv_scr[ci, h] = lax.dot_general(k_d[:, ks], v_h, (((0,), (0,)), ((), ())),
                                            preferred_element_type=F32)

    for ci in range(n_chunks):
        rows = slice(ci * chunk, (ci + 1) * chunk)
        seq = ci // chunks_per_seq
        for h in range(heads):
            ks = slice(h * dk, (h + 1) * dk)
            vs = slice(h * dv, (h + 1) * dv)
            s_old = s_scr[seq, h]
            o_scr[rows, vs] = (_bdot(p_scr[rows, h * chunk:(h + 1) * chunk], v_ref[rows, vs]) +
                               _bdot(q_all[rows, ks], s_old.astype(BF16)))
            s_scr[seq, h] = jnp.transpose(decays[ci][:, ks]) * s_old + kv_scr[ci, h]

    for h in range(heads):
        vs = slice(h * dv, (h + 1) * dv)
        o = o_scr[:, vs]
        mu = jnp.mean(o, axis=-1, keepdims=True)
        oc = o - mu
        var = jnp.mean(oc * oc, axis=-1, keepdims=True)
        on = oc * lax.rsqrt(var + LN_EPS) * ng_ref[:, vs]
        g = g_ref[:, vs].astype(F32)
        o_ref[:, vs] = (on * (g * _sigmoid(g))).astype(BF16)

    @pl.when(t == pl.num_programs(1) - 1)
    def _():
        if state_slices == 0:
            so_ref[...] = s_scr[...]
        for layer_slice in range(state_slices):
            so_ref[layer_slice] = s_scr[...] if layer_slice == state_slot else jnp.zeros_like(s_scr)


def _gla(qkvg, loga, norm_g, s0, layer, nb, s_len, rows, chunk, heads, dk, dv, stack=None):
    t = nb * s_len
    seqs = max(1, rows // s_len)
    assert s_len * seqs % rows == 0 and nb % seqs == 0 and s_len % chunk == 0
    nb, tps = nb // seqs, s_len * seqs // rows
    kw, vw = heads * dk, heads * dv
    has_state = s0 is not None
    state_shape, state_block, state_index = (nb * seqs, heads, dk, dv), (seqs, heads, dk, dv), lambda b, i: (b, 0, 0, 0)
    state_slices, state_slot, aliases = 0, 0, {}
    if stack is not None:
        depth, prev = stack
        state_shape = (depth,) + state_shape
        if prev is None:
            state_slices, state_slot = depth, layer
            state_block, state_index = (depth,) + state_block, lambda b, i: (0, b, 0, 0, 0)
        else:
            state_slices = 1
            state_block, state_index = (1,) + state_block, lambda b, i: (layer, b, 0, 0, 0)
    in_specs = [pl.BlockSpec((rows, kw), lambda b, i: (b * tps + i, 0)),
                pl.BlockSpec((rows, kw), lambda b, i: (b * tps + i, 1)),
                pl.BlockSpec((rows, vw), lambda b, i: (b * tps + i, (2 * kw) // vw)),
                pl.BlockSpec((rows, kw), lambda b, i: (b * tps + i, 0)),
                pl.BlockSpec((rows, vw), lambda b, i: (b * tps + i, (2 * kw) // vw + 1)),
                _const_spec(norm_g.shape)]
    args = [qkvg, qkvg, qkvg, loga, qkvg, norm_g]
    if has_state:
        in_specs.append(pl.BlockSpec((1, seqs, heads, dk, dv), lambda b, i: (layer, b, 0, 0, 0)))
        args.append(s0)
    if stack is not None and stack[1] is not None:
        in_specs.append(pl.BlockSpec(memory_space=pl.ANY))
        aliases[len(args)] = 1
        args.append(stack[1])
    return pl.pallas_call(
        functools.partial(_gla_kernel, chunk=chunk, n_chunks=rows // chunk, heads=heads, has_state=has_state,
                          n_inputs=len(args), state_slices=state_slices, state_slot=state_slot),
        grid=(nb, tps),
        in_specs=in_specs,
        out_specs=[pl.BlockSpec((rows, vw), lambda b, i: (b * tps + i, 0)),
                   pl.BlockSpec(state_block, state_index)],
        out_shape=[jax.ShapeDtypeStruct((t, vw), BF16), jax.ShapeDtypeStruct(state_shape, F32)],
        scratch_shapes=[pltpu.VMEM((seqs, heads, dk, dv), F32), pltpu.VMEM((rows, vw), F32),
                        pltpu.VMEM((rows, heads * chunk), BF16), pltpu.VMEM((rows // chunk, heads, dk, dv), F32)],
        input_output_aliases=aliases,
        compiler_params=_params("arbitrary", "arbitrary"),
        name="gla",
    )(*args)


def _rg_gates(xc, wax_ref, ba_ref, bx_ref, lam_ref):
    n_blocks, bw, _ = wax_ref.shape
    r_parts, i_parts = [], []
    for n in range(n_blocks):
        cs = slice(n * bw, (n + 1) * bw)
        pre = _bdot(xc[:, cs].astype(BF16), wax_ref[n])
        r_parts.append(jnp.tanh(pre[:, :bw] + ba_ref[:, cs]))
        i_parts.append(jnp.tanh(pre[:, bw:] + bx_ref[:, cs]))
    tanh_r = jnp.concatenate(r_parts, axis=1)
    tanh_i = jnp.concatenate(i_parts, axis=1)
    half_rate = (-0.5 * RG_C) * _softplus(-lam_ref[...])
    log_a = tanh_r * half_rate + half_rate
    a = jnp.exp(log_a)
    u = jnp.tanh(-log_a) * (1.0 + a * a)
    root = jnp.where(u > 0.0, u * lax.rsqrt(u), 0.0)
    bx = root * ((0.5 * tanh_i + 0.5) * xc)
    return a, bx


def _rglru_seq_kernel(xr_ref, cw_ref, cb_ref, wax_ref, ba_ref, bx_ref, lam_ref,
                      h_ref, hl_ref, cv_ref, cbuf, a_scr, b_scr, h_scr, hc):
    t = pl.program_id(1)
    rows = xr_ref.shape[0]
    taps = cw_ref.shape[0]
    head = cbuf.shape[0]
    assert taps - 1 <= head

    @pl.when(t == 0)
    def _():
        cbuf[...] = jnp.zeros_like(cbuf)
        hc[...] = jnp.zeros_like(hc)

    x = xr_ref[...]
    prev = cbuf[...]
    in_head = lax.broadcasted_iota(jnp.int32, prev.shape, 0)
    xc = cb_ref[...] + x * cw_ref[taps - 1:taps, :]
    for back in range(1, taps):
        rolled = pltpu.roll(x, back, 0)
        first_rows = jnp.where(in_head < back, pltpu.roll(prev, back, 0), rolled[:head])
        xc = xc + jnp.concatenate([first_rows, rolled[head:]], axis=0) * cw_ref[taps - 1 - back:taps - back, :]
    cbuf[...] = x[rows - head:, :]
    tail = x[rows - (taps - 1):, :]

    a, bx = _rg_gates(xc, wax_ref, ba_ref, bx_ref, lam_ref)
    a_scr[...] = a
    b_scr[...] = bx

    def step(i, h):
        h = a_scr[pl.ds(i, 1), :] * h + b_scr[pl.ds(i, 1), :]
        h_scr[pl.ds(i, 1), :] = h
        return h

    h_last = lax.fori_loop(0, rows, step, hc[...], unroll=16)
    hc[...] = h_last
    h_ref[...] = h_scr[...].astype(BF16)

    @pl.when(t == pl.num_programs(1) - 1)
    def _():
        hl_ref[0] = h_last
        cv_ref[0] = tail


def _rglru_seq(xr, w, nb, s_len, rows):
    width = xr.shape[1]
    t = nb * s_len
    tps = s_len // rows
    taps = w["cw"].shape[0]
    consts = [w["cw"], w["cb"], w["wax"], w["rba"], w["rbx"], w["lam"]]
    return pl.pallas_call(
        _rglru_seq_kernel,
        grid=(nb, tps),
        in_specs=[pl.BlockSpec((rows, width), lambda b, i: (b * tps + i, 0))] + [_const_spec(c.shape) for c in consts],
        out_specs=[pl.BlockSpec((rows, width), lambda b, i: (b * tps + i, 0)),
                   pl.BlockSpec((1, 1, width), lambda b, i: (b, 0, 0)),
                   pl.BlockSpec((1, taps - 1, width), lambda b, i: (b, 0, 0))],
        out_shape=[jax.ShapeDtypeStruct((t, width), BF16),
                   jax.ShapeDtypeStruct((nb, 1, width), F32),
                   jax.ShapeDtypeStruct((nb, taps - 1, width), F32)],
        scratch_shapes=[pltpu.VMEM((8, width), F32), pltpu.VMEM((rows, width), F32),
                        pltpu.VMEM((rows, width), F32), pltpu.VMEM((rows, width), F32),
                        pltpu.VMEM((1, width), F32)],
        compiler_params=_params("arbitrary", "arbitrary"),
        name="rglru_seq",
    )(xr, *consts)


GLA_INPUTS, RG_INPUTS = 6, 7


def _mixers_kernel(*refs, chunk, n_chunks, heads):
    ins, rest = refs[:GLA_INPUTS + RG_INPUTS], refs[GLA_INPUTS + RG_INPUTS:]
    gla_outs, rg_outs, gla_scr, rg_scr = rest[:2], rest[2:5], rest[5:9], rest[9:]
    _gla_kernel(*ins[:GLA_INPUTS], *gla_outs, *gla_scr, chunk=chunk, n_chunks=n_chunks, heads=heads,
                has_state=False, n_inputs=GLA_INPUTS, state_slices=0, state_slot=0)
    _rglru_seq_kernel(*ins[GLA_INPUTS:], *rg_outs, *rg_scr)


def _mixers_seq(qkvg, loga, norm_g, xr, w, nb, s_len, rows, chunk, heads, dk, dv):
    t = nb * s_len
    tps = s_len // rows
    kw, vw = heads * dk, heads * dv
    width = xr.shape[1]
    taps = w["cw"].shape[0]
    tile = lambda cols, col: pl.BlockSpec((rows, cols), lambda b, i: (b * tps + i, col))
    rg_consts = [w["cw"], w["cb"], w["wax"], w["rba"], w["rbx"], w["lam"]]
    in_specs = ([tile(kw, 0), tile(kw, 1), tile(vw, (2 * kw) // vw), tile(kw, 0), tile(vw, (2 * kw) // vw + 1),
                 _const_spec(norm_g.shape), tile(width, 0)] + [_const_spec(c.shape) for c in rg_consts])
    return pl.pallas_call(
        functools.partial(_mixers_kernel, chunk=chunk, n_chunks=rows // chunk, heads=heads),
        grid=(nb, tps),
        in_specs=in_specs,
        out_specs=[tile(vw, 0), pl.BlockSpec((1, heads, dk, dv), lambda b, i: (b, 0, 0, 0)),
                   tile(width, 0), pl.BlockSpec((1, 1, width), lambda b, i: (b, 0, 0)),
                   pl.BlockSpec((1, taps - 1, width), lambda b, i: (b, 0, 0))],
        out_shape=[jax.ShapeDtypeStruct((t, vw), BF16), jax.ShapeDtypeStruct((nb, heads, dk, dv), F32),
                   jax.ShapeDtypeStruct((t, width), BF16), jax.ShapeDtypeStruct((nb, 1, width), F32),
                   jax.ShapeDtypeStruct((nb, taps - 1, width), F32)],
        scratch_shapes=[pltpu.VMEM((1, heads, dk, dv), F32), pltpu.VMEM((rows, vw), F32),
                        pltpu.VMEM((rows, heads * chunk), BF16), pltpu.VMEM((rows // chunk, heads, dk, dv), F32),
                        pltpu.VMEM((8, width), F32), pltpu.VMEM((rows, width), F32),
                        pltpu.VMEM((rows, width), F32), pltpu.VMEM((rows, width), F32),
                        pltpu.VMEM((1, width), F32)],
        compiler_params=_params("arbitrary", "arbitrary"),
        name="mixers",
    )(qkvg, qkvg, qkvg, loga, qkvg, norm_g, xr, *rg_consts)


def _rglru_step_kernel(xr_ref, sc_ref, h0_ref, cw_ref, cb_ref, wax_ref, ba_ref, bx_ref, lam_ref,
                       h_ref, hn_ref, cv_ref):
    taps = cw_ref.shape[0]
    xr = xr_ref[...]
    xc = (cb_ref[...] + sum(sc_ref[:, j, :] * cw_ref[j:j + 1, :] for j in range(taps - 1)) +
          xr * cw_ref[taps - 1:taps, :])
    a, bx = _rg_gates(xc, wax_ref, ba_ref, bx_ref, lam_ref)
    h = a * h0_ref[...] + bx
    h_ref[...] = h.astype(BF16)
    hn_ref[...] = h
    for j in range(taps - 2):
        cv_ref[:, j, :] = sc_ref[:, j + 1, :]
    cv_ref[:, taps - 2, :] = xr


def _rglru_step(xr, conv, h0, w):
    n, width = xr.shape
    consts = [w["cw"], w["cb"], w["wax"], w["rba"], w["rbx"], w["lam"]]
    args = [xr, conv, h0] + consts
    return pl.pallas_call(
        _rglru_step_kernel,
        grid=(1,),
        in_specs=[_const_spec(a.shape) for a in args],
        out_specs=[_const_spec((n, width)), _const_spec((n, width)), _const_spec(conv.shape)],
        out_shape=[jax.ShapeDtypeStruct((n, width), BF16),
                   jax.ShapeDtypeStruct((n, width), F32),
                   jax.ShapeDtypeStruct(conv.shape, F32)],
        compiler_params=_params("arbitrary"),
        name="rglru_step",
    )(*args)


def _pack_bf16_pairs(x):
    half = x.shape[1] // 2
    hi = pltpu.bitcast(x[:, :half].astype(BF16).astype(F32), jnp.uint32)
    lo = pltpu.bitcast(x[:, half:].astype(BF16).astype(F32), jnp.uint32)
    return hi | (lo >> 16)


def _unpack_bf16_pairs(p):
    hi = pltpu.bitcast(p & jnp.uint32(0xFFFF0000), F32)
    lo = pltpu.bitcast(p << 16, F32)
    return jnp.concatenate([hi, lo], axis=1)


def _post_kernel(*refs, n_x, alpha, n_experts, n_tail_real):
    x_refs = refs[:n_x]
    (o_ref, ot_ref, h_ref, ht_ref, gate_ref, wb0_ref, wb1_ref, wo_ref, g1_ref, b1_ref, rw_ref, rb_ref,
     x1_ref, x1p_ref, idx_ref, wt_ref, cnt_ref, cnt_scr) = refs[n_x:]
    tm, d = o_ref.shape

    @pl.when(pl.program_id(0) == 0)
    def _():
        cnt_scr[...] = jnp.zeros_like(cnt_scr)

    x = _token_tile(x_refs)
    o = _head_or_tail(o_ref, ot_ref)
    h = _head_or_tail(h_ref, ht_ref)
    gate = gate_ref[...].astype(F32)
    merged = gate[:, :d] * _bdot(o, wb0_ref[...]) + gate[:, d:] * _bdot(h, wb1_ref[...])
    mix = _bdot(merged.astype(BF16), wo_ref[...])
    x1 = _layernorm(alpha * x + mix, g1_ref[...], b1_ref[...])
    x1_ref[...] = x1
    x1p_ref[...] = _pack_bf16_pairs(x1)

    xh = x1.astype(BF16)
    xl = (x1 - xh.astype(F32)).astype(BF16)
    n_lanes = rw_ref.shape[1] // 2
    hi_terms = _bdot(xh, rw_ref[...])
    logits = (hi_terms[:, n_lanes:] + _bdot(xl, rw_ref[:, :n_lanes])) + hi_terms[:, :n_lanes]
    logits = logits + rb_ref[...]
    lane = lax.broadcasted_iota(jnp.int32, logits.shape, 1)
    lane_f = lane.astype(F32)
    neg_inf = jnp.float32(-jnp.inf)
    cur = jnp.where(lane < n_experts, logits, neg_inf)
    vals, idxs = [], []
    for _ in range(TOP_K):
        m = jnp.max(cur, axis=-1, keepdims=True)
        sel = jnp.min(jnp.where(cur == m, lane_f, float(LANES)), axis=-1, keepdims=True)
        vals.append(m)
        idxs.append(sel)
        cur = jnp.where(lane_f == sel, neg_inf, cur)
    exps = [jnp.exp(v - vals[0]) for v in vals]
    total = sum(exps)

    onehots = [jnp.where(lane_f == idxs[j], 1.0, 0.0) for j in range(TOP_K)]
    chosen = sum(onehots)
    if n_tail_real < tm:
        is_fill = jnp.logical_and(pl.program_id(0) == pl.num_programs(0) - 1,
                                  lax.broadcasted_iota(jnp.int32, chosen.shape, 0) >= n_tail_real)
        chosen = jnp.where(is_fill, 0.0, chosen)
    r = lax.broadcasted_iota(jnp.int32, (tm, tm), 0)
    c = lax.broadcasted_iota(jnp.int32, (tm, tm), 1)
    earlier = jnp.where(r > c, 1.0, 0.0).astype(BF16)
    before = _bdot(earlier, chosen.astype(BF16)) + cnt_scr[...]
    ranks = [jnp.sum(onehots[j] * before, axis=-1, keepdims=True) for j in range(TOP_K)]
    cnt_scr[...] += jnp.sum(chosen, axis=0, keepdims=True)
    cnt_ref[...] = cnt_scr[...]

    idx_out = jnp.zeros(logits.shape, F32)
    wt_out = jnp.zeros(logits.shape, F32)
    for j in range(TOP_K):
        idx_out = jnp.where(lane == j, idxs[j], idx_out)
        idx_out = jnp.where(lane == TOP_K + j, ranks[j], idx_out)
        wt_out = jnp.where(lane == j, exps[j] / total, wt_out)
    idx_ref[...] = idx_out.astype(jnp.int32)
    wt_ref[...] = wt_out


def _post(x_head, x_tail, n_head_tiles, tail_tile, o, h, o_tail, h_tail, gate, w, tm, alpha, n_experts,
          n_tail_real):
    d = x_head.shape[1]
    t = (n_head_tiles + 1) * tm
    assert o.shape[0] == t - tm and o_tail.shape[0] == tm and gate.shape[0] == t
    consts = [w["wb0"], w["wb1"], w["wo"], w["ln1g"], w["ln1b"], w["rw"], w["rb"]]
    row = lambda width: pl.BlockSpec((tm, width), lambda i: (i, 0))
    xs, x_specs = _token_inputs(x_head, x_tail, n_head_tiles, tail_tile, tm)
    return pl.pallas_call(
        functools.partial(_post_kernel, n_x=len(xs), alpha=alpha, n_experts=n_experts, n_tail_real=n_tail_real),
        grid=(t // tm,),
        in_specs=x_specs + _head_tail_specs(n_head_tiles, 0, tm, d) + _head_tail_specs(n_head_tiles, 0, tm, d) +
                 [row(2 * d)] + [_weight_spec(c.shape) for c in consts],
        out_specs=[row(d), row(d // 2), row(LANES), row(LANES), _const_spec((1, LANES))],
        out_shape=[jax.ShapeDtypeStruct((t, d), F32),
                   jax.ShapeDtypeStruct((t, d // 2), jnp.uint32),
                   jax.ShapeDtypeStruct((t, LANES), jnp.int32),
                   jax.ShapeDtypeStruct((t, LANES), F32),
                   jax.ShapeDtypeStruct((1, LANES), F32)],
        scratch_shapes=[pltpu.VMEM((1, LANES), F32)],
        compiler_params=_params("arbitrary"),
        name="post",
    )(*xs, o, o_tail, h, h_tail, gate, *consts)


SC_INDEX_MAX = 128
SC_ALIGN = 8
SC_BUFFER_BYTES = 208 * 1024


def _sc_plan(n_rows, row_bytes):
    info = plsc.get_sparse_core_info()
    n_workers = info.num_cores * info.num_subcores
    per_worker, rem = divmod(n_rows, n_workers)
    assert rem == 0 and per_worker % SC_ALIGN == 0, "rows must split into aligned equal shares per subcore"
    cap = min(SC_INDEX_MAX, SC_BUFFER_BYTES // row_bytes)
    chunk = max(c for c in range(SC_ALIGN, cap + 1, SC_ALIGN) if per_worker % c == 0)
    return info.num_cores, n_workers, per_worker, chunk


def _sc_gather(table, idx):
    n_rows = idx.shape[0]
    d = table.shape[1]
    n_cores, _, per_worker, chunk = _sc_plan(n_rows, d * table.dtype.itemsize)
    mesh = plsc.VectorSubcoreMesh(core_axis_name="c", subcore_axis_name="s")

    n_chunks = per_worker // chunk

    @functools.partial(
        pl.kernel, mesh=mesh, out_type=jax.ShapeDtypeStruct((n_rows, d), table.dtype),
        scratch_types=[pltpu.VMEM((chunk,), jnp.int32)] * 2 + [pltpu.VMEM((chunk, d), table.dtype)] * 2 +
                      [pltpu.SemaphoreType.DMA] * 4)
    def gather(table_hbm, idx_hbm, out_hbm, idx0, idx1, rows0, rows1, g0, g1, w0, w1):
        idx_v, rows_v, gsem, wsem = (idx0, idx1), (rows0, rows1), (g0, g1), (w0, w1)
        worker = lax.axis_index("s") * n_cores + lax.axis_index("c")
        base = worker * per_worker

        def rows_of(c):
            return pl.ds(pl.multiple_of(base + c * chunk, SC_ALIGN), chunk)

        def start_gather(c):
            s = c % 2
            pltpu.sync_copy(idx_hbm.at[rows_of(c)], idx_v[s])
            return pltpu.async_copy(table_hbm.at[idx_v[s]], rows_v[s], gsem[s])

        gathers = {c: start_gather(c) for c in range(min(2, n_chunks))}
        writes = {}
        for c in range(n_chunks):
            s = c % 2
            gathers[c].wait()
            writes[c] = pltpu.async_copy(rows_v[s], out_hbm.at[rows_of(c)], wsem[s])
            if c + 2 < n_chunks:
                writes.pop(c).wait()
                gathers[c + 2] = start_gather(c + 2)
        for write in writes.values():
            write.wait()

    return gather(table, idx)


def _sc_scatter(x, pos, n_out):
    n, d = x.shape
    assert pos.shape == (TOP_K * n,)
    n_cores, _, per_worker, chunk = _sc_plan(n, d * x.dtype.itemsize)
    mesh = plsc.VectorSubcoreMesh(core_axis_name="c", subcore_axis_name="s")

    n_chunks = per_worker // chunk

    @functools.partial(
        pl.kernel, mesh=mesh, out_type=jax.ShapeDtypeStruct((n_out, d), x.dtype),
        scratch_types=[pltpu.VMEM((chunk, d), x.dtype)] * 2 + [pltpu.VMEM((chunk,), jnp.int32)] * TOP_K +
                      [pltpu.SemaphoreType.DMA] * 3)
    def scatter(x_hbm, pos_hbm, out_hbm, rows0, rows1, *rest):
        rows_v, idx_v, (l0, l1, ssem) = (rows0, rows1), rest[:TOP_K], rest[TOP_K:]
        lsem = (l0, l1)
        worker = lax.axis_index("s") * n_cores + lax.axis_index("c")
        base = worker * per_worker

        def first_row(c):
            return pl.multiple_of(base + c * chunk, SC_ALIGN)

        def start_load(c):
            return pltpu.async_copy(x_hbm.at[pl.ds(first_row(c), chunk)], rows_v[c % 2], lsem[c % 2])

        load = start_load(0)
        for c in range(n_chunks):
            next_load = start_load(c + 1) if c + 1 < n_chunks else None
            for k in range(TOP_K):
                pltpu.sync_copy(pos_hbm.at[pl.ds(k * n + first_row(c), chunk)], idx_v[k])
            load.wait()
            scatters = [pltpu.async_copy(rows_v[c % 2], out_hbm.at[idx_v[k]], ssem)
                        for k in range(TOP_K)]
            for scatter_k in scatters:
                scatter_k.wait()
            load = next_load

    return scatter(x, pos)


def _ffn_kernel(be_ref, valid_ref, nu_ref, x_ref, wgu_ref, bgu_ref, wd_ref, bd_ref, o_ref, wgu_bf, wd_bf):
    i = pl.program_id(0)
    bm = x_ref.shape[0]
    f = wd_ref.shape[2]
    valid = valid_ref[i]
    new_expert = jnp.logical_or(i == 0, be_ref[i] != be_ref[jnp.maximum(i - 1, 0)])

    @pl.when(new_expert)
    def _():
        wgu_bf[...] = wgu_ref[0, 0].astype(BF16)
        wd_bf[...] = wd_ref[0, 0].astype(BF16)

    def expert_rows(rows):
        x = _unpack_bf16_pairs(x_ref[rows, :]).astype(BF16)
        acc = jnp.zeros((x.shape[0], wd_ref.shape[3]), F32)
        for c in range(0, f, FFN_COLS):
            gate = _bdot(x, wgu_bf[:, c:c + FFN_COLS]) + bgu_ref[0, 0, :, c:c + FFN_COLS]
            up = _bdot(x, wgu_bf[:, f + c:f + c + FFN_COLS]) + bgu_ref[0, 0, :, f + c:f + c + FFN_COLS]
            gate = jnp.minimum(gate, SWIGLU_LIMIT)
            up = jnp.clip(up, -SWIGLU_LIMIT, SWIGLU_LIMIT)
            act = (up + 1.0) * gate * _sigmoid(SWIGLU_ALPHA * gate)
            acc = acc + _bdot(act.astype(BF16), wd_bf[c:c + FFN_COLS, :])
        o_ref[rows, :] = _pack_bf16_pairs(acc + bd_ref[0, 0])

    @pl.when(valid == bm)
    def _():
        expert_rows(slice(0, bm))

    @pl.when(jnp.logical_and(valid < bm, i < nu_ref[0]))
    def _():
        for s in range(0, bm, FFN_SUB_ROWS):
            rows = slice(s, s + FFN_SUB_ROWS)

            @pl.when(s < valid)
            def _():
                expert_rows(rows)

            @pl.when(s >= valid)
            def _():
                o_ref[rows, :] = jnp.zeros((FFN_SUB_ROWS, o_ref.shape[1]), o_ref.dtype)


def _ffn(block_e, valid, xs, layer, w_gu, b_gu, w_down, b_down, bm):
    n_layers, n_exp, d, f2 = w_gu.shape
    f = f2 // 2
    n_blocks = block_e.shape[0]
    n_used = jnp.sum(valid > 0, keepdims=True).astype(jnp.int32)
    expert = lambda i, be, va, nu: (layer, be[i], 0, 0)
    rows = pl.BlockSpec((bm, d // 2), lambda i, be, va, nu: (jnp.minimum(i, nu[0] - 1), 0))
    grid_spec = pltpu.PrefetchScalarGridSpec(
        num_scalar_prefetch=3,
        grid=(n_blocks,),
        in_specs=[rows,
                  pl.BlockSpec((1, 1, d, f2), expert),
                  pl.BlockSpec((1, 1, 1, f2), expert),
                  pl.BlockSpec((1, 1, f, d), expert),
                  pl.BlockSpec((1, 1, 1, d), expert)],
        out_specs=rows,
        scratch_shapes=[pltpu.VMEM((d, f2), BF16), pltpu.VMEM((f, d), BF16)],
    )
    return pl.pallas_call(
        _ffn_kernel,
        grid_spec=grid_spec,
        out_shape=jax.ShapeDtypeStruct((n_blocks * bm, d // 2), jnp.uint32),
        compiler_params=_params("arbitrary"),
        name="ffn",
    )(block_e, valid, n_used, xs, w_gu, b_gu.reshape(n_layers, n_exp, 1, f2),
      w_down, b_down.reshape(n_layers, n_exp, 1, d))


def _combine_kernel(x_ref, *refs, alpha, split):
    y_refs, (wt_ref, g_ref, b_ref), out_refs = refs[:TOP_K], refs[TOP_K:TOP_K + 3], refs[TOP_K + 3:]
    wt = wt_ref[...]
    ffn = sum(wt[:, j:j + 1] * _unpack_bf16_pairs(y_refs[j][...]) for j in range(TOP_K))
    res = _layernorm(alpha * x_ref[...] + ffn, g_ref[...], b_ref[...])
    if not split:
        out_refs[0][...] = res
    else:
        is_tail = pl.program_id(0) == pl.num_programs(0) - 1

        @pl.when(jnp.logical_not(is_tail))
        def _():
            out_refs[0][...] = res

        @pl.when(is_tail)
        def _():
            out_refs[1][...] = res


def _combine(x1, y, wt, g, b, tm, alpha, split=False):
    t, d = x1.shape
    n_tiles = t // tm
    planes = [pl.BlockSpec((tm, d // 2), lambda i, j=j: (j * n_tiles + i, 0)) for j in range(TOP_K)]
    if split:
        out_specs = [pl.BlockSpec((tm, d), lambda i: (jnp.minimum(i, n_tiles - 2), 0)), _const_spec((tm, d))]
        out_shape = [jax.ShapeDtypeStruct((t - tm, d), F32), jax.ShapeDtypeStruct((tm, d), F32)]
    else:
        out_specs = pl.BlockSpec((tm, d), lambda i: (i, 0))
        out_shape = jax.ShapeDtypeStruct((t, d), F32)
    return pl.pallas_call(
        functools.partial(_combine_kernel, alpha=alpha, split=split),
        grid=(n_tiles,),
        in_specs=[pl.BlockSpec((tm, d), lambda i: (i, 0))] + planes +
                 [pl.BlockSpec((tm, LANES), lambda i: (i, 0)), _const_spec(g.shape), _const_spec(b.shape)],
        out_specs=out_specs,
        out_shape=out_shape,
        compiler_params=_params("arbitrary"),
        name="combine",
    )(x1, *([y] * TOP_K), wt, g, b)


def _combine_proj_kernel(x_ref, *refs, alpha):
    y_refs, (wt_ref, g_ref, b_ref) = refs[:TOP_K], refs[TOP_K:TOP_K + 3]
    proj_weights, (x2_ref, *proj_outs) = refs[TOP_K + 3:TOP_K + 10], refs[TOP_K + 10:]
    wt = wt_ref[...]
    ffn = sum(wt[:, j:j + 1] * _unpack_bf16_pairs(y_refs[j][...]) for j in range(TOP_K))
    x2 = _layernorm(alpha * x_ref[...] + ffn, g_ref[...], b_ref[...])
    x2_ref[...] = x2
    _proj_tile(x2.astype(BF16), *proj_weights, *proj_outs)


def _combine_proj(x1, y, wt, g, b, w_next, tm, alpha):
    t, d = x1.shape
    n_tiles = t // tm
    row = lambda width: pl.BlockSpec((tm, width), lambda i: (i, 0))
    planes = [pl.BlockSpec((tm, d // 2), lambda i, j=j: (j * n_tiles + i, 0)) for j in range(TOP_K)]
    consts, w_specs, proj_out_specs, proj_out_shape = _proj_operands(w_next, t, d, tm)
    return pl.pallas_call(
        functools.partial(_combine_proj_kernel, alpha=alpha),
        grid=(n_tiles,),
        in_specs=[row(d)] + planes + [row(LANES), _const_spec(g.shape), _const_spec(b.shape)] + w_specs,
        out_specs=[row(d)] + proj_out_specs,
        out_shape=[jax.ShapeDtypeStruct((t, d), F32)] + proj_out_shape,
        compiler_params=_params("arbitrary"),
        name="combine_proj",
    )(x1, *([y] * TOP_K), wt, g, b, *consts)


def _route(experts, ranks, counts, bm):
    n_experts = counts.shape[0]
    i32 = jnp.int32
    padded = (counts + bm - 1) // bm * bm
    ends_pad = jnp.cumsum(padded)
    start_pad = ends_pad - padded
    n_blocks = -(-experts.size // bm) + n_experts
    first_row = jnp.arange(n_blocks, dtype=i32)[:, None] * bm
    block_e = jnp.minimum(jnp.sum(ends_pad[None, :] <= first_row, axis=1), n_experts - 1).astype(i32)
    valid = jnp.clip(counts[block_e] - (first_row[:, 0] - start_pad[block_e]), 0, bm).astype(i32)
    onehot = experts[..., None] == jnp.arange(n_experts, dtype=i32)
    pos = jnp.sum(jnp.where(onehot, start_pad, 0), axis=-1).astype(i32) + ranks
    return block_e, valid, pos


def _moe(x1, x1p, routing, counts, top_w_pad, w, layer, moe_weights, alpha, tm, n_real, w_next):
    n_tok = x1.shape[0]
    n_experts = moe_weights[0].shape[1]
    block_e, valid, pos = _route(routing[:n_real, :TOP_K], routing[:n_real, TOP_K:2 * TOP_K],
                                 counts[0, :n_experts].astype(jnp.int32), MOE_ROWS)
    n_rows = block_e.shape[0] * MOE_ROWS
    n_fill = n_tok - n_real
    spare = n_rows + jnp.arange(n_fill * TOP_K, dtype=jnp.int32).reshape(n_fill, TOP_K)
    xs = _sc_scatter(x1p, jnp.concatenate([pos, spare]).T.reshape(-1), n_rows + n_fill * TOP_K)
    out_rows = _ffn(block_e, valid, xs, layer, *moe_weights, MOE_ROWS)
    y = _sc_gather(out_rows, jnp.concatenate([pos, spare - n_rows]).T.reshape(-1))
    if w_next is None:
        return _combine(x1, y, top_w_pad, w["ln2g"], w["ln2b"], tm, alpha, split=True)
    x2, *proj_outs = _combine_proj(x1, y, top_w_pad, w["ln2g"], w["ln2b"], w_next, tm, alpha)
    return x2, proj_outs


def _hi_lo(w):
    hi = w.astype(BF16)
    return jnp.concatenate([hi, (w - hi.astype(F32)).astype(BF16)], axis=1)


def _layer_weights(layer, w_in, gla_w_a2, gla_b_a, gla_norm_g, rg_conv_w, rg_conv_b, rg_w_a, rg_b_a, rg_w_x,
                   rg_b_x, rg_lambda, b_merge, w_branch, w_o, ln1_g, ln1_b, ln2_g, ln2_b, router_w, router_b,
                   kw, vw):
    d = w_in.shape[1]
    rank = gla_w_a2.shape[1]
    width = rg_conv_w.shape[2]
    n_exp = router_w.shape[2]
    c0 = 2 * kw + 2 * vw
    wi = w_in[layer]
    row = lambda v: v.reshape(1, -1)
    return {
        "wq": wi[:, :c0].astype(BF16),
        "wal": jnp.pad(wi[:, c0:c0 + rank], ((0, 0), (0, LANES - rank))).astype(BF16),
        "wa2": jnp.pad(gla_w_a2[layer], ((0, LANES - rank), (0, 0))).astype(BF16),
        "ba": row(gla_b_a[layer]),
        "wxr": wi[:, c0 + rank:c0 + rank + width].astype(BF16),
        "wmg": wi[:, c0 + rank + width:].astype(BF16),
        "bmg": row(b_merge[layer]),
        "ng": row(gla_norm_g[layer]),
        "cw": rg_conv_w[layer], "cb": row(rg_conv_b[layer]),
        "wax": (0.5 * jnp.concatenate([rg_w_a[layer], rg_w_x[layer]], axis=-1)).astype(BF16),
        "rba": row(0.5 * rg_b_a[layer]), "rbx": row(0.5 * rg_b_x[layer]), "lam": row(rg_lambda[layer]),
        "wb0": w_branch[layer, 0].astype(BF16), "wb1": w_branch[layer, 1].astype(BF16),
        "wo": w_o[layer].astype(BF16),
        "ln1g": row(ln1_g[layer]), "ln1b": row(ln1_b[layer]),
        "ln2g": row(ln2_g[layer]), "ln2b": row(ln2_b[layer]),
        "rw": _hi_lo(jnp.pad(router_w[layer], ((0, 0), (0, LANES - n_exp)))),
        "rb": jnp.pad(row(router_b[layer]), ((0, 0), (0, LANES - n_exp))),
    }


def kernel(x_prompt, x_sample, state_gla, state_rglru, state_conv, ln1_g, ln1_b, w_in, gla_w_a2, gla_b_a, gla_norm_g, rg_conv_w, rg_conv_b, rg_w_a, rg_b_a, rg_w_x, rg_b_x, rg_lambda, b_merge, w_branch, w_o, ln2_g, ln2_b, router_w, router_b, moe_w_gu, moe_b_gu, moe_w_down, moe_b_down):
    n_p, seq, d = x_prompt.shape
    n_s, dec_seq, _ = x_sample.shape
    assert dec_seq == 1, "the sample group carries one new token per sequence"
    depth, _, heads, dk, dv = state_gla.shape
    kw, vw = heads * dk, heads * dv
    n_exp = router_w.shape[2]
    alpha = (2.0 * depth) ** 0.25
    t_p = n_p * seq
    pad_rows = BF16_SUBLANES

    tile = PROJ_ROWS
    assert t_p % tile == 0 and n_s <= tile
    n_head_tiles = t_p // tile
    tail_pad = lambda a: jnp.pad(a, ((0, tile - n_s), (0, 0)))
    x_head, x_tail, tail_tile = x_prompt.reshape(t_p, d), tail_pad(x_sample.reshape(n_s, d)), 0
    moe_weights = (moe_w_gu, moe_b_gu, moe_w_down, moe_b_down)
    outs = {k: [] for k in ("gla_p", "rg_p", "cv_p", "rg_s", "cv_s")}
    gla_s = None
    weights = [_layer_weights(layer, w_in, gla_w_a2, gla_b_a, gla_norm_g, rg_conv_w, rg_conv_b, rg_w_a, rg_b_a,
                              rg_w_x, rg_b_x, rg_lambda, b_merge, w_branch, w_o, ln1_g, ln1_b, ln2_g, ln2_b,
                              router_w, router_b, kw, vw) for layer in range(depth)]
    projections = _proj(x_head, x_tail, n_head_tiles, tail_tile, weights[0], tile)
    for layer in range(depth):
        w = weights[layer]
        qkvg, loga, xr, gate = projections
        o_p, s_p, h_p, hl_p, cv_p = _mixers_seq(qkvg, loga, w["ng"], xr, w, n_p, seq, SEQ_ROWS, GLA_CHUNK,
                                                heads, dk, dv)
        pad = lambda a: jnp.pad(a[t_p:t_p + n_s, None, :],
                                ((0, 0), (0, pad_rows - 1), (0, 0))).reshape(n_s * pad_rows, -1)
        o_s, gla_s = _gla(pad(qkvg), pad(loga), w["ng"], state_gla, layer, n_s, pad_rows,
                          DECODE_SEQS * pad_rows, pad_rows, heads, dk, dv, stack=(depth, gla_s))
        o_s = o_s.reshape(n_s, pad_rows, vw)[:, 0]
        h_s, hn_s, cv_s = _rglru_step(xr[t_p:t_p + n_s], state_conv[layer], state_rglru[layer], w)
        x1, x1p, routing, wt, cnt = _post(x_head, x_tail, n_head_tiles, tail_tile, o_p, h_p, tail_pad(o_s),
                                          tail_pad(h_s), gate, w, tile, alpha, n_exp, n_s)
        w_next = weights[layer + 1] if layer + 1 < depth else None
        result = _moe(x1, x1p, routing, cnt, wt, w, layer, moe_weights, alpha, tile, t_p + n_s, w_next)
        if w_next is not None:
            x, projections = result
            x_head, x_tail, tail_tile = x, x, n_head_tiles
        outs["gla_p"].append(s_p)
        outs["rg_p"].append(hl_p.reshape(n_p, -1))
        outs["cv_p"].append(cv_p)
        outs["rg_s"].append(hn_s)
        outs["cv_s"].append(cv_s)
    y_head, y_tail = result
    return (y_head.reshape(n_p, seq, d), y_tail[:n_s].reshape(n_s, dec_seq, d),
            jnp.stack(outs["gla_p"]), jnp.stack(outs["rg_p"]), jnp.stack(outs["cv_p"]),
            gla_s, jnp.stack(outs["rg_s"]), jnp.stack(outs["cv_s"]))
```

```python
import functools

import jax
import jax.numpy as jnp
from jax import lax
from jax.experimental import pallas as pl
from jax.experimental.pallas import tpu as pltpu
from jax.experimental.pallas import tpu_sc as plsc

F32 = jnp.float32
BF16 = jnp.bfloat16

TOP_K = 4
GLA_GATE_TAU = 16.0
GLA_CHUNK = 64
RG_C = 8.0
SWIGLU_LIMIT = 7.0
SWIGLU_ALPHA = 1.702
LN_EPS = 1e-5

LANES = 128
BF16_SUBLANES = 16
VMEM_LIMIT = 56 * 1024 * 1024

PROJ_ROWS = 512
SEQ_ROWS = 256
DECODE_SEQS = 8
MOE_ROWS = 1024
FFN_COLS = 512
FFN_SUB_ROWS = 256


def _params(*sem):
    return pltpu.CompilerParams(dimension_semantics=sem, vmem_limit_bytes=VMEM_LIMIT)


def _const_spec(shape):
    nd = len(shape)
    return pl.BlockSpec(shape, lambda *_: (0,) * nd)


def _weight_spec(shape):
    nd = len(shape)
    return pl.BlockSpec(shape, lambda *_: (0,) * nd, pipeline_mode=pl.Buffered(1))


def _bdot(a, b):
    return jnp.dot(a, b, preferred_element_type=F32)


def _split3(x):
    hi = x.astype(BF16)
    r1 = x - hi.astype(F32)
    mid = r1.astype(BF16)
    lo = (r1 - mid.astype(F32)).astype(BF16)
    return hi, mid, lo


def _sigmoid(x):
    return 0.5 * jnp.tanh(0.5 * x) + 0.5


def _log_sigmoid(x):
    return jnp.minimum(x, 0.0) - jnp.log1p(jnp.exp(-jnp.abs(x)))


def _softplus(x):
    return jnp.maximum(x, 0.0) + jnp.log1p(jnp.exp(-jnp.abs(x)))


def _layernorm(x, g, b):
    mu = jnp.mean(x, axis=-1, keepdims=True)
    xc = x - mu
    var = jnp.mean(xc * xc, axis=-1, keepdims=True)
    return xc * lax.rsqrt(var + LN_EPS) * g + b


def _head_tail_specs(n_head_tiles, tail_tile, tm, d):
    return [pl.BlockSpec((tm, d), lambda i: (jnp.minimum(i, n_head_tiles - 1), 0)),
            pl.BlockSpec((tm, d), lambda i: (tail_tile, 0))]


def _head_or_tail(head_ref, tail_ref):
    is_tail = pl.program_id(0) == pl.num_programs(0) - 1
    return jnp.where(is_tail, tail_ref[...], head_ref[...])


def _token_inputs(x_head, x_tail, n_head_tiles, tail_tile, tm):
    d = x_head.shape[1]
    if x_tail is x_head:
        assert tail_tile == n_head_tiles
        return [x_head], [pl.BlockSpec((tm, d), lambda i: (i, 0))]
    return [x_head, x_tail], _head_tail_specs(n_head_tiles, tail_tile, tm, d)


def _token_tile(x_refs):
    return x_refs[0][...] if len(x_refs) == 1 else _head_or_tail(*x_refs)


def _proj_kernel(*refs, n_x):
    _proj_tile(_token_tile(refs[:n_x]).astype(BF16), *refs[n_x:])


def _proj_tile(xb, wq_ref, wal_ref, wa2_ref, ba_ref, wxr_ref, wmg_ref, bmg_ref, qkvg_ref, loga_ref, xr_ref, gate_ref):
    d = xb.shape[1]
    for c in range(0, qkvg_ref.shape[1], d):
        qkvg_ref[:, c:c + d] = _bdot(xb, wq_ref[:, c:c + d]).astype(BF16)
    a_low = _bdot(xb, wal_ref[...])
    z = _bdot(a_low.astype(BF16), wa2_ref[...]) + ba_ref[...]
    loga_ref[...] = _log_sigmoid(z) * (1.0 / GLA_GATE_TAU)
    xr_ref[...] = _bdot(xb, wxr_ref[...])
    for c in range(0, gate_ref.shape[1], d):
        gate_ref[:, c:c + d] = _sigmoid(_bdot(xb, wmg_ref[:, c:c + d]) + bmg_ref[:, c:c + d]).astype(BF16)


def _proj_operands(w, t, d, tm):
    kw = w["wq"].shape[1]
    nk = w["wa2"].shape[1]
    consts = [w["wq"], w["wal"], w["wa2"], w["ba"], w["wxr"], w["wmg"], w["bmg"]]
    widths = [(kw, BF16), (nk, F32), (d, F32), (2 * d, BF16)]
    return (consts, [_weight_spec(c.shape) for c in consts],
            [pl.BlockSpec((tm, n), lambda i: (i, 0)) for n, _ in widths],
            [jax.ShapeDtypeStruct((t, n), dt) for n, dt in widths])


def _proj(x_head, x_tail, n_head_tiles, tail_tile, w, tm):
    d = x_head.shape[1]
    t = (n_head_tiles + 1) * tm
    consts, w_specs, out_specs, out_shape = _proj_operands(w, t, d, tm)
    xs, x_specs = _token_inputs(x_head, x_tail, n_head_tiles, tail_tile, tm)
    return pl.pallas_call(
        functools.partial(_proj_kernel, n_x=len(xs)),
        grid=(t // tm,),
        in_specs=x_specs + w_specs,
        out_specs=out_specs,
        out_shape=out_shape,
        compiler_params=_params("arbitrary"),
        name="proj",
    )(*xs, *consts)


def _gla_kernel(*refs, chunk, n_chunks, heads, has_state, n_inputs, state_slices, state_slot):
    q_ref, k_ref, v_ref, la_ref, g_ref, ng_ref = refs[:6]
    s0_ref = refs[6] if has_state else None
    o_ref, so_ref, s_scr, o_scr, p_scr, kv_scr = refs[n_inputs:]
    t = pl.program_id(1)
    dk = q_ref.shape[1] // heads
    dv = v_ref.shape[1] // heads
    scale = dk ** -0.5
    chunks_per_seq = n_chunks // s_scr.shape[0]

    @pl.when(t == 0)
    def _():
        if has_state:
            s_scr[...] = s0_ref[0]
        else:
            s_scr[...] = jnp.zeros_like(s_scr)

    n_rows = n_chunks * chunk
    r = lax.broadcasted_iota(jnp.int32, (n_rows, n_rows), 0)
    c = lax.broadcasted_iota(jnp.int32, (n_rows, n_rows), 1)
    shift = chunk.bit_length() - 1
    assert 1 << shift == chunk, "chunk must be a power of two"
    tril = jnp.where(jnp.logical_and(r >> shift == c >> shift, r >= c), 1.0, 0.0).astype(BF16)
    la_hi, la_mid, la_lo = _split3(la_ref[...])
    b_all = _bdot(tril, la_hi) + _bdot(tril, la_mid) + _bdot(tril, la_lo)
    q_all = (q_ref[...].astype(F32) * scale * jnp.exp(b_all)).astype(BF16)
    k_all = (k_ref[...].astype(F32) * jnp.exp(-b_all)).astype(BF16)
    causal = (lax.broadcasted_iota(jnp.int32, (chunk, chunk), 0) >=
              lax.broadcasted_iota(jnp.int32, (chunk, chunk), 1))

    decays = []
    for ci in range(n_chunks):
        rows = slice(ci * chunk, (ci + 1) * chunk)
        b = b_all[rows, :]
        b_last = b[chunk - 1:chunk, :]
        q_e, k_e = q_all[rows, :], k_all[rows, :]
        k_d = (k_ref[rows, :].astype(F32) * jnp.exp(b_last - b)).astype(BF16)
        decays.append(jnp.exp(b_last))
        for h in range(heads):
            ks = slice(h * dk, (h + 1) * dk)
            v_h = v_ref[rows, h * dv:(h + 1) * dv]
            scores = lax.dot_general(q_e[:, ks], k_e[:, ks], (((1,), (1,)), ((), ())),
                                     preferred_element_type=F32)
            p_scr[rows, h * chunk:(h + 1) * chunk] = jnp.where(causal, scores, 0.0).astype(BF16)
            kv_scr[ci, h] = lax.dot_general(k_d[:, ks], v_h, (((0,), (0,)), ((), ())),
                                            preferred_element_type=F32)

    for ci in range(n_chunks):
        rows = slice(ci * chunk, (ci + 1) * chunk)
        seq = ci // chunks_per_seq
        for h in range(heads):
            ks = slice(h * dk, (h + 1) * dk)
            vs = slice(h * dv, (h + 1) * dv)
            s_old = s_scr[seq, h]
            o_scr[rows, vs] = (_bdot(p_scr[rows, h * chunk:(h + 1) * chunk], v_ref[rows, vs]) +
                               _bdot(q_all[rows, ks], s_old.astype(BF16)))
            s_scr[seq, h] = jnp.transpose(decays[ci][:, ks]) * s_old + kv_scr[ci, h]

    for h in range(heads):
        vs = slice(h * dv, (h + 1) * dv)
        o = o_scr[:, vs]
        mu = jnp.mean(o, axis=-1, keepdims=True)
        oc = o - mu
        var = jnp.mean(oc * oc, axis=-1, keepdims=True)
        on = oc * lax.rsqrt(var + LN_EPS) * ng_ref[:, vs]
        g = g_ref[:, vs].astype(F32)
        o_ref[:, vs] = (on * (g * _sigmoid(g))).astype(BF16)

    @pl.when(t == pl.num_programs(1) - 1)
    def _():
        if state_slices == 0:
            so_ref[...] = s_scr[...]
        for layer_slice in range(state_slices):
            so_ref[layer_slice] = s_scr[...] if layer_slice == state_slot else jnp.zeros_like(s_scr)


def _gla(qkvg, loga, norm_g, s0, layer, nb, s_len, rows, chunk, heads, dk, dv, stack=None):
    t = nb * s_len
    seqs = max(1, rows // s_len)
    assert s_len * seqs % rows == 0 and nb % seqs == 0 and s_len % chunk == 0
    nb, tps = nb // seqs, s_len * seqs // rows
    kw, vw = heads * dk, heads * dv
    has_state = s0 is not None
    state_shape, state_block, state_index = (nb * seqs, heads, dk, dv), (seqs, heads, dk, dv), lambda b, i: (b, 0, 0, 0)
    state_slices, state_slot, aliases = 0, 0, {}
    if stack is not None:
        depth, prev = stack
        state_shape = (depth,) + state_shape
        if prev is None:
            state_slices, state_slot = depth, layer
            state_block, state_index = (depth,) + state_block, lambda b, i: (0, b, 0, 0, 0)
        else:
            state_slices = 1
            state_block, state_index = (1,) + state_block, lambda b, i: (layer, b, 0, 0, 0)
    in_specs = [pl.BlockSpec((rows, kw), lambda b, i: (b * tps + i, 0)),
                pl.BlockSpec((rows, kw), lambda b, i: (b * tps + i, 1)),
                pl.BlockSpec((rows, vw), lambda b, i: (b * tps + i, (2 * kw) // vw)),
                pl.BlockSpec((rows, kw), lambda b, i: (b * tps + i, 0)),
                pl.BlockSpec((rows, vw), lambda b, i: (b * tps + i, (2 * kw) // vw + 1)),
                _const_spec(norm_g.shape)]
    args = [qkvg, qkvg, qkvg, loga, qkvg, norm_g]
    if has_state:
        in_specs.append(pl.BlockSpec((1, seqs, heads, dk, dv), lambda b, i: (layer, b, 0, 0, 0)))
        args.append(s0)
    if stack is not None and stack[1] is not None:
        in_specs.append(pl.BlockSpec(memory_space=pl.ANY))
        aliases[len(args)] = 1
        args.append(stack[1])
    return pl.pallas_call(
        functools.partial(_gla_kernel, chunk=chunk, n_chunks=rows // chunk, heads=heads, has_state=has_state,
                          n_inputs=len(args), state_slices=state_slices, state_slot=state_slot),
        grid=(nb, tps),
        in_specs=in_specs,
        out_specs=[pl.BlockSpec((rows, vw), lambda b, i: (b * tps + i, 0)),
                   pl.BlockSpec(state_block, state_index)],
        out_shape=[jax.ShapeDtypeStruct((t, vw), BF16), jax.ShapeDtypeStruct(state_shape, F32)],
        scratch_shapes=[pltpu.VMEM((seqs, heads, dk, dv), F32), pltpu.VMEM((rows, vw), F32),
                        pltpu.VMEM((rows, heads * chunk), BF16), pltpu.VMEM((rows // chunk, heads, dk, dv), F32)],
        input_output_aliases=aliases,
        compiler_params=_params("arbitrary", "arbitrary"),
        name="gla",
    )(*args)


def _rg_gates(xc, wax_ref, ba_ref, bx_ref, lam_ref):
    n_blocks, bw, _ = wax_ref.shape
    r_parts, i_parts = [], []
    for n in range(n_blocks):
        cs = slice(n * bw, (n + 1) * bw)
        pre = _bdot(xc[:, cs].astype(BF16), wax_ref[n])
        r_parts.append(jnp.tanh(pre[:, :bw] + ba_ref[:, cs]))
        i_parts.append(jnp.tanh(pre[:, bw:] + bx_ref[:, cs]))
    tanh_r = jnp.concatenate(r_parts, axis=1)
    tanh_i = jnp.concatenate(i_parts, axis=1)
    half_rate = (-0.5 * RG_C) * _softplus(-lam_ref[...])
    log_a = tanh_r * half_rate + half_rate
    a = jnp.exp(log_a)
    u = jnp.tanh(-log_a) * (1.0 + a * a)
    root = jnp.where(u > 0.0, u * lax.rsqrt(u), 0.0)
    bx = root * ((0.5 * tanh_i + 0.5) * xc)
    return a, bx


def _rglru_seq_kernel(xr_ref, cw_ref, cb_ref, wax_ref, ba_ref, bx_ref, lam_ref,
                      h_ref, hl_ref, cv_ref, cbuf, a_scr, b_scr, h_scr, hc):
    t = pl.program_id(1)
    rows = xr_ref.shape[0]
    taps = cw_ref.shape[0]
    head = cbuf.shape[0]
    assert taps - 1 <= head

    @pl.when(t == 0)
    def _():
        cbuf[...] = jnp.zeros_like(cbuf)
        hc[...] = jnp.zeros_like(hc)

    x = xr_ref[...]
    prev = cbuf[...]
    in_head = lax.broadcasted_iota(jnp.int32, prev.shape, 0)
    xc = cb_ref[...] + x * cw_ref[taps - 1:taps, :]
    for back in range(1, taps):
        rolled = pltpu.roll(x, back, 0)
        first_rows = jnp.where(in_head < back, pltpu.roll(prev, back, 0), rolled[:head])
        xc = xc + jnp.concatenate([first_rows, rolled[head:]], axis=0) * cw_ref[taps - 1 - back:taps - back, :]
    cbuf[...] = x[rows - head:, :]
    tail = x[rows - (taps - 1):, :]

    a, bx = _rg_gates(xc, wax_ref, ba_ref, bx_ref, lam_ref)
    a_scr[...] = a
    b_scr[...] = bx

    def step(i, h):
        h = a_scr[pl.ds(i, 1), :] * h + b_scr[pl.ds(i, 1), :]
        h_scr[pl.ds(i, 1), :] = h
        return h

    h_last = lax.fori_loop(0, rows, step, hc[...], unroll=16)
    hc[...] = h_last
    h_ref[...] = h_scr[...].astype(BF16)

    @pl.when(t == pl.num_programs(1) - 1)
    def _():
        hl_ref[0] = h_last
        cv_ref[0] = tail


def _rglru_seq(xr, w, nb, s_len, rows):
    width = xr.shape[1]
    t = nb * s_len
    tps = s_len // rows
    taps = w["cw"].shape[0]
    consts = [w["cw"], w["cb"], w["wax"], w["rba"], w["rbx"], w["lam"]]
    return pl.pallas_call(
        _rglru_seq_kernel,
        grid=(nb, tps),
        in_specs=[pl.BlockSpec((rows, width), lambda b, i: (b * tps + i, 0))] + [_const_spec(c.shape) for c in consts],
        out_specs=[pl.BlockSpec((rows, width), lambda b, i: (b * tps + i, 0)),
                   pl.BlockSpec((1, 1, width), lambda b, i: (b, 0, 0)),
                   pl.BlockSpec((1, taps - 1, width), lambda b, i: (b, 0, 0))],
        out_shape=[jax.ShapeDtypeStruct((t, width), BF16),
                   jax.ShapeDtypeStruct((nb, 1, width), F32),
                   jax.ShapeDtypeStruct((nb, taps - 1, width), F32)],
        scratch_shapes=[pltpu.VMEM((8, width), F32), pltpu.VMEM((rows, width), F32),
                        pltpu.VMEM((rows, width), F32), pltpu.VMEM((rows, width), F32),
                        pltpu.VMEM((1, width), F32)],
        compiler_params=_params("arbitrary", "arbitrary"),
        name="rglru_seq",
    )(xr, *consts)


GLA_INPUTS, RG_INPUTS = 6, 7


def _mixers_kernel(*refs, chunk, n_chunks, heads):
    ins, rest = refs[:GLA_INPUTS + RG_INPUTS], refs[GLA_INPUTS + RG_INPUTS:]
    gla_outs, rg_outs, gla_scr, rg_scr = rest[:2], rest[2:5], rest[5:9], rest[9:]
    _gla_kernel(*ins[:GLA_INPUTS], *gla_outs, *gla_scr, chunk=chunk, n_chunks=n_chunks, heads=heads,
                has_state=False, n_inputs=GLA_INPUTS, state_slices=0, state_slot=0)
    _rglru_seq_kernel(*ins[GLA_INPUTS:], *rg_outs, *rg_scr)


def _mixers_seq(qkvg, loga, norm_g, xr, w, nb, s_len, rows, chunk, heads, dk, dv):
    t = nb * s_len
    tps = s_len // rows
    kw, vw = heads * dk, heads * dv
    width = xr.shape[1]
    taps = w["cw"].shape[0]
    tile = lambda cols, col: pl.BlockSpec((rows, cols), lambda b, i: (b * tps + i, col))
    rg_consts = [w["cw"], w["cb"], w["wax"], w["rba"], w["rbx"], w["lam"]]
    in_specs = ([tile(kw, 0), tile(kw, 1), tile(vw, (2 * kw) // vw), tile(kw, 0), tile(vw, (2 * kw) // vw + 1),
                 _const_spec(norm_g.shape), tile(width, 0)] + [_const_spec(c.shape) for c in rg_consts])
    return pl.pallas_call(
        functools.partial(_mixers_kernel, chunk=chunk, n_chunks=rows // chunk, heads=heads),
        grid=(nb, tps),
        in_specs=in_specs,
        out_specs=[tile(vw, 0), pl.BlockSpec((1, heads, dk, dv), lambda b, i: (b, 0, 0, 0)),
                   tile(width, 0), pl.BlockSpec((1, 1, width), lambda b, i: (b, 0, 0)),
                   pl.BlockSpec((1, taps - 1, width), lambda b, i: (b, 0, 0))],
        out_shape=[jax.ShapeDtypeStruct((t, vw), BF16), jax.ShapeDtypeStruct((nb, heads, dk, dv), F32),
                   jax.ShapeDtypeStruct((t, width), BF16), jax.ShapeDtypeStruct((nb, 1, width), F32),
                   jax.ShapeDtypeStruct((nb, taps - 1, width), F32)],
        scratch_shapes=[pltpu.VMEM((1, heads, dk, dv), F32), pltpu.VMEM((rows, vw), F32),
                        pltpu.VMEM((rows, heads * chunk), BF16), pltpu.VMEM((rows // chunk, heads, dk, dv), F32),
                        pltpu.VMEM((8, width), F32), pltpu.VMEM((rows, width), F32),
                        pltpu.VMEM((rows, width), F32), pltpu.VMEM((rows, width), F32),
                        pltpu.VMEM((1, width), F32)],
        compiler_params=_params("arbitrary", "arbitrary"),
        name="mixers",
    )(qkvg, qkvg, qkvg, loga, qkvg, norm_g, xr, *rg_consts)


def _rglru_step_kernel(xr_ref, sc_ref, h0_ref, cw_ref, cb_ref, wax_ref, ba_ref, bx_ref, lam_ref,
                       h_ref, hn_ref, cv_ref):
    taps = cw_ref.shape[0]
    xr = xr_ref[...]
    xc = (cb_ref[...] + sum(sc_ref[:, j, :] * cw_ref[j:j + 1, :] for j in range(taps - 1)) +
          xr * cw_ref[taps - 1:taps, :])
    a, bx = _rg_gates(xc, wax_ref, ba_ref, bx_ref, lam_ref)
    h = a * h0_ref[...] + bx
    h_ref[...] = h.astype(BF16)
    hn_ref[...] = h
    for j in range(taps - 2):
        cv_ref[:, j, :] = sc_ref[:, j + 1, :]
    cv_ref[:, taps - 2, :] = xr


def _rglru_step(xr, conv, h0, w):
    n, width = xr.shape
    consts = [w["cw"], w["cb"], w["wax"], w["rba"], w["rbx"], w["lam"]]
    args = [xr, conv, h0] + consts
    return pl.pallas_call(
        _rglru_step_kernel,
        grid=(1,),
        in_specs=[_const_spec(a.shape) for a in args],
        out_specs=[_const_spec((n, width)), _const_spec((n, width)), _const_spec(conv.shape)],
        out_shape=[jax.ShapeDtypeStruct((n, width), BF16),
                   jax.ShapeDtypeStruct((n, width), F32),
                   jax.ShapeDtypeStruct(conv.shape, F32)],
        compiler_params=_params("arbitrary"),
        name="rglru_step",
    )(*args)


def _pack_bf16_pairs(x):
    half = x.shape[1] // 2
    hi = pltpu.bitcast(x[:, :half].astype(BF16).astype(F32), jnp.uint32)
    lo = pltpu.bitcast(x[:, half:].astype(BF16).astype(F32), jnp.uint32)
    return hi | (lo >> 16)


def _unpack_bf16_pairs(p):
    hi = pltpu.bitcast(p & jnp.uint32(0xFFFF0000), F32)
    lo = pltpu.bitcast(p << 16, F32)
    return jnp.concatenate([hi, lo], axis=1)


def _post_kernel(*refs, n_x, alpha, n_experts, n_tail_real):
    x_refs = refs[:n_x]
    (o_ref, ot_ref, h_ref, ht_ref, gate_ref, wb0_ref, wb1_ref, wo_ref, g1_ref, b1_ref, rw_ref, rb_ref,
     x1_ref, x1p_ref, idx_ref, wt_ref, cnt_ref, cnt_scr) = refs[n_x:]
    tm, d = o_ref.shape

    @pl.when(pl.program_id(0) == 0)
    def _():
        cnt_scr[...] = jnp.zeros_like(cnt_scr)

    x = _token_tile(x_refs)
    o = _head_or_tail(o_ref, ot_ref)
    h = _head_or_tail(h_ref, ht_ref)
    gate = gate_ref[...].astype(F32)
    merged = gate[:, :d] * _bdot(o, wb0_ref[...]) + gate[:, d:] * _bdot(h, wb1_ref[...])
    mix = _bdot(merged.astype(BF16), wo_ref[...])
    x1 = _layernorm(alpha * x + mix, g1_ref[...], b1_ref[...])
    x1_ref[...] = x1
    x1p_ref[...] = _pack_bf16_pairs(x1)

    xh = x1.astype(BF16)
    xl = (x1 - xh.astype(F32)).astype(BF16)
    n_lanes = rw_ref.shape[1] // 2
    hi_terms = _bdot(xh, rw_ref[...])
    logits = (hi_terms[:, n_lanes:] + _bdot(xl, rw_ref[:, :n_lanes])) + hi_terms[:, :n_lanes]
    logits = logits + rb_ref[...]
    lane = lax.broadcasted_iota(jnp.int32, logits.shape, 1)
    lane_f = lane.astype(F32)
    neg_inf = jnp.float32(-jnp.inf)
    cur = jnp.where(lane < n_experts, logits, neg_inf)
    vals, idxs = [], []
    for _ in range(TOP_K):
        m = jnp.max(cur, axis=-1, keepdims=True)
        sel = jnp.min(jnp.where(cur == m, lane_f, float(LANES)), axis=-1, keepdims=True)
        vals.append(m)
        idxs.append(sel)
        cur = jnp.where(lane_f == sel, neg_inf, cur)
    exps = [jnp.exp(v - vals[0]) for v in vals]
    total = sum(exps)

    onehots = [jnp.where(lane_f == idxs[j], 1.0, 0.0) for j in range(TOP_K)]
    chosen = sum(onehots)
    if n_tail_real < tm:
        is_fill = jnp.logical_and(pl.program_id(0) == pl.num_programs(0) - 1,
                                  lax.broadcasted_iota(jnp.int32, chosen.shape, 0) >= n_tail_real)
        chosen = jnp.where(is_fill, 0.0, chosen)
    r = lax.broadcasted_iota(jnp.int32, (tm, tm), 0)
    c = lax.broadcasted_iota(jnp.int32, (tm, tm), 1)
    earlier = jnp.where(r > c, 1.0, 0.0).astype(BF16)
    before = _bdot(earlier, chosen.astype(BF16)) + cnt_scr[...]
    ranks = [jnp.sum(onehots[j] * before, axis=-1, keepdims=True) for j in range(TOP_K)]
    cnt_scr[...] += jnp.sum(chosen, axis=0, keepdims=True)
    cnt_ref[...] = cnt_scr[...]

    idx_out = jnp.zeros(logits.shape, F32)
    wt_out = jnp.zeros(logits.shape, F32)
    for j in range(TOP_K):
        idx_out = jnp.where(lane == j, idxs[j], idx_out)
        idx_out = jnp.where(lane == TOP_K + j, ranks[j], idx_out)
        wt_out = jnp.where(lane == j, exps[j] / total, wt_out)
    idx_ref[...] = idx_out.astype(jnp.int32)
    wt_ref[...] = wt_out


def _post(x_head, x_tail, n_head_tiles, tail_tile, o, h, o_tail, h_tail, gate, w, tm, alpha, n_experts,
          n_tail_real):
    d = x_head.shape[1]
    t = (n_head_tiles + 1) * tm
    assert o.shape[0] == t - tm and o_tail.shape[0] == tm and gate.shape[0] == t
    consts = [w["wb0"], w["wb1"], w["wo"], w["ln1g"], w["ln1b"], w["rw"], w["rb"]]
    row = lambda width: pl.BlockSpec((tm, width), lambda i: (i, 0))
    xs, x_specs = _token_inputs(x_head, x_tail, n_head_tiles, tail_tile, tm)
    return pl.pallas_call(
        functools.partial(_post_kernel, n_x=len(xs), alpha=alpha, n_experts=n_experts, n_tail_real=n_tail_real),
        grid=(t // tm,),
        in_specs=x_specs + _head_tail_specs(n_head_tiles, 0, tm, d) + _head_tail_specs(n_head_tiles, 0, tm, d) +
                 [row(2 * d)] + [_weight_spec(c.shape) for c in consts],
        out_specs=[row(d), row(d // 2), row(LANES), row(LANES), _const_spec((1, LANES))],
        out_shape=[jax.ShapeDtypeStruct((t, d), F32),
                   jax.ShapeDtypeStruct((t, d // 2), jnp.uint32),
                   jax.ShapeDtypeStruct((t, LANES), jnp.int32),
                   jax.ShapeDtypeStruct((t, LANES), F32),
                   jax.ShapeDtypeStruct((1, LANES), F32)],
        scratch_shapes=[pltpu.VMEM((1, LANES), F32)],
        compiler_params=_params("arbitrary"),
        name="post",
    )(*xs, o, o_tail, h, h_tail, gate, *consts)


SC_INDEX_MAX = 128
SC_ALIGN = 8
SC_BUFFER_BYTES = 208 * 1024


def _sc_plan(n_rows, row_bytes):
    info = plsc.get_sparse_core_info()
    n_workers = info.num_cores * info.num_subcores
    per_worker, rem = divmod(n_rows, n_workers)
    assert rem == 0 and per_worker % SC_ALIGN == 0, "rows must split into aligned equal shares per subcore"
    cap = min(SC_INDEX_MAX, SC_BUFFER_BYTES // row_bytes)
    chunk = max(c for c in range(SC_ALIGN, cap + 1, SC_ALIGN) if per_worker % c == 0)
    return info.num_cores, n_workers, per_worker, chunk


def _sc_gather(table, idx):
    n_rows = idx.shape[0]
    d = table.shape[1]
    n_cores, _, per_worker, chunk = _sc_plan(n_rows, d * table.dtype.itemsize)
    mesh = plsc.VectorSubcoreMesh(core_axis_name="c", subcore_axis_name="s")

    n_chunks = per_worker // chunk

    @functools.partial(
        pl.kernel, mesh=mesh, out_type=jax.ShapeDtypeStruct((n_rows, d), table.dtype),
        scratch_types=[pltpu.VMEM((chunk,), jnp.int32)] * 2 + [pltpu.VMEM((chunk, d), table.dtype)] * 2 +
                      [pltpu.SemaphoreType.DMA] * 4)
    def gather(table_hbm, idx_hbm, out_hbm, idx0, idx1, rows0, rows1, g0, g1, w0, w1):
        idx_v, rows_v, gsem, wsem = (idx0, idx1), (rows0, rows1), (g0, g1), (w0, w1)
        worker = lax.axis_index("s") * n_cores + lax.axis_index("c")
        base = worker * per_worker

        def rows_of(c):
            return pl.ds(pl.multiple_of(base + c * chunk, SC_ALIGN), chunk)

        def start_gather(c):
            s = c % 2
            pltpu.sync_copy(idx_hbm.at[rows_of(c)], idx_v[s])
            return pltpu.async_copy(table_hbm.at[idx_v[s]], rows_v[s], gsem[s])

        gathers = {c: start_gather(c) for c in range(min(2, n_chunks))}
        writes = {}
        for c in range(n_chunks):
            s = c % 2
            gathers[c].wait()
            writes[c] = pltpu.async_copy(rows_v[s], out_hbm.at[rows_of(c)], wsem[s])
            if c + 2 < n_chunks:
                writes.pop(c).wait()
                gathers[c + 2] = start_gather(c + 2)
        for write in writes.values():
            write.wait()

    return gather(table, idx)


def _sc_scatter(x, pos, n_out):
    n, d = x.shape
    assert pos.shape == (TOP_K * n,)
    n_cores, _, per_worker, chunk = _sc_plan(n, d * x.dtype.itemsize)
    mesh = plsc.VectorSubcoreMesh(core_axis_name="c", subcore_axis_name="s")

    n_chunks = per_worker // chunk

    @functools.partial(
        pl.kernel, mesh=mesh, out_type=jax.ShapeDtypeStruct((n_out, d), x.dtype),
        scratch_types=[pltpu.VMEM((chunk, d), x.dtype)] * 2 + [pltpu.VMEM((chunk,), jnp.int32)] * TOP_K +
                      [pltpu.SemaphoreType.DMA] * 3)
    def scatter(x_hbm, pos_hbm, out_hbm, rows0, rows1, *rest):
        rows_v, idx_v, (l0, l1, ssem) = (rows0, rows1), rest[:TOP_K], rest[TOP_K:]
        lsem = (l0, l1)
        worker = lax.axis_index("s") * n_cores + lax.axis_index("c")
        base = worker * per_worker

        def first_row(c):
            return pl.multiple_of(base + c * chunk, SC_ALIGN)

        def start_load(c):
            return pltpu.async_copy(x_hbm.at[pl.ds(first_row(c), chunk)], rows_v[c % 2], lsem[c % 2])

        load = start_load(0)
        for c in range(n_chunks):
            next_load = start_load(c + 1) if c + 1 < n_chunks else None
            for k in range(TOP_K):
                pltpu.sync_copy(pos_hbm.at[pl.ds(k * n + first_row(c), chunk)], idx_v[k])
            load.wait()
            scatters = [pltpu.async_copy(rows_v[c % 2], out_hbm.at[idx_v[k]], ssem)
                        for k in range(TOP_K)]
            for scatter_k in scatters:
                scatter_k.wait()
            load = next_load

    return scatter(x, pos)


def _ffn_kernel(be_ref, valid_ref, nu_ref, x_ref, wgu_ref, bgu_ref, wd_ref, bd_ref, o_ref, wgu_bf, wd_bf):
    i = pl.program_id(0)
    bm = x_ref.shape[0]
    f = wd_ref.shape[2]
    valid = valid_ref[i]
    new_expert = jnp.logical_or(i == 0, be_ref[i] != be_ref[jnp.maximum(i - 1, 0)])

    @pl.when(new_expert)
    def _():
        wgu_bf[...] = wgu_ref[0, 0].astype(BF16)
        wd_bf[...] = wd_ref[0, 0].astype(BF16)

    def expert_rows(rows):
        x = _unpack_bf16_pairs(x_ref[rows, :]).astype(BF16)
        acc = jnp.zeros((x.shape[0], wd_ref.shape[3]), F32)
        for c in range(0, f, FFN_COLS):
            gate = _bdot(x, wgu_bf[:, c:c + FFN_COLS]) + bgu_ref[0, 0, :, c:c + FFN_COLS]
            up = _bdot(x, wgu_bf[:, f + c:f + c + FFN_COLS]) + bgu_ref[0, 0, :, f + c:f + c + FFN_COLS]
            gate = jnp.minimum(gate, SWIGLU_LIMIT)
            up = jnp.clip(up, -SWIGLU_LIMIT, SWIGLU_LIMIT)
            act = (up + 1.0) * gate * _sigmoid(SWIGLU_ALPHA * gate)
            acc = acc + _bdot(act.astype(BF16), wd_bf[c:c + FFN_COLS, :])
        o_ref[rows, :] = _pack_bf16_pairs(acc + bd_ref[0, 0])

    @pl.when(valid == bm)
    def _():
        expert_rows(slice(0, bm))

    @pl.when(jnp.logical_and(valid < bm, i < nu_ref[0]))
    def _():
        for s in range(0, bm, FFN_SUB_ROWS):
            rows = slice(s, s + FFN_SUB_ROWS)

            @pl.when(s < valid)
            def _():
                expert_rows(rows)


def _ffn(block_e, valid, xs, layer, w_gu, b_gu, w_down, b_down, bm):
    n_layers, n_exp, d, f2 = w_gu.shape
    f = f2 // 2
    n_blocks = block_e.shape[0]
    n_used = jnp.sum(valid > 0, keepdims=True).astype(jnp.int32)
    expert = lambda i, be, va, nu: (layer, be[i], 0, 0)
    rows = pl.BlockSpec((bm, d // 2), lambda i, be, va, nu: (jnp.minimum(i, nu[0] - 1), 0))
    grid_spec = pltpu.PrefetchScalarGridSpec(
        num_scalar_prefetch=3,
        grid=(n_blocks,),
        in_specs=[rows,
                  pl.BlockSpec((1, 1, d, f2), expert),
                  pl.BlockSpec((1, 1, 1, f2), expert),
                  pl.BlockSpec((1, 1, f, d), expert),
                  pl.BlockSpec((1, 1, 1, d), expert)],
        out_specs=rows,
        scratch_shapes=[pltpu.VMEM((d, f2), BF16), pltpu.VMEM((f, d), BF16)],
    )
    return pl.pallas_call(
        _ffn_kernel,
        grid_spec=grid_spec,
        out_shape=jax.ShapeDtypeStruct((n_blocks * bm, d // 2), jnp.uint32),
        compiler_params=_params("arbitrary"),
        name="ffn",
    )(block_e, valid, n_used, xs, w_gu, b_gu.reshape(n_layers, n_exp, 1, f2),
      w_down, b_down.reshape(n_layers, n_exp, 1, d))


def _combine_kernel(x_ref, *refs, alpha, split):
    y_refs, (wt_ref, g_ref, b_ref), out_refs = refs[:TOP_K], refs[TOP_K:TOP_K + 3], refs[TOP_K + 3:]
    wt = wt_ref[...]
    ffn = sum(wt[:, j:j + 1] * _unpack_bf16_pairs(y_refs[j][...]) for j in range(TOP_K))
    res = _layernorm(alpha * x_ref[...] + ffn, g_ref[...], b_ref[...])
    if not split:
        out_refs[0][...] = res
    else:
        is_tail = pl.program_id(0) == pl.num_programs(0) - 1

        @pl.when(jnp.logical_not(is_tail))
        def _():
            out_refs[0][...] = res

        @pl.when(is_tail)
        def _():
            out_refs[1][...] = res


def _combine(x1, y, wt, g, b, tm, alpha, split=False):
    t, d = x1.shape
    n_tiles = t // tm
    planes = [pl.BlockSpec((tm, d // 2), lambda i, j=j: (j * n_tiles + i, 0)) for j in range(TOP_K)]
    if split:
        out_specs = [pl.BlockSpec((tm, d), lambda i: (jnp.minimum(i, n_tiles - 2), 0)), _const_spec((tm, d))]
        out_shape = [jax.ShapeDtypeStruct((t - tm, d), F32), jax.ShapeDtypeStruct((tm, d), F32)]
    else:
        out_specs = pl.BlockSpec((tm, d), lambda i: (i, 0))
        out_shape = jax.ShapeDtypeStruct((t, d), F32)
    return pl.pallas_call(
        functools.partial(_combine_kernel, alpha=alpha, split=split),
        grid=(n_tiles,),
        in_specs=[pl.BlockSpec((tm, d), lambda i: (i, 0))] + planes +
                 [pl.BlockSpec((tm, LANES), lambda i: (i, 0)), _const_spec(g.shape), _const_spec(b.shape)],
        out_specs=out_specs,
        out_shape=out_shape,
        compiler_params=_params("arbitrary"),
        name="combine",
    )(x1, *([y] * TOP_K), wt, g, b)


def _combine_proj_kernel(x_ref, *refs, alpha):
    y_refs, (wt_ref, g_ref, b_ref) = refs[:TOP_K], refs[TOP_K:TOP_K + 3]
    proj_weights, (x2_ref, *proj_outs) = refs[TOP_K + 3:TOP_K + 10], refs[TOP_K + 10:]
    wt = wt_ref[...]
    ffn = sum(wt[:, j:j + 1] * _unpack_bf16_pairs(y_refs[j][...]) for j in range(TOP_K))
    x2 = _layernorm(alpha * x_ref[...] + ffn, g_ref[...], b_ref[...])
    x2_ref[...] = x2
    _proj_tile(x2.astype(BF16), *proj_weights, *proj_outs)


def _combine_proj(x1, y, wt, g, b, w_next, tm, alpha):
    t, d = x1.shape
    n_tiles = t // tm
    row = lambda width: pl.BlockSpec((tm, width), lambda i: (i, 0))
    planes = [pl.BlockSpec((tm, d // 2), lambda i, j=j: (j * n_tiles + i, 0)) for j in range(TOP_K)]
    consts, w_specs, proj_out_specs, proj_out_shape = _proj_operands(w_next, t, d, tm)
    return pl.pallas_call(
        functools.partial(_combine_proj_kernel, alpha=alpha),
        grid=(n_tiles,),
        in_specs=[row(d)] + planes + [row(LANES), _const_spec(g.shape), _const_spec(b.shape)] + w_specs,
        out_specs=[row(d)] + proj_out_specs,
        out_shape=[jax.ShapeDtypeStruct((t, d), F32)] + proj_out_shape,
        compiler_params=_params("arbitrary"),
        name="combine_proj",
    )(x1, *([y] * TOP_K), wt, g, b, *consts)


def _route(experts, ranks, counts, bm):
    n_experts = counts.shape[0]
    i32 = jnp.int32
    padded = (counts + bm - 1) // bm * bm
    ends_pad = jnp.cumsum(padded)
    start_pad = ends_pad - padded
    n_blocks = -(-experts.size // bm) + n_experts
    first_row = jnp.arange(n_blocks, dtype=i32)[:, None] * bm
    block_e = jnp.minimum(jnp.sum(ends_pad[None, :] <= first_row, axis=1), n_experts - 1).astype(i32)
    valid = jnp.clip(counts[block_e] - (first_row[:, 0] - start_pad[block_e]), 0, bm).astype(i32)
    onehot = experts[..., None] == jnp.arange(n_experts, dtype=i32)
    pos = jnp.sum(jnp.where(onehot, start_pad, 0), axis=-1).astype(i32) + ranks
    return block_e, valid, pos


def _moe(x1, x1p, routing, counts, top_w_pad, w, layer, moe_weights, alpha, tm, n_real, w_next):
    n_tok = x1.shape[0]
    n_experts = moe_weights[0].shape[1]
    block_e, valid, pos = _route(routing[:n_real, :TOP_K], routing[:n_real, TOP_K:2 * TOP_K],
                                 counts[0, :n_experts].astype(jnp.int32), MOE_ROWS)
    n_rows = block_e.shape[0] * MOE_ROWS
    n_fill = n_tok - n_real
    spare = n_rows + jnp.arange(n_fill * TOP_K, dtype=jnp.int32).reshape(n_fill, TOP_K)
    xs = _sc_scatter(x1p, jnp.concatenate([pos, spare]).T.reshape(-1), n_rows + n_fill * TOP_K)
    out_rows = _ffn(block_e, valid, xs, layer, *moe_weights, MOE_ROWS)
    y = _sc_gather(out_rows, jnp.concatenate([pos, spare - n_rows]).T.reshape(-1))
    if w_next is None:
        return _combine(x1, y, top_w_pad, w["ln2g"], w["ln2b"], tm, alpha, split=True)
    x2, *proj_outs = _combine_proj(x1, y, top_w_pad, w["ln2g"], w["ln2b"], w_next, tm, alpha)
    return x2, proj_outs


def _hi_lo(w):
    hi = w.astype(BF16)
    return jnp.concatenate([hi, (w - hi.astype(F32)).astype(BF16)], axis=1)


def _layer_weights(layer, w_in, gla_w_a2, gla_b_a, gla_norm_g, rg_conv_w, rg_conv_b, rg_w_a, rg_b_a, rg_w_x,
                   rg_b_x, rg_lambda, b_merge, w_branch, w_o, ln1_g, ln1_b, ln2_g, ln2_b, router_w, router_b,
                   kw, vw):
    d = w_in.shape[1]
    rank = gla_w_a2.shape[1]
    width = rg_conv_w.shape[2]
    n_exp = router_w.shape[2]
    c0 = 2 * kw + 2 * vw
    wi = w_in[layer]
    row = lambda v: v.reshape(1, -1)
    return {
        "wq": wi[:, :c0].astype(BF16),
        "wal": jnp.pad(wi[:, c0:c0 + rank], ((0, 0), (0, LANES - rank))).astype(BF16),
        "wa2": jnp.pad(gla_w_a2[layer], ((0, LANES - rank), (0, 0))).astype(BF16),
        "ba": row(gla_b_a[layer]),
        "wxr": wi[:, c0 + rank:c0 + rank + width].astype(BF16),
        "wmg": wi[:, c0 + rank + width:].astype(BF16),
        "bmg": row(b_merge[layer]),
        "ng": row(gla_norm_g[layer]),
        "cw": rg_conv_w[layer], "cb": row(rg_conv_b[layer]),
        "wax": (0.5 * jnp.concatenate([rg_w_a[layer], rg_w_x[layer]], axis=-1)).astype(BF16),
        "rba": row(0.5 * rg_b_a[layer]), "rbx": row(0.5 * rg_b_x[layer]), "lam": row(rg_lambda[layer]),
        "wb0": w_branch[layer, 0].astype(BF16), "wb1": w_branch[layer, 1].astype(BF16),
        "wo": w_o[layer].astype(BF16),
        "ln1g": row(ln1_g[layer]), "ln1b": row(ln1_b[layer]),
        "ln2g": row(ln2_g[layer]), "ln2b": row(ln2_b[layer]),
        "rw": _hi_lo(jnp.pad(router_w[layer], ((0, 0), (0, LANES - n_exp)))),
        "rb": jnp.pad(row(router_b[layer]), ((0, 0), (0, LANES - n_exp))),
    }


def kernel(x_prompt, x_sample, state_gla, state_rglru, state_conv, ln1_g, ln1_b, w_in, gla_w_a2, gla_b_a, gla_norm_g, rg_conv_w, rg_conv_b, rg_w_a, rg_b_a, rg_w_x, rg_b_x, rg_lambda, b_merge, w_branch, w_o, ln2_g, ln2_b, router_w, router_b, moe_w_gu, moe_b_gu, moe_w_down, moe_b_down):
    n_p, seq, d = x_prompt.shape
    n_s, dec_seq, _ = x_sample.shape
    assert dec_seq == 1, "the sample group carries one new token per sequence"
    depth, _, heads, dk, dv = state_gla.shape
    kw, vw = heads * dk, heads * dv
    n_exp = router_w.shape[2]
    alpha = (2.0 * depth) ** 0.25
    t_p = n_p * seq
    pad_rows = BF16_SUBLANES

    tile = PROJ_ROWS
    assert t_p % tile == 0 and n_s <= tile
    n_head_tiles = t_p // tile
    tail_pad = lambda a: jnp.pad(a, ((0, tile - n_s), (0, 0)))
    x_head, x_tail, tail_tile = x_prompt.reshape(t_p, d), tail_pad(x_sample.reshape(n_s, d)), 0
    moe_weights = (moe_w_gu, moe_b_gu, moe_w_down, moe_b_down)
    outs = {k: [] for k in ("gla_p", "rg_p", "cv_p", "rg_s", "cv_s")}
    gla_s = None
    weights = [_layer_weights(layer, w_in, gla_w_a2, gla_b_a, gla_norm_g, rg_conv_w, rg_conv_b, rg_w_a, rg_b_a,
                              rg_w_x, rg_b_x, rg_lambda, b_merge, w_branch, w_o, ln1_g, ln1_b, ln2_g, ln2_b,
                              router_w, router_b, kw, vw) for layer in range(depth)]
    projections = _proj(x_head, x_tail, n_head_tiles, tail_tile, weights[0], tile)
    for layer in range(depth):
        w = weights[layer]
        qkvg, loga, xr, gate = projections
        o_p, s_p, h_p, hl_p, cv_p = _mixers_seq(qkvg, loga, w["ng"], xr, w, n_p, seq, SEQ_ROWS, GLA_CHUNK,
                                                heads, dk, dv)
        pad = lambda a: jnp.pad(a[t_p:t_p + n_s, None, :],
                                ((0, 0), (0, pad_rows - 1), (0, 0))).reshape(n_s * pad_rows, -1)
        o_s, gla_s = _gla(pad(qkvg), pad(loga), w["ng"], state_gla, layer, n_s, pad_rows,
                          DECODE_SEQS * pad_rows, pad_rows, heads, dk, dv, stack=(depth, gla_s))
        o_s = o_s.reshape(n_s, pad_rows, vw)[:, 0]
        h_s, hn_s, cv_s = _rglru_step(xr[t_p:t_p + n_s], state_conv[layer], state_rglru[layer], w)
        x1, x1p, routing, wt, cnt = _post(x_head, x_tail, n_head_tiles, tail_tile, o_p, h_p, tail_pad(o_s),
                                          tail_pad(h_s), gate, w, tile, alpha, n_exp, n_s)
        w_next = weights[layer + 1] if layer + 1 < depth else None
        result = _moe(x1, x1p, routing, cnt, wt, w, layer, moe_weights, alpha, tile, t_p + n_s, w_next)
        if w_next is not None:
            x, projections = result
            x_head, x_tail, tail_tile = x, x, n_head_tiles
        outs["gla_p"].append(s_p)
        outs["rg_p"].append(hl_p.reshape(n_p, -1))
        outs["cv_p"].append(cv_p)
        outs["rg_s"].append(hn_s)
        outs["cv_s"].append(cv_s)
    y_head, y_tail = result
    return (y_head.reshape(n_p, seq, d), y_tail[:n_s].reshape(n_s, dec_seq, d),
            jnp.stack(outs["gla_p"]), jnp.stack(outs["rg_p"]), jnp.stack(outs["cv_p"]),
            gla_s, jnp.stack(outs["rg_s"]), jnp.stack(outs["cv_s"]))
```
